```python
import math
import jax
import jax.numpy as jnp
from jax import lax
import numpy as np

D_MODEL = 2048
BATCH = 1
SEQ = 16384
DEPTH = 2
DEC_BATCH = 16
DEC_SEQ = 16
PAST_LEN = 1024

CHUNK = 64
Q_BLOCK = 128
EPS = 1e-6

D_FF = 5504
FFN_RES = 0.5

MLA_HEADS = 8
MLA_NOPE = 128
MLA_ROPE = 64
MLA_V = 128
MLA_Q_LORA = 512
MLA_KV_LORA = 512
MLA_WIDTH = MLA_HEADS * MLA_V
MLA_SCALE = (MLA_NOPE + MLA_ROPE) ** -0.5
ROPE_BASE = 10000.0

SSM_HEADS = 16
SSM_HEAD_DIM = 64
SSM_D_INNER = SSM_HEADS * SSM_HEAD_DIM
SSM_GROUPS = 2
SSM_STATE = 128
SSM_CONV_W = 4
SSM_CONV_DIM = SSM_D_INNER + 2 * SSM_GROUPS * SSM_STATE
SSD_CHUNK = CHUNK
DT_MIN = 1e-3
DT_MAX = 1e-1

FOX_HEADS = 8
FOX_HEAD_DIM = 128
FOX_WIDTH = FOX_HEADS * FOX_HEAD_DIM
FOX_SCALE = FOX_HEAD_DIM ** -0.5
FOX_FORGET_BIAS = 2.0

N_BRANCH = 3
IN_SIZES = (MLA_Q_LORA, MLA_KV_LORA, MLA_ROPE,
            SSM_D_INNER, SSM_CONV_DIM, SSM_HEADS,
            FOX_WIDTH, FOX_WIDTH, FOX_WIDTH, FOX_HEADS,
            N_BRANCH * D_MODEL)
IN_COLS = sum(IN_SIZES)

kernel_name = "hybrid_streaming_mla_ssd_fox_step"


def rmsnorm(x, g):
    xf = x.astype(jnp.float32)
    y = xf * lax.rsqrt(jnp.mean(xf * xf, axis=-1, keepdims=True) + EPS)
    return (y * g.astype(jnp.float32)).astype(x.dtype)


def swiglu(x, w_gate, w_up, w_down):
    return (jax.nn.silu(x @ w_gate) * (x @ w_up)) @ w_down


def split_cols(u, sizes):
    parts, start = [], 0
    for n in sizes:
        parts.append(u[..., start:start + n])
        start += n
    return parts


def rope(x, pos):
    half = x.shape[-1] // 2
    inv_freq = ROPE_BASE ** (-jnp.arange(half, dtype=jnp.float32) / half)
    ang = pos.astype(jnp.float32)[:, None] * inv_freq[None, :]
    cos, sin = jnp.cos(ang)[:, None, :], jnp.sin(ang)[:, None, :]
    xf = x.astype(jnp.float32)
    x1, x2 = xf[..., :half], xf[..., half:]
    return jnp.concatenate([x1 * cos - x2 * sin, x2 * cos + x1 * sin], axis=-1).astype(x.dtype)


def sweep_query_blocks(attend, qs, qpos):
    sq = qpos.shape[0]
    if sq <= Q_BLOCK:
        return attend(qs, qpos)
    nb = sq // Q_BLOCK

    def to_blocks(t):
        return jnp.moveaxis(t.reshape((t.shape[0], nb, Q_BLOCK) + t.shape[2:]), 1, 0)

    out = lax.map(lambda a: attend(a[0], a[1]),
                  (tuple(to_blocks(t) for t in qs), qpos.reshape(nb, Q_BLOCK)))
    out = jnp.moveaxis(out, 0, 1)
    return out.reshape((out.shape[0], sq) + out.shape[3:])


def mla_block(qs, qpos, k_nope, k_pe, v, kpos):
    q_nope, q_pe = qs
    s = jnp.einsum("bqhd,bkhd->bhqk", q_nope, k_nope) + jnp.einsum("bqhr,bkr->bhqk", q_pe, k_pe)
    s = s.astype(jnp.float32) * MLA_SCALE
    visible = (kpos[None, :] // CHUNK) <= (qpos[:, None] // CHUNK)
    prob = jax.nn.softmax(jnp.where(visible, s, -jnp.inf), axis=-1)
    return jnp.einsum("bhqk,bkhd->bqhd", prob.astype(v.dtype), v)


def fox_block(qs, qpos, k, v, cum_k, kpos):
    q, cum_q = qs
    s = jnp.einsum("bqhd,bkhd->bhqk", q, k).astype(jnp.float32) * FOX_SCALE
    decay = jnp.swapaxes(cum_q, 1, 2)[..., :, None] - jnp.swapaxes(cum_k, 1, 2)[..., None, :]
    visible = kpos[None, :] <= qpos[:, None]
    prob = jax.nn.softmax(jnp.where(visible, s + decay, -jnp.inf), axis=-1)
    return jnp.einsum("bhqk,bkhd->bqhd", prob.astype(v.dtype), v)


def causal_conv(xbc, conv_state, w, b):
    s = xbc.shape[1]
    xp = jnp.concatenate([conv_state.astype(xbc.dtype), xbc], axis=1)
    y = b
    for k in range(SSM_CONV_W):
        y = y + xp[:, k:k + s] * w[k]
    return y, xp[:, xp.shape[1] - (SSM_CONV_W - 1):]


def ssd_scan(x, dt, a, bm, cm, h0):
    bsz, length, nh, hp = x.shape
    q = min(SSD_CHUNK, length)
    nc = length // q
    f32 = jnp.float32
    adt = dt.astype(f32) * a.astype(f32)
    xdt = x.astype(f32) * dt.astype(f32)[..., None]

    def chunks(t):
        return t.astype(f32).reshape((bsz, nc, q) + t.shape[2:])

    adt, xdt, bm, cm = chunks(adt), chunks(xdt), chunks(bm), chunks(cm)
    acs = jnp.cumsum(adt, axis=2)
    seg = acs[:, :, :, None, :] - acs[:, :, None, :, :]
    causal = jnp.tril(jnp.ones((q, q), dtype=bool))[None, None, :, :, None]
    decay = jnp.exp(jnp.where(causal, seg, -jnp.inf))
    y_diag = jnp.einsum("bclhn,bcshn,bclsh,bcshp->bclhp", cm, bm, decay, xdt)
    decay_to_end = jnp.exp(acs[:, :, -1:, :] - acs)
    chunk_states = jnp.einsum("bclhn,bclh,bclhp->bchpn", bm, decay_to_end, xdt)
    chunk_decay = jnp.exp(acs[:, :, -1, :])

    def step(h, inp):
        st, d = inp
        return h * d[:, :, None, None] + st, h

    h_last, h_in = lax.scan(step, h0.astype(f32),
                            (jnp.moveaxis(chunk_states, 1, 0), jnp.moveaxis(chunk_decay, 1, 0)))
    h_in = jnp.moveaxis(h_in, 0, 1)
    y_off = jnp.einsum("bclhn,bchpn,bclh->bclhp", cm, h_in, jnp.exp(acs))
    return (y_diag + y_off).reshape(bsz, length, nh, hp), h_last


def token_mixers(h, pos, p, past):
    bsz, s, _ = h.shape
    f32 = jnp.float32
    (u_q, u_ckv, u_kpe, u_z, u_xbc, u_dt,
     u_fq, u_fk, u_fv, u_ff, u_gate) = split_cols(h @ p["w_in"], IN_SIZES)

    q = (rmsnorm(u_q, p["mla_q_norm"]) @ p["mla_w_uq"]).reshape(bsz, s, MLA_HEADS, MLA_NOPE + MLA_ROPE)
    q_nope, q_pe = q[..., :MLA_NOPE], rope(q[..., MLA_NOPE:], pos)
    ckv_new = rmsnorm(u_ckv, p["mla_kv_norm"])
    kpe_new = rope(u_kpe[:, :, None, :], pos)[:, :, 0, :]
    if past is None:
        ckv, kpe = ckv_new, kpe_new
    else:
        ckv = jnp.concatenate([past["mla_ckv"], ckv_new], axis=1)
        kpe = jnp.concatenate([past["mla_kpe"], kpe_new], axis=1)
    kpos = jnp.arange(ckv.shape[1], dtype=jnp.int32)
    kv = (ckv @ p["mla_w_ukv"]).reshape(bsz, ckv.shape[1], MLA_HEADS, MLA_NOPE + MLA_V)
    k_nope, v_mla = kv[..., :MLA_NOPE], kv[..., MLA_NOPE:]
    o_mla = sweep_query_blocks(lambda qs, qp: mla_block(qs, qp, k_nope, kpe, v_mla, kpos),
                               (q_nope, q_pe), pos).reshape(bsz, s, MLA_WIDTH)

    if past is None:
        conv_state = jnp.zeros((bsz, SSM_CONV_W - 1, SSM_CONV_DIM), h.dtype)
        h0 = jnp.zeros((bsz, SSM_HEADS, SSM_HEAD_DIM, SSM_STATE), f32)
    else:
        conv_state, h0 = past["conv"], past["ssm"]
    xbc, conv_new = causal_conv(u_xbc, conv_state, p["ssm_conv_w"], p["ssm_conv_b"])
    xbc = jax.nn.silu(xbc)
    xs, bm, cm = split_cols(xbc, (SSM_D_INNER, SSM_GROUPS * SSM_STATE, SSM_GROUPS * SSM_STATE))
    xs = xs.reshape(bsz, s, SSM_HEADS, SSM_HEAD_DIM)
    rep = SSM_HEADS // SSM_GROUPS
    bm = jnp.repeat(bm.reshape(bsz, s, SSM_GROUPS, SSM_STATE), rep, axis=2)
    cm = jnp.repeat(cm.reshape(bsz, s, SSM_GROUPS, SSM_STATE), rep, axis=2)
    dt = jax.nn.softplus(u_dt.astype(f32) + p["ssm_dt_bias"].astype(f32))
    a = -jnp.exp(p["ssm_a_log"].astype(f32))
    y, ssm_new = ssd_scan(xs, dt, a, bm, cm, h0)
    y = y + p["ssm_d"].astype(f32)[:, None] * xs.astype(f32)
    y = y.reshape(bsz, s, SSM_D_INNER) * jax.nn.silu(u_z.astype(f32))
    o_ssd = rmsnorm(y.reshape(bsz, s, SSM_GROUPS, SSM_D_INNER // SSM_GROUPS),
                    p["ssm_norm"].reshape(SSM_GROUPS, SSM_D_INNER // SSM_GROUPS))
    o_ssd = o_ssd.reshape(bsz, s, SSM_D_INNER).astype(h.dtype)

    fq = u_fq.reshape(bsz, s, FOX_HEADS, FOX_HEAD_DIM)
    fk = u_fk.reshape(bsz, s, FOX_HEADS, FOX_HEAD_DIM)
    fv = u_fv.reshape(bsz, s, FOX_HEADS, FOX_HEAD_DIM)
    logf_new = jax.nn.log_sigmoid(u_ff.astype(f32) + p["fox_b_f"].astype(f32))
    if past is None:
        k_all, v_all, logf_all = fk, fv, logf_new
    else:
        k_all = jnp.concatenate([past["fox_k"], fk], axis=1)
        v_all = jnp.concatenate([past["fox_v"], fv], axis=1)
        logf_all = jnp.concatenate([past["fox_logf"].astype(f32), logf_new], axis=1)
    cum = jnp.cumsum(logf_all, axis=1)
    fpos = jnp.arange(k_all.shape[1], dtype=jnp.int32)
    o_fox = sweep_query_blocks(lambda qs, qp: fox_block(qs, qp, k_all, v_all, cum, fpos),
                               (fq, cum[:, cum.shape[1] - s:]), pos).reshape(bsz, s, FOX_WIDTH)

    gates = jax.nn.sigmoid(u_gate.astype(f32)).astype(h.dtype).reshape(bsz, s, N_BRANCH, D_MODEL)
    merged = (gates[:, :, 0] * (o_mla @ p["w_br_mla"])
              + gates[:, :, 1] * (o_ssd @ p["w_br_ssd"])
              + gates[:, :, 2] * (o_fox @ p["w_br_fox"]))
    out = merged @ p["w_out"]
    return out, (ckv_new, kpe_new, fk, fv, logf_new, ssm_new, conv_new)


def trunk_layer(x, pos, p, past):
    x = x + FFN_RES * swiglu(rmsnorm(x, p["ffn1_norm"]), p["ffn1_w_gate"], p["ffn1_w_up"], p["ffn1_w_down"])
    mix, state = token_mixers(rmsnorm(x, p["mix_norm"]), pos, p, past)
    x = x + mix
    x = x + FFN_RES * swiglu(rmsnorm(x, p["ffn2_norm"]), p["ffn2_w_gate"], p["ffn2_w_up"], p["ffn2_w_down"])
    return x, state


def setup_inputs(seed: int = 0) -> dict:
    key = jax.random.key(seed)
    ks = iter(jax.random.split(key, 48))
    L = DEPTH
    f32 = jnp.float32

    def nrm(shape):
        return jax.random.normal(next(ks), shape, f32)

    def w(shape, fan_in):
        return nrm(shape) * fan_in ** -0.5

    def gain(shape):
        return 1.0 + 0.02 * nrm(shape)

    x_prompt = nrm((BATCH, SEQ, D_MODEL))
    x_sample = nrm((DEC_BATCH, DEC_SEQ, D_MODEL))
    cache_mla_ckv = nrm((L, DEC_BATCH, PAST_LEN, MLA_KV_LORA))
    cache_mla_kpe = nrm((L, DEC_BATCH, PAST_LEN, MLA_ROPE))
    cache_fox_k = nrm((L, DEC_BATCH, PAST_LEN, FOX_HEADS, FOX_HEAD_DIM))
    cache_fox_v = nrm((L, DEC_BATCH, PAST_LEN, FOX_HEADS, FOX_HEAD_DIM))
    cache_fox_logf = jax.nn.log_sigmoid(FOX_FORGET_BIAS + nrm((L, DEC_BATCH, PAST_LEN, FOX_HEADS)))
    state_ssm = 0.1 * nrm((L, DEC_BATCH, SSM_HEADS, SSM_HEAD_DIM, SSM_STATE))
    state_conv = nrm((L, DEC_BATCH, SSM_CONV_W - 1, SSM_CONV_DIM))

    dt = jnp.exp(jax.random.uniform(next(ks), (L, SSM_HEADS), f32,
                                    minval=math.log(DT_MIN), maxval=math.log(DT_MAX)))
    ssm_dt_bias = dt + jnp.log(-jnp.expm1(-dt))
    ssm_a_log = jnp.log(jax.random.uniform(next(ks), (L, SSM_HEADS), f32, minval=1.0, maxval=16.0))

    return {
        "x_prompt": x_prompt,
        "x_sample": x_sample,
        "cache_mla_ckv": cache_mla_ckv,
        "cache_mla_kpe": cache_mla_kpe,
        "cache_fox_k": cache_fox_k,
        "cache_fox_v": cache_fox_v,
        "cache_fox_logf": cache_fox_logf,
        "state_ssm": state_ssm,
        "state_conv": state_conv,
        "ffn1_norm": gain((L, D_MODEL)),
        "ffn1_w_gate": w((L, D_MODEL, D_FF), D_MODEL),
        "ffn1_w_up": w((L, D_MODEL, D_FF), D_MODEL),
        "ffn1_w_down": w((L, D_FF, D_MODEL), D_FF),
        "mix_norm": gain((L, D_MODEL)),
        "w_in": w((L, D_MODEL, IN_COLS), D_MODEL),
        "mla_q_norm": gain((L, MLA_Q_LORA)),
        "mla_w_uq": w((L, MLA_Q_LORA, MLA_HEADS * (MLA_NOPE + MLA_ROPE)), MLA_Q_LORA),
        "mla_kv_norm": gain((L, MLA_KV_LORA)),
        "mla_w_ukv": w((L, MLA_KV_LORA, MLA_HEADS * (MLA_NOPE + MLA_V)), MLA_KV_LORA),
        "ssm_conv_w": w((L, SSM_CONV_W, SSM_CONV_DIM), SSM_CONV_W),
        "ssm_conv_b": 0.02 * nrm((L, SSM_CONV_DIM)),
        "ssm_dt_bias": ssm_dt_bias,
        "ssm_a_log": ssm_a_log,
        "ssm_d": 1.0 + 0.1 * nrm((L, SSM_HEADS)),
        "ssm_norm": gain((L, SSM_D_INNER)),
        "fox_b_f": FOX_FORGET_BIAS + 0.5 * nrm((L, FOX_HEADS)),
        "w_br_mla": w((L, MLA_WIDTH, D_MODEL), MLA_WIDTH),
        "w_br_ssd": w((L, SSM_D_INNER, D_MODEL), SSM_D_INNER),
        "w_br_fox": w((L, FOX_WIDTH, D_MODEL), FOX_WIDTH),
        "w_out": w((L, D_MODEL, D_MODEL), D_MODEL),
        "ffn2_norm": gain((L, D_MODEL)),
        "ffn2_w_gate": w((L, D_MODEL, D_FF), D_MODEL),
        "ffn2_w_up": w((L, D_MODEL, D_FF), D_MODEL),
        "ffn2_w_down": w((L, D_FF, D_MODEL), D_FF),
        "final_norm": gain((D_MODEL,)),
    }


def reference(x_prompt, x_sample, cache_mla_ckv, cache_mla_kpe, cache_fox_k, cache_fox_v,
              cache_fox_logf, state_ssm, state_conv,
              ffn1_norm, ffn1_w_gate, ffn1_w_up, ffn1_w_down, mix_norm, w_in,
              mla_q_norm, mla_w_uq, mla_kv_norm, mla_w_ukv,
              ssm_conv_w, ssm_conv_b, ssm_dt_bias, ssm_a_log, ssm_d, ssm_norm,
              fox_b_f, w_br_mla, w_br_ssd, w_br_fox, w_out,
              ffn2_norm, ffn2_w_gate, ffn2_w_up, ffn2_w_down, final_norm):
    pos_p = jnp.arange(x_prompt.shape[1], dtype=jnp.int32)
    past_len = cache_mla_ckv.shape[2]
    pos_s = past_len + jnp.arange(x_sample.shape[1], dtype=jnp.int32)
    hp, hs = x_prompt, x_sample
    new_p, new_s = [], []
    for l in range(DEPTH):
        p = {
            "ffn1_norm": ffn1_norm[l], "ffn1_w_gate": ffn1_w_gate[l], "ffn1_w_up": ffn1_w_up[l],
            "ffn1_w_down": ffn1_w_down[l], "mix_norm": mix_norm[l], "w_in": w_in[l],
            "mla_q_norm": mla_q_norm[l], "mla_w_uq": mla_w_uq[l], "mla_kv_norm": mla_kv_norm[l],
            "mla_w_ukv": mla_w_ukv[l], "ssm_conv_w": ssm_conv_w[l], "ssm_conv_b": ssm_conv_b[l],
            "ssm_dt_bias": ssm_dt_bias[l], "ssm_a_log": ssm_a_log[l], "ssm_d": ssm_d[l],
            "ssm_norm": ssm_norm[l], "fox_b_f": fox_b_f[l], "w_br_mla": w_br_mla[l],
            "w_br_ssd": w_br_ssd[l], "w_br_fox": w_br_fox[l], "w_out": w_out[l],
            "ffn2_norm": ffn2_norm[l], "ffn2_w_gate": ffn2_w_gate[l], "ffn2_w_up": ffn2_w_up[l],
            "ffn2_w_down": ffn2_w_down[l],
        }
        past = {
            "mla_ckv": cache_mla_ckv[l], "mla_kpe": cache_mla_kpe[l], "fox_k": cache_fox_k[l],
            "fox_v": cache_fox_v[l], "fox_logf": cache_fox_logf[l], "ssm": state_ssm[l],
            "conv": state_conv[l],
        }
        hp, st_p = trunk_layer(hp, pos_p, p, None)
        hs, st_s = trunk_layer(hs, pos_s, p, past)
        new_p.append(st_p)
        new_s.append(st_s)
    y_prompt = rmsnorm(hp, final_norm)
    y_sample = rmsnorm(hs, final_norm)

    def stk(states, i):
        return jnp.stack([st[i] for st in states], axis=0)

    return (y_prompt, y_sample,
            stk(new_p, 0), stk(new_p, 1), stk(new_p, 2), stk(new_p, 3), stk(new_p, 4), stk(new_p, 5), stk(new_p, 6),
            stk(new_s, 0), stk(new_s, 1), stk(new_s, 2), stk(new_s, 3), stk(new_s, 4), stk(new_s, 5), stk(new_s, 6))
```

```python
import functools
import math

import jax
import jax.numpy as jnp
from jax import lax
from jax.experimental import pallas as pl
from jax.experimental.pallas import tpu as pltpu

F32 = jnp.float32
BF16 = jnp.bfloat16

EPS = 1e-6
CHUNK = 64
FFN_RES = 0.5
MLA_HEADS, MLA_NOPE, MLA_ROPE, MLA_V = 8, 128, 64, 128
MLA_SCALE = (MLA_NOPE + MLA_ROPE) ** -0.5
ROPE_BASE = 10000.0
SSM_HEADS, SSM_HEAD_DIM, SSM_GROUPS, SSM_STATE, SSM_CONV_W = 16, 64, 2, 128, 4
SSM_D_INNER = SSM_HEADS * SSM_HEAD_DIM
FOX_HEADS, FOX_HEAD_DIM = 8, 128
FOX_SCALE = FOX_HEAD_DIM ** -0.5
N_BRANCH = 3

LANES = 128
SUBLANES = 8
MXU_DIM = 256
VMEM_LIMIT = 56 * 1024 * 1024

MLA_QK_PAD = MXU_DIM
NEG_BIG = -1e30
HI = lax.Precision.HIGHEST


def _tile(n, pref, align):
    t = (min(pref, n) // align) * align
    while t >= align:
        if n % t == 0:
            return t
        t -= align
    return n


def _params(*sem):
    return pltpu.CompilerParams(dimension_semantics=sem, vmem_limit_bytes=VMEM_LIMIT)


def _rms(x, g):
    return x * lax.rsqrt(jnp.mean(x * x, axis=-1, keepdims=True) + EPS) * g


def _softplus(x):
    return jnp.maximum(x, 0.0) + jnp.log1p(jnp.exp(-jnp.abs(x)))


def _rope_lanes(pe, cos, s1, s2):
    half = MLA_ROPE // 2
    return pe * cos + pltpu.roll(pe, LANES - half, 1) * s1 + pltpu.roll(pe, half, 1) * s2


def _mm_kernel(*refs, prologue, rope, residual, tn):
    it = iter(refs)
    x_ref = next(it)
    g_ref = next(it) if prologue == "rms" else None
    w_ref = next(it)
    res_ref = next(it) if residual else None
    tabs = (next(it), next(it), next(it)) if rope else None
    o_ref = next(it)
    xn_ref = next(it) if prologue != "none" else None

    if prologue == "none":
        lhs = x_ref[...]
    else:
        @pl.when(pl.program_id(1) == 0)
        def _():
            x = x_ref[...].astype(F32)
            if prologue == "rms":
                x = _rms(x, g_ref[...])
            xn_ref[...] = x.astype(BF16)
        lhs = xn_ref[...]
    acc = jnp.dot(lhs, w_ref[...], preferred_element_type=F32)
    if residual:
        acc = res_ref[...] + acc
    if rope:
        cos, s1, s2 = (t[...] for t in tabs)
        for c in range(tn // MLA_QK_PAD):
            a = c * MLA_QK_PAD
            o_ref[:, a:a + LANES] = acc[:, a:a + LANES].astype(o_ref.dtype)
            o_ref[:, a + LANES:a + 2 * LANES] = _rope_lanes(acc[:, a + LANES:a + 2 * LANES], cos, s1, s2).astype(o_ref.dtype)
    else:
        o_ref[...] = acc.astype(o_ref.dtype)


def _mm(x, w, *, out_dtype, tm=1024, tn=512, prologue="none", gain=None, x_col=0, residual=None, rope_tabs=None,
        name="mm"):
    m = x.shape[0]
    k, n = w.shape
    assert x.shape[1] % k == 0 and (prologue != "none" or x.dtype == BF16)
    tm = _tile(m, tm, SUBLANES)
    tn = _tile(n, tn, MLA_QK_PAD if rope_tabs is not None else LANES)
    grid = (m // tm, n // tn)
    in_specs = [pl.BlockSpec((tm, k), lambda i, j: (i, x_col))]
    args = [x]
    if prologue == "rms":
        in_specs.append(pl.BlockSpec((1, k), lambda i, j: (0, 0)))
        args.append(gain.reshape(1, k).astype(F32))
    in_specs.append(pl.BlockSpec((k, tn), lambda i, j: (0, j)))
    args.append(w)
    if residual is not None:
        in_specs.append(pl.BlockSpec((tm, tn), lambda i, j: (i, j)))
        args.append(residual)
    if rope_tabs is not None:
        for t in rope_tabs:
            in_specs.append(pl.BlockSpec((tm, LANES), lambda i, j: (i, 0)))
            args.append(t)
    scratch = [pltpu.VMEM((tm, k), BF16)] if prologue != "none" else []
    kern = functools.partial(_mm_kernel, prologue=prologue, rope=rope_tabs is not None,
                             residual=residual is not None, tn=tn)
    return pl.pallas_call(
        kern, grid=grid, in_specs=in_specs,
        out_specs=pl.BlockSpec((tm, tn), lambda i, j: (i, j)),
        out_shape=jax.ShapeDtypeStruct((m, n), out_dtype),
        scratch_shapes=scratch, compiler_params=_params("parallel", "arbitrary"), name=name,
    )(*args)


def _ffn_kernel(x_ref, g_ref, wg_ref, wu_ref, wd_ref, o_ref, xn_ref, acc_ref, *, nf):
    j = pl.program_id(1)

    @pl.when(j == 0)
    def _():
        xn_ref[...] = _rms(x_ref[...], g_ref[...]).astype(BF16)
        acc_ref[...] = jnp.zeros_like(acc_ref)

    xn = xn_ref[...]
    a = jnp.dot(xn, wg_ref[...], preferred_element_type=F32)
    b = jnp.dot(xn, wu_ref[...], preferred_element_type=F32)
    h = (a * jax.nn.sigmoid(a) * b).astype(BF16)
    acc_ref[...] += jnp.dot(h, wd_ref[...], preferred_element_type=F32)

    @pl.when(j == nf - 1)
    def _():
        o_ref[...] = x_ref[...] + FFN_RES * acc_ref[...]


def _ffn(x, gain, wg, wu, wd, *, tm=512, tf=512):
    m, d = x.shape
    f = wg.shape[1]
    tm = _tile(m, tm, SUBLANES)
    tf = _tile(f, tf, LANES)
    nf = f // tf
    return pl.pallas_call(
        functools.partial(_ffn_kernel, nf=nf), grid=(m // tm, nf),
        in_specs=[pl.BlockSpec((tm, d), lambda i, j: (i, 0)),
                  pl.BlockSpec((1, d), lambda i, j: (0, 0)),
                  pl.BlockSpec((d, tf), lambda i, j: (0, j)),
                  pl.BlockSpec((d, tf), lambda i, j: (0, j)),
                  pl.BlockSpec((tf, d), lambda i, j: (j, 0))],
        out_specs=pl.BlockSpec((tm, d), lambda i, j: (i, 0)),
        out_shape=jax.ShapeDtypeStruct((m, d), F32),
        scratch_shapes=[pltpu.VMEM((tm, d), BF16), pltpu.VMEM((tm, d), F32)],
        compiler_params=_params("parallel", "arbitrary"), name="ffn",
    )(x, gain.reshape(1, d).astype(F32), wg, wu, wd)


def _prep_kernel(uc_ref, g_ref, us_ref, cos_ref, s1_ref, s2_ref, fb_ref, ckv_ref, sm_ref, *, ff_lo, ff_hi):
    ckv_ref[...] = _rms(uc_ref[...], g_ref[...])
    us = us_ref[...]
    lane = lax.broadcasted_iota(jnp.int32, us.shape, 1)
    pe = jnp.where(lane < MLA_ROPE, us, 0.0)
    rot = _rope_lanes(pe, cos_ref[...], s1_ref[...], s2_ref[...])
    logf = -_softplus(-(us + fb_ref[...]))
    sm_ref[...] = jnp.where((lane >= ff_lo) & (lane < ff_hi), logf, rot)


def _prep(u_ckv, kv_gain, u_small, tabs, fb_lanes, *, ff_lo, ff_hi, tm=1024):
    m = u_small.shape[0]
    kv = kv_gain.shape[0]
    tm = _tile(m, tm, SUBLANES)
    row = lambda i: (i, 0)
    return pl.pallas_call(
        functools.partial(_prep_kernel, ff_lo=ff_lo, ff_hi=ff_hi), grid=(m // tm,),
        in_specs=[pl.BlockSpec((tm, kv), row),
                  pl.BlockSpec((1, kv), lambda i: (0, 0)),
                  pl.BlockSpec((tm, LANES), row), pl.BlockSpec((tm, LANES), row),
                  pl.BlockSpec((tm, LANES), row), pl.BlockSpec((tm, LANES), row),
                  pl.BlockSpec((1, LANES), lambda i: (0, 0))],
        out_specs=[pl.BlockSpec((tm, kv), row), pl.BlockSpec((tm, LANES), row)],
        out_shape=[jax.ShapeDtypeStruct((m, kv), F32), jax.ShapeDtypeStruct((m, LANES), F32)],
        compiler_params=_params("parallel"), name="prep",
    )(u_ckv, kv_gain.reshape(1, kv).astype(F32), u_small, *tabs, fb_lanes)


def _cumsum_kernel(x_ref, o_ref, carry_ref, *, tc):
    @pl.when(pl.program_id(1) == 0)
    def _():
        carry_ref[...] = jnp.zeros_like(carry_ref)

    r = lax.broadcasted_iota(jnp.int32, (tc, tc), 0)
    c = lax.broadcasted_iota(jnp.int32, (tc, tc), 1)
    upper = (r <= c).astype(F32)
    y = jnp.dot(x_ref[0], upper, preferred_element_type=F32, precision=HI) + carry_ref[:, :1]
    o_ref[0] = y
    carry_ref[...] = jnp.broadcast_to(y[:, tc - 1:tc], carry_ref.shape)


def _cumsum_last(x, *, tc=256):
    b, h, s = x.shape
    tc = _tile(s, tc, LANES)
    return pl.pallas_call(
        functools.partial(_cumsum_kernel, tc=tc), grid=(b, s // tc),
        in_specs=[pl.BlockSpec((1, h, tc), lambda i, j: (i, 0, j))],
        out_specs=pl.BlockSpec((1, h, tc), lambda i, j: (i, 0, j)),
        out_shape=jax.ShapeDtypeStruct((b, h, s), F32),
        scratch_shapes=[pltpu.VMEM((h, LANES), F32)],
        compiler_params=_params("parallel", "arbitrary"), name="cumsum",
    )(x)


def _last_visible(q_end, mode):
    if mode == "chunk":
        return (q_end // CHUNK) * CHUNK + (CHUNK - 1)
    return q_end


def _attn_kernel(*refs, heads, dq, dv, tq, tk, nk, scale, mode, q_off, n_valid, has_bias):
    it = iter(refs)
    q_ref, k_ref, v_ref = next(it), next(it), next(it)
    b_ref = next(it) if has_bias else None
    o_ref, m_ref, l_ref, acc_ref = next(it), next(it), next(it), next(it)
    qi, ki = pl.program_id(1), pl.program_id(2)

    @pl.when(ki == 0)
    def _():
        m_ref[...] = jnp.full_like(m_ref, NEG_BIG)
        l_ref[...] = jnp.zeros_like(l_ref)
        acc_ref[...] = jnp.zeros_like(acc_ref)

    q_lo = q_off + qi * tq

    @pl.when(ki * tk <= _last_visible(q_lo + (tq - 1), mode))
    def _():
        qpos = q_lo + lax.broadcasted_iota(jnp.int32, (tq, tk), 0)
        kpos = ki * tk + lax.broadcasted_iota(jnp.int32, (tq, tk), 1)
        if mode == "chunk":
            sh = CHUNK.bit_length() - 1
            vis = lax.shift_right_logical(kpos, sh) <= lax.shift_right_logical(qpos, sh)
        else:
            vis = kpos <= qpos
        vis = vis & (kpos < n_valid)
        for h in range(heads):
            q = q_ref[0, :, h * dq:(h + 1) * dq]
            k = k_ref[0, :, h * dq:(h + 1) * dq]
            s = lax.dot_general(q, k, (((1,), (1,)), ((), ())), preferred_element_type=F32) * scale
            if has_bias:
                s = s + b_ref[0, h:h + 1, :]
            s = jnp.where(vis, s, NEG_BIG)
            m_prev = m_ref[h][:, :1]
            m_new = jnp.maximum(m_prev, jnp.max(s, axis=1, keepdims=True))
            p = jnp.exp(s - m_new)
            alpha = jnp.exp(m_prev - m_new)
            l_ref[h] = jnp.broadcast_to(alpha * l_ref[h][:, :1] + jnp.sum(p, axis=1, keepdims=True), (tq, LANES))
            m_ref[h] = jnp.broadcast_to(m_new, (tq, LANES))
            pv = jnp.dot(p.astype(BF16), v_ref[0, :, h * dv:(h + 1) * dv], preferred_element_type=F32)
            acc_ref[:, h * dv:(h + 1) * dv] = alpha * acc_ref[:, h * dv:(h + 1) * dv] + pv

    @pl.when(ki == nk - 1)
    def _():
        for h in range(heads):
            o_ref[0, :, h * dv:(h + 1) * dv] = (acc_ref[:, h * dv:(h + 1) * dv] / l_ref[h][:, :1]).astype(o_ref.dtype)


def _attention(q, k, v, bias, *, heads, dq, dv, scale, mode, q_off, n_valid, tq=512, tk=512):
    b, sq, _ = q.shape
    sk = k.shape[1]
    tq = _tile(sq, tq, SUBLANES)
    tk = _tile(sk, tk, LANES)
    nq, nk = sq // tq, sk // tk

    def kmap(bi, qi, ki):
        last = _last_visible(q_off + qi * tq + (tq - 1), mode) // tk
        return (bi, jnp.minimum(ki, jnp.minimum(last, nk - 1)), 0)

    in_specs = [pl.BlockSpec((1, tq, heads * dq), lambda bi, qi, ki: (bi, qi, 0)),
                pl.BlockSpec((1, tk, heads * dq), kmap),
                pl.BlockSpec((1, tk, heads * dv), kmap)]
    args = [q, k, v]
    if bias is not None:
        def bmap(bi, qi, ki):
            bb, kk, _ = kmap(bi, qi, ki)
            return (bb, 0, kk)

        in_specs.append(pl.BlockSpec((1, heads, tk), bmap))
        args.append(bias)
    kern = functools.partial(_attn_kernel, heads=heads, dq=dq, dv=dv, tq=tq, tk=tk, nk=nk, scale=scale, mode=mode,
                             q_off=q_off, n_valid=n_valid, has_bias=bias is not None)
    return pl.pallas_call(
        kern, grid=(b, nq, nk), in_specs=in_specs,
        out_specs=pl.BlockSpec((1, tq, heads * dv), lambda bi, qi, ki: (bi, qi, 0)),
        out_shape=jax.ShapeDtypeStruct((b, sq, heads * dv), BF16),
        scratch_shapes=[pltpu.VMEM((heads, tq, LANES), F32), pltpu.VMEM((heads, tq, LANES), F32),
                        pltpu.VMEM((tq, heads * dv), F32)],
        compiler_params=_params("parallel", "parallel", "arbitrary"), name="attn_" + mode,
    )(*args)


def _ssd_kernel(z_ref, xbc_ref, dt_ref, dtt_ref, cw_ref, cb_ref, dtb_ref, dtbt_ref, al_ref, alt_ref, dx_ref, nw_ref,
                h0_ref, c0_ref, y_ref, hout_ref, state_ref, carry_ref, *, lc, nc):
    c = pl.program_id(1)
    gw = SSM_D_INNER // SSM_GROUPS
    hpg = SSM_HEADS // SSM_GROUPS
    halo = SUBLANES

    @pl.when(c == 0)
    def _():
        state_ref[...] = h0_ref[0]
        carry_ref[...] = c0_ref[0]

    x = xbc_ref[0]
    cat = jnp.concatenate([carry_ref[...], x], axis=0)
    conv = cb_ref[...]
    for kk in range(SSM_CONV_W):
        shift = SSM_CONV_W - 1 - kk
        src = pltpu.roll(cat, shift, 0) if shift else cat
        conv = conv + src[halo:, :] * cw_ref[kk:kk + 1, :]
    carry_ref[...] = x[lc - halo:, :]
    act = conv * jax.nn.sigmoid(conv)
    xs = act[:, :SSM_D_INNER]
    bm = act[:, SSM_D_INNER:SSM_D_INNER + SSM_GROUPS * SSM_STATE]
    cm = act[:, SSM_D_INNER + SSM_GROUPS * SSM_STATE:]

    dt = _softplus(dt_ref[0] + dtb_ref[...])
    dtt = _softplus(dtt_ref[0] + dtbt_ref[...])
    adt = dt * (-jnp.exp(al_ref[...]))
    adtt = dtt * (-jnp.exp(alt_ref[...]))
    r = lax.broadcasted_iota(jnp.int32, (lc, lc), 0)
    cc = lax.broadcasted_iota(jnp.int32, (lc, lc), 1)
    tril = cc <= r
    acs = jnp.dot(tril.astype(F32), adt, preferred_element_type=F32, precision=HI)
    acst = jnp.dot(adtt, (r <= cc).astype(F32), preferred_element_type=F32, precision=HI)
    hh = lax.broadcasted_iota(jnp.int32, (SSM_HEADS, SSM_D_INNER), 0)
    ll = lax.broadcasted_iota(jnp.int32, (SSM_HEADS, SSM_D_INNER), 1)
    expand = ((ll >= hh * SSM_HEAD_DIM) & (ll < (hh + 1) * SSM_HEAD_DIM)).astype(F32)
    dt_x = jnp.dot(dt, expand, preferred_element_type=F32, precision=HI)
    acs_x = jnp.dot(acs, expand, preferred_element_type=F32, precision=HI)
    tot_x = acs_x[lc - 1:lc, :]
    xdt = xs * dt_x
    xdt_b = xdt.astype(BF16)
    w_end = (xdt * jnp.exp(tot_x - acs_x)).astype(BF16)
    state = state_ref[...]
    state_b = state.astype(BF16)

    y_parts, new_parts = [], []
    for g in range(SSM_GROUPS):
        bg = bm[:, g * SSM_STATE:(g + 1) * SSM_STATE].astype(BF16)
        cg = cm[:, g * SSM_STATE:(g + 1) * SSM_STATE].astype(BF16)
        cb = lax.dot_general(cg, bg, (((1,), (1,)), ((), ())), preferred_element_type=F32)
        for hl in range(hpg):
            h = g * hpg + hl
            seg = acs[:, h:h + 1] - acst[h:h + 1, :]
            mh = (cb * jnp.exp(jnp.where(tril, seg, NEG_BIG))).astype(BF16)
            y_parts.append(jnp.dot(mh, xdt_b[:, h * SSM_HEAD_DIM:(h + 1) * SSM_HEAD_DIM], preferred_element_type=F32))
        new_parts.append(lax.dot_general(bg, w_end[:, g * gw:(g + 1) * gw], (((0,), (0,)), ((), ())),
                                         preferred_element_type=F32))
    y_off = jnp.concatenate(
        [jnp.dot(cm[:, g * SSM_STATE:(g + 1) * SSM_STATE].astype(BF16), state_b[:, g * gw:(g + 1) * gw],
                 preferred_element_type=F32) for g in range(SSM_GROUPS)], axis=1) * jnp.exp(acs_x)
    y = jnp.concatenate(y_parts, axis=1) + y_off + dx_ref[...] * xs
    state_ref[...] = jnp.exp(tot_x) * state + jnp.concatenate(new_parts, axis=1)

    zz = z_ref[0]
    y = y * (zz * jax.nn.sigmoid(zz))
    for g in range(SSM_GROUPS):
        y_ref[0, :, g * gw:(g + 1) * gw] = _rms(y[:, g * gw:(g + 1) * gw], nw_ref[:, g * gw:(g + 1) * gw]).astype(y_ref.dtype)

    @pl.when(c == nc - 1)
    def _():
        hout_ref[0] = state_ref[...]


def _ssd(z, xbc, dt, dtt, p, h0, c0, *, lc=256):
    b, s, cd = xbc.shape
    lc = _tile(s, lc, LANES) if s % LANES == 0 else s
    nc = s // lc
    hh = SSM_HEADS
    full2 = lambda shape: pl.BlockSpec(shape, lambda i, j: (0, 0))
    return pl.pallas_call(
        functools.partial(_ssd_kernel, lc=lc, nc=nc), grid=(b, nc),
        in_specs=[pl.BlockSpec((1, lc, SSM_D_INNER), lambda i, j: (i, j, 0)),
                  pl.BlockSpec((1, lc, cd), lambda i, j: (i, j, 0)),
                  pl.BlockSpec((1, lc, hh), lambda i, j: (i, j, 0)),
                  pl.BlockSpec((1, hh, lc), lambda i, j: (i, 0, j)),
                  full2((SSM_CONV_W, cd)), full2((1, cd)),
                  full2((1, hh)), full2((hh, 1)), full2((1, hh)), full2((hh, 1)),
                  full2((1, SSM_D_INNER)), full2((1, SSM_D_INNER)),
                  pl.BlockSpec((1, SSM_STATE, SSM_D_INNER), lambda i, j: (i, 0, 0)),
                  pl.BlockSpec((1, SUBLANES, cd), lambda i, j: (i, 0, 0))],
        out_specs=[pl.BlockSpec((1, lc, SSM_D_INNER), lambda i, j: (i, j, 0)),
                   pl.BlockSpec((1, SSM_STATE, SSM_D_INNER), lambda i, j: (i, 0, 0))],
        out_shape=[jax.ShapeDtypeStruct((b, s, SSM_D_INNER), BF16),
                   jax.ShapeDtypeStruct((b, SSM_STATE, SSM_D_INNER), F32)],
        scratch_shapes=[pltpu.VMEM((SSM_STATE, SSM_D_INNER), F32), pltpu.VMEM((SUBLANES, cd), F32)],
        compiler_params=_params("parallel", "arbitrary"), name="ssd",
    )(z, xbc, dt, dtt, p["conv_w"], p["conv_b"], p["dt_b"], p["dt_bt"], p["a_log"], p["a_logt"], p["d_x"], p["norm_w"],
      h0, c0)


def _merge_kernel(o0, o1, o2, w0, w1, w2, g0, g1, g2, out_ref):
    acc = None
    for o_ref, w_ref, g_ref in ((o0, w0, g0), (o1, w1, g1), (o2, w2, g2)):
        t = jax.nn.sigmoid(g_ref[...]) * jnp.dot(o_ref[...], w_ref[...], preferred_element_type=F32)
        acc = t if acc is None else acc + t
    out_ref[...] = acc.astype(out_ref.dtype)


def _merge(o_list, w_list, gate, *, tm=1024, tn=512):
    m = gate.shape[0]
    d = w_list[0].shape[1]
    tm = _tile(m, tm, SUBLANES)
    tn = _tile(d, tn, LANES)
    nb = d // tn
    in_specs = [pl.BlockSpec((tm, o.shape[1]), lambda i, j: (i, 0)) for o in o_list]
    in_specs += [pl.BlockSpec((w.shape[0], tn), lambda i, j: (0, j)) for w in w_list]
    in_specs += [pl.BlockSpec((tm, tn), functools.partial(lambda i, j, br: (i, br * nb + j), br=br))
                 for br in range(N_BRANCH)]
    return pl.pallas_call(
        _merge_kernel, grid=(m // tm, nb), in_specs=in_specs,
        out_specs=pl.BlockSpec((tm, tn), lambda i, j: (i, j)),
        out_shape=jax.ShapeDtypeStruct((m, d), BF16),
        compiler_params=_params("parallel", "arbitrary"), name="merge",
    )(*o_list, *w_list, gate, gate, gate)


def _norm_kernel(x_ref, g_ref, o_ref):
    o_ref[...] = _rms(x_ref[...], g_ref[...])


def _final_norm(x, gain, *, tm=1024):
    m, d = x.shape
    tm = _tile(m, tm, SUBLANES)
    return pl.pallas_call(
        _norm_kernel, grid=(m // tm,),
        in_specs=[pl.BlockSpec((tm, d), lambda i: (i, 0)), pl.BlockSpec((1, d), lambda i: (0, 0))],
        out_specs=pl.BlockSpec((tm, d), lambda i: (i, 0)),
        out_shape=jax.ShapeDtypeStruct((m, d), F32),
        compiler_params=_params("parallel"), name="final_norm",
    )(x, gain.reshape(1, d).astype(F32))


def _pad_cols(w, n):
    return jnp.pad(w, ((0, 0), (0, n - w.shape[1])))


def _layer_weights(l, a):
    d_model = a["w_in"].shape[1]
    q_lora, kv_lora = a["mla_q_norm"].shape[1], a["mla_kv_norm"].shape[1]
    conv_dim = a["ssm_conv_w"].shape[2]
    sizes = (q_lora, kv_lora, MLA_ROPE, SSM_D_INNER, conv_dim, SSM_HEADS,
             FOX_HEADS * FOX_HEAD_DIM, FOX_HEADS * FOX_HEAD_DIM, FOX_HEADS * FOX_HEAD_DIM, FOX_HEADS,
             N_BRANCH * d_model)
    w_in = a["w_in"][l]
    assert w_in.shape[1] == sum(sizes)
    cols, start = [], 0
    for n in sizes:
        cols.append(w_in[:, start:start + n])
        start += n
    w_q, w_ckv, w_kpe, w_z, w_xbc, w_dt, w_fq, w_fk, w_fv, w_ff, w_gate = cols
    bf = lambda t: t.astype(BF16)
    w = {
        "in_q": bf(w_q), "in_ckv": bf(w_ckv),
        "in_small": bf(_pad_cols(jnp.concatenate([w_kpe, w_dt, w_ff], axis=1), LANES)),
        "in_z": bf(w_z), "in_xbc": bf(w_xbc), "in_fq": bf(w_fq), "in_fk": bf(w_fk), "in_fv": bf(w_fv),
        "in_gate": bf(w_gate),
    }
    wq = a["mla_w_uq"][l].reshape(q_lora, MLA_HEADS, MLA_NOPE + MLA_ROPE)
    wq = jnp.pad(wq, ((0, 0), (0, 0), (0, MLA_QK_PAD - MLA_NOPE - MLA_ROPE)))
    w["uq"] = bf(wq.reshape(q_lora, MLA_HEADS * MLA_QK_PAD))
    wkv = a["mla_w_ukv"][l].reshape(kv_lora, MLA_HEADS, MLA_NOPE + MLA_V)
    w["uk"] = bf(wkv[:, :, :MLA_NOPE].reshape(kv_lora, MLA_HEADS * MLA_NOPE))
    w["uv"] = bf(wkv[:, :, MLA_NOPE:].reshape(kv_lora, MLA_HEADS * MLA_V))
    for nm in ("w_br_mla", "w_br_ssd", "w_br_fox", "w_out"):
        w[nm] = bf(a[nm][l])
    for pre in ("ffn1", "ffn2"):
        f = a[pre + "_w_gate"].shape[2]
        fp = -(-f // (4 * LANES)) * (4 * LANES)
        w[pre + "_g"] = bf(_pad_cols(a[pre + "_w_gate"][l], fp))
        w[pre + "_u"] = bf(_pad_cols(a[pre + "_w_up"][l], fp))
        w[pre + "_d"] = bf(jnp.pad(a[pre + "_w_down"][l], ((0, fp - f), (0, 0))))
        w[pre + "_norm"] = a[pre + "_norm"][l]
    w["mix_norm"] = a["mix_norm"][l]
    w["q_norm"], w["kv_norm"] = a["mla_q_norm"][l], a["mla_kv_norm"][l]
    ff_lo = MLA_ROPE + SSM_HEADS
    w["fb_lanes"] = jnp.pad(a["fox_b_f"][l].astype(F32), (ff_lo, LANES - ff_lo - FOX_HEADS)).reshape(1, LANES)
    w["ssd"] = {
        "conv_w": a["ssm_conv_w"][l].astype(F32), "conv_b": a["ssm_conv_b"][l].astype(F32).reshape(1, conv_dim),
        "dt_b": a["ssm_dt_bias"][l].astype(F32).reshape(1, SSM_HEADS),
        "dt_bt": a["ssm_dt_bias"][l].astype(F32).reshape(SSM_HEADS, 1),
        "a_log": a["ssm_a_log"][l].astype(F32).reshape(1, SSM_HEADS),
        "a_logt": a["ssm_a_log"][l].astype(F32).reshape(SSM_HEADS, 1),
        "d_x": jnp.repeat(a["ssm_d"][l].astype(F32), SSM_HEAD_DIM).reshape(1, SSM_D_INNER),
        "norm_w": a["ssm_norm"][l].astype(F32).reshape(1, SSM_D_INNER),
    }
    return w


def _rope_tables(pos):
    half = MLA_ROPE // 2
    inv_freq = ROPE_BASE ** (-jnp.arange(half, dtype=F32) / half)
    ang = pos.astype(F32)[:, None] * inv_freq[None, :]
    cos, sin = jnp.cos(ang), jnp.sin(ang)
    z = jnp.zeros_like(cos)
    pad = jnp.zeros((pos.shape[0], LANES - MLA_ROPE), F32)
    return (jnp.concatenate([cos, cos, pad], axis=1),
            jnp.concatenate([-sin, z, pad], axis=1),
            jnp.concatenate([z, sin, pad], axis=1))


def _pad_keys(t, sk_pad):
    return jnp.pad(t, ((0, 0), (0, sk_pad - t.shape[1])) + ((0, 0),) * (t.ndim - 2))


def _layer(x, bsz, s, w, tabs, past):
    m = bsz * s
    kv_lora = w["kv_norm"].shape[0]
    conv_dim = w["ssd"]["conv_w"].shape[1]
    x = _ffn(x, w["ffn1_norm"], w["ffn1_g"], w["ffn1_u"], w["ffn1_d"])

    proj = functools.partial(_mm, x, prologue="rms", gain=w["mix_norm"])
    u_q = proj(w["in_q"], out_dtype=F32, name="in_q")
    u_ckv = proj(w["in_ckv"], out_dtype=F32, name="in_ckv")
    u_small = proj(w["in_small"], out_dtype=F32, name="in_small")
    u_z = proj(w["in_z"], out_dtype=F32, name="in_z")
    u_xbc = proj(w["in_xbc"], out_dtype=F32, name="in_xbc")
    fq = proj(w["in_fq"], out_dtype=BF16, name="in_fq")
    fk = proj(w["in_fk"], out_dtype=F32, name="in_fk")
    fv = proj(w["in_fv"], out_dtype=F32, name="in_fv")
    gate = proj(w["in_gate"], out_dtype=F32, name="in_gate")

    ff_lo = MLA_ROPE + SSM_HEADS
    ckv_new, small2 = _prep(u_ckv, w["kv_norm"], u_small, tabs, w["fb_lanes"], ff_lo=ff_lo, ff_hi=ff_lo + FOX_HEADS)
    kpe_new = small2[:, :MLA_ROPE]
    logf_new = small2[:, ff_lo:ff_lo + FOX_HEADS]
    u_dt = u_small[:, MLA_ROPE:ff_lo]

    past_len = 0 if past is None else past["mla_ckv"].shape[1]
    sk = past_len + s
    sk_pad = -(-sk // LANES) * LANES

    def with_past(new, key):
        new = new.reshape(bsz, s, -1)
        if past is None:
            return new
        return jnp.concatenate([past[key].reshape(bsz, past_len, -1).astype(new.dtype), new], axis=1)

    q_full = _mm(u_q, w["uq"], out_dtype=BF16, prologue="rms", gain=w["q_norm"], rope_tabs=tabs, name="mla_q")
    ckv_all = with_past(ckv_new, "mla_ckv").reshape(bsz * sk, kv_lora)
    k_nope = _mm(ckv_all, w["uk"], out_dtype=BF16, prologue="cast", name="mla_uk").reshape(bsz, sk, MLA_HEADS, MLA_NOPE)
    v_mla = _mm(ckv_all, w["uv"], out_dtype=BF16, prologue="cast", name="mla_uv").reshape(bsz, sk, MLA_HEADS * MLA_V)
    kpe_all = with_past(kpe_new, "mla_kpe").astype(BF16)
    k_full = jnp.concatenate(
        [k_nope, jnp.broadcast_to(kpe_all[:, :, None, :], (bsz, sk, MLA_HEADS, MLA_ROPE)),
         jnp.zeros((bsz, sk, MLA_HEADS, MLA_QK_PAD - MLA_NOPE - MLA_ROPE), BF16)], axis=-1,
    ).reshape(bsz, sk, MLA_HEADS * MLA_QK_PAD)
    o_mla = _attention(q_full.reshape(bsz, s, -1), _pad_keys(k_full, sk_pad), _pad_keys(v_mla, sk_pad), None,
                       heads=MLA_HEADS, dq=MLA_QK_PAD, dv=MLA_V, scale=MLA_SCALE, mode="chunk",
                       q_off=past_len, n_valid=sk)

    if past is None:
        conv_state = jnp.zeros((bsz, SSM_CONV_W - 1, conv_dim), F32)
        h0 = jnp.zeros((bsz, SSM_STATE, SSM_D_INNER), F32)
    else:
        conv_state = past["conv"].astype(F32)
        h0 = jnp.transpose(past["ssm"].astype(F32), (0, 3, 1, 2)).reshape(bsz, SSM_STATE, SSM_D_INNER)
    c0 = jnp.pad(conv_state, ((0, 0), (SUBLANES - (SSM_CONV_W - 1), 0), (0, 0)))
    xbc3 = u_xbc.reshape(bsz, s, conv_dim)
    dt3 = u_dt.reshape(bsz, s, SSM_HEADS)
    o_ssd, h_new = _ssd(u_z.reshape(bsz, s, SSM_D_INNER), xbc3, dt3, jnp.swapaxes(dt3, 1, 2), w["ssd"], h0, c0)
    ssm_new = jnp.transpose(h_new.reshape(bsz, SSM_STATE, SSM_HEADS, SSM_HEAD_DIM), (0, 2, 3, 1))
    keep = SSM_CONV_W - 1
    conv_new = xbc3[:, s - keep:] if s >= keep else jnp.concatenate([conv_state, xbc3], axis=1)[:, -keep:]

    logf_all = with_past(logf_new, "fox_logf")
    neg_cum = -_cumsum_last(jnp.swapaxes(_pad_keys(logf_all, sk_pad), 1, 2))
    hw = FOX_HEADS * FOX_HEAD_DIM
    k_all = with_past(fk, "fox_k").reshape(bsz, sk, hw).astype(BF16)
    v_all = with_past(fv, "fox_v").reshape(bsz, sk, hw).astype(BF16)
    o_fox = _attention(fq.reshape(bsz, s, hw), _pad_keys(k_all, sk_pad), _pad_keys(v_all, sk_pad), neg_cum,
                       heads=FOX_HEADS, dq=FOX_HEAD_DIM, dv=FOX_HEAD_DIM, scale=FOX_SCALE, mode="causal",
                       q_off=past_len, n_valid=sk)

    merged = _merge([o_mla.reshape(m, -1), o_ssd.reshape(m, -1), o_fox.reshape(m, -1)],
                    [w["w_br_mla"], w["w_br_ssd"], w["w_br_fox"]], gate)
    x = _mm(merged, w["w_out"], out_dtype=F32, residual=x, name="out_proj")
    x = _ffn(x, w["ffn2_norm"], w["ffn2_g"], w["ffn2_u"], w["ffn2_d"])
    state = (ckv_new.reshape(bsz, s, kv_lora), kpe_new.reshape(bsz, s, MLA_ROPE),
             fk.reshape(bsz, s, FOX_HEADS, FOX_HEAD_DIM), fv.reshape(bsz, s, FOX_HEADS, FOX_HEAD_DIM),
             logf_new.reshape(bsz, s, FOX_HEADS), ssm_new, conv_new)
    return x, state


def kernel(x_prompt, x_sample, cache_mla_ckv, cache_mla_kpe, cache_fox_k, cache_fox_v, cache_fox_logf, state_ssm,
           state_conv, ffn1_norm, ffn1_w_gate, ffn1_w_up, ffn1_w_down, mix_norm, w_in, mla_q_norm, mla_w_uq,
           mla_kv_norm, mla_w_ukv, ssm_conv_w, ssm_conv_b, ssm_dt_bias, ssm_a_log, ssm_d, ssm_norm, fox_b_f,
           w_br_mla, w_br_ssd, w_br_fox, w_out, ffn2_norm, ffn2_w_gate, ffn2_w_up, ffn2_w_down, final_norm):
    a = dict(ffn1_norm=ffn1_norm, ffn1_w_gate=ffn1_w_gate, ffn1_w_up=ffn1_w_up, ffn1_w_down=ffn1_w_down,
             mix_norm=mix_norm, w_in=w_in, mla_q_norm=mla_q_norm, mla_w_uq=mla_w_uq, mla_kv_norm=mla_kv_norm,
             mla_w_ukv=mla_w_ukv, ssm_conv_w=ssm_conv_w, ssm_conv_b=ssm_conv_b, ssm_dt_bias=ssm_dt_bias,
             ssm_a_log=ssm_a_log, ssm_d=ssm_d, ssm_norm=ssm_norm, fox_b_f=fox_b_f, w_br_mla=w_br_mla,
             w_br_ssd=w_br_ssd, w_br_fox=w_br_fox, w_out=w_out, ffn2_norm=ffn2_norm, ffn2_w_gate=ffn2_w_gate,
             ffn2_w_up=ffn2_w_up, ffn2_w_down=ffn2_w_down)
    depth = w_in.shape[0]
    bp, sp, d_model = x_prompt.shape
    bs, ss, _ = x_sample.shape
    past_len = cache_mla_ckv.shape[2]
    tabs_p = tuple(jnp.tile(t, (bp, 1)) for t in _rope_tables(jnp.arange(sp, dtype=jnp.int32)))
    tabs_s = tuple(jnp.tile(t, (bs, 1)) for t in _rope_tables(past_len + jnp.arange(ss, dtype=jnp.int32)))
    hp = x_prompt.reshape(bp * sp, d_model).astype(F32)
    hs = x_sample.reshape(bs * ss, d_model).astype(F32)
    new_p, new_s = [], []
    for l in range(depth):
        w = _layer_weights(l, a)
        past = {"mla_ckv": cache_mla_ckv[l], "mla_kpe": cache_mla_kpe[l], "fox_k": cache_fox_k[l],
                "fox_v": cache_fox_v[l], "fox_logf": cache_fox_logf[l], "ssm": state_ssm[l], "conv": state_conv[l]}
        hp, st_p = _layer(hp, bp, sp, w, tabs_p, None)
        hs, st_s = _layer(hs, bs, ss, w, tabs_s, past)
        new_p.append(st_p)
        new_s.append(st_s)
    y_prompt = _final_norm(hp, final_norm).reshape(bp, sp, d_model)
    y_sample = _final_norm(hs, final_norm).reshape(bs, ss, d_model)
    stk = lambda states, i: jnp.stack([st[i] for st in states], axis=0)
    return (y_prompt, y_sample) + tuple(stk(new_p, i) for i in range(7)) + tuple(stk(new_s, i) for i in range(7))
```

```python
import functools
import math

import jax
import jax.numpy as jnp
from jax import lax
from jax.experimental import pallas as pl
from jax.experimental.pallas import tpu as pltpu

F32 = jnp.float32
BF16 = jnp.bfloat16

EPS = 1e-6
CHUNK = 64
FFN_RES = 0.5
MLA_HEADS, MLA_NOPE, MLA_ROPE, MLA_V = 8, 128, 64, 128
MLA_SCALE = (MLA_NOPE + MLA_ROPE) ** -0.5
ROPE_BASE = 10000.0
SSM_HEADS, SSM_HEAD_DIM, SSM_GROUPS, SSM_STATE, SSM_CONV_W = 16, 64, 2, 128, 4
SSM_D_INNER = SSM_HEADS * SSM_HEAD_DIM
FOX_HEADS, FOX_HEAD_DIM = 8, 128
FOX_SCALE = FOX_HEAD_DIM ** -0.5
N_BRANCH = 3

LANES = 128
SUBLANES = 8
MXU_DIM = 256
VMEM_LIMIT = 56 * 1024 * 1024

MLA_QK_PAD = MXU_DIM
NEG_BIG = -1e30
LOG2E = math.log2(math.e)
HI = lax.Precision.HIGHEST


def _tile(n, pref, align):
    t = (min(pref, n) // align) * align
    while t >= align:
        if n % t == 0:
            return t
        t -= align
    return n


def _params(*sem):
    return pltpu.CompilerParams(dimension_semantics=sem, vmem_limit_bytes=VMEM_LIMIT)


def _rms(x, g):
    return x * lax.rsqrt(jnp.mean(x * x, axis=-1, keepdims=True) + EPS) * g


def _softplus(x):
    return jnp.maximum(x, 0.0) + jnp.log1p(jnp.exp(-jnp.abs(x)))


def _rope_lanes(pe, cos, s1, s2):
    half = MLA_ROPE // 2
    return pe * cos + pltpu.roll(pe, LANES - half, 1) * s1 + pltpu.roll(pe, half, 1) * s2


def _mm_kernel(*refs, prologue, rope, residual, tn):
    it = iter(refs)
    x_ref = next(it)
    g_ref = next(it) if prologue == "rms" else None
    w_ref = next(it)
    res_ref = next(it) if residual else None
    tabs = (next(it), next(it), next(it)) if rope else None
    o_ref = next(it)
    xn_ref = next(it) if prologue != "none" else None

    if prologue == "none":
        lhs = x_ref[...]
    else:
        @pl.when(pl.program_id(1) == 0)
        def _():
            x = x_ref[...].astype(F32)
            if prologue == "rms":
                x = _rms(x, g_ref[...])
            xn_ref[...] = x.astype(BF16)
        lhs = xn_ref[...]
    acc = jnp.dot(lhs, w_ref[...], preferred_element_type=F32)
    if residual:
        acc = res_ref[...] + acc
    if rope:
        cos, s1, s2 = (t[...] for t in tabs)
        for c in range(tn // MLA_QK_PAD):
            a = c * MLA_QK_PAD
            o_ref[:, a:a + LANES] = acc[:, a:a + LANES].astype(o_ref.dtype)
            o_ref[:, a + LANES:a + 2 * LANES] = _rope_lanes(acc[:, a + LANES:a + 2 * LANES], cos, s1, s2).astype(o_ref.dtype)
    else:
        o_ref[...] = acc.astype(o_ref.dtype)


def _mm(x, w, *, out_dtype, tm=1024, tn=512, prologue="none", gain=None, x_col=0, residual=None, rope_tabs=None,
        name="mm"):
    m = x.shape[0]
    k, n = w.shape
    assert x.shape[1] % k == 0 and (prologue != "none" or x.dtype == BF16)
    tm = _tile(m, tm, SUBLANES)
    tn = _tile(n, tn, MLA_QK_PAD if rope_tabs is not None else LANES)
    grid = (m // tm, n // tn)
    in_specs = [pl.BlockSpec((tm, k), lambda i, j: (i, x_col))]
    args = [x]
    if prologue == "rms":
        in_specs.append(pl.BlockSpec((1, k), lambda i, j: (0, 0)))
        args.append(gain.reshape(1, k).astype(F32))
    in_specs.append(pl.BlockSpec((k, tn), lambda i, j: (0, j)))
    args.append(w)
    if residual is not None:
        in_specs.append(pl.BlockSpec((tm, tn), lambda i, j: (i, j)))
        args.append(residual)
    if rope_tabs is not None:
        for t in rope_tabs:
            in_specs.append(pl.BlockSpec((tm, LANES), lambda i, j: (i, 0)))
            args.append(t)
    scratch = [pltpu.VMEM((tm, k), BF16)] if prologue != "none" else []
    kern = functools.partial(_mm_kernel, prologue=prologue, rope=rope_tabs is not None,
                             residual=residual is not None, tn=tn)
    return pl.pallas_call(
        kern, grid=grid, in_specs=in_specs,
        out_specs=pl.BlockSpec((tm, tn), lambda i, j: (i, j)),
        out_shape=jax.ShapeDtypeStruct((m, n), out_dtype),
        scratch_shapes=scratch, compiler_params=_params("parallel", "arbitrary"), name=name,
    )(*args)


def _ffn_kernel(x_ref, g_ref, wg_ref, wu_ref, wd_ref, o_ref, xn_ref, acc_ref, *, nf):
    j = pl.program_id(1)

    @pl.when(j == 0)
    def _():
        xn_ref[...] = _rms(x_ref[...], g_ref[...]).astype(BF16)
        acc_ref[...] = jnp.zeros_like(acc_ref)

    xn = xn_ref[...]
    a = jnp.dot(xn, wg_ref[...], preferred_element_type=F32)
    b = jnp.dot(xn, wu_ref[...], preferred_element_type=F32)
    h = (a * jax.nn.sigmoid(a) * b).astype(BF16)
    acc_ref[...] += jnp.dot(h, wd_ref[...], preferred_element_type=F32)

    @pl.when(j == nf - 1)
    def _():
        o_ref[...] = x_ref[...] + FFN_RES * acc_ref[...]


def _ffn(x, gain, wg, wu, wd, *, tm=512, tf=512):
    m, d = x.shape
    f = wg.shape[1]
    tm = _tile(m, tm, SUBLANES)
    tf = _tile(f, tf, LANES)
    nf = f // tf
    return pl.pallas_call(
        functools.partial(_ffn_kernel, nf=nf), grid=(m // tm, nf),
        in_specs=[pl.BlockSpec((tm, d), lambda i, j: (i, 0)),
                  pl.BlockSpec((1, d), lambda i, j: (0, 0)),
                  pl.BlockSpec((d, tf), lambda i, j: (0, j)),
                  pl.BlockSpec((d, tf), lambda i, j: (0, j)),
                  pl.BlockSpec((tf, d), lambda i, j: (j, 0))],
        out_specs=pl.BlockSpec((tm, d), lambda i, j: (i, 0)),
        out_shape=jax.ShapeDtypeStruct((m, d), F32),
        scratch_shapes=[pltpu.VMEM((tm, d), BF16), pltpu.VMEM((tm, d), F32)],
        compiler_params=_params("parallel", "arbitrary"), name="ffn",
    )(x, gain.reshape(1, d).astype(F32), wg, wu, wd)


def _prep_kernel(uc_ref, g_ref, us_ref, cos_ref, s1_ref, s2_ref, fb_ref, ckv_ref, sm_ref, *, ff_lo, ff_hi):
    ckv_ref[...] = _rms(uc_ref[...], g_ref[...])
    us = us_ref[...]
    lane = lax.broadcasted_iota(jnp.int32, us.shape, 1)
    pe = jnp.where(lane < MLA_ROPE, us, 0.0)
    rot = _rope_lanes(pe, cos_ref[...], s1_ref[...], s2_ref[...])
    logf = -_softplus(-(us + fb_ref[...]))
    sm_ref[...] = jnp.where((lane >= ff_lo) & (lane < ff_hi), logf, rot)


def _prep(u_ckv, kv_gain, u_small, tabs, fb_lanes, *, ff_lo, ff_hi, tm=1024):
    m = u_small.shape[0]
    kv = kv_gain.shape[0]
    tm = _tile(m, tm, SUBLANES)
    row = lambda i: (i, 0)
    return pl.pallas_call(
        functools.partial(_prep_kernel, ff_lo=ff_lo, ff_hi=ff_hi), grid=(m // tm,),
        in_specs=[pl.BlockSpec((tm, kv), row),
                  pl.BlockSpec((1, kv), lambda i: (0, 0)),
                  pl.BlockSpec((tm, LANES), row), pl.BlockSpec((tm, LANES), row),
                  pl.BlockSpec((tm, LANES), row), pl.BlockSpec((tm, LANES), row),
                  pl.BlockSpec((1, LANES), lambda i: (0, 0))],
        out_specs=[pl.BlockSpec((tm, kv), row), pl.BlockSpec((tm, LANES), row)],
        out_shape=[jax.ShapeDtypeStruct((m, kv), F32), jax.ShapeDtypeStruct((m, LANES), F32)],
        compiler_params=_params("parallel"), name="prep",
    )(u_ckv, kv_gain.reshape(1, kv).astype(F32), u_small, *tabs, fb_lanes)


def _cumsum_kernel(x_ref, o_ref, carry_ref, *, tc, mult):
    @pl.when(pl.program_id(1) == 0)
    def _():
        carry_ref[...] = jnp.zeros_like(carry_ref)

    r = lax.broadcasted_iota(jnp.int32, (tc, tc), 0)
    c = lax.broadcasted_iota(jnp.int32, (tc, tc), 1)
    upper = (r <= c).astype(F32)
    y = jnp.dot(x_ref[0], upper, preferred_element_type=F32, precision=HI) + carry_ref[:, :1]
    o_ref[0] = y * mult
    carry_ref[...] = jnp.broadcast_to(y[:, tc - 1:tc], carry_ref.shape)


def _cumsum_last(x, mult, *, tc=256):
    b, h, s = x.shape
    tc = _tile(s, tc, LANES)
    return pl.pallas_call(
        functools.partial(_cumsum_kernel, tc=tc, mult=mult), grid=(b, s // tc),
        in_specs=[pl.BlockSpec((1, h, tc), lambda i, j: (i, 0, j))],
        out_specs=pl.BlockSpec((1, h, tc), lambda i, j: (i, 0, j)),
        out_shape=jax.ShapeDtypeStruct((b, h, s), F32),
        scratch_shapes=[pltpu.VMEM((h, LANES), F32)],
        compiler_params=_params("parallel", "arbitrary"), name="cumsum",
    )(x)


def _last_visible(q_end, mode):
    if mode == "chunk":
        return (q_end // CHUNK) * CHUNK + (CHUNK - 1)
    return q_end


def _attn_kernel(*refs, heads, dq, dv, tq, tk, nk, scale, mode, q_off, n_valid, has_bias):
    it = iter(refs)
    qi_ref, ki_ref = next(it), next(it)
    q_ref, k_ref, v_ref = next(it), next(it), next(it)
    b_ref = next(it) if has_bias else None
    o_ref, m_ref, acc_ref = next(it), next(it), next(it)
    t = pl.program_id(1)
    qi, ki = qi_ref[t], ki_ref[t]
    nch = tk // LANES
    c = scale * LOG2E
    aw = dv + LANES

    @pl.when(ki == 0)
    def _():
        m_ref[...] = jnp.full_like(m_ref, NEG_BIG)
        acc_ref[...] = jnp.zeros_like(acc_ref)

    q_lo = q_off + qi * tq
    k_lo = ki * tk
    last_tile = jnp.minimum(_last_visible(q_lo + (tq - 1), mode) // tk, nk - 1)
    first_maskable = _last_visible(q_lo, mode) + 1
    ones_col = (lax.broadcasted_iota(jnp.int32, (tk, LANES), 1) == 0).astype(BF16)

    def body(masked):
        if masked:
            qpos = q_lo + lax.broadcasted_iota(jnp.int32, (tq, tk), 0)
            kpos = k_lo + lax.broadcasted_iota(jnp.int32, (tq, tk), 1)
            if mode == "chunk":
                sh = CHUNK.bit_length() - 1
                vis = lax.shift_right_logical(kpos, sh) <= lax.shift_right_logical(qpos, sh)
            else:
                vis = kpos <= qpos
            vis = vis & (kpos < n_valid)
        for h in range(heads):
            q = q_ref[0, :, h * dq:(h + 1) * dq]
            k = k_ref[0, :, h * dq:(h + 1) * dq]
            s = lax.dot_general(q, k, (((1,), (1,)), ((), ())), preferred_element_type=F32)
            if has_bias:
                s = s + b_ref[0, h:h + 1, :]
            if masked:
                s = jnp.where(vis, s, NEG_BIG)
            m_prev = m_ref[h]
            mc = s[:, :LANES]
            for j in range(1, nch):
                mc = jnp.maximum(mc, s[:, j * LANES:(j + 1) * LANES])
            m_new = jnp.maximum(m_prev, jnp.max(mc, axis=1, keepdims=True))
            m_ref[h] = m_new
            alpha = jnp.exp2((m_prev - m_new) * c)
            p = jnp.concatenate([jnp.exp2((s[:, j * LANES:(j + 1) * LANES] - m_new) * c).astype(BF16)
                                 for j in range(nch)], axis=1)
            vx = jnp.concatenate([v_ref[0, :, h * dv:(h + 1) * dv], ones_col], axis=1)
            pv = jnp.dot(p, vx, preferred_element_type=F32)
            for a0 in range(h * aw, (h + 1) * aw, LANES):
                acc_ref[:, a0:a0 + LANES] = alpha * acc_ref[:, a0:a0 + LANES] + pv[:, a0 - h * aw:a0 - h * aw + LANES]

    need_mask = (k_lo + (tk - 1) >= first_maskable) | (k_lo + tk > n_valid)

    @pl.when(need_mask)
    def _():
        body(True)

    @pl.when(jnp.logical_not(need_mask))
    def _():
        body(False)

    @pl.when(ki == last_tile)
    def _():
        for h in range(heads):
            l = acc_ref[:, h * aw + dv:h * aw + dv + 1]
            o_ref[0, :, h * dv:(h + 1) * dv] = (acc_ref[:, h * aw:h * aw + dv] / l).astype(o_ref.dtype)


def _attention(q, k, v, bias, *, heads, dq, dv, scale, mode, q_off, n_valid, tq=512, tk=512):
    b, sq, _ = q.shape
    sk = k.shape[1]
    tq = _tile(sq, tq, SUBLANES)
    tk = _tile(sk, tk, LANES)
    nq, nk = sq // tq, sk // tk
    pairs = [(i, j) for i in range(nq)
             for j in range(min(_last_visible(q_off + i * tq + (tq - 1), mode) // tk, nk - 1) + 1)]
    qi_arr = jnp.asarray([p[0] for p in pairs], jnp.int32)
    ki_arr = jnp.asarray([p[1] for p in pairs], jnp.int32)

    in_specs = [pl.BlockSpec((1, tq, heads * dq), lambda bi, t, qi, ki: (bi, qi[t], 0)),
                pl.BlockSpec((1, tk, heads * dq), lambda bi, t, qi, ki: (bi, ki[t], 0)),
                pl.BlockSpec((1, tk, heads * dv), lambda bi, t, qi, ki: (bi, ki[t], 0))]
    args = [q, k, v]
    if bias is not None:
        in_specs.append(pl.BlockSpec((1, heads, tk), lambda bi, t, qi, ki: (bi, 0, ki[t])))
        args.append(bias)
    kern = functools.partial(_attn_kernel, heads=heads, dq=dq, dv=dv, tq=tq, tk=tk, nk=nk, scale=scale, mode=mode,
                             q_off=q_off, n_valid=n_valid, has_bias=bias is not None)
    return pl.pallas_call(
        kern,
        grid_spec=pltpu.PrefetchScalarGridSpec(
            num_scalar_prefetch=2, grid=(b, len(pairs)), in_specs=in_specs,
            out_specs=pl.BlockSpec((1, tq, heads * dv), lambda bi, t, qi, ki: (bi, qi[t], 0)),
            scratch_shapes=[pltpu.VMEM((heads, tq, LANES), F32), pltpu.VMEM((tq, heads * (dv + LANES)), F32)]),
        out_shape=jax.ShapeDtypeStruct((b, sq, heads * dv), BF16),
        compiler_params=_params("parallel", "arbitrary"), name="attn_" + mode,
    )(qi_arr, ki_arr, *args)


def _ssd_kernel(z_ref, xbc_ref, dt_ref, dtt_ref, cw_ref, cb_ref, dtb_ref, dtbt_ref, al_ref, alt_ref, dx_ref, nw_ref,
                h0_ref, c0_ref, y_ref, hout_ref, state_ref, carry_ref, *, lc, nc):
    c = pl.program_id(1)
    gw = SSM_D_INNER // SSM_GROUPS
    hpg = SSM_HEADS // SSM_GROUPS
    halo = SUBLANES

    @pl.when(c == 0)
    def _():
        state_ref[...] = h0_ref[0]
        carry_ref[...] = c0_ref[0]

    x = xbc_ref[0]
    cat = jnp.concatenate([carry_ref[...], x], axis=0)
    conv = cb_ref[...]
    for kk in range(SSM_CONV_W):
        shift = SSM_CONV_W - 1 - kk
        src = pltpu.roll(cat, shift, 0) if shift else cat
        conv = conv + src[halo:, :] * cw_ref[kk:kk + 1, :]
    carry_ref[...] = x[lc - halo:, :]
    act = conv * jax.nn.sigmoid(conv)
    xs = act[:, :SSM_D_INNER]
    bm = act[:, SSM_D_INNER:SSM_D_INNER + SSM_GROUPS * SSM_STATE]
    cm = act[:, SSM_D_INNER + SSM_GROUPS * SSM_STATE:]

    dt = _softplus(dt_ref[0] + dtb_ref[...])
    dtt = _softplus(dtt_ref[0] + dtbt_ref[...])
    adt = dt * (-jnp.exp(al_ref[...]))
    adtt = dtt * (-jnp.exp(alt_ref[...]))
    r = lax.broadcasted_iota(jnp.int32, (lc, lc), 0)
    cc = lax.broadcasted_iota(jnp.int32, (lc, lc), 1)
    tril = cc <= r
    acs = jnp.dot(tril.astype(F32), adt, preferred_element_type=F32, precision=HI)
    acst = jnp.dot(adtt, (r <= cc).astype(F32), preferred_element_type=F32, precision=HI)
    hh = lax.broadcasted_iota(jnp.int32, (SSM_HEADS, SSM_D_INNER), 0)
    ll = lax.broadcasted_iota(jnp.int32, (SSM_HEADS, SSM_D_INNER), 1)
    expand = ((ll >= hh * SSM_HEAD_DIM) & (ll < (hh + 1) * SSM_HEAD_DIM)).astype(F32)
    dt_x = jnp.dot(dt, expand, preferred_element_type=F32, precision=HI)
    acs_x = jnp.dot(acs, expand, preferred_element_type=F32, precision=HI)
    tot_x = acs_x[lc - 1:lc, :]
    xdt = xs * dt_x
    xdt_b = xdt.astype(BF16)
    w_end = (xdt * jnp.exp(tot_x - acs_x)).astype(BF16)
    state = state_ref[...]
    state_b = state.astype(BF16)

    y_parts, new_parts = [], []
    for g in range(SSM_GROUPS):
        bg = bm[:, g * SSM_STATE:(g + 1) * SSM_STATE].astype(BF16)
        cg = cm[:, g * SSM_STATE:(g + 1) * SSM_STATE].astype(BF16)
        cb = lax.dot_general(cg, bg, (((1,), (1,)), ((), ())), preferred_element_type=F32)
        for hl in range(hpg):
            h = g * hpg + hl
            seg = acs[:, h:h + 1] - acst[h:h + 1, :]
            mh = (cb * jnp.exp(jnp.where(tril, seg, NEG_BIG))).astype(BF16)
            y_parts.append(jnp.dot(mh, xdt_b[:, h * SSM_HEAD_DIM:(h + 1) * SSM_HEAD_DIM], preferred_element_type=F32))
        new_parts.append(lax.dot_general(bg, w_end[:, g * gw:(g + 1) * gw], (((0,), (0,)), ((), ())),
                                         preferred_element_type=F32))
    y_off = jnp.concatenate(
        [jnp.dot(cm[:, g * SSM_STATE:(g + 1) * SSM_STATE].astype(BF16), state_b[:, g * gw:(g + 1) * gw],
                 preferred_element_type=F32) for g in range(SSM_GROUPS)], axis=1) * jnp.exp(acs_x)
    y = jnp.concatenate(y_parts, axis=1) + y_off + dx_ref[...] * xs
    state_ref[...] = jnp.exp(tot_x) * state + jnp.concatenate(new_parts, axis=1)

    zz = z_ref[0]
    y = y * (zz * jax.nn.sigmoid(zz))
    for g in range(SSM_GROUPS):
        y_ref[0, :, g * gw:(g + 1) * gw] = _rms(y[:, g * gw:(g + 1) * gw], nw_ref[:, g * gw:(g + 1) * gw]).astype(y_ref.dtype)

    @pl.when(c == nc - 1)
    def _():
        hout_ref[0] = state_ref[...]


def _ssd(z, xbc, dt, dtt, p, h0, c0, *, lc=256):
    b, s, cd = xbc.shape
    lc = _tile(s, lc, LANES) if s % LANES == 0 else s
    nc = s // lc
    hh = SSM_HEADS
    full2 = lambda shape: pl.BlockSpec(shape, lambda i, j: (0, 0))
    return pl.pallas_call(
        functools.partial(_ssd_kernel, lc=lc, nc=nc), grid=(b, nc),
        in_specs=[pl.BlockSpec((1, lc, SSM_D_INNER), lambda i, j: (i, j, 0)),
                  pl.BlockSpec((1, lc, cd), lambda i, j: (i, j, 0)),
                  pl.BlockSpec((1, lc, hh), lambda i, j: (i, j, 0)),
                  pl.BlockSpec((1, hh, lc), lambda i, j: (i, 0, j)),
                  full2((SSM_CONV_W, cd)), full2((1, cd)),
                  full2((1, hh)), full2((hh, 1)), full2((1, hh)), full2((hh, 1)),
                  full2((1, SSM_D_INNER)), full2((1, SSM_D_INNER)),
                  pl.BlockSpec((1, SSM_STATE, SSM_D_INNER), lambda i, j: (i, 0, 0)),
                  pl.BlockSpec((1, SUBLANES, cd), lambda i, j: (i, 0, 0))],
        out_specs=[pl.BlockSpec((1, lc, SSM_D_INNER), lambda i, j: (i, j, 0)),
                   pl.BlockSpec((1, SSM_STATE, SSM_D_INNER), lambda i, j: (i, 0, 0))],
        out_shape=[jax.ShapeDtypeStruct((b, s, SSM_D_INNER), BF16),
                   jax.ShapeDtypeStruct((b, SSM_STATE, SSM_D_INNER), F32)],
        scratch_shapes=[pltpu.VMEM((SSM_STATE, SSM_D_INNER), F32), pltpu.VMEM((SUBLANES, cd), F32)],
        compiler_params=_params("parallel", "arbitrary"), name="ssd",
    )(z, xbc, dt, dtt, p["conv_w"], p["conv_b"], p["dt_b"], p["dt_bt"], p["a_log"], p["a_logt"], p["d_x"], p["norm_w"],
      h0, c0)


def _merge_kernel(o0, o1, o2, w0, w1, w2, g0, g1, g2, out_ref):
    acc = None
    for o_ref, w_ref, g_ref in ((o0, w0, g0), (o1, w1, g1), (o2, w2, g2)):
        t = jax.nn.sigmoid(g_ref[...]) * jnp.dot(o_ref[...], w_ref[...], preferred_element_type=F32)
        acc = t if acc is None else acc + t
    out_ref[...] = acc.astype(out_ref.dtype)


def _merge(o_list, w_list, gate, *, tm=1024, tn=512):
    m = gate.shape[0]
    d = w_list[0].shape[1]
    tm = _tile(m, tm, SUBLANES)
    tn = _tile(d, tn, LANES)
    nb = d // tn
    in_specs = [pl.BlockSpec((tm, o.shape[1]), lambda i, j: (i, 0)) for o in o_list]
    in_specs += [pl.BlockSpec((w.shape[0], tn), lambda i, j: (0, j)) for w in w_list]
    in_specs += [pl.BlockSpec((tm, tn), functools.partial(lambda i, j, br: (i, br * nb + j), br=br))
                 for br in range(N_BRANCH)]
    return pl.pallas_call(
        _merge_kernel, grid=(m // tm, nb), in_specs=in_specs,
        out_specs=pl.BlockSpec((tm, tn), lambda i, j: (i, j)),
        out_shape=jax.ShapeDtypeStruct((m, d), BF16),
        compiler_params=_params("parallel", "arbitrary"), name="merge",
    )(*o_list, *w_list, gate, gate, gate)


def _norm_kernel(x_ref, g_ref, o_ref):
    o_ref[...] = _rms(x_ref[...], g_ref[...])


def _final_norm(x, gain, *, tm=1024):
    m, d = x.shape
    tm = _tile(m, tm, SUBLANES)
    return pl.pallas_call(
        _norm_kernel, grid=(m // tm,),
        in_specs=[pl.BlockSpec((tm, d), lambda i: (i, 0)), pl.BlockSpec((1, d), lambda i: (0, 0))],
        out_specs=pl.BlockSpec((tm, d), lambda i: (i, 0)),
        out_shape=jax.ShapeDtypeStruct((m, d), F32),
        compiler_params=_params("parallel"), name="final_norm",
    )(x, gain.reshape(1, d).astype(F32))


def _pad_cols(w, n):
    return jnp.pad(w, ((0, 0), (0, n - w.shape[1])))


def _layer_weights(l, a):
    d_model = a["w_in"].shape[1]
    q_lora, kv_lora = a["mla_q_norm"].shape[1], a["mla_kv_norm"].shape[1]
    conv_dim = a["ssm_conv_w"].shape[2]
    sizes = (q_lora, kv_lora, MLA_ROPE, SSM_D_INNER, conv_dim, SSM_HEADS,
             FOX_HEADS * FOX_HEAD_DIM, FOX_HEADS * FOX_HEAD_DIM, FOX_HEADS * FOX_HEAD_DIM, FOX_HEADS,
             N_BRANCH * d_model)
    w_in = a["w_in"][l]
    assert w_in.shape[1] == sum(sizes)
    cols, start = [], 0
    for n in sizes:
        cols.append(w_in[:, start:start + n])
        start += n
    w_q, w_ckv, w_kpe, w_z, w_xbc, w_dt, w_fq, w_fk, w_fv, w_ff, w_gate = cols
    bf = lambda t: t.astype(BF16)
    w = {
        "in_q": bf(w_q), "in_ckv": bf(w_ckv),
        "in_small": bf(_pad_cols(jnp.concatenate([w_kpe, w_dt, w_ff], axis=1), LANES)),
        "in_z": bf(w_z), "in_xbc": bf(w_xbc), "in_fq": bf(w_fq), "in_fk": bf(w_fk), "in_fv": bf(w_fv),
        "in_gate": bf(w_gate),
    }
    wq = a["mla_w_uq"][l].reshape(q_lora, MLA_HEADS, MLA_NOPE + MLA_ROPE)
    wq = jnp.pad(wq, ((0, 0), (0, 0), (0, MLA_QK_PAD - MLA_NOPE - MLA_ROPE)))
    w["uq"] = bf(wq.reshape(q_lora, MLA_HEADS * MLA_QK_PAD))
    wkv = a["mla_w_ukv"][l].reshape(kv_lora, MLA_HEADS, MLA_NOPE + MLA_V)
    w["uk"] = bf(wkv[:, :, :MLA_NOPE].reshape(kv_lora, MLA_HEADS * MLA_NOPE))
    w["uv"] = bf(wkv[:, :, MLA_NOPE:].reshape(kv_lora, MLA_HEADS * MLA_V))
    for nm in ("w_br_mla", "w_br_ssd", "w_br_fox", "w_out"):
        w[nm] = bf(a[nm][l])
    for pre in ("ffn1", "ffn2"):
        f = a[pre + "_w_gate"].shape[2]
        fp = -(-f // (4 * LANES)) * (4 * LANES)
        w[pre + "_g"] = bf(_pad_cols(a[pre + "_w_gate"][l], fp))
        w[pre + "_u"] = bf(_pad_cols(a[pre + "_w_up"][l], fp))
        w[pre + "_d"] = bf(jnp.pad(a[pre + "_w_down"][l], ((0, fp - f), (0, 0))))
        w[pre + "_norm"] = a[pre + "_norm"][l]
    w["mix_norm"] = a["mix_norm"][l]
    w["q_norm"], w["kv_norm"] = a["mla_q_norm"][l], a["mla_kv_norm"][l]
    ff_lo = MLA_ROPE + SSM_HEADS
    w["fb_lanes"] = jnp.pad(a["fox_b_f"][l].astype(F32), (ff_lo, LANES - ff_lo - FOX_HEADS)).reshape(1, LANES)
    w["ssd"] = {
        "conv_w": a["ssm_conv_w"][l].astype(F32), "conv_b": a["ssm_conv_b"][l].astype(F32).reshape(1, conv_dim),
        "dt_b": a["ssm_dt_bias"][l].astype(F32).reshape(1, SSM_HEADS),
        "dt_bt": a["ssm_dt_bias"][l].astype(F32).reshape(SSM_HEADS, 1),
        "a_log": a["ssm_a_log"][l].astype(F32).reshape(1, SSM_HEADS),
        "a_logt": a["ssm_a_log"][l].astype(F32).reshape(SSM_HEADS, 1),
        "d_x": jnp.repeat(a["ssm_d"][l].astype(F32), SSM_HEAD_DIM).reshape(1, SSM_D_INNER),
        "norm_w": a["ssm_norm"][l].astype(F32).reshape(1, SSM_D_INNER),
    }
    return w


def _rope_tables(pos):
    half = MLA_ROPE // 2
    inv_freq = ROPE_BASE ** (-jnp.arange(half, dtype=F32) / half)
    ang = pos.astype(F32)[:, None] * inv_freq[None, :]
    cos, sin = jnp.cos(ang), jnp.sin(ang)
    z = jnp.zeros_like(cos)
    pad = jnp.zeros((pos.shape[0], LANES - MLA_ROPE), F32)
    return (jnp.concatenate([cos, cos, pad], axis=1),
            jnp.concatenate([-sin, z, pad], axis=1),
            jnp.concatenate([z, sin, pad], axis=1))


def _pad_keys(t, sk_pad):
    return jnp.pad(t, ((0, 0), (0, sk_pad - t.shape[1])) + ((0, 0),) * (t.ndim - 2))


def _layer(x, bsz, s, w, tabs, past):
    m = bsz * s
    kv_lora = w["kv_norm"].shape[0]
    conv_dim = w["ssd"]["conv_w"].shape[1]
    x = _ffn(x, w["ffn1_norm"], w["ffn1_g"], w["ffn1_u"], w["ffn1_d"])

    proj = functools.partial(_mm, x, prologue="rms", gain=w["mix_norm"])
    u_q = proj(w["in_q"], out_dtype=F32, name="in_q")
    u_ckv = proj(w["in_ckv"], out_dtype=F32, name="in_ckv")
    u_small = proj(w["in_small"], out_dtype=F32, name="in_small")
    u_z = proj(w["in_z"], out_dtype=F32, name="in_z")
    u_xbc = proj(w["in_xbc"], out_dtype=F32, name="in_xbc")
    fq = proj(w["in_fq"], out_dtype=BF16, name="in_fq")
    fk = proj(w["in_fk"], out_dtype=F32, name="in_fk")
    fv = proj(w["in_fv"], out_dtype=F32, name="in_fv")
    gate = proj(w["in_gate"], out_dtype=F32, name="in_gate")

    ff_lo = MLA_ROPE + SSM_HEADS
    ckv_new, small2 = _prep(u_ckv, w["kv_norm"], u_small, tabs, w["fb_lanes"], ff_lo=ff_lo, ff_hi=ff_lo + FOX_HEADS)
    kpe_new = small2[:, :MLA_ROPE]
    logf_new = small2[:, ff_lo:ff_lo + FOX_HEADS]
    u_dt = u_small[:, MLA_ROPE:ff_lo]

    past_len = 0 if past is None else past["mla_ckv"].shape[1]
    sk = past_len + s
    sk_pad = -(-sk // LANES) * LANES

    def with_past(new, key):
        new = new.reshape(bsz, s, -1)
        if past is None:
            return new
        return jnp.concatenate([past[key].reshape(bsz, past_len, -1).astype(new.dtype), new], axis=1)

    q_full = _mm(u_q, w["uq"], out_dtype=BF16, prologue="rms", gain=w["q_norm"], rope_tabs=tabs, name="mla_q")
    ckv_all = with_past(ckv_new, "mla_ckv").reshape(bsz * sk, kv_lora)
    k_nope = _mm(ckv_all, w["uk"], out_dtype=BF16, prologue="cast", name="mla_uk").reshape(bsz, sk, MLA_HEADS, MLA_NOPE)
    v_mla = _mm(ckv_all, w["uv"], out_dtype=BF16, prologue="cast", name="mla_uv").reshape(bsz, sk, MLA_HEADS * MLA_V)
    kpe_all = with_past(kpe_new, "mla_kpe").astype(BF16)
    k_full = jnp.concatenate(
        [k_nope, jnp.broadcast_to(kpe_all[:, :, None, :], (bsz, sk, MLA_HEADS, MLA_ROPE)),
         jnp.zeros((bsz, sk, MLA_HEADS, MLA_QK_PAD - MLA_NOPE - MLA_ROPE), BF16)], axis=-1,
    ).reshape(bsz, sk, MLA_HEADS * MLA_QK_PAD)
    o_mla = _attention(q_full.reshape(bsz, s, -1), _pad_keys(k_full, sk_pad), _pad_keys(v_mla, sk_pad), None,
                       heads=MLA_HEADS, dq=MLA_QK_PAD, dv=MLA_V, scale=MLA_SCALE, mode="chunk",
                       q_off=past_len, n_valid=sk)

    if past is None:
        conv_state = jnp.zeros((bsz, SSM_CONV_W - 1, conv_dim), F32)
        h0 = jnp.zeros((bsz, SSM_STATE, SSM_D_INNER), F32)
    else:
        conv_state = past["conv"].astype(F32)
        h0 = jnp.transpose(past["ssm"].astype(F32), (0, 3, 1, 2)).reshape(bsz, SSM_STATE, SSM_D_INNER)
    c0 = jnp.pad(conv_state, ((0, 0), (SUBLANES - (SSM_CONV_W - 1), 0), (0, 0)))
    xbc3 = u_xbc.reshape(bsz, s, conv_dim)
    dt3 = u_dt.reshape(bsz, s, SSM_HEADS)
    o_ssd, h_new = _ssd(u_z.reshape(bsz, s, SSM_D_INNER), xbc3, dt3, jnp.swapaxes(dt3, 1, 2), w["ssd"], h0, c0)
    ssm_new = jnp.transpose(h_new.reshape(bsz, SSM_STATE, SSM_HEADS, SSM_HEAD_DIM), (0, 2, 3, 1))
    keep = SSM_CONV_W - 1
    conv_new = xbc3[:, s - keep:] if s >= keep else jnp.concatenate([conv_state, xbc3], axis=1)[:, -keep:]

    logf_all = with_past(logf_new, "fox_logf")
    neg_cum = _cumsum_last(jnp.swapaxes(_pad_keys(logf_all, sk_pad), 1, 2), -1.0 / FOX_SCALE)
    hw = FOX_HEADS * FOX_HEAD_DIM
    k_all = with_past(fk, "fox_k").reshape(bsz, sk, hw).astype(BF16)
    v_all = with_past(fv, "fox_v").reshape(bsz, sk, hw).astype(BF16)
    o_fox = _attention(fq.reshape(bsz, s, hw), _pad_keys(k_all, sk_pad), _pad_keys(v_all, sk_pad), neg_cum,
                       heads=FOX_HEADS, dq=FOX_HEAD_DIM, dv=FOX_HEAD_DIM, scale=FOX_SCALE, mode="causal",
                       q_off=past_len, n_valid=sk)

    merged = _merge([o_mla.reshape(m, -1), o_ssd.reshape(m, -1), o_fox.reshape(m, -1)],
                    [w["w_br_mla"], w["w_br_ssd"], w["w_br_fox"]], gate)
    x = _mm(merged, w["w_out"], out_dtype=F32, residual=x, name="out_proj")
    x = _ffn(x, w["ffn2_norm"], w["ffn2_g"], w["ffn2_u"], w["ffn2_d"])
    state = (ckv_new.reshape(bsz, s, kv_lora), kpe_new.reshape(bsz, s, MLA_ROPE),
             fk.reshape(bsz, s, FOX_HEADS, FOX_HEAD_DIM), fv.reshape(bsz, s, FOX_HEADS, FOX_HEAD_DIM),
             logf_new.reshape(bsz, s, FOX_HEADS), ssm_new, conv_new)
    return x, state


def kernel(x_prompt, x_sample, cache_mla_ckv, cache_mla_kpe, cache_fox_k, cache_fox_v, cache_fox_logf, state_ssm,
           state_conv, ffn1_norm, ffn1_w_gate, ffn1_w_up, ffn1_w_down, mix_norm, w_in, mla_q_norm, mla_w_uq,
           mla_kv_norm, mla_w_ukv, ssm_conv_w, ssm_conv_b, ssm_dt_bias, ssm_a_log, ssm_d, ssm_norm, fox_b_f,
           w_br_mla, w_br_ssd, w_br_fox, w_out, ffn2_norm, ffn2_w_gate, ffn2_w_up, ffn2_w_down, final_norm):
    a = dict(ffn1_norm=ffn1_norm, ffn1_w_gate=ffn1_w_gate, ffn1_w_up=ffn1_w_up, ffn1_w_down=ffn1_w_down,
             mix_norm=mix_norm, w_in=w_in, mla_q_norm=mla_q_norm, mla_w_uq=mla_w_uq, mla_kv_norm=mla_kv_norm,
             mla_w_ukv=mla_w_ukv, ssm_conv_w=ssm_conv_w, ssm_conv_b=ssm_conv_b, ssm_dt_bias=ssm_dt_bias,
             ssm_a_log=ssm_a_log, ssm_d=ssm_d, ssm_norm=ssm_norm, fox_b_f=fox_b_f, w_br_mla=w_br_mla,
             w_br_ssd=w_br_ssd, w_br_fox=w_br_fox, w_out=w_out, ffn2_norm=ffn2_norm, ffn2_w_gate=ffn2_w_gate,
             ffn2_w_up=ffn2_w_up, ffn2_w_down=ffn2_w_down)
    depth = w_in.shape[0]
    bp, sp, d_model = x_prompt.shape
    bs, ss, _ = x_sample.shape
    past_len = cache_mla_ckv.shape[2]
    tabs_p = tuple(jnp.tile(t, (bp, 1)) for t in _rope_tables(jnp.arange(sp, dtype=jnp.int32)))
    tabs_s = tuple(jnp.tile(t, (bs, 1)) for t in _rope_tables(past_len + jnp.arange(ss, dtype=jnp.int32)))
    hp = x_prompt.reshape(bp * sp, d_model).astype(F32)
    hs = x_sample.reshape(bs * ss, d_model).astype(F32)
    new_p, new_s = [], []
    for l in range(depth):
        w = _layer_weights(l, a)
        past = {"mla_ckv": cache_mla_ckv[l], "mla_kpe": cache_mla_kpe[l], "fox_k": cache_fox_k[l],
                "fox_v": cache_fox_v[l], "fox_logf": cache_fox_logf[l], "ssm": state_ssm[l], "conv": state_conv[l]}
        hp, st_p = _layer(hp, bp, sp, w, tabs_p, None)
        hs, st_s = _layer(hs, bs, ss, w, tabs_s, past)
        new_p.append(st_p)
        new_s.append(st_s)
    y_prompt = _final_norm(hp, final_norm).reshape(bp, sp, d_model)
    y_sample = _final_norm(hs, final_norm).reshape(bs, ss, d_model)
    stk = lambda states, i: jnp.stack([st[i] for st in states], axis=0)
    return (y_prompt, y_sample) + tuple(stk(new_p, i) for i in range(7)) + tuple(stk(new_s, i) for i in range(7))
```

```python
import functools
import math

import jax
import jax.numpy as jnp
from jax import lax
from jax.experimental import pallas as pl
from jax.experimental.pallas import tpu as pltpu

F32 = jnp.float32
BF16 = jnp.bfloat16

EPS = 1e-6
CHUNK = 64
FFN_RES = 0.5
MLA_HEADS, MLA_NOPE, MLA_ROPE, MLA_V = 8, 128, 64, 128
MLA_SCALE = (MLA_NOPE + MLA_ROPE) ** -0.5
ROPE_BASE = 10000.0
SSM_HEADS, SSM_HEAD_DIM, SSM_GROUPS, SSM_STATE, SSM_CONV_W = 16, 64, 2, 128, 4
SSM_D_INNER = SSM_HEADS * SSM_HEAD_DIM
FOX_HEADS, FOX_HEAD_DIM = 8, 128
FOX_SCALE = FOX_HEAD_DIM ** -0.5
N_BRANCH = 3

LANES = 128
SUBLANES = 8
MXU_DIM = 256
VMEM_LIMIT = 56 * 1024 * 1024

MLA_QK_PAD = MXU_DIM
INPROJ_TN = 512
NEG_BIG = -1e30
LOG2E = math.log2(math.e)
HI = lax.Precision.HIGHEST


def _tile(n, pref, align):
    t = (min(pref, n) // align) * align
    while t >= align:
        if n % t == 0:
            return t
        t -= align
    return n


def _params(*sem):
    return pltpu.CompilerParams(dimension_semantics=sem, vmem_limit_bytes=VMEM_LIMIT)


def _rms(x, g):
    return x * lax.rsqrt(jnp.mean(x * x, axis=-1, keepdims=True) + EPS) * g


def _softplus(x):
    return jnp.maximum(x, 0.0) + jnp.log1p(jnp.exp(-jnp.abs(x)))


def _rope_lanes(pe, cos, s1, s2):
    half = MLA_ROPE // 2
    return pe * cos + pltpu.roll(pe, LANES - half, 1) * s1 + pltpu.roll(pe, half, 1) * s2


def _mm_kernel(*refs, prologue, rope, residual, tn):
    it = iter(refs)
    x_ref = next(it)
    g_ref = next(it) if prologue == "rms" else None
    w_ref = next(it)
    res_ref = next(it) if residual else None
    tabs = (next(it), next(it), next(it)) if rope else None
    o_ref = next(it)
    xn_ref = next(it) if prologue != "none" else None

    if prologue == "none":
        lhs = x_ref[...]
    else:
        @pl.when(pl.program_id(1) == 0)
        def _():
            x = x_ref[...].astype(F32)
            if prologue == "rms":
                x = _rms(x, g_ref[...])
            xn_ref[...] = x.astype(BF16)
        lhs = xn_ref[...]
    acc = jnp.dot(lhs, w_ref[...], preferred_element_type=F32)
    if residual:
        acc = res_ref[...] + acc
    if rope:
        cos, s1, s2 = (t[...] for t in tabs)
        for c in range(tn // MLA_QK_PAD):
            a = c * MLA_QK_PAD
            o_ref[:, a:a + LANES] = acc[:, a:a + LANES].astype(o_ref.dtype)
            o_ref[:, a + LANES:a + 2 * LANES] = _rope_lanes(acc[:, a + LANES:a + 2 * LANES], cos, s1, s2).astype(o_ref.dtype)
    else:
        o_ref[...] = acc.astype(o_ref.dtype)


def _mm(x, w, *, out_dtype, tm=1024, tn=512, prologue="none", gain=None, x_col=0, residual=None, rope_tabs=None,
        name="mm"):
    m = x.shape[0]
    k, n = w.shape
    assert x.shape[1] % k == 0 and (prologue != "none" or x.dtype == BF16)
    tm = _tile(m, tm, SUBLANES)
    tn = _tile(n, tn, MLA_QK_PAD if rope_tabs is not None else LANES)
    grid = (m // tm, n // tn)
    in_specs = [pl.BlockSpec((tm, k), lambda i, j: (i, x_col))]
    args = [x]
    if prologue == "rms":
        in_specs.append(pl.BlockSpec((1, k), lambda i, j: (0, 0)))
        args.append(gain.reshape(1, k).astype(F32))
    in_specs.append(pl.BlockSpec((k, tn), lambda i, j: (0, j)))
    args.append(w)
    if residual is not None:
        in_specs.append(pl.BlockSpec((tm, tn), lambda i, j: (i, j)))
        args.append(residual)
    if rope_tabs is not None:
        for t in rope_tabs:
            in_specs.append(pl.BlockSpec((tm, LANES), lambda i, j: (i, 0)))
            args.append(t)
    scratch = [pltpu.VMEM((tm, k), BF16)] if prologue != "none" else []
    kern = functools.partial(_mm_kernel, prologue=prologue, rope=rope_tabs is not None,
                             residual=residual is not None, tn=tn)
    return pl.pallas_call(
        kern, grid=grid, in_specs=in_specs,
        out_specs=pl.BlockSpec((tm, tn), lambda i, j: (i, j)),
        out_shape=jax.ShapeDtypeStruct((m, n), out_dtype),
        scratch_shapes=scratch, compiler_params=_params("parallel", "arbitrary"), name=name,
    )(*args)


def _inproj_kernel(x_ref, g_ref, w_ref, *rest, groups):
    outs, xn_ref = rest[:len(groups)], rest[len(groups)]
    j = pl.program_id(1)

    @pl.when(j == 0)
    def _():
        xn_ref[...] = _rms(x_ref[...], g_ref[...]).astype(BF16)

    acc = jnp.dot(xn_ref[...], w_ref[...], preferred_element_type=F32)
    for (lo, hi, width), o_ref in zip(groups, outs):
        @pl.when((j >= lo) & (j < hi))
        def _(o_ref=o_ref, width=width):
            o_ref[...] = acc[:, :width].astype(o_ref.dtype)


def _inproj_layout(widths, tn):
    groups, start = [], 0
    for n in widths:
        nt = -(-n // tn)
        assert n % tn == 0 or n < tn
        groups.append((start, start + nt, min(n, tn)))
        start += nt
    return groups, start


def _inproj(x, gain, w_cat, widths, dtypes, *, tm=512, tn=512):
    m, d = x.shape
    tm = _tile(m, tm, SUBLANES)
    groups, n_tiles = _inproj_layout(widths, tn)
    assert w_cat.shape == (d, n_tiles * tn)
    out_specs = [pl.BlockSpec((tm, bw), functools.partial(lambda i, j, lo, hi: (i, jnp.clip(j - lo, 0, hi - lo - 1)),
                                                           lo=lo, hi=hi)) for lo, hi, bw in groups]
    out_shape = [jax.ShapeDtypeStruct((m, n), dt) for n, dt in zip(widths, dtypes)]
    return pl.pallas_call(
        functools.partial(_inproj_kernel, groups=groups), grid=(m // tm, n_tiles),
        in_specs=[pl.BlockSpec((tm, d), lambda i, j: (i, 0)),
                  pl.BlockSpec((1, d), lambda i, j: (0, 0)),
                  pl.BlockSpec((d, tn), lambda i, j: (0, j))],
        out_specs=out_specs, out_shape=out_shape,
        scratch_shapes=[pltpu.VMEM((tm, d), BF16)],
        compiler_params=_params("parallel", "arbitrary"), name="inproj",
    )(x, gain.reshape(1, d).astype(F32), w_cat)


def _ffn_kernel(x_ref, g_ref, wg_ref, wu_ref, wd_ref, o_ref, xn_ref, acc_ref, *, nf, tf, f):
    j = pl.program_id(1)

    @pl.when(j == 0)
    def _():
        xn_ref[...] = _rms(x_ref[...], g_ref[...]).astype(BF16)
        acc_ref[...] = jnp.zeros_like(acc_ref)

    xn = xn_ref[...]
    a = jnp.dot(xn, wg_ref[...], preferred_element_type=F32)
    b = jnp.dot(xn, wu_ref[...], preferred_element_type=F32)
    h = a * jax.nn.sigmoid(a) * b
    wd = wd_ref[...]
    if f % tf:
        valid = f - j * tf
        h = jnp.where(lax.broadcasted_iota(jnp.int32, h.shape, 1) < valid, h, 0.0)
        wd = jnp.where(lax.broadcasted_iota(jnp.int32, wd.shape, 0) < valid, wd, jnp.zeros_like(wd))
    acc_ref[...] += jnp.dot(h.astype(BF16), wd, preferred_element_type=F32)

    @pl.when(j == nf - 1)
    def _():
        o_ref[...] = x_ref[...] + FFN_RES * acc_ref[...]


def _ffn(x, gain, wg, wu, wd, *, tm=512, tf=512):
    m, d = x.shape
    f = wg.shape[1]
    tm = _tile(m, tm, SUBLANES)
    nf = pl.cdiv(f, tf)
    return pl.pallas_call(
        functools.partial(_ffn_kernel, nf=nf, tf=tf, f=f), grid=(m // tm, nf),
        in_specs=[pl.BlockSpec((tm, d), lambda i, j: (i, 0)),
                  pl.BlockSpec((1, d), lambda i, j: (0, 0)),
                  pl.BlockSpec((d, tf), lambda i, j: (0, j)),
                  pl.BlockSpec((d, tf), lambda i, j: (0, j)),
                  pl.BlockSpec((tf, d), lambda i, j: (j, 0))],
        out_specs=pl.BlockSpec((tm, d), lambda i, j: (i, 0)),
        out_shape=jax.ShapeDtypeStruct((m, d), F32),
        scratch_shapes=[pltpu.VMEM((tm, d), BF16), pltpu.VMEM((tm, d), F32)],
        compiler_params=_params("parallel", "arbitrary"), name="ffn",
    )(x, gain.reshape(1, d).astype(F32), wg, wu, wd)


def _prep_kernel(uc_ref, g_ref, us_ref, cos_ref, s1_ref, s2_ref, fb_ref, ckv_ref, sm_ref, *, ff_lo, ff_hi):
    ckv_ref[...] = _rms(uc_ref[...], g_ref[...])
    us = us_ref[...]
    lane = lax.broadcasted_iota(jnp.int32, us.shape, 1)
    pe = jnp.where(lane < MLA_ROPE, us, 0.0)
    rot = _rope_lanes(pe, cos_ref[...], s1_ref[...], s2_ref[...])
    logf = -_softplus(-(us + fb_ref[...]))
    sm_ref[...] = jnp.where((lane >= ff_lo) & (lane < ff_hi), logf, rot)


def _prep(u_ckv, kv_gain, u_small, tabs, fb_lanes, *, ff_lo, ff_hi, tm=1024):
    m = u_small.shape[0]
    kv = kv_gain.shape[0]
    tm = _tile(m, tm, SUBLANES)
    row = lambda i: (i, 0)
    return pl.pallas_call(
        functools.partial(_prep_kernel, ff_lo=ff_lo, ff_hi=ff_hi), grid=(m // tm,),
        in_specs=[pl.BlockSpec((tm, kv), row),
                  pl.BlockSpec((1, kv), lambda i: (0, 0)),
                  pl.BlockSpec((tm, LANES), row), pl.BlockSpec((tm, LANES), row),
                  pl.BlockSpec((tm, LANES), row), pl.BlockSpec((tm, LANES), row),
                  pl.BlockSpec((1, LANES), lambda i: (0, 0))],
        out_specs=[pl.BlockSpec((tm, kv), row), pl.BlockSpec((tm, LANES), row)],
        out_shape=[jax.ShapeDtypeStruct((m, kv), F32), jax.ShapeDtypeStruct((m, LANES), F32)],
        compiler_params=_params("parallel"), name="prep",
    )(u_ckv, kv_gain.reshape(1, kv).astype(F32), u_small, *tabs, fb_lanes)


def _cumsum_kernel(x_ref, o_ref, carry_ref, *, tc, mult):
    @pl.when(pl.program_id(1) == 0)
    def _():
        carry_ref[...] = jnp.zeros_like(carry_ref)

    r = lax.broadcasted_iota(jnp.int32, (tc, tc), 0)
    c = lax.broadcasted_iota(jnp.int32, (tc, tc), 1)
    upper = (r <= c).astype(F32)
    y = jnp.dot(x_ref[0], upper, preferred_element_type=F32, precision=HI) + carry_ref[:, :1]
    o_ref[0] = y * mult
    carry_ref[...] = jnp.broadcast_to(y[:, tc - 1:tc], carry_ref.shape)


def _cumsum_last(x, mult, *, tc=256):
    b, h, s = x.shape
    tc = _tile(s, tc, LANES)
    return pl.pallas_call(
        functools.partial(_cumsum_kernel, tc=tc, mult=mult), grid=(b, s // tc),
        in_specs=[pl.BlockSpec((1, h, tc), lambda i, j: (i, 0, j))],
        out_specs=pl.BlockSpec((1, h, tc), lambda i, j: (i, 0, j)),
        out_shape=jax.ShapeDtypeStruct((b, h, s), F32),
        scratch_shapes=[pltpu.VMEM((h, LANES), F32)],
        compiler_params=_params("parallel", "arbitrary"), name="cumsum",
    )(x)


def _last_visible(q_end, mode):
    if mode == "chunk":
        return (q_end // CHUNK) * CHUNK + (CHUNK - 1)
    return q_end


def _attn_kernel(*refs, heads, dq, dk, dv, tq, tk, nk, scale, mode, q_off, n_valid, has_bias, has_shared):
    it = iter(refs)
    qi_ref, ki_ref = next(it), next(it)
    q_ref, k_ref, v_ref = next(it), next(it), next(it)
    ks_ref = next(it) if has_shared else None
    b_ref = next(it) if has_bias else None
    o_ref, m_ref, acc_ref = next(it), next(it), next(it)
    t = pl.program_id(1)
    qi, ki = qi_ref[t], ki_ref[t]
    nch = tk // LANES
    c = scale * LOG2E
    aw = dv + LANES

    @pl.when(ki == 0)
    def _():
        m_ref[...] = jnp.full_like(m_ref, NEG_BIG)
        acc_ref[...] = jnp.zeros_like(acc_ref)

    q_lo = q_off + qi * tq
    k_lo = ki * tk
    last_tile = jnp.minimum(_last_visible(q_lo + (tq - 1), mode) // tk, nk - 1)
    first_maskable = _last_visible(q_lo, mode) + 1
    ones_col = (lax.broadcasted_iota(jnp.int32, (tk, LANES), 1) == 0).astype(BF16)

    def body(masked):
        if masked:
            qpos = q_lo + lax.broadcasted_iota(jnp.int32, (tq, tk), 0)
            kpos = k_lo + lax.broadcasted_iota(jnp.int32, (tq, tk), 1)
            if mode == "chunk":
                sh = CHUNK.bit_length() - 1
                vis = lax.shift_right_logical(kpos, sh) <= lax.shift_right_logical(qpos, sh)
            else:
                vis = kpos <= qpos
            vis = vis & (kpos < n_valid)
        for h in range(heads):
            q = q_ref[0, :, h * dq:(h + 1) * dq]
            k = k_ref[0, :, h * dk:(h + 1) * dk]
            if has_shared:
                k = jnp.concatenate([k, ks_ref[0]], axis=1)
            s = lax.dot_general(q, k, (((1,), (1,)), ((), ())), preferred_element_type=F32)
            if has_bias:
                s = s + b_ref[0, h:h + 1, :]
            if masked:
                s = jnp.where(vis, s, NEG_BIG)
            m_prev = m_ref[h]
            mc = s[:, :LANES]
            for j in range(1, nch):
                mc = jnp.maximum(mc, s[:, j * LANES:(j + 1) * LANES])
            m_new = jnp.maximum(m_prev, jnp.max(mc, axis=1, keepdims=True))
            m_ref[h] = m_new
            alpha = jnp.exp2((m_prev - m_new) * c)
            p = jnp.concatenate([jnp.exp2((s[:, j * LANES:(j + 1) * LANES] - m_new) * c).astype(BF16)
                                 for j in range(nch)], axis=1)
            vx = jnp.concatenate([v_ref[0, :, h * dv:(h + 1) * dv], ones_col], axis=1)
            pv = jnp.dot(p, vx, preferred_element_type=F32)
            for a0 in range(h * aw, (h + 1) * aw, LANES):
                acc_ref[:, a0:a0 + LANES] = alpha * acc_ref[:, a0:a0 + LANES] + pv[:, a0 - h * aw:a0 - h * aw + LANES]

    need_mask = (k_lo + (tk - 1) >= first_maskable) | (k_lo + tk > n_valid)

    @pl.when(need_mask)
    def _():
        body(True)

    @pl.when(jnp.logical_not(need_mask))
    def _():
        body(False)

    @pl.when(ki == last_tile)
    def _():
        for h in range(heads):
            l = acc_ref[:, h * aw + dv:h * aw + dv + 1]
            o_ref[0, :, h * dv:(h + 1) * dv] = (acc_ref[:, h * aw:h * aw + dv] / l).astype(o_ref.dtype)


def _attention(q, k, v, k_shared, bias, *, heads, dq, dv, scale, mode, q_off, n_valid, tq=512, tk=512):
    b, sq, _ = q.shape
    sk = k.shape[1]
    dk = k.shape[2] // heads
    assert dk + (0 if k_shared is None else k_shared.shape[2]) == dq
    tq = _tile(sq, tq, SUBLANES)
    tk = _tile(sk, tk, LANES)
    nq, nk = sq // tq, sk // tk
    pairs = [(i, j) for i in range(nq)
             for j in range(min(_last_visible(q_off + i * tq + (tq - 1), mode) // tk, nk - 1) + 1)]
    qi_arr = jnp.asarray([p[0] for p in pairs], jnp.int32)
    ki_arr = jnp.asarray([p[1] for p in pairs], jnp.int32)

    in_specs = [pl.BlockSpec((1, tq, heads * dq), lambda bi, t, qi, ki: (bi, qi[t], 0)),
                pl.BlockSpec((1, tk, heads * dk), lambda bi, t, qi, ki: (bi, ki[t], 0)),
                pl.BlockSpec((1, tk, heads * dv), lambda bi, t, qi, ki: (bi, ki[t], 0))]
    args = [q, k, v]
    if k_shared is not None:
        in_specs.append(pl.BlockSpec((1, tk, dq - dk), lambda bi, t, qi, ki: (bi, ki[t], 0)))
        args.append(k_shared)
    if bias is not None:
        in_specs.append(pl.BlockSpec((1, heads, tk), lambda bi, t, qi, ki: (bi, 0, ki[t])))
        args.append(bias)
    kern = functools.partial(_attn_kernel, heads=heads, dq=dq, dk=dk, dv=dv, tq=tq, tk=tk, nk=nk, scale=scale,
                             mode=mode, q_off=q_off, n_valid=n_valid, has_bias=bias is not None,
                             has_shared=k_shared is not None)
    return pl.pallas_call(
        kern,
        grid_spec=pltpu.PrefetchScalarGridSpec(
            num_scalar_prefetch=2, grid=(b, len(pairs)), in_specs=in_specs,
            out_specs=pl.BlockSpec((1, tq, heads * dv), lambda bi, t, qi, ki: (bi, qi[t], 0)),
            scratch_shapes=[pltpu.VMEM((heads, tq, LANES), F32), pltpu.VMEM((tq, heads * (dv + LANES)), F32)]),
        out_shape=jax.ShapeDtypeStruct((b, sq, heads * dv), BF16),
        compiler_params=_params("parallel", "arbitrary"), name="attn_" + mode,
    )(qi_arr, ki_arr, *args)


def _ssd_kernel(z_ref, xbc_ref, dt_ref, dtt_ref, cw_ref, cb_ref, dtb_ref, dtbt_ref, al_ref, alt_ref, dx_ref, nw_ref,
                h0_ref, c0_ref, y_ref, hout_ref, state_ref, carry_ref, *, lc, nc):
    c = pl.program_id(1)
    gw = SSM_D_INNER // SSM_GROUPS
    hpg = SSM_HEADS // SSM_GROUPS
    halo = SUBLANES

    @pl.when(c == 0)
    def _():
        state_ref[...] = h0_ref[0]
        carry_ref[...] = c0_ref[0]

    x = xbc_ref[0]
    cat = jnp.concatenate([carry_ref[...], x], axis=0)
    conv = cb_ref[...]
    for kk in range(SSM_CONV_W):
        shift = SSM_CONV_W - 1 - kk
        src = pltpu.roll(cat, shift, 0) if shift else cat
        conv = conv + src[halo:, :] * cw_ref[kk:kk + 1, :]
    carry_ref[...] = x[lc - halo:, :]
    act = conv * jax.nn.sigmoid(conv)
    xs = act[:, :SSM_D_INNER]
    bm = act[:, SSM_D_INNER:SSM_D_INNER + SSM_GROUPS * SSM_STATE]
    cm = act[:, SSM_D_INNER + SSM_GROUPS * SSM_STATE:]

    dt = _softplus(dt_ref[0] + dtb_ref[...])
    dtt = _softplus(dtt_ref[0] + dtbt_ref[...])
    adt = dt * (-jnp.exp(al_ref[...]))
    adtt = dtt * (-jnp.exp(alt_ref[...]))
    r = lax.broadcasted_iota(jnp.int32, (lc, lc), 0)
    cc = lax.broadcasted_iota(jnp.int32, (lc, lc), 1)
    tril = cc <= r
    acs = jnp.dot(tril.astype(F32), adt, preferred_element_type=F32, precision=HI)
    acst = jnp.dot(adtt, (r <= cc).astype(F32), preferred_element_type=F32, precision=HI)
    hh = lax.broadcasted_iota(jnp.int32, (SSM_HEADS, SSM_D_INNER), 0)
    ll = lax.broadcasted_iota(jnp.int32, (SSM_HEADS, SSM_D_INNER), 1)
    expand = ((ll >= hh * SSM_HEAD_DIM) & (ll < (hh + 1) * SSM_HEAD_DIM)).astype(F32)
    dt_x = jnp.dot(dt, expand, preferred_element_type=F32, precision=HI)
    acs_x = jnp.dot(acs, expand, preferred_element_type=F32, precision=HI)
    tot_x = acs_x[lc - 1:lc, :]
    xdt = xs * dt_x
    xdt_b = xdt.astype(BF16)
    w_end = (xdt * jnp.exp(tot_x - acs_x)).astype(BF16)
    state = state_ref[...]
    state_b = state.astype(BF16)

    y_parts, new_parts = [], []
    for g in range(SSM_GROUPS):
        bg = bm[:, g * SSM_STATE:(g + 1) * SSM_STATE].astype(BF16)
        cg = cm[:, g * SSM_STATE:(g + 1) * SSM_STATE].astype(BF16)
        cb = lax.dot_general(cg, bg, (((1,), (1,)), ((), ())), preferred_element_type=F32)
        for hl in range(hpg):
            h = g * hpg + hl
            seg = acs[:, h:h + 1] - acst[h:h + 1, :]
            mh = (cb * jnp.exp(jnp.where(tril, seg, NEG_BIG))).astype(BF16)
            y_parts.append(jnp.dot(mh, xdt_b[:, h * SSM_HEAD_DIM:(h + 1) * SSM_HEAD_DIM], preferred_element_type=F32))
        new_parts.append(lax.dot_general(bg, w_end[:, g * gw:(g + 1) * gw], (((0,), (0,)), ((), ())),
                                         preferred_element_type=F32))
    y_off = jnp.concatenate(
        [jnp.dot(cm[:, g * SSM_STATE:(g + 1) * SSM_STATE].astype(BF16), state_b[:, g * gw:(g + 1) * gw],
                 preferred_element_type=F32) for g in range(SSM_GROUPS)], axis=1) * jnp.exp(acs_x)
    y = jnp.concatenate(y_parts, axis=1) + y_off + dx_ref[...] * xs
    state_ref[...] = jnp.exp(tot_x) * state + jnp.concatenate(new_parts, axis=1)

    zz = z_ref[0]
    y = y * (zz * jax.nn.sigmoid(zz))
    for g in range(SSM_GROUPS):
        y_ref[0, :, g * gw:(g + 1) * gw] = _rms(y[:, g * gw:(g + 1) * gw], nw_ref[:, g * gw:(g + 1) * gw]).astype(y_ref.dtype)

    @pl.when(c == nc - 1)
    def _():
        hout_ref[0] = state_ref[...]


def _ssd(z, xbc, dt, dtt, p, h0, c0, *, lc=256):
    b, s, cd = xbc.shape
    lc = _tile(s, lc, LANES) if s % LANES == 0 else s
    nc = s // lc
    hh = SSM_HEADS
    full2 = lambda shape: pl.BlockSpec(shape, lambda i, j: (0, 0))
    return pl.pallas_call(
        functools.partial(_ssd_kernel, lc=lc, nc=nc), grid=(b, nc),
        in_specs=[pl.BlockSpec((1, lc, SSM_D_INNER), lambda i, j: (i, j, 0)),
                  pl.BlockSpec((1, lc, cd), lambda i, j: (i, j, 0)),
                  pl.BlockSpec((1, lc, hh), lambda i, j: (i, j, 0)),
                  pl.BlockSpec((1, hh, lc), lambda i, j: (i, 0, j)),
                  full2((SSM_CONV_W, cd)), full2((1, cd)),
                  full2((1, hh)), full2((hh, 1)), full2((1, hh)), full2((hh, 1)),
                  full2((1, SSM_D_INNER)), full2((1, SSM_D_INNER)),
                  pl.BlockSpec((1, SSM_STATE, SSM_D_INNER), lambda i, j: (i, 0, 0)),
                  pl.BlockSpec((1, SUBLANES, cd), lambda i, j: (i, 0, 0))],
        out_specs=[pl.BlockSpec((1, lc, SSM_D_INNER), lambda i, j: (i, j, 0)),
                   pl.BlockSpec((1, SSM_STATE, SSM_D_INNER), lambda i, j: (i, 0, 0))],
        out_shape=[jax.ShapeDtypeStruct((b, s, SSM_D_INNER), BF16),
                   jax.ShapeDtypeStruct((b, SSM_STATE, SSM_D_INNER), F32)],
        scratch_shapes=[pltpu.VMEM((SSM_STATE, SSM_D_INNER), F32), pltpu.VMEM((SUBLANES, cd), F32)],
        compiler_params=_params("parallel", "arbitrary"), name="ssd",
    )(z, xbc, dt, dtt, p["conv_w"], p["conv_b"], p["dt_b"], p["dt_bt"], p["a_log"], p["a_logt"], p["d_x"], p["norm_w"],
      h0, c0)


def _merge_kernel(x_ref, g_ref, o0, o1, o2, w0, w1, w2, wg0, wg1, wg2, out_ref, xn_ref):
    @pl.when(pl.program_id(1) == 0)
    def _():
        xn_ref[...] = _rms(x_ref[...], g_ref[...]).astype(BF16)

    xn = xn_ref[...]
    acc = None
    for o_ref, w_ref, wg_ref in ((o0, w0, wg0), (o1, w1, wg1), (o2, w2, wg2)):
        gate = jax.nn.sigmoid(jnp.dot(xn, wg_ref[...], preferred_element_type=F32))
        t = gate * jnp.dot(o_ref[...], w_ref[...], preferred_element_type=F32)
        acc = t if acc is None else acc + t
    out_ref[...] = acc.astype(out_ref.dtype)


def _merge(x, gain, o_list, w_list, w_gate, *, tm=512, tn=512):
    m, d = x.shape
    tm = _tile(m, tm, SUBLANES)
    tn = _tile(d, tn, LANES)
    nb = d // tn
    in_specs = [pl.BlockSpec((tm, d), lambda i, j: (i, 0)), pl.BlockSpec((1, d), lambda i, j: (0, 0))]
    in_specs += [pl.BlockSpec((tm, o.shape[1]), lambda i, j: (i, 0)) for o in o_list]
    in_specs += [pl.BlockSpec((w.shape[0], tn), lambda i, j: (0, j)) for w in w_list]
    in_specs += [pl.BlockSpec((d, tn), functools.partial(lambda i, j, br: (0, br * nb + j), br=br))
                 for br in range(N_BRANCH)]
    return pl.pallas_call(
        _merge_kernel, grid=(m // tm, nb), in_specs=in_specs,
        out_specs=pl.BlockSpec((tm, tn), lambda i, j: (i, j)),
        out_shape=jax.ShapeDtypeStruct((m, d), BF16),
        scratch_shapes=[pltpu.VMEM((tm, d), BF16)],
        compiler_params=_params("parallel", "arbitrary"), name="merge",
    )(x, gain.reshape(1, d).astype(F32), *o_list, *w_list, w_gate, w_gate, w_gate)


def _norm_kernel(x_ref, g_ref, o_ref):
    o_ref[...] = _rms(x_ref[...], g_ref[...])


def _final_norm(x, gain, *, tm=1024):
    m, d = x.shape
    tm = _tile(m, tm, SUBLANES)
    return pl.pallas_call(
        _norm_kernel, grid=(m // tm,),
        in_specs=[pl.BlockSpec((tm, d), lambda i: (i, 0)), pl.BlockSpec((1, d), lambda i: (0, 0))],
        out_specs=pl.BlockSpec((tm, d), lambda i: (i, 0)),
        out_shape=jax.ShapeDtypeStruct((m, d), F32),
        compiler_params=_params("parallel"), name="final_norm",
    )(x, gain.reshape(1, d).astype(F32))


def _layer_weights(l, a):
    d_model = a["w_in"].shape[1]
    q_lora, kv_lora = a["mla_q_norm"].shape[1], a["mla_kv_norm"].shape[1]
    conv_dim = a["ssm_conv_w"].shape[2]
    sizes = (q_lora, kv_lora, MLA_ROPE, SSM_D_INNER, conv_dim, SSM_HEADS,
             FOX_HEADS * FOX_HEAD_DIM, FOX_HEADS * FOX_HEAD_DIM, FOX_HEADS * FOX_HEAD_DIM, FOX_HEADS,
             N_BRANCH * d_model)
    w_in = a["w_in"][l]
    assert w_in.shape[1] == sum(sizes)
    cols, start = [], 0
    for n in sizes:
        cols.append(w_in[:, start:start + n])
        start += n
    w_q, w_ckv, w_kpe, w_z, w_xbc, w_dt, w_fq, w_fk, w_fv, w_ff, w_gate = cols
    bf = lambda t: t.astype(BF16)
    small = [w_kpe, w_dt, w_ff]
    n_small = sum(t.shape[1] for t in small)
    assert n_small <= LANES
    parts = [w_q, w_ckv, *small, jnp.zeros((d_model, INPROJ_TN - n_small), w_in.dtype), w_z, w_xbc, w_fq, w_fk, w_fv]
    w = {"in_cat": bf(jnp.concatenate(parts, axis=1)), "in_gate": bf(w_gate),
         "in_widths": (q_lora, kv_lora, LANES, SSM_D_INNER, conv_dim) + (FOX_HEADS * FOX_HEAD_DIM,) * 3}
    wq = a["mla_w_uq"][l].reshape(q_lora, MLA_HEADS, MLA_NOPE + MLA_ROPE)
    wq = jnp.pad(wq, ((0, 0), (0, 0), (0, MLA_QK_PAD - MLA_NOPE - MLA_ROPE)))
    w["uq"] = bf(wq.reshape(q_lora, MLA_HEADS * MLA_QK_PAD))
    wkv = a["mla_w_ukv"][l].reshape(kv_lora, MLA_HEADS, MLA_NOPE + MLA_V)
    w["uk"] = bf(wkv[:, :, :MLA_NOPE].reshape(kv_lora, MLA_HEADS * MLA_NOPE))
    w["uv"] = bf(wkv[:, :, MLA_NOPE:].reshape(kv_lora, MLA_HEADS * MLA_V))
    for nm in ("w_br_mla", "w_br_ssd", "w_br_fox", "w_out"):
        w[nm] = bf(a[nm][l])
    for pre in ("ffn1", "ffn2"):
        w[pre + "_g"] = bf(a[pre + "_w_gate"])[l]
        w[pre + "_u"] = bf(a[pre + "_w_up"])[l]
        w[pre + "_d"] = bf(a[pre + "_w_down"])[l]
        w[pre + "_norm"] = a[pre + "_norm"][l]
    w["mix_norm"] = a["mix_norm"][l]
    w["q_norm"], w["kv_norm"] = a["mla_q_norm"][l], a["mla_kv_norm"][l]
    ff_lo = MLA_ROPE + SSM_HEADS
    w["fb_lanes"] = jnp.pad(a["fox_b_f"][l].astype(F32), (ff_lo, LANES - ff_lo - FOX_HEADS)).reshape(1, LANES)
    w["ssd"] = {
        "conv_w": a["ssm_conv_w"][l].astype(F32), "conv_b": a["ssm_conv_b"][l].astype(F32).reshape(1, conv_dim),
        "dt_b": a["ssm_dt_bias"][l].astype(F32).reshape(1, SSM_HEADS),
        "dt_bt": a["ssm_dt_bias"][l].astype(F32).reshape(SSM_HEADS, 1),
        "a_log": a["ssm_a_log"][l].astype(F32).reshape(1, SSM_HEADS),
        "a_logt": a["ssm_a_log"][l].astype(F32).reshape(SSM_HEADS, 1),
        "d_x": jnp.repeat(a["ssm_d"][l].astype(F32), SSM_HEAD_DIM).reshape(1, SSM_D_INNER),
        "norm_w": a["ssm_norm"][l].astype(F32).reshape(1, SSM_D_INNER),
    }
    return w


def _rope_tables(pos):
    half = MLA_ROPE // 2
    inv_freq = ROPE_BASE ** (-jnp.arange(half, dtype=F32) / half)
    ang = pos.astype(F32)[:, None] * inv_freq[None, :]
    cos, sin = jnp.cos(ang), jnp.sin(ang)
    z = jnp.zeros_like(cos)
    pad = jnp.zeros((pos.shape[0], LANES - MLA_ROPE), F32)
    return (jnp.concatenate([cos, cos, pad], axis=1),
            jnp.concatenate([-sin, z, pad], axis=1),
            jnp.concatenate([z, sin, pad], axis=1))


def _pad_keys(t, sk_pad):
    return jnp.pad(t, ((0, 0), (0, sk_pad - t.shape[1])) + ((0, 0),) * (t.ndim - 2))


def _layer(x, bsz, s, w, tabs, past):
    m = bsz * s
    kv_lora = w["kv_norm"].shape[0]
    conv_dim = w["ssd"]["conv_w"].shape[1]
    x = _ffn(x, w["ffn1_norm"], w["ffn1_g"], w["ffn1_u"], w["ffn1_d"])

    u_q, u_ckv, u_small, u_z, u_xbc, fq, fk, fv = _inproj(
        x, w["mix_norm"], w["in_cat"], w["in_widths"], (F32, F32, F32, F32, F32, BF16, F32, F32), tn=INPROJ_TN)

    ff_lo = MLA_ROPE + SSM_HEADS
    ckv_new, small2 = _prep(u_ckv, w["kv_norm"], u_small, tabs, w["fb_lanes"], ff_lo=ff_lo, ff_hi=ff_lo + FOX_HEADS)
    kpe_new = small2[:, :MLA_ROPE]
    logf_new = small2[:, ff_lo:ff_lo + FOX_HEADS]
    u_dt = u_small[:, MLA_ROPE:ff_lo]

    past_len = 0 if past is None else past["mla_ckv"].shape[1]
    sk = past_len + s
    sk_pad = -(-sk // LANES) * LANES

    def with_past(new, key):
        new = new.reshape(bsz, s, -1)
        if past is None:
            return new
        return jnp.concatenate([past[key].reshape(bsz, past_len, -1).astype(new.dtype), new], axis=1)

    q_full = _mm(u_q, w["uq"], out_dtype=BF16, prologue="rms", gain=w["q_norm"], rope_tabs=tabs, name="mla_q")
    ckv_all = with_past(ckv_new, "mla_ckv").reshape(bsz * sk, kv_lora)
    k_nope = _mm(ckv_all, w["uk"], out_dtype=BF16, prologue="cast", name="mla_uk").reshape(bsz, sk, MLA_HEADS * MLA_NOPE)
    v_mla = _mm(ckv_all, w["uv"], out_dtype=BF16, prologue="cast", name="mla_uv").reshape(bsz, sk, MLA_HEADS * MLA_V)
    kpe_all = jnp.pad(with_past(kpe_new, "mla_kpe").astype(BF16),
                      ((0, 0), (0, sk_pad - sk), (0, MLA_QK_PAD - MLA_NOPE - MLA_ROPE)))
    o_mla = _attention(q_full.reshape(bsz, s, -1), _pad_keys(k_nope, sk_pad), _pad_keys(v_mla, sk_pad), kpe_all, None,
                       heads=MLA_HEADS, dq=MLA_QK_PAD, dv=MLA_V, scale=MLA_SCALE, mode="chunk",
                       q_off=past_len, n_valid=sk)

    if past is None:
        conv_state = jnp.zeros((bsz, SSM_CONV_W - 1, conv_dim), F32)
        h0 = jnp.zeros((bsz, SSM_STATE, SSM_D_INNER), F32)
    else:
        conv_state = past["conv"].astype(F32)
        h0 = jnp.transpose(past["ssm"].astype(F32), (0, 3, 1, 2)).reshape(bsz, SSM_STATE, SSM_D_INNER)
    c0 = jnp.pad(conv_state, ((0, 0), (SUBLANES - (SSM_CONV_W - 1), 0), (0, 0)))
    xbc3 = u_xbc.reshape(bsz, s, conv_dim)
    dt3 = u_dt.reshape(bsz, s, SSM_HEADS)
    o_ssd, h_new = _ssd(u_z.reshape(bsz, s, SSM_D_INNER), xbc3, dt3, jnp.swapaxes(dt3, 1, 2), w["ssd"], h0, c0)
    ssm_new = jnp.transpose(h_new.reshape(bsz, SSM_STATE, SSM_HEADS, SSM_HEAD_DIM), (0, 2, 3, 1))
    keep = SSM_CONV_W - 1
    conv_new = xbc3[:, s - keep:] if s >= keep else jnp.concatenate([conv_state, xbc3], axis=1)[:, -keep:]

    logf_all = with_past(logf_new, "fox_logf")
    neg_cum = _cumsum_last(jnp.swapaxes(_pad_keys(logf_all, sk_pad), 1, 2), -1.0 / FOX_SCALE)
    hw = FOX_HEADS * FOX_HEAD_DIM
    k_all = with_past(fk, "fox_k").reshape(bsz, sk, hw).astype(BF16)
    v_all = with_past(fv, "fox_v").reshape(bsz, sk, hw).astype(BF16)
    o_fox = _attention(fq.reshape(bsz, s, hw), _pad_keys(k_all, sk_pad), _pad_keys(v_all, sk_pad), None, neg_cum,
                       heads=FOX_HEADS, dq=FOX_HEAD_DIM, dv=FOX_HEAD_DIM, scale=FOX_SCALE, mode="causal",
                       q_off=past_len, n_valid=sk)

    merged = _merge(x, w["mix_norm"], [o_mla.reshape(m, -1), o_ssd.reshape(m, -1), o_fox.reshape(m, -1)],
                    [w["w_br_mla"], w["w_br_ssd"], w["w_br_fox"]], w["in_gate"])
    x = _mm(merged, w["w_out"], out_dtype=F32, residual=x, name="out_proj")
    x = _ffn(x, w["ffn2_norm"], w["ffn2_g"], w["ffn2_u"], w["ffn2_d"])
    state = (ckv_new.reshape(bsz, s, kv_lora), kpe_new.reshape(bsz, s, MLA_ROPE),
             fk.reshape(bsz, s, FOX_HEADS, FOX_HEAD_DIM), fv.reshape(bsz, s, FOX_HEADS, FOX_HEAD_DIM),
             logf_new.reshape(bsz, s, FOX_HEADS), ssm_new, conv_new)
    return x, state


def kernel(x_prompt, x_sample, cache_mla_ckv, cache_mla_kpe, cache_fox_k, cache_fox_v, cache_fox_logf, state_ssm,
           state_conv, ffn1_norm, ffn1_w_gate, ffn1_w_up, ffn1_w_down, mix_norm, w_in, mla_q_norm, mla_w_uq,
           mla_kv_norm, mla_w_ukv, ssm_conv_w, ssm_conv_b, ssm_dt_bias, ssm_a_log, ssm_d, ssm_norm, fox_b_f,
           w_br_mla, w_br_ssd, w_br_fox, w_out, ffn2_norm, ffn2_w_gate, ffn2_w_up, ffn2_w_down, final_norm):
    a = dict(ffn1_norm=ffn1_norm, ffn1_w_gate=ffn1_w_gate, ffn1_w_up=ffn1_w_up, ffn1_w_down=ffn1_w_down,
             mix_norm=mix_norm, w_in=w_in, mla_q_norm=mla_q_norm, mla_w_uq=mla_w_uq, mla_kv_norm=mla_kv_norm,
             mla_w_ukv=mla_w_ukv, ssm_conv_w=ssm_conv_w, ssm_conv_b=ssm_conv_b, ssm_dt_bias=ssm_dt_bias,
             ssm_a_log=ssm_a_log, ssm_d=ssm_d, ssm_norm=ssm_norm, fox_b_f=fox_b_f, w_br_mla=w_br_mla,
             w_br_ssd=w_br_ssd, w_br_fox=w_br_fox, w_out=w_out, ffn2_norm=ffn2_norm, ffn2_w_gate=ffn2_w_gate,
             ffn2_w_up=ffn2_w_up, ffn2_w_down=ffn2_w_down)
    depth = w_in.shape[0]
    bp, sp, d_model = x_prompt.shape
    bs, ss, _ = x_sample.shape
    past_len = cache_mla_ckv.shape[2]
    tabs_p = tuple(jnp.tile(t, (bp, 1)) for t in _rope_tables(jnp.arange(sp, dtype=jnp.int32)))
    tabs_s = tuple(jnp.tile(t, (bs, 1)) for t in _rope_tables(past_len + jnp.arange(ss, dtype=jnp.int32)))
    hp = x_prompt.reshape(bp * sp, d_model).astype(F32)
    hs = x_sample.reshape(bs * ss, d_model).astype(F32)
    new_p, new_s = [], []
    for l in range(depth):
        w = _layer_weights(l, a)
        past = {"mla_ckv": cache_mla_ckv[l], "mla_kpe": cache_mla_kpe[l], "fox_k": cache_fox_k[l],
                "fox_v": cache_fox_v[l], "fox_logf": cache_fox_logf[l], "ssm": state_ssm[l], "conv": state_conv[l]}
        hp, st_p = _layer(hp, bp, sp, w, tabs_p, None)
        hs, st_s = _layer(hs, bs, ss, w, tabs_s, past)
        new_p.append(st_p)
        new_s.append(st_s)
    y_prompt = _final_norm(hp, final_norm).reshape(bp, sp, d_model)
    y_sample = _final_norm(hs, final_norm).reshape(bs, ss, d_model)
    stk = lambda states, i: jnp.stack([st[i] for st in states], axis=0)
    return (y_prompt, y_sample) + tuple(stk(new_p, i) for i in range(7)) + tuple(stk(new_s, i) for i in range(7))
```

```python
import functools
import math

import jax
import jax.numpy as jnp
from jax import lax
from jax.experimental import pallas as pl
from jax.experimental.pallas import tpu as pltpu

F32 = jnp.float32
BF16 = jnp.bfloat16

EPS = 1e-6
CHUNK = 64
FFN_RES = 0.5
MLA_HEADS, MLA_NOPE, MLA_ROPE, MLA_V = 8, 128, 64, 128
MLA_SCALE = (MLA_NOPE + MLA_ROPE) ** -0.5
ROPE_BASE = 10000.0
SSM_HEADS, SSM_HEAD_DIM, SSM_GROUPS, SSM_STATE, SSM_CONV_W = 16, 64, 2, 128, 4
SSM_D_INNER = SSM_HEADS * SSM_HEAD_DIM
FOX_HEADS, FOX_HEAD_DIM = 8, 128
FOX_SCALE = FOX_HEAD_DIM ** -0.5
N_BRANCH = 3

LANES = 128
SUBLANES = 8
MXU_DIM = 256
VMEM_LIMIT = 56 * 1024 * 1024

MLA_QK_PAD = MXU_DIM
INPROJ_TN = 512
NEG_BIG = -1e30
LOG2E = math.log2(math.e)
HI = lax.Precision.HIGHEST


def _tile(n, pref, align):
    t = (min(pref, n) // align) * align
    while t >= align:
        if n % t == 0:
            return t
        t -= align
    return n


def _params(*sem):
    return pltpu.CompilerParams(dimension_semantics=sem, vmem_limit_bytes=VMEM_LIMIT)


def _rms(x, g):
    return x * lax.rsqrt(jnp.mean(x * x, axis=-1, keepdims=True) + EPS) * g


def _softplus(x):
    return jnp.maximum(x, 0.0) + jnp.log1p(jnp.exp(-jnp.abs(x)))


def _rope_lanes(pe, cos, s1, s2):
    half = MLA_ROPE // 2
    return pe * cos + pltpu.roll(pe, LANES - half, 1) * s1 + pltpu.roll(pe, half, 1) * s2


def _mm_kernel(*refs, prologue, rope, residual, tn):
    it = iter(refs)
    x_ref = next(it)
    g_ref = next(it) if prologue == "rms" else None
    w_ref = next(it)
    res_ref = next(it) if residual else None
    tabs = (next(it), next(it), next(it)) if rope else None
    o_ref = next(it)
    xn_ref = next(it) if prologue != "none" else None

    if prologue == "none":
        lhs = x_ref[...]
    else:
        @pl.when(pl.program_id(1) == 0)
        def _():
            x = x_ref[...].astype(F32)
            if prologue == "rms":
                x = _rms(x, g_ref[...])
            xn_ref[...] = x.astype(BF16)
        lhs = xn_ref[...]
    acc = jnp.dot(lhs, w_ref[...], preferred_element_type=F32)
    if residual:
        acc = res_ref[...] + acc
    if rope:
        cos, s1, s2 = (t[...] for t in tabs)
        for c in range(tn // MLA_QK_PAD):
            a = c * MLA_QK_PAD
            o_ref[:, a:a + LANES] = acc[:, a:a + LANES].astype(o_ref.dtype)
            o_ref[:, a + LANES:a + 2 * LANES] = _rope_lanes(acc[:, a + LANES:a + 2 * LANES], cos, s1, s2).astype(o_ref.dtype)
    else:
        o_ref[...] = acc.astype(o_ref.dtype)


def _mm(x, w, *, out_dtype, tm=1024, tn=512, prologue="none", gain=None, x_col=0, residual=None, rope_tabs=None,
        name="mm"):
    m = x.shape[0]
    k, n = w.shape
    assert x.shape[1] % k == 0 and (prologue != "none" or x.dtype == BF16)
    tm = _tile(m, tm, SUBLANES)
    tn = _tile(n, tn, MLA_QK_PAD if rope_tabs is not None else LANES)
    grid = (m // tm, n // tn)
    in_specs = [pl.BlockSpec((tm, k), lambda i, j: (i, x_col))]
    args = [x]
    if prologue == "rms":
        in_specs.append(pl.BlockSpec((1, k), lambda i, j: (0, 0)))
        args.append(gain.reshape(1, k).astype(F32))
    in_specs.append(pl.BlockSpec((k, tn), lambda i, j: (0, j)))
    args.append(w)
    if residual is not None:
        in_specs.append(pl.BlockSpec((tm, tn), lambda i, j: (i, j)))
        args.append(residual)
    if rope_tabs is not None:
        for t in rope_tabs:
            in_specs.append(pl.BlockSpec((tm, LANES), lambda i, j: (i, 0)))
            args.append(t)
    scratch = [pltpu.VMEM((tm, k), BF16)] if prologue != "none" else []
    kern = functools.partial(_mm_kernel, prologue=prologue, rope=rope_tabs is not None,
                             residual=residual is not None, tn=tn)
    return pl.pallas_call(
        kern, grid=grid, in_specs=in_specs,
        out_specs=pl.BlockSpec((tm, tn), lambda i, j: (i, j)),
        out_shape=jax.ShapeDtypeStruct((m, n), out_dtype),
        scratch_shapes=scratch, compiler_params=_params("parallel", "arbitrary"), name=name,
    )(*args)


def _inproj_kernel(xn_ref, w_ref, *outs, groups):
    j = pl.program_id(1)
    acc = jnp.dot(xn_ref[...], w_ref[...], preferred_element_type=F32)
    for (lo, hi, width), o_ref in zip(groups, outs):
        @pl.when((j >= lo) & (j < hi))
        def _(o_ref=o_ref, width=width):
            o_ref[...] = acc[:, :width].astype(o_ref.dtype)


def _inproj_layout(widths, tn):
    groups, start = [], 0
    for n in widths:
        nt = -(-n // tn)
        assert n % tn == 0 or n < tn
        groups.append((start, start + nt, min(n, tn)))
        start += nt
    return groups, start


def _inproj(xn, w_cat, widths, dtypes, *, tm=1024, tn=512):
    m, d = xn.shape
    tm = _tile(m, tm, SUBLANES)
    groups, n_tiles = _inproj_layout(widths, tn)
    assert w_cat.shape == (d, n_tiles * tn)
    out_specs = [pl.BlockSpec((tm, bw), functools.partial(lambda i, j, lo, hi: (i, jnp.clip(j - lo, 0, hi - lo - 1)),
                                                           lo=lo, hi=hi)) for lo, hi, bw in groups]
    out_shape = [jax.ShapeDtypeStruct((m, n), dt) for n, dt in zip(widths, dtypes)]
    return pl.pallas_call(
        functools.partial(_inproj_kernel, groups=groups), grid=(m // tm, n_tiles),
        in_specs=[pl.BlockSpec((tm, d), lambda i, j: (i, 0)),
                  pl.BlockSpec((d, tn), lambda i, j: (0, j))],
        out_specs=out_specs, out_shape=out_shape,
        compiler_params=_params("parallel", "arbitrary"), name="inproj",
    )(xn, w_cat)


def _ffn_kernel(*refs, nf, tf, f, post):
    x_ref, g_ref, wg_ref, wu_ref, wd_ref = refs[:5]
    refs = refs[5:]
    pg_ref = None
    if post is not None:
        pg_ref, refs = refs[0], refs[1:]
    outs, (xn_ref, acc_ref) = refs[:-2], refs[-2:]
    j = pl.program_id(1)

    @pl.when(j == 0)
    def _():
        xn_ref[...] = _rms(x_ref[...], g_ref[...]).astype(BF16)
        acc_ref[...] = jnp.zeros_like(acc_ref)

    xn = xn_ref[...]
    a = jnp.dot(xn, wg_ref[...], preferred_element_type=F32)
    b = jnp.dot(xn, wu_ref[...], preferred_element_type=F32)
    h = a * jax.nn.sigmoid(a) * b
    wd = wd_ref[...]
    if f % tf:
        valid = f - j * tf
        h = jnp.where(lax.broadcasted_iota(jnp.int32, h.shape, 1) < valid, h, 0.0)
        wd = jnp.where(lax.broadcasted_iota(jnp.int32, wd.shape, 0) < valid, wd, jnp.zeros_like(wd))
    acc_ref[...] += jnp.dot(h.astype(BF16), wd, preferred_element_type=F32)

    @pl.when(j == nf - 1)
    def _():
        y = x_ref[...] + FFN_RES * acc_ref[...]
        if post is None:
            outs[0][...] = y
        elif post == "norm_bf16":
            outs[0][...] = y
            outs[1][...] = _rms(y, pg_ref[...]).astype(BF16)
        else:
            outs[0][...] = _rms(y, pg_ref[...])


def _ffn(x, gain, wg, wu, wd, *, post=None, post_gain=None, tm=512, tf=512):
    m, d = x.shape
    f = wg.shape[1]
    tm = _tile(m, tm, SUBLANES)
    nf = pl.cdiv(f, tf)
    row = pl.BlockSpec((tm, d), lambda i, j: (i, 0))
    vec = pl.BlockSpec((1, d), lambda i, j: (0, 0))
    in_specs = [row, vec,
                pl.BlockSpec((d, tf), lambda i, j: (0, j)),
                pl.BlockSpec((d, tf), lambda i, j: (0, j)),
                pl.BlockSpec((tf, d), lambda i, j: (j, 0))]
    args = [x, gain.reshape(1, d).astype(F32), wg, wu, wd]
    out_specs, out_shape = [row], [jax.ShapeDtypeStruct((m, d), F32)]
    if post is not None:
        in_specs.append(vec)
        args.append(post_gain.reshape(1, d).astype(F32))
    if post == "norm_bf16":
        out_specs.append(row)
        out_shape.append(jax.ShapeDtypeStruct((m, d), BF16))
    return pl.pallas_call(
        functools.partial(_ffn_kernel, nf=nf, tf=tf, f=f, post=post), grid=(m // tm, nf),
        in_specs=in_specs, out_specs=out_specs, out_shape=out_shape,
        scratch_shapes=[pltpu.VMEM((tm, d), BF16), pltpu.VMEM((tm, d), F32)],
        compiler_params=_params("parallel", "arbitrary"), name="ffn",
    )(*args)


def _prep_kernel(uc_ref, g_ref, us_ref, cos_ref, s1_ref, s2_ref, fb_ref, ckv_ref, sm_ref, *, ff_lo, ff_hi):
    ckv_ref[...] = _rms(uc_ref[...], g_ref[...])
    us = us_ref[...]
    lane = lax.broadcasted_iota(jnp.int32, us.shape, 1)
    pe = jnp.where(lane < MLA_ROPE, us, 0.0)
    rot = _rope_lanes(pe, cos_ref[...], s1_ref[...], s2_ref[...])
    logf = -_softplus(-(us + fb_ref[...]))
    sm_ref[...] = jnp.where((lane >= ff_lo) & (lane < ff_hi), logf, rot)


def _prep(u_ckv, kv_gain, u_small, tabs, fb_lanes, *, ff_lo, ff_hi, tm=1024):
    m = u_small.shape[0]
    kv = kv_gain.shape[0]
    tm = _tile(m, tm, SUBLANES)
    row = lambda i: (i, 0)
    return pl.pallas_call(
        functools.partial(_prep_kernel, ff_lo=ff_lo, ff_hi=ff_hi), grid=(m // tm,),
        in_specs=[pl.BlockSpec((tm, kv), row),
                  pl.BlockSpec((1, kv), lambda i: (0, 0)),
                  pl.BlockSpec((tm, LANES), row), pl.BlockSpec((tm, LANES), row),
                  pl.BlockSpec((tm, LANES), row), pl.BlockSpec((tm, LANES), row),
                  pl.BlockSpec((1, LANES), lambda i: (0, 0))],
        out_specs=[pl.BlockSpec((tm, kv), row), pl.BlockSpec((tm, LANES), row)],
        out_shape=[jax.ShapeDtypeStruct((m, kv), F32), jax.ShapeDtypeStruct((m, LANES), F32)],
        compiler_params=_params("parallel"), name="prep",
    )(u_ckv, kv_gain.reshape(1, kv).astype(F32), u_small, *tabs, fb_lanes)


def _cumsum_kernel(x_ref, o_ref, carry_ref, *, tc, mult):
    @pl.when(pl.program_id(1) == 0)
    def _():
        carry_ref[...] = jnp.zeros_like(carry_ref)

    r = lax.broadcasted_iota(jnp.int32, (tc, tc), 0)
    c = lax.broadcasted_iota(jnp.int32, (tc, tc), 1)
    upper = (r <= c).astype(F32)
    y = jnp.dot(x_ref[0], upper, preferred_element_type=F32, precision=HI) + carry_ref[:, :1]
    o_ref[0] = y * mult
    carry_ref[...] = jnp.broadcast_to(y[:, tc - 1:tc], carry_ref.shape)


def _cumsum_last(x, mult, *, tc=512):
    b, h, s = x.shape
    tc = _tile(s, tc, LANES)
    return pl.pallas_call(
        functools.partial(_cumsum_kernel, tc=tc, mult=mult), grid=(b, s // tc),
        in_specs=[pl.BlockSpec((1, h, tc), lambda i, j: (i, 0, j))],
        out_specs=pl.BlockSpec((1, h, tc), lambda i, j: (i, 0, j)),
        out_shape=jax.ShapeDtypeStruct((b, h, s), F32),
        scratch_shapes=[pltpu.VMEM((h, LANES), F32)],
        compiler_params=_params("parallel", "arbitrary"), name="cumsum",
    )(x)


def _last_visible(q_end, mode):
    if mode == "chunk":
        return (q_end // CHUNK) * CHUNK + (CHUNK - 1)
    return q_end


def _attn_kernel(*refs, heads, dq, dk, dv, tq, tk, nk, scale, mode, q_off, n_valid, has_bias, has_shared):
    it = iter(refs)
    qi_ref, ki_ref = next(it), next(it)
    q_ref, k_ref, v_ref = next(it), next(it), next(it)
    ks_ref = next(it) if has_shared else None
    b_ref = next(it) if has_bias else None
    o_ref, m_ref, acc_ref = next(it), next(it), next(it)
    t = pl.program_id(1)
    qi, ki = qi_ref[t], ki_ref[t]
    nch = tk // LANES
    c = scale * LOG2E
    aw = dv + LANES

    @pl.when(ki == 0)
    def _():
        m_ref[...] = jnp.full_like(m_ref, NEG_BIG)
        acc_ref[...] = jnp.zeros_like(acc_ref)

    q_lo = q_off + qi * tq
    k_lo = ki * tk
    last_tile = jnp.minimum(_last_visible(q_lo + (tq - 1), mode) // tk, nk - 1)
    first_maskable = _last_visible(q_lo, mode) + 1
    ones_col = (lax.broadcasted_iota(jnp.int32, (tk, LANES), 1) == 0).astype(BF16)

    def body(masked):
        if masked:
            qpos = q_lo + lax.broadcasted_iota(jnp.int32, (tq, tk), 0)
            kpos = k_lo + lax.broadcasted_iota(jnp.int32, (tq, tk), 1)
            if mode == "chunk":
                sh = CHUNK.bit_length() - 1
                vis = lax.shift_right_logical(kpos, sh) <= lax.shift_right_logical(qpos, sh)
            else:
                vis = kpos <= qpos
            vis = vis & (kpos < n_valid)
        for h in range(heads):
            q = q_ref[0, :, h * dq:(h + 1) * dq]
            k = k_ref[0, :, h * dk:(h + 1) * dk]
            if has_shared:
                k = jnp.concatenate([k, ks_ref[0]], axis=1)
            s = lax.dot_general(q, k, (((1,), (1,)), ((), ())), preferred_element_type=F32)
            if has_bias:
                s = s + b_ref[0, h:h + 1, :]
            if masked:
                s = jnp.where(vis, s, NEG_BIG)
            m_prev = m_ref[h]
            mc = s[:, :LANES]
            for j in range(1, nch):
                mc = jnp.maximum(mc, s[:, j * LANES:(j + 1) * LANES])
            m_new = jnp.maximum(m_prev, jnp.max(mc, axis=1, keepdims=True))
            m_ref[h] = m_new
            alpha = jnp.exp2((m_prev - m_new) * c)
            p = jnp.concatenate([jnp.exp2((s[:, j * LANES:(j + 1) * LANES] - m_new) * c).astype(BF16)
                                 for j in range(nch)], axis=1)
            vx = jnp.concatenate([v_ref[0, :, h * dv:(h + 1) * dv], ones_col], axis=1)
            pv = jnp.dot(p, vx, preferred_element_type=F32)
            for a0 in range(h * aw, (h + 1) * aw, LANES):
                acc_ref[:, a0:a0 + LANES] = alpha * acc_ref[:, a0:a0 + LANES] + pv[:, a0 - h * aw:a0 - h * aw + LANES]

    need_mask = (k_lo + (tk - 1) >= first_maskable) | (k_lo + tk > n_valid)

    @pl.when(need_mask)
    def _():
        body(True)

    @pl.when(jnp.logical_not(need_mask))
    def _():
        body(False)

    @pl.when(ki == last_tile)
    def _():
        for h in range(heads):
            l = acc_ref[:, h * aw + dv:h * aw + dv + 1]
            o_ref[0, :, h * dv:(h + 1) * dv] = (acc_ref[:, h * aw:h * aw + dv] / l).astype(o_ref.dtype)


def _attention(q, k, v, k_shared, bias, *, heads, dq, dk, dv, scale, mode, q_off, n_valid, k_col=0, v_col=0,
               tq=512, tk=1024):
    b, sq, _ = q.shape
    sk = k.shape[1]
    assert dk + (0 if k_shared is None else k_shared.shape[2]) == dq
    tq = _tile(sq, tq, SUBLANES)
    tk = sk if sk <= 2 * tk else _tile(sk, tk, LANES)
    nq, nk = sq // tq, sk // tk
    pairs = [(i, j) for i in range(nq)
             for j in range(min(_last_visible(q_off + i * tq + (tq - 1), mode) // tk, nk - 1) + 1)]
    qi_arr = jnp.asarray([p[0] for p in pairs], jnp.int32)
    ki_arr = jnp.asarray([p[1] for p in pairs], jnp.int32)

    in_specs = [pl.BlockSpec((1, tq, heads * dq), lambda bi, t, qi, ki: (bi, qi[t], 0)),
                pl.BlockSpec((1, tk, heads * dk), lambda bi, t, qi, ki: (bi, ki[t], k_col)),
                pl.BlockSpec((1, tk, heads * dv), lambda bi, t, qi, ki: (bi, ki[t], v_col))]
    args = [q, k, v]
    if k_shared is not None:
        in_specs.append(pl.BlockSpec((1, tk, dq - dk), lambda bi, t, qi, ki: (bi, ki[t], 0)))
        args.append(k_shared)
    if bias is not None:
        in_specs.append(pl.BlockSpec((1, heads, tk), lambda bi, t, qi, ki: (bi, 0, ki[t])))
        args.append(bias)
    kern = functools.partial(_attn_kernel, heads=heads, dq=dq, dk=dk, dv=dv, tq=tq, tk=tk, nk=nk, scale=scale,
                             mode=mode, q_off=q_off, n_valid=n_valid, has_bias=bias is not None,
                             has_shared=k_shared is not None)
    return pl.pallas_call(
        kern,
        grid_spec=pltpu.PrefetchScalarGridSpec(
            num_scalar_prefetch=2, grid=(b, len(pairs)), in_specs=in_specs,
            out_specs=pl.BlockSpec((1, tq, heads * dv), lambda bi, t, qi, ki: (bi, qi[t], 0)),
            scratch_shapes=[pltpu.VMEM((heads, tq, LANES), F32), pltpu.VMEM((tq, heads * (dv + LANES)), F32)]),
        out_shape=jax.ShapeDtypeStruct((b, sq, heads * dv), BF16),
        compiler_params=_params("parallel", "arbitrary"), name="attn_" + mode,
    )(qi_arr, ki_arr, *args)


def _ssd_kernel(z_ref, xbc_ref, dt_ref, dtt_ref, cw_ref, cb_ref, dtb_ref, dtbt_ref, al_ref, alt_ref, dx_ref, nw_ref,
                h0_ref, c0_ref, y_ref, hout_ref, state_ref, carry_ref, *, lc, nc):
    c = pl.program_id(1)
    gw = SSM_D_INNER // SSM_GROUPS
    hpg = SSM_HEADS // SSM_GROUPS
    halo = SUBLANES

    @pl.when(c == 0)
    def _():
        state_ref[...] = h0_ref[0]
        carry_ref[...] = c0_ref[0]

    x = xbc_ref[0]
    cat = jnp.concatenate([carry_ref[...], x], axis=0)
    conv = cb_ref[...]
    for kk in range(SSM_CONV_W):
        shift = SSM_CONV_W - 1 - kk
        src = pltpu.roll(cat, shift, 0) if shift else cat
        conv = conv + src[halo:, :] * cw_ref[kk:kk + 1, :]
    carry_ref[...] = x[lc - halo:, :]
    act = conv * jax.nn.sigmoid(conv)
    xs = act[:, :SSM_D_INNER]
    bm = act[:, SSM_D_INNER:SSM_D_INNER + SSM_GROUPS * SSM_STATE]
    cm = act[:, SSM_D_INNER + SSM_GROUPS * SSM_STATE:]

    dt = _softplus(dt_ref[0] + dtb_ref[...])
    dtt = _softplus(dtt_ref[0] + dtbt_ref[...])
    adt = dt * (-jnp.exp(al_ref[...]))
    adtt = dtt * (-jnp.exp(alt_ref[...]))
    r = lax.broadcasted_iota(jnp.int32, (lc, lc), 0)
    cc = lax.broadcasted_iota(jnp.int32, (lc, lc), 1)
    tril = cc <= r
    acs = jnp.dot(tril.astype(F32), adt, preferred_element_type=F32, precision=HI)
    acst = jnp.dot(adtt, (r <= cc).astype(F32), preferred_element_type=F32, precision=HI)
    hh = lax.broadcasted_iota(jnp.int32, (SSM_HEADS, SSM_D_INNER), 0)
    ll = lax.broadcasted_iota(jnp.int32, (SSM_HEADS, SSM_D_INNER), 1)
    expand = ((ll >= hh * SSM_HEAD_DIM) & (ll < (hh + 1) * SSM_HEAD_DIM)).astype(F32)
    dt_x = jnp.dot(dt, expand, preferred_element_type=F32, precision=HI)
    acs_x = jnp.dot(acs, expand, preferred_element_type=F32, precision=HI)
    tot_x = acs_x[lc - 1:lc, :]
    xdt = xs * dt_x
    xdt_b = xdt.astype(BF16)
    w_end = (xdt * jnp.exp(tot_x - acs_x)).astype(BF16)
    state = state_ref[...]
    state_b = state.astype(BF16)

    y_parts, new_parts = [], []
    for g in range(SSM_GROUPS):
        bg = bm[:, g * SSM_STATE:(g + 1) * SSM_STATE].astype(BF16)
        cg = cm[:, g * SSM_STATE:(g + 1) * SSM_STATE].astype(BF16)
        cb = lax.dot_general(cg, bg, (((1,), (1,)), ((), ())), preferred_element_type=F32)
        for hl in range(hpg):
            h = g * hpg + hl
            seg = acs[:, h:h + 1] - acst[h:h + 1, :]
            mh = (cb * jnp.exp(jnp.where(tril, seg, NEG_BIG))).astype(BF16)
            y_parts.append(jnp.dot(mh, xdt_b[:, h * SSM_HEAD_DIM:(h + 1) * SSM_HEAD_DIM], preferred_element_type=F32))
        new_parts.append(lax.dot_general(bg, w_end[:, g * gw:(g + 1) * gw], (((0,), (0,)), ((), ())),
                                         preferred_element_type=F32))
    y_off = jnp.concatenate(
        [jnp.dot(cm[:, g * SSM_STATE:(g + 1) * SSM_STATE].astype(BF16), state_b[:, g * gw:(g + 1) * gw],
                 preferred_element_type=F32) for g in range(SSM_GROUPS)], axis=1) * jnp.exp(acs_x)
    y = jnp.concatenate(y_parts, axis=1) + y_off + dx_ref[...] * xs
    state_ref[...] = jnp.exp(tot_x) * state + jnp.concatenate(new_parts, axis=1)

    zz = z_ref[0]
    y = y * (zz * jax.nn.sigmoid(zz))
    for g in range(SSM_GROUPS):
        y_ref[0, :, g * gw:(g + 1) * gw] = _rms(y[:, g * gw:(g + 1) * gw], nw_ref[:, g * gw:(g + 1) * gw]).astype(y_ref.dtype)

    @pl.when(c == nc - 1)
    def _():
        hout_ref[0] = state_ref[...]


def _ssd(z, xbc, dt, dtt, p, h0, c0, *, lc=256):
    b, s, cd = xbc.shape
    lc = _tile(s, lc, LANES) if s % LANES == 0 else s
    nc = s // lc
    hh = SSM_HEADS
    full2 = lambda shape: pl.BlockSpec(shape, lambda i, j: (0, 0))
    return pl.pallas_call(
        functools.partial(_ssd_kernel, lc=lc, nc=nc), grid=(b, nc),
        in_specs=[pl.BlockSpec((1, lc, SSM_D_INNER), lambda i, j: (i, j, 0)),
                  pl.BlockSpec((1, lc, cd), lambda i, j: (i, j, 0)),
                  pl.BlockSpec((1, lc, hh), lambda i, j: (i, j, 0)),
                  pl.BlockSpec((1, hh, lc), lambda i, j: (i, 0, j)),
                  full2((SSM_CONV_W, cd)), full2((1, cd)),
                  full2((1, hh)), full2((hh, 1)), full2((1, hh)), full2((hh, 1)),
                  full2((1, SSM_D_INNER)), full2((1, SSM_D_INNER)),
                  pl.BlockSpec((1, SSM_STATE, SSM_D_INNER), lambda i, j: (i, 0, 0)),
                  pl.BlockSpec((1, SUBLANES, cd), lambda i, j: (i, 0, 0))],
        out_specs=[pl.BlockSpec((1, lc, SSM_D_INNER), lambda i, j: (i, j, 0)),
                   pl.BlockSpec((1, SSM_STATE, SSM_D_INNER), lambda i, j: (i, 0, 0))],
        out_shape=[jax.ShapeDtypeStruct((b, s, SSM_D_INNER), BF16),
                   jax.ShapeDtypeStruct((b, SSM_STATE, SSM_D_INNER), F32)],
        scratch_shapes=[pltpu.VMEM((SSM_STATE, SSM_D_INNER), F32), pltpu.VMEM((SUBLANES, cd), F32)],
        compiler_params=_params("parallel", "arbitrary"), name="ssd",
    )(z, xbc, dt, dtt, p["conv_w"], p["conv_b"], p["dt_b"], p["dt_bt"], p["a_log"], p["a_logt"], p["d_x"], p["norm_w"],
      h0, c0)


def _merge_kernel(xn_ref, o0, o1, o2, w0, w1, w2, wg0, wg1, wg2, out_ref):
    xn = xn_ref[...]
    acc = None
    for o_ref, w_ref, wg_ref in ((o0, w0, wg0), (o1, w1, wg1), (o2, w2, wg2)):
        gate = jax.nn.sigmoid(jnp.dot(xn, wg_ref[...], preferred_element_type=F32))
        t = gate * jnp.dot(o_ref[...], w_ref[...], preferred_element_type=F32)
        acc = t if acc is None else acc + t
    out_ref[...] = acc.astype(out_ref.dtype)


def _merge(xn, o_list, w_list, w_gate, *, tm=1024, tn=512):
    m, d = xn.shape
    tm = _tile(m, tm, SUBLANES)
    tn = _tile(d, tn, LANES)
    nb = d // tn
    in_specs = [pl.BlockSpec((tm, d), lambda i, j: (i, 0))]
    in_specs += [pl.BlockSpec((tm, o.shape[1]), lambda i, j: (i, 0)) for o in o_list]
    in_specs += [pl.BlockSpec((w.shape[0], tn), lambda i, j: (0, j)) for w in w_list]
    in_specs += [pl.BlockSpec((d, tn), functools.partial(lambda i, j, br: (0, br * nb + j), br=br))
                 for br in range(N_BRANCH)]
    return pl.pallas_call(
        _merge_kernel, grid=(m // tm, nb), in_specs=in_specs,
        out_specs=pl.BlockSpec((tm, tn), lambda i, j: (i, j)),
        out_shape=jax.ShapeDtypeStruct((m, d), BF16),
        compiler_params=_params("parallel", "arbitrary"), name="merge",
    )(xn, *o_list, *w_list, w_gate, w_gate, w_gate)


def _layer_weights(l, a):
    d_model = a["w_in"].shape[1]
    q_lora, kv_lora = a["mla_q_norm"].shape[1], a["mla_kv_norm"].shape[1]
    conv_dim = a["ssm_conv_w"].shape[2]
    sizes = (q_lora, kv_lora, MLA_ROPE, SSM_D_INNER, conv_dim, SSM_HEADS,
             FOX_HEADS * FOX_HEAD_DIM, FOX_HEADS * FOX_HEAD_DIM, FOX_HEADS * FOX_HEAD_DIM, FOX_HEADS,
             N_BRANCH * d_model)
    w_in = a["w_in"][l]
    assert w_in.shape[1] == sum(sizes)
    cols, start = [], 0
    for n in sizes:
        cols.append(w_in[:, start:start + n])
        start += n
    w_q, w_ckv, w_kpe, w_z, w_xbc, w_dt, w_fq, w_fk, w_fv, w_ff, w_gate = cols
    bf = lambda t: t.astype(BF16)
    small = [w_kpe, w_dt, w_ff]
    n_small = sum(t.shape[1] for t in small)
    assert n_small <= LANES
    parts = [w_q, w_ckv, *small, jnp.zeros((d_model, INPROJ_TN - n_small), w_in.dtype), w_z, w_xbc, w_fq, w_fk, w_fv]
    w = {"in_cat": bf(jnp.concatenate(parts, axis=1)), "in_gate": bf(w_gate),
         "in_widths": (q_lora, kv_lora, LANES, SSM_D_INNER, conv_dim) + (FOX_HEADS * FOX_HEAD_DIM,) * 3}
    wq = a["mla_w_uq"][l].reshape(q_lora, MLA_HEADS, MLA_NOPE + MLA_ROPE)
    wq = jnp.pad(wq, ((0, 0), (0, 0), (0, MLA_QK_PAD - MLA_NOPE - MLA_ROPE)))
    w["uq"] = bf(wq.reshape(q_lora, MLA_HEADS * MLA_QK_PAD))
    wkv = a["mla_w_ukv"][l].reshape(kv_lora, MLA_HEADS, MLA_NOPE + MLA_V)
    w["ukv"] = bf(jnp.concatenate([wkv[:, :, :MLA_NOPE].reshape(kv_lora, MLA_HEADS * MLA_NOPE),
                                   wkv[:, :, MLA_NOPE:].reshape(kv_lora, MLA_HEADS * MLA_V)], axis=1))
    for nm in ("w_br_mla", "w_br_ssd", "w_br_fox", "w_out"):
        w[nm] = bf(a[nm][l])
    for pre in ("ffn1", "ffn2"):
        w[pre + "_g"] = bf(a[pre + "_w_gate"])[l]
        w[pre + "_u"] = bf(a[pre + "_w_up"])[l]
        w[pre + "_d"] = bf(a[pre + "_w_down"])[l]
        w[pre + "_norm"] = a[pre + "_norm"][l]
    w["mix_norm"] = a["mix_norm"][l]
    w["q_norm"], w["kv_norm"] = a["mla_q_norm"][l], a["mla_kv_norm"][l]
    ff_lo = MLA_ROPE + SSM_HEADS
    w["fb_lanes"] = jnp.pad(a["fox_b_f"][l].astype(F32), (ff_lo, LANES - ff_lo - FOX_HEADS)).reshape(1, LANES)
    w["ssd"] = {
        "conv_w": a["ssm_conv_w"][l].astype(F32), "conv_b": a["ssm_conv_b"][l].astype(F32).reshape(1, conv_dim),
        "dt_b": a["ssm_dt_bias"][l].astype(F32).reshape(1, SSM_HEADS),
        "dt_bt": a["ssm_dt_bias"][l].astype(F32).reshape(SSM_HEADS, 1),
        "a_log": a["ssm_a_log"][l].astype(F32).reshape(1, SSM_HEADS),
        "a_logt": a["ssm_a_log"][l].astype(F32).reshape(SSM_HEADS, 1),
        "d_x": jnp.repeat(a["ssm_d"][l].astype(F32), SSM_HEAD_DIM).reshape(1, SSM_D_INNER),
        "norm_w": a["ssm_norm"][l].astype(F32).reshape(1, SSM_D_INNER),
    }
    return w


def _rope_tables(pos):
    half = MLA_ROPE // 2
    inv_freq = ROPE_BASE ** (-jnp.arange(half, dtype=F32) / half)
    ang = pos.astype(F32)[:, None] * inv_freq[None, :]
    cos, sin = jnp.cos(ang), jnp.sin(ang)
    z = jnp.zeros_like(cos)
    pad = jnp.zeros((pos.shape[0], LANES - MLA_ROPE), F32)
    return (jnp.concatenate([cos, cos, pad], axis=1),
            jnp.concatenate([-sin, z, pad], axis=1),
            jnp.concatenate([z, sin, pad], axis=1))


def _pad_keys(t, sk_pad):
    return jnp.pad(t, ((0, 0), (0, sk_pad - t.shape[1])) + ((0, 0),) * (t.ndim - 2))


def _layer(x, bsz, s, w, tabs, past, final_gain):
    m = bsz * s
    kv_lora = w["kv_norm"].shape[0]
    conv_dim = w["ssd"]["conv_w"].shape[1]
    x, xn = _ffn(x, w["ffn1_norm"], w["ffn1_g"], w["ffn1_u"], w["ffn1_d"], post="norm_bf16", post_gain=w["mix_norm"])

    u_q, u_ckv, u_small, u_z, u_xbc, fq, fk, fv = _inproj(
        xn, w["in_cat"], w["in_widths"], (F32, F32, F32, F32, F32, BF16, F32, F32), tn=INPROJ_TN)

    ff_lo = MLA_ROPE + SSM_HEADS
    ckv_new, small2 = _prep(u_ckv, w["kv_norm"], u_small, tabs, w["fb_lanes"], ff_lo=ff_lo, ff_hi=ff_lo + FOX_HEADS)
    kpe_new = small2[:, :MLA_ROPE]
    logf_new = small2[:, ff_lo:ff_lo + FOX_HEADS]
    u_dt = u_small[:, MLA_ROPE:ff_lo]

    past_len = 0 if past is None else past["mla_ckv"].shape[1]
    sk = past_len + s
    sk_pad = -(-sk // LANES) * LANES

    def with_past(new, key):
        new = new.reshape(bsz, s, -1)
        if past is None:
            return new
        return jnp.concatenate([past[key].reshape(bsz, past_len, -1).astype(new.dtype), new], axis=1)

    q_full = _mm(u_q, w["uq"], out_dtype=BF16, prologue="rms", gain=w["q_norm"], rope_tabs=tabs, name="mla_q")
    ckv_all = with_past(ckv_new, "mla_ckv").reshape(bsz * sk, kv_lora)
    kv = _pad_keys(_mm(ckv_all, w["ukv"], out_dtype=BF16, prologue="cast", tn=1024, name="mla_ukv").reshape(bsz, sk, -1),
                   sk_pad)
    kpe_all = jnp.pad(with_past(kpe_new, "mla_kpe").astype(BF16),
                      ((0, 0), (0, sk_pad - sk), (0, MLA_QK_PAD - MLA_NOPE - MLA_ROPE)))
    o_mla = _attention(q_full.reshape(bsz, s, -1), kv, kv, kpe_all, None, k_col=0, v_col=1,
                       heads=MLA_HEADS, dq=MLA_QK_PAD, dk=MLA_NOPE, dv=MLA_V, scale=MLA_SCALE, mode="chunk",
                       q_off=past_len, n_valid=sk)

    if past is None:
        conv_state = jnp.zeros((bsz, SSM_CONV_W - 1, conv_dim), F32)
        h0 = jnp.zeros((bsz, SSM_STATE, SSM_D_INNER), F32)
    else:
        conv_state = past["conv"].astype(F32)
        h0 = jnp.transpose(past["ssm"].astype(F32), (0, 3, 1, 2)).reshape(bsz, SSM_STATE, SSM_D_INNER)
    c0 = jnp.pad(conv_state, ((0, 0), (SUBLANES - (SSM_CONV_W - 1), 0), (0, 0)))
    xbc3 = u_xbc.reshape(bsz, s, conv_dim)
    dt3 = u_dt.reshape(bsz, s, SSM_HEADS)
    o_ssd, h_new = _ssd(u_z.reshape(bsz, s, SSM_D_INNER), xbc3, dt3, jnp.swapaxes(dt3, 1, 2), w["ssd"], h0, c0)
    ssm_new = jnp.transpose(h_new.reshape(bsz, SSM_STATE, SSM_HEADS, SSM_HEAD_DIM), (0, 2, 3, 1))
    keep = SSM_CONV_W - 1
    conv_new = xbc3[:, s - keep:] if s >= keep else jnp.concatenate([conv_state, xbc3], axis=1)[:, -keep:]

    logf_all = with_past(logf_new, "fox_logf")
    neg_cum = _cumsum_last(jnp.swapaxes(_pad_keys(logf_all, sk_pad), 1, 2), -1.0 / FOX_SCALE)
    hw = FOX_HEADS * FOX_HEAD_DIM
    k_all = with_past(fk, "fox_k").reshape(bsz, sk, hw).astype(BF16)
    v_all = with_past(fv, "fox_v").reshape(bsz, sk, hw).astype(BF16)
    o_fox = _attention(fq.reshape(bsz, s, hw), _pad_keys(k_all, sk_pad), _pad_keys(v_all, sk_pad), None, neg_cum,
                       heads=FOX_HEADS, dq=FOX_HEAD_DIM, dk=FOX_HEAD_DIM, dv=FOX_HEAD_DIM, scale=FOX_SCALE, mode="causal",
                       q_off=past_len, n_valid=sk)

    merged = _merge(xn, [o_mla.reshape(m, -1), o_ssd.reshape(m, -1), o_fox.reshape(m, -1)],
                    [w["w_br_mla"], w["w_br_ssd"], w["w_br_fox"]], w["in_gate"])
    x = _mm(merged, w["w_out"], out_dtype=F32, residual=x, name="out_proj")
    (x,) = _ffn(x, w["ffn2_norm"], w["ffn2_g"], w["ffn2_u"], w["ffn2_d"],
                post=None if final_gain is None else "norm_only", post_gain=final_gain)
    state = (ckv_new.reshape(bsz, s, kv_lora), kpe_new.reshape(bsz, s, MLA_ROPE),
             fk.reshape(bsz, s, FOX_HEADS, FOX_HEAD_DIM), fv.reshape(bsz, s, FOX_HEADS, FOX_HEAD_DIM),
             logf_new.reshape(bsz, s, FOX_HEADS), ssm_new, conv_new)
    return x, state


def kernel(x_prompt, x_sample, cache_mla_ckv, cache_mla_kpe, cache_fox_k, cache_fox_v, cache_fox_logf, state_ssm,
           state_conv, ffn1_norm, ffn1_w_gate, ffn1_w_up, ffn1_w_down, mix_norm, w_in, mla_q_norm, mla_w_uq,
           mla_kv_norm, mla_w_ukv, ssm_conv_w, ssm_conv_b, ssm_dt_bias, ssm_a_log, ssm_d, ssm_norm, fox_b_f,
           w_br_mla, w_br_ssd, w_br_fox, w_out, ffn2_norm, ffn2_w_gate, ffn2_w_up, ffn2_w_down, final_norm):
    a = dict(ffn1_norm=ffn1_norm, ffn1_w_gate=ffn1_w_gate, ffn1_w_up=ffn1_w_up, ffn1_w_down=ffn1_w_down,
             mix_norm=mix_norm, w_in=w_in, mla_q_norm=mla_q_norm, mla_w_uq=mla_w_uq, mla_kv_norm=mla_kv_norm,
             mla_w_ukv=mla_w_ukv, ssm_conv_w=ssm_conv_w, ssm_conv_b=ssm_conv_b, ssm_dt_bias=ssm_dt_bias,
             ssm_a_log=ssm_a_log, ssm_d=ssm_d, ssm_norm=ssm_norm, fox_b_f=fox_b_f, w_br_mla=w_br_mla,
             w_br_ssd=w_br_ssd, w_br_fox=w_br_fox, w_out=w_out, ffn2_norm=ffn2_norm, ffn2_w_gate=ffn2_w_gate,
             ffn2_w_up=ffn2_w_up, ffn2_w_down=ffn2_w_down)
    depth = w_in.shape[0]
    bp, sp, d_model = x_prompt.shape
    bs, ss, _ = x_sample.shape
    past_len = cache_mla_ckv.shape[2]
    tabs_p = tuple(jnp.tile(t, (bp, 1)) for t in _rope_tables(jnp.arange(sp, dtype=jnp.int32)))
    tabs_s = tuple(jnp.tile(t, (bs, 1)) for t in _rope_tables(past_len + jnp.arange(ss, dtype=jnp.int32)))
    hp = x_prompt.reshape(bp * sp, d_model).astype(F32)
    hs = x_sample.reshape(bs * ss, d_model).astype(F32)
    new_p, new_s = [], []
    for l in range(depth):
        w = _layer_weights(l, a)
        past = {"mla_ckv": cache_mla_ckv[l], "mla_kpe": cache_mla_kpe[l], "fox_k": cache_fox_k[l],
                "fox_v": cache_fox_v[l], "fox_logf": cache_fox_logf[l], "ssm": state_ssm[l], "conv": state_conv[l]}
        fg = final_norm if l == depth - 1 else None
        hp, st_p = _layer(hp, bp, sp, w, tabs_p, None, fg)
        hs, st_s = _layer(hs, bs, ss, w, tabs_s, past, fg)
        new_p.append(st_p)
        new_s.append(st_s)
    y_prompt = hp.reshape(bp, sp, d_model)
    y_sample = hs.reshape(bs, ss, d_model)
    stk = lambda states, i: jnp.stack([st[i] for st in states], axis=0)
    return (y_prompt, y_sample) + tuple(stk(new_p, i) for i in range(7)) + tuple(stk(new_s, i) for i in range(7))
```

```python
import functools
import math

import jax
import jax.numpy as jnp
from jax import lax
from jax.experimental import pallas as pl
from jax.experimental.pallas import tpu as pltpu

F32 = jnp.float32
BF16 = jnp.bfloat16

EPS = 1e-6
CHUNK = 64
FFN_RES = 0.5
MLA_HEADS, MLA_NOPE, MLA_ROPE, MLA_V = 8, 128, 64, 128
MLA_SCALE = (MLA_NOPE + MLA_ROPE) ** -0.5
ROPE_BASE = 10000.0
SSM_HEADS, SSM_HEAD_DIM, SSM_GROUPS, SSM_STATE, SSM_CONV_W = 16, 64, 2, 128, 4
SSM_D_INNER = SSM_HEADS * SSM_HEAD_DIM
FOX_HEADS, FOX_HEAD_DIM = 8, 128
FOX_SCALE = FOX_HEAD_DIM ** -0.5
N_BRANCH = 3

LANES = 128
SUBLANES = 8
MXU_DIM = 256
VMEM_LIMIT = 56 * 1024 * 1024

MLA_QK_PAD = MXU_DIM
INPROJ_TN = 512
NEG_BIG = -1e30
LOG2E = math.log2(math.e)
HI = lax.Precision.HIGHEST


def _tile(n, pref, align):
    t = (min(pref, n) // align) * align
    while t >= align:
        if n % t == 0:
            return t
        t -= align
    return n


def _params(*sem):
    return pltpu.CompilerParams(dimension_semantics=sem, vmem_limit_bytes=VMEM_LIMIT)


def _rms(x, g):
    return x * lax.rsqrt(jnp.mean(x * x, axis=-1, keepdims=True) + EPS) * g


def _softplus(x):
    return jnp.maximum(x, 0.0) + jnp.log1p(jnp.exp(-jnp.abs(x)))


def _rope_lanes(pe, cos, s1, s2):
    half = MLA_ROPE // 2
    return pe * cos + pltpu.roll(pe, LANES - half, 1) * s1 + pltpu.roll(pe, half, 1) * s2


def _mm_kernel(*refs, prologue, rope, residual, tn):
    it = iter(refs)
    x_ref = next(it)
    g_ref = next(it) if prologue == "rms" else None
    w_ref = next(it)
    res_ref = next(it) if residual else None
    tabs = (next(it), next(it), next(it)) if rope else None
    o_ref = next(it)
    xn_ref = next(it) if prologue != "none" else None

    if prologue == "none":
        lhs = x_ref[...]
    else:
        @pl.when(pl.program_id(1) == 0)
        def _():
            x = x_ref[...].astype(F32)
            if prologue == "rms":
                x = _rms(x, g_ref[...])
            xn_ref[...] = x.astype(BF16)
        lhs = xn_ref[...]
    acc = jnp.dot(lhs, w_ref[...], preferred_element_type=F32)
    if residual:
        acc = res_ref[...] + acc
    if rope:
        cos, s1, s2 = (t[...] for t in tabs)
        for c in range(tn // MLA_QK_PAD):
            a = c * MLA_QK_PAD
            o_ref[:, a:a + LANES] = acc[:, a:a + LANES].astype(o_ref.dtype)
            o_ref[:, a + LANES:a + 2 * LANES] = _rope_lanes(acc[:, a + LANES:a + 2 * LANES], cos, s1, s2).astype(o_ref.dtype)
    else:
        o_ref[...] = acc.astype(o_ref.dtype)


def _mm(x, w, *, out_dtype, tm=1024, tn=512, prologue="none", gain=None, x_col=0, residual=None, rope_tabs=None,
        name="mm"):
    m = x.shape[0]
    k, n = w.shape
    assert x.shape[1] % k == 0 and (prologue != "none" or x.dtype == BF16)
    tm = _tile(m, tm, SUBLANES)
    tn = _tile(n, tn, MLA_QK_PAD if rope_tabs is not None else LANES)
    grid = (m // tm, n // tn)
    in_specs = [pl.BlockSpec((tm, k), lambda i, j: (i, x_col))]
    args = [x]
    if prologue == "rms":
        in_specs.append(pl.BlockSpec((1, k), lambda i, j: (0, 0)))
        args.append(gain.reshape(1, k).astype(F32))
    in_specs.append(pl.BlockSpec((k, tn), lambda i, j: (0, j)))
    args.append(w)
    if residual is not None:
        in_specs.append(pl.BlockSpec((tm, tn), lambda i, j: (i, j)))
        args.append(residual)
    if rope_tabs is not None:
        for t in rope_tabs:
            in_specs.append(pl.BlockSpec((tm, LANES), lambda i, j: (i, 0)))
            args.append(t)
    scratch = [pltpu.VMEM((tm, k), BF16)] if prologue != "none" else []
    kern = functools.partial(_mm_kernel, prologue=prologue, rope=rope_tabs is not None,
                             residual=residual is not None, tn=tn)
    return pl.pallas_call(
        kern, grid=grid, in_specs=in_specs,
        out_specs=pl.BlockSpec((tm, tn), lambda i, j: (i, j)),
        out_shape=jax.ShapeDtypeStruct((m, n), out_dtype),
        scratch_shapes=scratch, compiler_params=_params("parallel", "arbitrary"), name=name,
    )(*args)


def _inproj_kernel(xn_ref, w_ref, *outs, groups):
    j = pl.program_id(1)
    acc = jnp.dot(xn_ref[...], w_ref[...], preferred_element_type=F32)
    for (lo, hi, width), o_ref in zip(groups, outs):
        @pl.when((j >= lo) & (j < hi))
        def _(o_ref=o_ref, width=width):
            o_ref[...] = acc[:, :width].astype(o_ref.dtype)


def _inproj_layout(widths, tn):
    groups, start = [], 0
    for n in widths:
        nt = -(-n // tn)
        assert n % tn == 0 or n < tn
        groups.append((start, start + nt, min(n, tn)))
        start += nt
    return groups, start


def _inproj(xn, w_cat, widths, dtypes, *, tm=1024, tn=512):
    m, d = xn.shape
    tm = _tile(m, tm, SUBLANES)
    groups, n_tiles = _inproj_layout(widths, tn)
    assert w_cat.shape == (d, n_tiles * tn)
    out_specs = [pl.BlockSpec((tm, bw), functools.partial(lambda i, j, lo, hi: (i, jnp.clip(j - lo, 0, hi - lo - 1)),
                                                           lo=lo, hi=hi)) for lo, hi, bw in groups]
    out_shape = [jax.ShapeDtypeStruct((m, n), dt) for n, dt in zip(widths, dtypes)]
    return pl.pallas_call(
        functools.partial(_inproj_kernel, groups=groups), grid=(m // tm, n_tiles),
        in_specs=[pl.BlockSpec((tm, d), lambda i, j: (i, 0)),
                  pl.BlockSpec((d, tn), lambda i, j: (0, j))],
        out_specs=out_specs, out_shape=out_shape,
        compiler_params=_params("parallel", "arbitrary"), name="inproj",
    )(xn, w_cat)


def _ffn_kernel(*refs, nf, tf, f, post):
    x_ref, g_ref, wg_ref, wu_ref, wd_ref = refs[:5]
    refs = refs[5:]
    pg_ref = None
    if post is not None:
        pg_ref, refs = refs[0], refs[1:]
    outs, (xn_ref, acc_ref) = refs[:-2], refs[-2:]
    j = pl.program_id(1)

    @pl.when(j == 0)
    def _():
        xn_ref[...] = _rms(x_ref[...], g_ref[...]).astype(BF16)
        acc_ref[...] = jnp.zeros_like(acc_ref)

    xn = xn_ref[...]
    a = jnp.dot(xn, wg_ref[...], preferred_element_type=F32)
    b = jnp.dot(xn, wu_ref[...], preferred_element_type=F32)
    h = a * jax.nn.sigmoid(a) * b
    wd = wd_ref[...]
    if f % tf:
        valid = f - j * tf
        h = jnp.where(lax.broadcasted_iota(jnp.int32, h.shape, 1) < valid, h, 0.0)
        wd = jnp.where(lax.broadcasted_iota(jnp.int32, wd.shape, 0) < valid, wd, jnp.zeros_like(wd))
    acc_ref[...] += jnp.dot(h.astype(BF16), wd, preferred_element_type=F32)

    @pl.when(j == nf - 1)
    def _():
        y = x_ref[...] + FFN_RES * acc_ref[...]
        if post is None:
            outs[0][...] = y
        elif post == "norm_bf16":
            outs[0][...] = y
            outs[1][...] = _rms(y, pg_ref[...]).astype(BF16)
        else:
            outs[0][...] = _rms(y, pg_ref[...])


def _ffn(x, gain, wg, wu, wd, *, post=None, post_gain=None, tm=512, tf=512):
    m, d = x.shape
    f = wg.shape[1]
    tm = _tile(m, tm, SUBLANES)
    nf = pl.cdiv(f, tf)
    row = pl.BlockSpec((tm, d), lambda i, j: (i, 0))
    vec = pl.BlockSpec((1, d), lambda i, j: (0, 0))
    in_specs = [row, vec,
                pl.BlockSpec((d, tf), lambda i, j: (0, j)),
                pl.BlockSpec((d, tf), lambda i, j: (0, j)),
                pl.BlockSpec((tf, d), lambda i, j: (j, 0))]
    args = [x, gain.reshape(1, d).astype(F32), wg, wu, wd]
    out_specs, out_shape = [row], [jax.ShapeDtypeStruct((m, d), F32)]
    if post is not None:
        in_specs.append(vec)
        args.append(post_gain.reshape(1, d).astype(F32))
    if post == "norm_bf16":
        out_specs.append(row)
        out_shape.append(jax.ShapeDtypeStruct((m, d), BF16))
    return pl.pallas_call(
        functools.partial(_ffn_kernel, nf=nf, tf=tf, f=f, post=post), grid=(m // tm, nf),
        in_specs=in_specs, out_specs=out_specs, out_shape=out_shape,
        scratch_shapes=[pltpu.VMEM((tm, d), BF16), pltpu.VMEM((tm, d), F32)],
        compiler_params=_params("parallel", "arbitrary"), name="ffn",
    )(*args)


def _prep_kernel(uc_ref, g_ref, us_ref, cos_ref, s1_ref, s2_ref, fb_ref, ckv_ref, sm_ref, *, ff_lo, ff_hi):
    ckv_ref[...] = _rms(uc_ref[...], g_ref[...])
    us = us_ref[...]
    lane = lax.broadcasted_iota(jnp.int32, us.shape, 1)
    pe = jnp.where(lane < MLA_ROPE, us, 0.0)
    rot = _rope_lanes(pe, cos_ref[...], s1_ref[...], s2_ref[...])
    logf = -_softplus(-(us + fb_ref[...]))
    sm_ref[...] = jnp.where((lane >= ff_lo) & (lane < ff_hi), logf, rot)


def _prep(u_ckv, kv_gain, u_small, tabs, fb_lanes, *, ff_lo, ff_hi, tm=1024):
    m = u_small.shape[0]
    kv = kv_gain.shape[0]
    tm = _tile(m, tm, SUBLANES)
    row = lambda i: (i, 0)
    return pl.pallas_call(
        functools.partial(_prep_kernel, ff_lo=ff_lo, ff_hi=ff_hi), grid=(m // tm,),
        in_specs=[pl.BlockSpec((tm, kv), row),
                  pl.BlockSpec((1, kv), lambda i: (0, 0)),
                  pl.BlockSpec((tm, LANES), row), pl.BlockSpec((tm, LANES), row),
                  pl.BlockSpec((tm, LANES), row), pl.BlockSpec((tm, LANES), row),
                  pl.BlockSpec((1, LANES), lambda i: (0, 0))],
        out_specs=[pl.BlockSpec((tm, kv), row), pl.BlockSpec((tm, LANES), row)],
        out_shape=[jax.ShapeDtypeStruct((m, kv), F32), jax.ShapeDtypeStruct((m, LANES), F32)],
        compiler_params=_params("parallel"), name="prep",
    )(u_ckv, kv_gain.reshape(1, kv).astype(F32), u_small, *tabs, fb_lanes)


def _cumsum_kernel(x_ref, o_ref, carry_ref, *, tc, mult):
    @pl.when(pl.program_id(1) == 0)
    def _():
        carry_ref[...] = jnp.zeros_like(carry_ref)

    r = lax.broadcasted_iota(jnp.int32, (tc, tc), 0)
    c = lax.broadcasted_iota(jnp.int32, (tc, tc), 1)
    upper = (r <= c).astype(F32)
    y = jnp.dot(x_ref[0], upper, preferred_element_type=F32, precision=HI) + carry_ref[:, :1]
    o_ref[0] = y * mult
    carry_ref[...] = jnp.broadcast_to(y[:, tc - 1:tc], carry_ref.shape)


def _cumsum_last(x, mult, *, tc=512):
    b, h, s = x.shape
    tc = _tile(s, tc, LANES)
    return pl.pallas_call(
        functools.partial(_cumsum_kernel, tc=tc, mult=mult), grid=(b, s // tc),
        in_specs=[pl.BlockSpec((1, h, tc), lambda i, j: (i, 0, j))],
        out_specs=pl.BlockSpec((1, h, tc), lambda i, j: (i, 0, j)),
        out_shape=jax.ShapeDtypeStruct((b, h, s), F32),
        scratch_shapes=[pltpu.VMEM((h, LANES), F32)],
        compiler_params=_params("parallel", "arbitrary"), name="cumsum",
    )(x)


def _last_visible(q_end, mode):
    if mode == "chunk":
        return (q_end // CHUNK) * CHUNK + (CHUNK - 1)
    return q_end


def _attn_kernel(*refs, heads, dq, dk, dv, tq, tk, nk, scale, mode, q_off, n_valid, has_bias, has_shared):
    it = iter(refs)
    qi_ref, ki_ref = next(it), next(it)
    q_ref, k_ref, v_ref = next(it), next(it), next(it)
    ks_ref = next(it) if has_shared else None
    b_ref = next(it) if has_bias else None
    o_ref, m_ref, acc_ref = next(it), next(it), next(it)
    t = pl.program_id(1)
    qi, ki = qi_ref[t], ki_ref[t]
    nch = tk // LANES
    c = scale * LOG2E
    aw = dv + LANES

    @pl.when(ki == 0)
    def _():
        m_ref[...] = jnp.full_like(m_ref, NEG_BIG)
        acc_ref[...] = jnp.zeros_like(acc_ref)

    q_lo = q_off + qi * tq
    k_lo = ki * tk
    last_tile = jnp.minimum(_last_visible(q_lo + (tq - 1), mode) // tk, nk - 1)
    first_maskable = _last_visible(q_lo, mode) + 1
    ones_col = (lax.broadcasted_iota(jnp.int32, (tk, LANES), 1) == 0).astype(BF16)

    def body(masked):
        if masked:
            qpos = q_lo + lax.broadcasted_iota(jnp.int32, (tq, tk), 0)
            kpos = k_lo + lax.broadcasted_iota(jnp.int32, (tq, tk), 1)
            if mode == "chunk":
                sh = CHUNK.bit_length() - 1
                vis = lax.shift_right_logical(kpos, sh) <= lax.shift_right_logical(qpos, sh)
            else:
                vis = kpos <= qpos
            vis = vis & (kpos < n_valid)
        for h in range(heads):
            q = q_ref[0, :, h * dq:(h + 1) * dq]
            k = k_ref[0, :, h * dk:(h + 1) * dk]
            if has_shared:
                k = jnp.concatenate([k, ks_ref[0]], axis=1)
            s = lax.dot_general(q, k, (((1,), (1,)), ((), ())), preferred_element_type=F32)
            if has_bias:
                s = s + b_ref[0, h:h + 1, :]
            if masked:
                s = jnp.where(vis, s, NEG_BIG)
            m_prev = m_ref[h]
            mc = s[:, :LANES]
            for j in range(1, nch):
                mc = jnp.maximum(mc, s[:, j * LANES:(j + 1) * LANES])
            m_new = jnp.maximum(m_prev, jnp.max(mc, axis=1, keepdims=True))
            m_ref[h] = m_new
            alpha = jnp.exp2((m_prev - m_new) * c)
            p = jnp.concatenate([jnp.exp2((s[:, j * LANES:(j + 1) * LANES] - m_new) * c).astype(BF16)
                                 for j in range(nch)], axis=1)
            vx = jnp.concatenate([v_ref[0, :, h * dv:(h + 1) * dv], ones_col], axis=1)
            pv = jnp.dot(p, vx, preferred_element_type=F32)
            for a0 in range(h * aw, (h + 1) * aw, LANES):
                acc_ref[:, a0:a0 + LANES] = alpha * acc_ref[:, a0:a0 + LANES] + pv[:, a0 - h * aw:a0 - h * aw + LANES]

    need_mask = (k_lo + (tk - 1) >= first_maskable) | (k_lo + tk > n_valid)

    @pl.when(need_mask)
    def _():
        body(True)

    @pl.when(jnp.logical_not(need_mask))
    def _():
        body(False)

    @pl.when(ki == last_tile)
    def _():
        for h in range(heads):
            l = acc_ref[:, h * aw + dv:h * aw + dv + 1]
            o_ref[0, :, h * dv:(h + 1) * dv] = (acc_ref[:, h * aw:h * aw + dv] / l).astype(o_ref.dtype)


def _attention(q, k, v, k_shared, bias, *, heads, dq, dk, dv, scale, mode, q_off, n_valid, k_col=0, v_col=0,
               tq=512, tk=1024):
    b, sq, _ = q.shape
    sk = k.shape[1]
    assert dk + (0 if k_shared is None else k_shared.shape[2]) == dq
    tq = _tile(sq, tq, SUBLANES)
    tk = sk if sk <= 2 * tk else _tile(sk, tk, LANES)
    nq, nk = sq // tq, sk // tk
    pairs = [(i, j) for i in range(nq)
             for j in range(min(_last_visible(q_off + i * tq + (tq - 1), mode) // tk, nk - 1) + 1)]
    qi_arr = jnp.asarray([p[0] for p in pairs], jnp.int32)
    ki_arr = jnp.asarray([p[1] for p in pairs], jnp.int32)

    in_specs = [pl.BlockSpec((1, tq, heads * dq), lambda bi, t, qi, ki: (bi, qi[t], 0)),
                pl.BlockSpec((1, tk, heads * dk), lambda bi, t, qi, ki: (bi, ki[t], k_col)),
                pl.BlockSpec((1, tk, heads * dv), lambda bi, t, qi, ki: (bi, ki[t], v_col))]
    args = [q, k, v]
    if k_shared is not None:
        in_specs.append(pl.BlockSpec((1, tk, dq - dk), lambda bi, t, qi, ki: (bi, ki[t], 0)))
        args.append(k_shared)
    if bias is not None:
        in_specs.append(pl.BlockSpec((1, heads, tk), lambda bi, t, qi, ki: (bi, 0, ki[t])))
        args.append(bias)
    kern = functools.partial(_attn_kernel, heads=heads, dq=dq, dk=dk, dv=dv, tq=tq, tk=tk, nk=nk, scale=scale,
                             mode=mode, q_off=q_off, n_valid=n_valid, has_bias=bias is not None,
                             has_shared=k_shared is not None)
    return pl.pallas_call(
        kern,
        grid_spec=pltpu.PrefetchScalarGridSpec(
            num_scalar_prefetch=2, grid=(b, len(pairs)), in_specs=in_specs,
            out_specs=pl.BlockSpec((1, tq, heads * dv), lambda bi, t, qi, ki: (bi, qi[t], 0)),
            scratch_shapes=[pltpu.VMEM((heads, tq, LANES), F32), pltpu.VMEM((tq, heads * (dv + LANES)), F32)]),
        out_shape=jax.ShapeDtypeStruct((b, sq, heads * dv), BF16),
        compiler_params=_params("parallel", "arbitrary"), name="attn_" + mode,
    )(qi_arr, ki_arr, *args)


def _decode_kernel(*refs, heads, dq, dk, dv, sq, p_len, scale, mode, has_shared, has_bias):
    it = iter(refs)
    q_ref, kp_ref, vp_ref, kn_ref, vn_ref = (next(it) for _ in range(5))
    ksp_ref, ksn_ref = (next(it), next(it)) if has_shared else (None, None)
    bp_ref, bn_ref = (next(it), next(it)) if has_bias else (None, None)
    o_ref = next(it)
    c = scale * LOG2E

    def head(ref, h, d):
        x = ref[:, h, :] if len(ref.shape) == 3 else ref[:, h * d:(h + 1) * d]
        return x.astype(BF16)

    row = lax.broadcasted_iota(jnp.int32, (sq, sq), 0)
    col = lax.broadcasted_iota(jnp.int32, (sq, sq), 1)
    if mode == "chunk":
        sh = CHUNK.bit_length() - 1
        vis = lax.shift_right_logical(p_len + col, sh) <= lax.shift_right_logical(p_len + row, sh)
    else:
        vis = col <= row
    if has_shared:
        pad = dq - dk - ksp_ref.shape[1]
        ksp = jnp.concatenate([ksp_ref[...].astype(BF16), jnp.zeros((p_len, pad), BF16)], axis=1)
        ksn = jnp.concatenate([ksn_ref[...].astype(BF16), jnp.zeros((sq, pad), BF16)], axis=1)
    contract_last = (((1,), (1,)), ((), ()))
    for h in range(heads):
        q = q_ref[:, h * dq:(h + 1) * dq]
        kp, kn = head(kp_ref, h, dk), head(kn_ref, h, dk)
        if has_shared:
            kp = jnp.concatenate([kp, ksp], axis=1)
            kn = jnp.concatenate([kn, ksn], axis=1)
        s_p = lax.dot_general(q, kp, contract_last, preferred_element_type=F32)
        s_n = lax.dot_general(q, kn, contract_last, preferred_element_type=F32)
        if has_bias:
            s_p = s_p + bp_ref[h:h + 1, :]
            s_n = s_n + bn_ref[h:h + 1, :]
        s_n = jnp.where(vis, s_n, NEG_BIG)
        m = jnp.maximum(jnp.max(s_p, axis=1, keepdims=True), jnp.max(s_n, axis=1, keepdims=True))
        p_p = jnp.exp2((s_p - m) * c)
        p_n = jnp.exp2((s_n - m) * c)
        l = jnp.sum(p_p, axis=1, keepdims=True) + jnp.sum(p_n, axis=1, keepdims=True)
        pv = (jnp.dot(p_p.astype(BF16), head(vp_ref, h, dv), preferred_element_type=F32)
              + jnp.dot(p_n.astype(BF16), head(vn_ref, h, dv), preferred_element_type=F32))
        o_ref[:, h * dv:(h + 1) * dv] = (pv / l).astype(o_ref.dtype)


def _decode_attention(q, past_kv, new_kv, shared, bias, *, heads, dq, dk, dv, scale, mode):
    b, sq, _ = q.shape

    def spec(arr, prefix, colblk, width):
        inner = arr.shape[len(prefix) + 1:]
        if len(inner) == 3:
            block, idx = inner, (0, 0, 0)
        else:
            block, idx = (inner[0], width), (0, colblk)
        return pl.BlockSpec((None,) * (len(prefix) + 1) + tuple(block), lambda i: tuple(prefix) + (i,) + idx)

    (kp, kp_pre, kp_col), (vp, vp_pre, vp_col) = past_kv
    (kn, kn_pre, kn_col), (vn, vn_pre, vn_col) = new_kv
    p_len = kp.shape[len(kp_pre) + 1]
    in_specs = [pl.BlockSpec((None, sq, heads * dq), lambda i: (i, 0, 0)),
                spec(kp, kp_pre, kp_col, heads * dk), spec(vp, vp_pre, vp_col, heads * dv),
                spec(kn, kn_pre, kn_col, heads * dk), spec(vn, vn_pre, vn_col, heads * dv)]
    args = [q, kp, vp, kn, vn]
    if shared is not None:
        for arr, pre in shared:
            in_specs.append(spec(arr, pre, 0, arr.shape[-1]))
            args.append(arr)
    if bias is not None:
        for arr in bias:
            in_specs.append(pl.BlockSpec((None,) + arr.shape[1:], lambda i: (i, 0, 0)))
            args.append(arr)
    kern = functools.partial(_decode_kernel, heads=heads, dq=dq, dk=dk, dv=dv, sq=sq, p_len=p_len, scale=scale,
                             mode=mode, has_shared=shared is not None, has_bias=bias is not None)
    return pl.pallas_call(
        kern, grid=(b,), in_specs=in_specs,
        out_specs=pl.BlockSpec((None, sq, heads * dv), lambda i: (i, 0, 0)),
        out_shape=jax.ShapeDtypeStruct((b, sq, heads * dv), BF16),
        compiler_params=_params("parallel"), name="decode_" + mode,
    )(*args)


def _ssd_kernel(z_ref, xbc_ref, dt_ref, dtt_ref, cw_ref, cb_ref, dtb_ref, dtbt_ref, al_ref, alt_ref, dx_ref, nw_ref,
                h0_ref, c0_ref, y_ref, hout_ref, state_ref, carry_ref, *, lc, nc):
    c = pl.program_id(1)
    gw = SSM_D_INNER // SSM_GROUPS
    hpg = SSM_HEADS // SSM_GROUPS
    halo = SUBLANES

    @pl.when(c == 0)
    def _():
        state_ref[...] = h0_ref[0]
        carry_ref[...] = c0_ref[0]

    x = xbc_ref[0]
    cat = jnp.concatenate([carry_ref[...], x], axis=0)
    conv = cb_ref[...]
    for kk in range(SSM_CONV_W):
        shift = SSM_CONV_W - 1 - kk
        src = pltpu.roll(cat, shift, 0) if shift else cat
        conv = conv + src[halo:, :] * cw_ref[kk:kk + 1, :]
    carry_ref[...] = x[lc - halo:, :]
    act = conv * jax.nn.sigmoid(conv)
    xs = act[:, :SSM_D_INNER]
    bm = act[:, SSM_D_INNER:SSM_D_INNER + SSM_GROUPS * SSM_STATE]
    cm = act[:, SSM_D_INNER + SSM_GROUPS * SSM_STATE:]

    dt = _softplus(dt_ref[0] + dtb_ref[...])
    dtt = _softplus(dtt_ref[0] + dtbt_ref[...])
    adt = dt * (-jnp.exp(al_ref[...]))
    adtt = dtt * (-jnp.exp(alt_ref[...]))
    r = lax.broadcasted_iota(jnp.int32, (lc, lc), 0)
    cc = lax.broadcasted_iota(jnp.int32, (lc, lc), 1)
    tril = cc <= r
    acs = jnp.dot(tril.astype(F32), adt, preferred_element_type=F32, precision=HI)
    acst = jnp.dot(adtt, (r <= cc).astype(F32), preferred_element_type=F32, precision=HI)
    hh = lax.broadcasted_iota(jnp.int32, (SSM_HEADS, SSM_D_INNER), 0)
    ll = lax.broadcasted_iota(jnp.int32, (SSM_HEADS, SSM_D_INNER), 1)
    expand = ((ll >= hh * SSM_HEAD_DIM) & (ll < (hh + 1) * SSM_HEAD_DIM)).astype(F32)
    dt_x = jnp.dot(dt, expand, preferred_element_type=F32, precision=HI)
    acs_x = jnp.dot(acs, expand, preferred_element_type=F32, precision=HI)
    tot_x = acs_x[lc - 1:lc, :]
    xdt = xs * dt_x
    xdt_b = xdt.astype(BF16)
    w_end = (xdt * jnp.exp(tot_x - acs_x)).astype(BF16)
    state = state_ref[...]
    state_b = state.astype(BF16)

    y_parts, new_parts = [], []
    for g in range(SSM_GROUPS):
        bg = bm[:, g * SSM_STATE:(g + 1) * SSM_STATE].astype(BF16)
        cg = cm[:, g * SSM_STATE:(g + 1) * SSM_STATE].astype(BF16)
        cb = lax.dot_general(cg, bg, (((1,), (1,)), ((), ())), preferred_element_type=F32)
        for hl in range(hpg):
            h = g * hpg + hl
            seg = acs[:, h:h + 1] - acst[h:h + 1, :]
            mh = (cb * jnp.exp(jnp.where(tril, seg, NEG_BIG))).astype(BF16)
            y_parts.append(jnp.dot(mh, xdt_b[:, h * SSM_HEAD_DIM:(h + 1) * SSM_HEAD_DIM], preferred_element_type=F32))
        new_parts.append(lax.dot_general(bg, w_end[:, g * gw:(g + 1) * gw], (((0,), (0,)), ((), ())),
                                         preferred_element_type=F32))
    y_off = jnp.concatenate(
        [jnp.dot(cm[:, g * SSM_STATE:(g + 1) * SSM_STATE].astype(BF16), state_b[:, g * gw:(g + 1) * gw],
                 preferred_element_type=F32) for g in range(SSM_GROUPS)], axis=1) * jnp.exp(acs_x)
    y = jnp.concatenate(y_parts, axis=1) + y_off + dx_ref[...] * xs
    state_ref[...] = jnp.exp(tot_x) * state + jnp.concatenate(new_parts, axis=1)

    zz = z_ref[0]
    y = y * (zz * jax.nn.sigmoid(zz))
    for g in range(SSM_GROUPS):
        y_ref[0, :, g * gw:(g + 1) * gw] = _rms(y[:, g * gw:(g + 1) * gw], nw_ref[:, g * gw:(g + 1) * gw]).astype(y_ref.dtype)

    @pl.when(c == nc - 1)
    def _():
        hout_ref[0] = state_ref[...]


def _ssd(z, xbc, dt, dtt, p, h0, c0, *, lc=256):
    b, s, cd = xbc.shape
    lc = _tile(s, lc, LANES) if s % LANES == 0 else s
    nc = s // lc
    hh = SSM_HEADS
    full2 = lambda shape: pl.BlockSpec(shape, lambda i, j: (0, 0))
    return pl.pallas_call(
        functools.partial(_ssd_kernel, lc=lc, nc=nc), grid=(b, nc),
        in_specs=[pl.BlockSpec((1, lc, SSM_D_INNER), lambda i, j: (i, j, 0)),
                  pl.BlockSpec((1, lc, cd), lambda i, j: (i, j, 0)),
                  pl.BlockSpec((1, lc, hh), lambda i, j: (i, j, 0)),
                  pl.BlockSpec((1, hh, lc), lambda i, j: (i, 0, j)),
                  full2((SSM_CONV_W, cd)), full2((1, cd)),
                  full2((1, hh)), full2((hh, 1)), full2((1, hh)), full2((hh, 1)),
                  full2((1, SSM_D_INNER)), full2((1, SSM_D_INNER)),
                  pl.BlockSpec((1, SSM_STATE, SSM_D_INNER), lambda i, j: (i, 0, 0)),
                  pl.BlockSpec((1, SUBLANES, cd), lambda i, j: (i, 0, 0))],
        out_specs=[pl.BlockSpec((1, lc, SSM_D_INNER), lambda i, j: (i, j, 0)),
                   pl.BlockSpec((1, SSM_STATE, SSM_D_INNER), lambda i, j: (i, 0, 0))],
        out_shape=[jax.ShapeDtypeStruct((b, s, SSM_D_INNER), BF16),
                   jax.ShapeDtypeStruct((b, SSM_STATE, SSM_D_INNER), F32)],
        scratch_shapes=[pltpu.VMEM((SSM_STATE, SSM_D_INNER), F32), pltpu.VMEM((SUBLANES, cd), F32)],
        compiler_params=_params("parallel", "arbitrary"), name="ssd",
    )(z, xbc, dt, dtt, p["conv_w"], p["conv_b"], p["dt_b"], p["dt_bt"], p["a_log"], p["a_logt"], p["d_x"], p["norm_w"],
      h0, c0)


def _merge_kernel(xn_ref, o0, o1, o2, w0, w1, w2, wg0, wg1, wg2, out_ref):
    xn = xn_ref[...]
    acc = None
    for o_ref, w_ref, wg_ref in ((o0, w0, wg0), (o1, w1, wg1), (o2, w2, wg2)):
        gate = jax.nn.sigmoid(jnp.dot(xn, wg_ref[...], preferred_element_type=F32))
        t = gate * jnp.dot(o_ref[...], w_ref[...], preferred_element_type=F32)
        acc = t if acc is None else acc + t
    out_ref[...] = acc.astype(out_ref.dtype)


def _merge(xn, o_list, w_list, w_gate, *, tm=1024, tn=512):
    m, d = xn.shape
    tm = _tile(m, tm, SUBLANES)
    tn = _tile(d, tn, LANES)
    nb = d // tn
    in_specs = [pl.BlockSpec((tm, d), lambda i, j: (i, 0))]
    in_specs += [pl.BlockSpec((tm, o.shape[1]), lambda i, j: (i, 0)) for o in o_list]
    in_specs += [pl.BlockSpec((w.shape[0], tn), lambda i, j: (0, j)) for w in w_list]
    in_specs += [pl.BlockSpec((d, tn), functools.partial(lambda i, j, br: (0, br * nb + j), br=br))
                 for br in range(N_BRANCH)]
    return pl.pallas_call(
        _merge_kernel, grid=(m // tm, nb), in_specs=in_specs,
        out_specs=pl.BlockSpec((tm, tn), lambda i, j: (i, j)),
        out_shape=jax.ShapeDtypeStruct((m, d), BF16),
        compiler_params=_params("parallel", "arbitrary"), name="merge",
    )(xn, *o_list, *w_list, w_gate, w_gate, w_gate)


def _layer_weights(l, a):
    d_model = a["w_in"].shape[1]
    q_lora, kv_lora = a["mla_q_norm"].shape[1], a["mla_kv_norm"].shape[1]
    conv_dim = a["ssm_conv_w"].shape[2]
    sizes = (q_lora, kv_lora, MLA_ROPE, SSM_D_INNER, conv_dim, SSM_HEADS,
             FOX_HEADS * FOX_HEAD_DIM, FOX_HEADS * FOX_HEAD_DIM, FOX_HEADS * FOX_HEAD_DIM, FOX_HEADS,
             N_BRANCH * d_model)
    w_in = a["w_in"][l]
    assert w_in.shape[1] == sum(sizes)
    cols, start = [], 0
    for n in sizes:
        cols.append(w_in[:, start:start + n])
        start += n
    w_q, w_ckv, w_kpe, w_z, w_xbc, w_dt, w_fq, w_fk, w_fv, w_ff, w_gate = cols
    bf = lambda t: t.astype(BF16)
    small = [w_kpe, w_dt, w_ff]
    n_small = sum(t.shape[1] for t in small)
    assert n_small <= LANES
    parts = [w_q, w_ckv, *small, jnp.zeros((d_model, INPROJ_TN - n_small), w_in.dtype), w_z, w_xbc, w_fq, w_fk, w_fv]
    w = {"in_cat": bf(jnp.concatenate(parts, axis=1)), "in_gate": bf(w_gate),
         "in_widths": (q_lora, kv_lora, LANES, SSM_D_INNER, conv_dim) + (FOX_HEADS * FOX_HEAD_DIM,) * 3}
    wq = a["mla_w_uq"][l].reshape(q_lora, MLA_HEADS, MLA_NOPE + MLA_ROPE)
    wq = jnp.pad(wq, ((0, 0), (0, 0), (0, MLA_QK_PAD - MLA_NOPE - MLA_ROPE)))
    w["uq"] = bf(wq.reshape(q_lora, MLA_HEADS * MLA_QK_PAD))
    wkv = a["mla_w_ukv"][l].reshape(kv_lora, MLA_HEADS, MLA_NOPE + MLA_V)
    w["ukv"] = bf(jnp.concatenate([wkv[:, :, :MLA_NOPE].reshape(kv_lora, MLA_HEADS * MLA_NOPE),
                                   wkv[:, :, MLA_NOPE:].reshape(kv_lora, MLA_HEADS * MLA_V)], axis=1))
    for nm in ("w_br_mla", "w_br_ssd", "w_br_fox", "w_out"):
        w[nm] = bf(a[nm][l])
    for pre in ("ffn1", "ffn2"):
        w[pre + "_g"] = bf(a[pre + "_w_gate"])[l]
        w[pre + "_u"] = bf(a[pre + "_w_up"])[l]
        w[pre + "_d"] = bf(a[pre + "_w_down"])[l]
        w[pre + "_norm"] = a[pre + "_norm"][l]
    w["mix_norm"] = a["mix_norm"][l]
    w["q_norm"], w["kv_norm"] = a["mla_q_norm"][l], a["mla_kv_norm"][l]
    ff_lo = MLA_ROPE + SSM_HEADS
    w["fb_lanes"] = jnp.pad(a["fox_b_f"][l].astype(F32), (ff_lo, LANES - ff_lo - FOX_HEADS)).reshape(1, LANES)
    w["ssd"] = {
        "conv_w": a["ssm_conv_w"][l].astype(F32), "conv_b": a["ssm_conv_b"][l].astype(F32).reshape(1, conv_dim),
        "dt_b": a["ssm_dt_bias"][l].astype(F32).reshape(1, SSM_HEADS),
        "dt_bt": a["ssm_dt_bias"][l].astype(F32).reshape(SSM_HEADS, 1),
        "a_log": a["ssm_a_log"][l].astype(F32).reshape(1, SSM_HEADS),
        "a_logt": a["ssm_a_log"][l].astype(F32).reshape(SSM_HEADS, 1),
        "d_x": jnp.repeat(a["ssm_d"][l].astype(F32), SSM_HEAD_DIM).reshape(1, SSM_D_INNER),
        "norm_w": a["ssm_norm"][l].astype(F32).reshape(1, SSM_D_INNER),
    }
    return w


def _rope_tables(pos):
    half = MLA_ROPE // 2
    inv_freq = ROPE_BASE ** (-jnp.arange(half, dtype=F32) / half)
    ang = pos.astype(F32)[:, None] * inv_freq[None, :]
    cos, sin = jnp.cos(ang), jnp.sin(ang)
    z = jnp.zeros_like(cos)
    pad = jnp.zeros((pos.shape[0], LANES - MLA_ROPE), F32)
    return (jnp.concatenate([cos, cos, pad], axis=1),
            jnp.concatenate([-sin, z, pad], axis=1),
            jnp.concatenate([z, sin, pad], axis=1))


def _pad_keys(t, sk_pad):
    return jnp.pad(t, ((0, 0), (0, sk_pad - t.shape[1])) + ((0, 0),) * (t.ndim - 2))


def _layer(x, bsz, s, w, tabs, past, final_gain):
    m = bsz * s
    kv_lora = w["kv_norm"].shape[0]
    conv_dim = w["ssd"]["conv_w"].shape[1]
    x, xn = _ffn(x, w["ffn1_norm"], w["ffn1_g"], w["ffn1_u"], w["ffn1_d"], post="norm_bf16", post_gain=w["mix_norm"])

    u_q, u_ckv, u_small, u_z, u_xbc, fq, fk, fv = _inproj(
        xn, w["in_cat"], w["in_widths"], (F32, F32, F32, F32, F32, BF16, F32, F32), tn=INPROJ_TN)

    ff_lo = MLA_ROPE + SSM_HEADS
    ckv_new, small2 = _prep(u_ckv, w["kv_norm"], u_small, tabs, w["fb_lanes"], ff_lo=ff_lo, ff_hi=ff_lo + FOX_HEADS)
    kpe_new = small2[:, :MLA_ROPE]
    logf_new = small2[:, ff_lo:ff_lo + FOX_HEADS]
    u_dt = u_small[:, MLA_ROPE:ff_lo]

    if past is not None:
        caches, l = past
        past_len = caches["mla_ckv"].shape[2]
    ukv = functools.partial(_mm, w=w["ukv"], out_dtype=BF16, prologue="cast", tn=1024, name="mla_ukv")

    q_full = _mm(u_q, w["uq"], out_dtype=BF16, prologue="rms", gain=w["q_norm"], rope_tabs=tabs, name="mla_q")
    q_full = q_full.reshape(bsz, s, -1)
    kv_new = ukv(ckv_new).reshape(bsz, s, -1)
    mla = dict(heads=MLA_HEADS, dq=MLA_QK_PAD, dk=MLA_NOPE, dv=MLA_V, scale=MLA_SCALE, mode="chunk")
    sk_pad = -(-s // LANES) * LANES
    if past is None:
        kpe_pad = jnp.pad(kpe_new.astype(BF16).reshape(bsz, s, -1),
                          ((0, 0), (0, sk_pad - s), (0, MLA_QK_PAD - MLA_NOPE - MLA_ROPE)))
        kv_all = _pad_keys(kv_new, sk_pad)
        o_mla = _attention(q_full, kv_all, kv_all, kpe_pad, None, k_col=0, v_col=1, q_off=0, n_valid=s, **mla)
    else:
        kv_past = ukv(caches["mla_ckv"][l].reshape(bsz * past_len, kv_lora)).reshape(bsz, past_len, -1)
        o_mla = _decode_attention(
            q_full, ((kv_past, (), 0), (kv_past, (), 1)), ((kv_new, (), 0), (kv_new, (), 1)),
            ((caches["mla_kpe"], (l,)), (kpe_new.reshape(bsz, s, MLA_ROPE), ())), None, **mla)

    if past is None:
        conv_state = jnp.zeros((bsz, SSM_CONV_W - 1, conv_dim), F32)
        h0 = jnp.zeros((bsz, SSM_STATE, SSM_D_INNER), F32)
    else:
        conv_state = caches["conv"][l].astype(F32)
        h0 = jnp.transpose(caches["ssm"][l].astype(F32), (0, 3, 1, 2)).reshape(bsz, SSM_STATE, SSM_D_INNER)
    c0 = jnp.pad(conv_state, ((0, 0), (SUBLANES - (SSM_CONV_W - 1), 0), (0, 0)))
    xbc3 = u_xbc.reshape(bsz, s, conv_dim)
    dt3 = u_dt.reshape(bsz, s, SSM_HEADS)
    o_ssd, h_new = _ssd(u_z.reshape(bsz, s, SSM_D_INNER), xbc3, dt3, jnp.swapaxes(dt3, 1, 2), w["ssd"], h0, c0)
    ssm_new = jnp.transpose(h_new.reshape(bsz, SSM_STATE, SSM_HEADS, SSM_HEAD_DIM), (0, 2, 3, 1))
    keep = SSM_CONV_W - 1
    conv_new = xbc3[:, s - keep:] if s >= keep else jnp.concatenate([conv_state, xbc3], axis=1)[:, -keep:]

    hw = FOX_HEADS * FOX_HEAD_DIM
    fox = dict(heads=FOX_HEADS, dq=FOX_HEAD_DIM, dk=FOX_HEAD_DIM, dv=FOX_HEAD_DIM, scale=FOX_SCALE, mode="causal")
    logf_all = logf_new.reshape(bsz, s, FOX_HEADS)
    if past is not None:
        logf_all = jnp.concatenate([caches["fox_logf"][l].astype(F32), logf_all], axis=1)
    n_keys = logf_all.shape[1]
    neg_cum = _cumsum_last(jnp.swapaxes(_pad_keys(logf_all, -(-n_keys // LANES) * LANES), 1, 2), -1.0 / FOX_SCALE)
    fq3 = fq.reshape(bsz, s, hw)
    if past is None:
        o_fox = _attention(fq3, _pad_keys(fk.reshape(bsz, s, hw).astype(BF16), sk_pad),
                           _pad_keys(fv.reshape(bsz, s, hw).astype(BF16), sk_pad), None, neg_cum,
                           q_off=0, n_valid=s, **fox)
    else:
        o_fox = _decode_attention(
            fq3, ((caches["fox_k"], (l,), 0), (caches["fox_v"], (l,), 0)),
            ((fk.reshape(bsz, s, hw), (), 0), (fv.reshape(bsz, s, hw), (), 0)), None,
            (neg_cum[:, :, :past_len], neg_cum[:, :, past_len:n_keys]), **fox)

    merged = _merge(xn, [o_mla.reshape(m, -1), o_ssd.reshape(m, -1), o_fox.reshape(m, -1)],
                    [w["w_br_mla"], w["w_br_ssd"], w["w_br_fox"]], w["in_gate"])
    x = _mm(merged, w["w_out"], out_dtype=F32, residual=x, name="out_proj")
    (x,) = _ffn(x, w["ffn2_norm"], w["ffn2_g"], w["ffn2_u"], w["ffn2_d"],
                post=None if final_gain is None else "norm_only", post_gain=final_gain)
    state = (ckv_new.reshape(bsz, s, kv_lora), kpe_new.reshape(bsz, s, MLA_ROPE),
             fk.reshape(bsz, s, FOX_HEADS, FOX_HEAD_DIM), fv.reshape(bsz, s, FOX_HEADS, FOX_HEAD_DIM),
             logf_new.reshape(bsz, s, FOX_HEADS), ssm_new, conv_new)
    return x, state


def kernel(x_prompt, x_sample, cache_mla_ckv, cache_mla_kpe, cache_fox_k, cache_fox_v, cache_fox_logf, state_ssm,
           state_conv, ffn1_norm, ffn1_w_gate, ffn1_w_up, ffn1_w_down, mix_norm, w_in, mla_q_norm, mla_w_uq,
           mla_kv_norm, mla_w_ukv, ssm_conv_w, ssm_conv_b, ssm_dt_bias, ssm_a_log, ssm_d, ssm_norm, fox_b_f,
           w_br_mla, w_br_ssd, w_br_fox, w_out, ffn2_norm, ffn2_w_gate, ffn2_w_up, ffn2_w_down, final_norm):
    a = dict(ffn1_norm=ffn1_norm, ffn1_w_gate=ffn1_w_gate, ffn1_w_up=ffn1_w_up, ffn1_w_down=ffn1_w_down,
             mix_norm=mix_norm, w_in=w_in, mla_q_norm=mla_q_norm, mla_w_uq=mla_w_uq, mla_kv_norm=mla_kv_norm,
             mla_w_ukv=mla_w_ukv, ssm_conv_w=ssm_conv_w, ssm_conv_b=ssm_conv_b, ssm_dt_bias=ssm_dt_bias,
             ssm_a_log=ssm_a_log, ssm_d=ssm_d, ssm_norm=ssm_norm, fox_b_f=fox_b_f, w_br_mla=w_br_mla,
             w_br_ssd=w_br_ssd, w_br_fox=w_br_fox, w_out=w_out, ffn2_norm=ffn2_norm, ffn2_w_gate=ffn2_w_gate,
             ffn2_w_up=ffn2_w_up, ffn2_w_down=ffn2_w_down)
    depth = w_in.shape[0]
    bp, sp, d_model = x_prompt.shape
    bs, ss, _ = x_sample.shape
    past_len = cache_mla_ckv.shape[2]
    tabs_p = tuple(jnp.tile(t, (bp, 1)) for t in _rope_tables(jnp.arange(sp, dtype=jnp.int32)))
    tabs_s = tuple(jnp.tile(t, (bs, 1)) for t in _rope_tables(past_len + jnp.arange(ss, dtype=jnp.int32)))
    hp = x_prompt.reshape(bp * sp, d_model).astype(F32)
    hs = x_sample.reshape(bs * ss, d_model).astype(F32)
    caches = {"mla_ckv": cache_mla_ckv, "mla_kpe": cache_mla_kpe, "fox_k": cache_fox_k, "fox_v": cache_fox_v,
              "fox_logf": cache_fox_logf, "ssm": state_ssm, "conv": state_conv}
    new_p, new_s = [], []
    for l in range(depth):
        w = _layer_weights(l, a)
        fg = final_norm if l == depth - 1 else None
        hp, st_p = _layer(hp, bp, sp, w, tabs_p, None, fg)
        hs, st_s = _layer(hs, bs, ss, w, tabs_s, (caches, l), fg)
        new_p.append(st_p)
        new_s.append(st_s)
    y_prompt = hp.reshape(bp, sp, d_model)
    y_sample = hs.reshape(bs, ss, d_model)
    stk = lambda states, i: jnp.stack([st[i] for st in states], axis=0)
    return (y_prompt, y_sample) + tuple(stk(new_p, i) for i in range(7)) + tuple(stk(new_s, i) for i in range(7))
```

```python
import functools
import math

import jax
import jax.numpy as jnp
from jax import lax
from jax.experimental import pallas as pl
from jax.experimental.pallas import tpu as pltpu

F32 = jnp.float32
BF16 = jnp.bfloat16

EPS = 1e-6
CHUNK = 64
FFN_RES = 0.5
MLA_HEADS, MLA_NOPE, MLA_ROPE, MLA_V = 8, 128, 64, 128
MLA_SCALE = (MLA_NOPE + MLA_ROPE) ** -0.5
ROPE_BASE = 10000.0
SSM_HEADS, SSM_HEAD_DIM, SSM_GROUPS, SSM_STATE, SSM_CONV_W = 16, 64, 2, 128, 4
SSM_D_INNER = SSM_HEADS * SSM_HEAD_DIM
FOX_HEADS, FOX_HEAD_DIM = 8, 128
FOX_SCALE = FOX_HEAD_DIM ** -0.5
N_BRANCH = 3

LANES = 128
SUBLANES = 8
MXU_DIM = 256
VMEM_LIMIT = 56 * 1024 * 1024

MLA_QK_PAD = MXU_DIM
INPROJ_TN = 512
NEG_BIG = -1e30
LOG2E = math.log2(math.e)
HI = lax.Precision.HIGHEST


def _tile(n, pref, align):
    t = (min(pref, n) // align) * align
    while t >= align:
        if n % t == 0:
            return t
        t -= align
    return n


def _params(*sem):
    return pltpu.CompilerParams(dimension_semantics=sem, vmem_limit_bytes=VMEM_LIMIT)


def _wshape(w):
    return w[0].shape[1:] if isinstance(w, tuple) else w.shape


def _wspec(w, block, index):
    if isinstance(w, tuple):
        arr, layer = w
        return arr, pl.BlockSpec((None,) + tuple(block), lambda *g: (layer,) + tuple(index(*g)))
    return w, pl.BlockSpec(tuple(block), index)


def _rms(x, g):
    return x * lax.rsqrt(jnp.mean(x * x, axis=-1, keepdims=True) + EPS) * g


def _softplus(x):
    return jnp.maximum(x, 0.0) + jnp.log1p(jnp.exp(-jnp.abs(x)))


def _rope_lanes(pe, cos, s1, s2):
    half = MLA_ROPE // 2
    return pe * cos + pltpu.roll(pe, LANES - half, 1) * s1 + pltpu.roll(pe, half, 1) * s2


def _mm_kernel(*refs, prologue, rope, residual, tn):
    it = iter(refs)
    x_ref = next(it)
    g_ref = next(it) if prologue == "rms" else None
    w_ref = next(it)
    res_ref = next(it) if residual else None
    tabs = (next(it), next(it), next(it)) if rope else None
    o_ref = next(it)
    xn_ref = next(it) if prologue != "none" else None

    if prologue == "none":
        lhs = x_ref[...]
    else:
        @pl.when(pl.program_id(1) == 0)
        def _():
            x = x_ref[...].astype(F32)
            if prologue == "rms":
                x = _rms(x, g_ref[...])
            xn_ref[...] = x.astype(BF16)
        lhs = xn_ref[...]
    acc = jnp.dot(lhs, w_ref[...], preferred_element_type=F32)
    if residual:
        acc = res_ref[...] + acc
    if rope:
        cos, s1, s2 = (t[...] for t in tabs)
        for c in range(tn // MLA_QK_PAD):
            a = c * MLA_QK_PAD
            o_ref[:, a:a + LANES] = acc[:, a:a + LANES].astype(o_ref.dtype)
            o_ref[:, a + LANES:a + 2 * LANES] = _rope_lanes(acc[:, a + LANES:a + 2 * LANES], cos, s1, s2).astype(o_ref.dtype)
    else:
        o_ref[...] = acc.astype(o_ref.dtype)


def _mm(x, w, *, out_dtype, tm=1024, tn=512, prologue="none", gain=None, x_col=0, residual=None, rope_tabs=None,
        name="mm"):
    m = x.shape[0]
    k, n = _wshape(w)
    assert x.shape[1] % k == 0 and (prologue != "none" or x.dtype == BF16)
    tm = _tile(m, tm, SUBLANES)
    tn = _tile(n, tn, MLA_QK_PAD if rope_tabs is not None else LANES)
    grid = (m // tm, n // tn)
    in_specs = [pl.BlockSpec((tm, k), lambda i, j: (i, x_col))]
    args = [x]
    if prologue == "rms":
        in_specs.append(pl.BlockSpec((1, k), lambda i, j: (0, 0)))
        args.append(gain.reshape(1, k).astype(F32))
    w_arr, w_spec = _wspec(w, (k, tn), lambda i, j: (0, j))
    in_specs.append(w_spec)
    args.append(w_arr)
    if residual is not None:
        in_specs.append(pl.BlockSpec((tm, tn), lambda i, j: (i, j)))
        args.append(residual)
    if rope_tabs is not None:
        for t in rope_tabs:
            in_specs.append(pl.BlockSpec((tm, LANES), lambda i, j: (i, 0)))
            args.append(t)
    scratch = [pltpu.VMEM((tm, k), BF16)] if prologue != "none" else []
    kern = functools.partial(_mm_kernel, prologue=prologue, rope=rope_tabs is not None,
                             residual=residual is not None, tn=tn)
    return pl.pallas_call(
        kern, grid=grid, in_specs=in_specs,
        out_specs=pl.BlockSpec((tm, tn), lambda i, j: (i, j)),
        out_shape=jax.ShapeDtypeStruct((m, n), out_dtype),
        scratch_shapes=scratch, compiler_params=_params("parallel", "arbitrary"), name=name,
    )(*args)


def _inproj_kernel(xn_ref, w_ref, *outs, groups):
    j = pl.program_id(1)
    acc = jnp.dot(xn_ref[...], w_ref[...], preferred_element_type=F32)
    for (lo, hi, width), o_ref in zip(groups, outs):
        @pl.when((j >= lo) & (j < hi))
        def _(o_ref=o_ref, width=width):
            o_ref[...] = acc[:, :width].astype(o_ref.dtype)


def _inproj_layout(widths, tn):
    groups, start = [], 0
    for n in widths:
        nt = -(-n // tn)
        assert n % tn == 0 or n < tn
        groups.append((start, start + nt, min(n, tn)))
        start += nt
    return groups, start


def _inproj(xn, w_cat, widths, dtypes, *, tm=1024, tn=512):
    m, d = xn.shape
    tm = _tile(m, tm, SUBLANES)
    groups, n_tiles = _inproj_layout(widths, tn)
    assert _wshape(w_cat) == (d, n_tiles * tn)
    w_arr, w_spec = _wspec(w_cat, (d, tn), lambda i, j: (0, j))
    out_specs = [pl.BlockSpec((tm, bw), functools.partial(lambda i, j, lo, hi: (i, jnp.clip(j - lo, 0, hi - lo - 1)),
                                                           lo=lo, hi=hi)) for lo, hi, bw in groups]
    out_shape = [jax.ShapeDtypeStruct((m, n), dt) for n, dt in zip(widths, dtypes)]
    return pl.pallas_call(
        functools.partial(_inproj_kernel, groups=groups), grid=(m // tm, n_tiles),
        in_specs=[pl.BlockSpec((tm, d), lambda i, j: (i, 0)), w_spec],
        out_specs=out_specs, out_shape=out_shape,
        compiler_params=_params("parallel", "arbitrary"), name="inproj",
    )(xn, w_arr)


def _ffn_kernel(*refs, nf, tf, f, post):
    x_ref, g_ref, wg_ref, wu_ref, wd_ref = refs[:5]
    refs = refs[5:]
    pg_ref = None
    if post is not None:
        pg_ref, refs = refs[0], refs[1:]
    outs, (xn_ref, acc_ref) = refs[:-2], refs[-2:]
    j = pl.program_id(1)

    @pl.when(j == 0)
    def _():
        xn_ref[...] = _rms(x_ref[...], g_ref[...]).astype(BF16)
        acc_ref[...] = jnp.zeros_like(acc_ref)

    xn = xn_ref[...]
    a = jnp.dot(xn, wg_ref[...], preferred_element_type=F32)
    b = jnp.dot(xn, wu_ref[...], preferred_element_type=F32)
    h = a * jax.nn.sigmoid(a) * b
    wd = wd_ref[...]
    if f % tf:
        valid = f - j * tf
        h = jnp.where(lax.broadcasted_iota(jnp.int32, h.shape, 1) < valid, h, 0.0)
        wd = jnp.where(lax.broadcasted_iota(jnp.int32, wd.shape, 0) < valid, wd, jnp.zeros_like(wd))
    acc_ref[...] += jnp.dot(h.astype(BF16), wd, preferred_element_type=F32)

    @pl.when(j == nf - 1)
    def _():
        y = x_ref[...] + FFN_RES * acc_ref[...]
        if post is None:
            outs[0][...] = y
        elif post == "norm_bf16":
            outs[0][...] = y
            outs[1][...] = _rms(y, pg_ref[...]).astype(BF16)
        else:
            outs[0][...] = _rms(y, pg_ref[...])


def _ffn(x, gain, wg, wu, wd, *, post=None, post_gain=None, tm=512, tf=512):
    m, d = x.shape
    f = _wshape(wg)[1]
    tm = _tile(m, tm, SUBLANES)
    nf = pl.cdiv(f, tf)
    row = pl.BlockSpec((tm, d), lambda i, j: (i, 0))
    vec = pl.BlockSpec((1, d), lambda i, j: (0, 0))
    wg_arr, wg_spec = _wspec(wg, (d, tf), lambda i, j: (0, j))
    wu_arr, wu_spec = _wspec(wu, (d, tf), lambda i, j: (0, j))
    wd_arr, wd_spec = _wspec(wd, (tf, d), lambda i, j: (j, 0))
    in_specs = [row, vec, wg_spec, wu_spec, wd_spec]
    args = [x, gain.reshape(1, d).astype(F32), wg_arr, wu_arr, wd_arr]
    out_specs, out_shape = [row], [jax.ShapeDtypeStruct((m, d), F32)]
    if post is not None:
        in_specs.append(vec)
        args.append(post_gain.reshape(1, d).astype(F32))
    if post == "norm_bf16":
        out_specs.append(row)
        out_shape.append(jax.ShapeDtypeStruct((m, d), BF16))
    return pl.pallas_call(
        functools.partial(_ffn_kernel, nf=nf, tf=tf, f=f, post=post), grid=(m // tm, nf),
        in_specs=in_specs, out_specs=out_specs, out_shape=out_shape,
        scratch_shapes=[pltpu.VMEM((tm, d), BF16), pltpu.VMEM((tm, d), F32)],
        compiler_params=_params("parallel", "arbitrary"), name="ffn",
    )(*args)


def _prep_kernel(uc_ref, g_ref, us_ref, cos_ref, s1_ref, s2_ref, fb_ref, ckv_ref, sm_ref, *, ff_lo, ff_hi):
    ckv_ref[...] = _rms(uc_ref[...], g_ref[...])
    us = us_ref[...]
    lane = lax.broadcasted_iota(jnp.int32, us.shape, 1)
    pe = jnp.where(lane < MLA_ROPE, us, 0.0)
    rot = _rope_lanes(pe, cos_ref[...], s1_ref[...], s2_ref[...])
    logf = -_softplus(-(us + fb_ref[...]))
    sm_ref[...] = jnp.where((lane >= ff_lo) & (lane < ff_hi), logf, rot)


def _prep(u_ckv, kv_gain, u_small, tabs, fb_lanes, *, ff_lo, ff_hi, tm=1024):
    m = u_small.shape[0]
    kv = kv_gain.shape[0]
    tm = _tile(m, tm, SUBLANES)
    row = lambda i: (i, 0)
    return pl.pallas_call(
        functools.partial(_prep_kernel, ff_lo=ff_lo, ff_hi=ff_hi), grid=(m // tm,),
        in_specs=[pl.BlockSpec((tm, kv), row),
                  pl.BlockSpec((1, kv), lambda i: (0, 0)),
                  pl.BlockSpec((tm, LANES), row), pl.BlockSpec((tm, LANES), row),
                  pl.BlockSpec((tm, LANES), row), pl.BlockSpec((tm, LANES), row),
                  pl.BlockSpec((1, LANES), lambda i: (0, 0))],
        out_specs=[pl.BlockSpec((tm, kv), row), pl.BlockSpec((tm, LANES), row)],
        out_shape=[jax.ShapeDtypeStruct((m, kv), F32), jax.ShapeDtypeStruct((m, LANES), F32)],
        compiler_params=_params("parallel"), name="prep",
    )(u_ckv, kv_gain.reshape(1, kv).astype(F32), u_small, *tabs, fb_lanes)


def _cumsum_kernel(x_ref, o_ref, carry_ref, *, tc, mult):
    @pl.when(pl.program_id(1) == 0)
    def _():
        carry_ref[...] = jnp.zeros_like(carry_ref)

    r = lax.broadcasted_iota(jnp.int32, (tc, tc), 0)
    c = lax.broadcasted_iota(jnp.int32, (tc, tc), 1)
    upper = (r <= c).astype(F32)
    y = jnp.dot(x_ref[0], upper, preferred_element_type=F32, precision=HI) + carry_ref[:, :1]
    o_ref[0] = y * mult
    carry_ref[...] = jnp.broadcast_to(y[:, tc - 1:tc], carry_ref.shape)


def _cumsum_last(x, mult, *, tc=512):
    b, h, s = x.shape
    tc = _tile(s, tc, LANES)
    return pl.pallas_call(
        functools.partial(_cumsum_kernel, tc=tc, mult=mult), grid=(b, s // tc),
        in_specs=[pl.BlockSpec((1, h, tc), lambda i, j: (i, 0, j))],
        out_specs=pl.BlockSpec((1, h, tc), lambda i, j: (i, 0, j)),
        out_shape=jax.ShapeDtypeStruct((b, h, s), F32),
        scratch_shapes=[pltpu.VMEM((h, LANES), F32)],
        compiler_params=_params("parallel", "arbitrary"), name="cumsum",
    )(x)


def _last_visible(q_end, mode):
    if mode == "chunk":
        return (q_end // CHUNK) * CHUNK + (CHUNK - 1)
    return q_end


def _attn_kernel(*refs, heads, dq, dk, dv, tq, tk, nk, scale, mode, q_off, n_valid, has_bias, has_shared):
    it = iter(refs)
    qi_ref, ki_ref = next(it), next(it)
    q_ref, k_ref, v_ref = next(it), next(it), next(it)
    ks_ref = next(it) if has_shared else None
    b_ref = next(it) if has_bias else None
    o_ref, m_ref, acc_ref = next(it), next(it), next(it)
    t = pl.program_id(1)
    qi, ki = qi_ref[t], ki_ref[t]
    nch = tk // LANES
    c = scale * LOG2E
    aw = dv + LANES

    @pl.when(ki == 0)
    def _():
        m_ref[...] = jnp.full_like(m_ref, NEG_BIG)
        acc_ref[...] = jnp.zeros_like(acc_ref)

    q_lo = q_off + qi * tq
    k_lo = ki * tk
    last_tile = jnp.minimum(_last_visible(q_lo + (tq - 1), mode) // tk, nk - 1)
    first_maskable = _last_visible(q_lo, mode) + 1
    ones_col = (lax.broadcasted_iota(jnp.int32, (tk, LANES), 1) == 0).astype(BF16)

    def body(masked):
        if masked:
            qpos = q_lo + lax.broadcasted_iota(jnp.int32, (tq, tk), 0)
            kpos = k_lo + lax.broadcasted_iota(jnp.int32, (tq, tk), 1)
            if mode == "chunk":
                sh = CHUNK.bit_length() - 1
                vis = lax.shift_right_logical(kpos, sh) <= lax.shift_right_logical(qpos, sh)
            else:
                vis = kpos <= qpos
            vis = vis & (kpos < n_valid)
        for h in range(heads):
            q = q_ref[0, :, h * dq:(h + 1) * dq]
            k = k_ref[0, :, h * dk:(h + 1) * dk]
            if has_shared:
                k = jnp.concatenate([k, ks_ref[0]], axis=1)
            s = lax.dot_general(q, k, (((1,), (1,)), ((), ())), preferred_element_type=F32)
            if has_bias:
                s = s + b_ref[0, h:h + 1, :]
            if masked:
                s = jnp.where(vis, s, NEG_BIG)
            m_prev = m_ref[h]
            mc = s[:, :LANES]
            for j in range(1, nch):
                mc = jnp.maximum(mc, s[:, j * LANES:(j + 1) * LANES])
            m_new = jnp.maximum(m_prev, jnp.max(mc, axis=1, keepdims=True))
            m_ref[h] = m_new
            alpha = jnp.exp2((m_prev - m_new) * c)
            p = jnp.concatenate([jnp.exp2((s[:, j * LANES:(j + 1) * LANES] - m_new) * c).astype(BF16)
                                 for j in range(nch)], axis=1)
            vx = jnp.concatenate([v_ref[0, :, h * dv:(h + 1) * dv], ones_col], axis=1)
            pv = jnp.dot(p, vx, preferred_element_type=F32)
            for a0 in range(h * aw, (h + 1) * aw, LANES):
                acc_ref[:, a0:a0 + LANES] = alpha * acc_ref[:, a0:a0 + LANES] + pv[:, a0 - h * aw:a0 - h * aw + LANES]

    need_mask = (k_lo + (tk - 1) >= first_maskable) | (k_lo + tk > n_valid)

    @pl.when(need_mask)
    def _():
        body(True)

    @pl.when(jnp.logical_not(need_mask))
    def _():
        body(False)

    @pl.when(ki == last_tile)
    def _():
        for h in range(heads):
            l = acc_ref[:, h * aw + dv:h * aw + dv + 1]
            o_ref[0, :, h * dv:(h + 1) * dv] = (acc_ref[:, h * aw:h * aw + dv] / l).astype(o_ref.dtype)


def _attention(q, k, v, k_shared, bias, *, heads, dq, dk, dv, scale, mode, q_off, n_valid, k_col=0, v_col=0,
               tq=1024, tk=1024):
    b, sq, _ = q.shape
    sk = k.shape[1]
    assert dk + (0 if k_shared is None else k_shared.shape[2]) == dq
    tq = _tile(sq, tq, SUBLANES)
    tk = sk if sk <= 2 * tk else _tile(sk, tk, LANES)
    nq, nk = sq // tq, sk // tk
    pairs = [(i, j) for i in range(nq)
             for j in range(min(_last_visible(q_off + i * tq + (tq - 1), mode) // tk, nk - 1) + 1)]
    qi_arr = jnp.asarray([p[0] for p in pairs], jnp.int32)
    ki_arr = jnp.asarray([p[1] for p in pairs], jnp.int32)

    in_specs = [pl.BlockSpec((1, tq, heads * dq), lambda bi, t, qi, ki: (bi, qi[t], 0)),
                pl.BlockSpec((1, tk, heads * dk), lambda bi, t, qi, ki: (bi, ki[t], k_col)),
                pl.BlockSpec((1, tk, heads * dv), lambda bi, t, qi, ki: (bi, ki[t], v_col))]
    args = [q, k, v]
    if k_shared is not None:
        in_specs.append(pl.BlockSpec((1, tk, dq - dk), lambda bi, t, qi, ki: (bi, ki[t], 0)))
        args.append(k_shared)
    if bias is not None:
        in_specs.append(pl.BlockSpec((1, heads, tk), lambda bi, t, qi, ki: (bi, 0, ki[t])))
        args.append(bias)
    kern = functools.partial(_attn_kernel, heads=heads, dq=dq, dk=dk, dv=dv, tq=tq, tk=tk, nk=nk, scale=scale,
                             mode=mode, q_off=q_off, n_valid=n_valid, has_bias=bias is not None,
                             has_shared=k_shared is not None)
    return pl.pallas_call(
        kern,
        grid_spec=pltpu.PrefetchScalarGridSpec(
            num_scalar_prefetch=2, grid=(b, len(pairs)), in_specs=in_specs,
            out_specs=pl.BlockSpec((1, tq, heads * dv), lambda bi, t, qi, ki: (bi, qi[t], 0)),
            scratch_shapes=[pltpu.VMEM((heads, tq, LANES), F32), pltpu.VMEM((tq, heads * (dv + LANES)), F32)]),
        out_shape=jax.ShapeDtypeStruct((b, sq, heads * dv), BF16),
        compiler_params=_params("parallel", "arbitrary"), name="attn_" + mode,
    )(qi_arr, ki_arr, *args)


def _decode_kernel(*refs, heads, dq, dk, dv, sq, p_len, scale, mode, has_shared, has_bias):
    it = iter(refs)
    q_ref, kp_ref, vp_ref, kn_ref, vn_ref = (next(it) for _ in range(5))
    ksp_ref, ksn_ref = (next(it), next(it)) if has_shared else (None, None)
    bp_ref, bn_ref = (next(it), next(it)) if has_bias else (None, None)
    o_ref = next(it)
    c = scale * LOG2E

    def head(ref, h, d):
        x = ref[:, h, :] if len(ref.shape) == 3 else ref[:, h * d:(h + 1) * d]
        return x.astype(BF16)

    row = lax.broadcasted_iota(jnp.int32, (sq, sq), 0)
    col = lax.broadcasted_iota(jnp.int32, (sq, sq), 1)
    if mode == "chunk":
        sh = CHUNK.bit_length() - 1
        vis = lax.shift_right_logical(p_len + col, sh) <= lax.shift_right_logical(p_len + row, sh)
    else:
        vis = col <= row
    if has_shared:
        pad = dq - dk - ksp_ref.shape[1]
        ksp = jnp.concatenate([ksp_ref[...].astype(BF16), jnp.zeros((p_len, pad), BF16)], axis=1)
        ksn = jnp.concatenate([ksn_ref[...].astype(BF16), jnp.zeros((sq, pad), BF16)], axis=1)
    contract_last = (((1,), (1,)), ((), ()))
    for h in range(heads):
        q = q_ref[:, h * dq:(h + 1) * dq]
        kp, kn = head(kp_ref, h, dk), head(kn_ref, h, dk)
        if has_shared:
            kp = jnp.concatenate([kp, ksp], axis=1)
            kn = jnp.concatenate([kn, ksn], axis=1)
        s_p = lax.dot_general(q, kp, contract_last, preferred_element_type=F32)
        s_n = lax.dot_general(q, kn, contract_last, preferred_element_type=F32)
        if has_bias:
            s_p = s_p + bp_ref[h:h + 1, :]
            s_n = s_n + bn_ref[h:h + 1, :]
        s_n = jnp.where(vis, s_n, NEG_BIG)
        m = jnp.maximum(jnp.max(s_p, axis=1, keepdims=True), jnp.max(s_n, axis=1, keepdims=True))
        p_p = jnp.exp2((s_p - m) * c)
        p_n = jnp.exp2((s_n - m) * c)
        l = jnp.sum(p_p, axis=1, keepdims=True) + jnp.sum(p_n, axis=1, keepdims=True)
        pv = (jnp.dot(p_p.astype(BF16), head(vp_ref, h, dv), preferred_element_type=F32)
              + jnp.dot(p_n.astype(BF16), head(vn_ref, h, dv), preferred_element_type=F32))
        o_ref[:, h * dv:(h + 1) * dv] = (pv / l).astype(o_ref.dtype)


def _decode_attention(q, past_kv, new_kv, shared, bias, *, heads, dq, dk, dv, scale, mode):
    b, sq, _ = q.shape

    def spec(arr, prefix, colblk, width):
        inner = arr.shape[len(prefix) + 1:]
        if len(inner) == 3:
            block, idx = inner, (0, 0, 0)
        else:
            block, idx = (inner[0], width), (0, colblk)
        return pl.BlockSpec((None,) * (len(prefix) + 1) + tuple(block), lambda i: tuple(prefix) + (i,) + idx)

    (kp, kp_pre, kp_col), (vp, vp_pre, vp_col) = past_kv
    (kn, kn_pre, kn_col), (vn, vn_pre, vn_col) = new_kv
    p_len = kp.shape[len(kp_pre) + 1]
    in_specs = [pl.BlockSpec((None, sq, heads * dq), lambda i: (i, 0, 0)),
                spec(kp, kp_pre, kp_col, heads * dk), spec(vp, vp_pre, vp_col, heads * dv),
                spec(kn, kn_pre, kn_col, heads * dk), spec(vn, vn_pre, vn_col, heads * dv)]
    args = [q, kp, vp, kn, vn]
    if shared is not None:
        for arr, pre in shared:
            in_specs.append(spec(arr, pre, 0, arr.shape[-1]))
            args.append(arr)
    if bias is not None:
        for arr in bias:
            in_specs.append(pl.BlockSpec((None,) + arr.shape[1:], lambda i: (i, 0, 0)))
            args.append(arr)
    kern = functools.partial(_decode_kernel, heads=heads, dq=dq, dk=dk, dv=dv, sq=sq, p_len=p_len, scale=scale,
                             mode=mode, has_shared=shared is not None, has_bias=bias is not None)
    return pl.pallas_call(
        kern, grid=(b,), in_specs=in_specs,
        out_specs=pl.BlockSpec((None, sq, heads * dv), lambda i: (i, 0, 0)),
        out_shape=jax.ShapeDtypeStruct((b, sq, heads * dv), BF16),
        compiler_params=_params("parallel"), name="decode_" + mode,
    )(*args)


def _ssd_kernel(z_ref, xbc_ref, dt_ref, dtt_ref, cw_ref, cb_ref, dtb_ref, dtbt_ref, al_ref, alt_ref, dx_ref, nw_ref,
                h0_ref, c0_ref, y_ref, hout_ref, state_ref, carry_ref, *, lc, nc):
    c = pl.program_id(1)
    gw = SSM_D_INNER // SSM_GROUPS
    hpg = SSM_HEADS // SSM_GROUPS
    halo = SUBLANES

    @pl.when(c == 0)
    def _():
        state_ref[...] = h0_ref[0]
        carry_ref[...] = c0_ref[0]

    x = xbc_ref[0]
    cat = jnp.concatenate([carry_ref[...], x], axis=0)
    conv = cb_ref[...]
    for kk in range(SSM_CONV_W):
        shift = SSM_CONV_W - 1 - kk
        src = pltpu.roll(cat, shift, 0) if shift else cat
        conv = conv + src[halo:, :] * cw_ref[kk:kk + 1, :]
    carry_ref[...] = x[lc - halo:, :]
    act = conv * jax.nn.sigmoid(conv)
    xs = act[:, :SSM_D_INNER]
    bm = act[:, SSM_D_INNER:SSM_D_INNER + SSM_GROUPS * SSM_STATE]
    cm = act[:, SSM_D_INNER + SSM_GROUPS * SSM_STATE:]

    dt = _softplus(dt_ref[0] + dtb_ref[...])
    dtt = _softplus(dtt_ref[0] + dtbt_ref[...])
    adt = dt * (-jnp.exp(al_ref[...]))
    adtt = dtt * (-jnp.exp(alt_ref[...]))
    r = lax.broadcasted_iota(jnp.int32, (lc, lc), 0)
    cc = lax.broadcasted_iota(jnp.int32, (lc, lc), 1)
    tril = cc <= r
    acs = jnp.dot(tril.astype(F32), adt, preferred_element_type=F32, precision=HI)
    acst = jnp.dot(adtt, (r <= cc).astype(F32), preferred_element_type=F32, precision=HI)
    hh = lax.broadcasted_iota(jnp.int32, (SSM_HEADS, SSM_D_INNER), 0)
    ll = lax.broadcasted_iota(jnp.int32, (SSM_HEADS, SSM_D_INNER), 1)
    expand = ((ll >= hh * SSM_HEAD_DIM) & (ll < (hh + 1) * SSM_HEAD_DIM)).astype(F32)
    dt_x = jnp.dot(dt, expand, preferred_element_type=F32, precision=HI)
    acs_x = jnp.dot(acs, expand, preferred_element_type=F32, precision=HI)
    tot_x = acs_x[lc - 1:lc, :]
    xdt = xs * dt_x
    xdt_b = xdt.astype(BF16)
    w_end = (xdt * jnp.exp(tot_x - acs_x)).astype(BF16)
    state = state_ref[...]
    state_b = state.astype(BF16)

    y_parts, new_parts = [], []
    for g in range(SSM_GROUPS):
        bg = bm[:, g * SSM_STATE:(g + 1) * SSM_STATE].astype(BF16)
        cg = cm[:, g * SSM_STATE:(g + 1) * SSM_STATE].astype(BF16)
        cb = lax.dot_general(cg, bg, (((1,), (1,)), ((), ())), preferred_element_type=F32)
        for hl in range(hpg):
            h = g * hpg + hl
            seg = acs[:, h:h + 1] - acst[h:h + 1, :]
            mh = (cb * jnp.exp(jnp.where(tril, seg, NEG_BIG))).astype(BF16)
            y_parts.append(jnp.dot(mh, xdt_b[:, h * SSM_HEAD_DIM:(h + 1) * SSM_HEAD_DIM], preferred_element_type=F32))
        new_parts.append(lax.dot_general(bg, w_end[:, g * gw:(g + 1) * gw], (((0,), (0,)), ((), ())),
                                         preferred_element_type=F32))
    y_off = jnp.concatenate(
        [jnp.dot(cm[:, g * SSM_STATE:(g + 1) * SSM_STATE].astype(BF16), state_b[:, g * gw:(g + 1) * gw],
                 preferred_element_type=F32) for g in range(SSM_GROUPS)], axis=1) * jnp.exp(acs_x)
    y = jnp.concatenate(y_parts, axis=1) + y_off + dx_ref[...] * xs
    state_ref[...] = jnp.exp(tot_x) * state + jnp.concatenate(new_parts, axis=1)

    zz = z_ref[0]
    y = y * (zz * jax.nn.sigmoid(zz))
    for g in range(SSM_GROUPS):
        y_ref[0, :, g * gw:(g + 1) * gw] = _rms(y[:, g * gw:(g + 1) * gw], nw_ref[:, g * gw:(g + 1) * gw]).astype(y_ref.dtype)

    @pl.when(c == nc - 1)
    def _():
        hout_ref[0] = state_ref[...]


def _ssd(z, xbc, dt, dtt, p, h0, c0, *, lc=256):
    b, s, cd = xbc.shape
    lc = _tile(s, lc, LANES) if s % LANES == 0 else s
    nc = s // lc
    hh = SSM_HEADS
    full2 = lambda shape: pl.BlockSpec(shape, lambda i, j: (0, 0))
    return pl.pallas_call(
        functools.partial(_ssd_kernel, lc=lc, nc=nc), grid=(b, nc),
        in_specs=[pl.BlockSpec((1, lc, SSM_D_INNER), lambda i, j: (i, j, 0)),
                  pl.BlockSpec((1, lc, cd), lambda i, j: (i, j, 0)),
                  pl.BlockSpec((1, lc, hh), lambda i, j: (i, j, 0)),
                  pl.BlockSpec((1, hh, lc), lambda i, j: (i, 0, j)),
                  full2((SSM_CONV_W, cd)), full2((1, cd)),
                  full2((1, hh)), full2((hh, 1)), full2((1, hh)), full2((hh, 1)),
                  full2((1, SSM_D_INNER)), full2((1, SSM_D_INNER)),
                  pl.BlockSpec((1, SSM_STATE, SSM_D_INNER), lambda i, j: (i, 0, 0)),
                  pl.BlockSpec((1, SUBLANES, cd), lambda i, j: (i, 0, 0))],
        out_specs=[pl.BlockSpec((1, lc, SSM_D_INNER), lambda i, j: (i, j, 0)),
                   pl.BlockSpec((1, SSM_STATE, SSM_D_INNER), lambda i, j: (i, 0, 0))],
        out_shape=[jax.ShapeDtypeStruct((b, s, SSM_D_INNER), BF16),
                   jax.ShapeDtypeStruct((b, SSM_STATE, SSM_D_INNER), F32)],
        scratch_shapes=[pltpu.VMEM((SSM_STATE, SSM_D_INNER), F32), pltpu.VMEM((SUBLANES, cd), F32)],
        compiler_params=_params("parallel", "arbitrary"), name="ssd",
    )(z, xbc, dt, dtt, p["conv_w"], p["conv_b"], p["dt_b"], p["dt_bt"], p["a_log"], p["a_logt"], p["d_x"], p["norm_w"],
      h0, c0)


def _merge_kernel(xn_ref, o0, o1, o2, w0, w1, w2, wg0, wg1, wg2, out_ref):
    xn = xn_ref[...]
    acc = None
    for o_ref, w_ref, wg_ref in ((o0, w0, wg0), (o1, w1, wg1), (o2, w2, wg2)):
        gate = jax.nn.sigmoid(jnp.dot(xn, wg_ref[...], preferred_element_type=F32))
        t = gate * jnp.dot(o_ref[...], w_ref[...], preferred_element_type=F32)
        acc = t if acc is None else acc + t
    out_ref[...] = acc.astype(out_ref.dtype)


def _merge(xn, o_list, w_list, w_gate, *, tm=1024, tn=512):
    m, d = xn.shape
    tm = _tile(m, tm, SUBLANES)
    tn = _tile(d, tn, LANES)
    nb = d // tn
    in_specs = [pl.BlockSpec((tm, d), lambda i, j: (i, 0))]
    in_specs += [pl.BlockSpec((tm, o.shape[1]), lambda i, j: (i, 0)) for o in o_list]
    w_args = []
    for w in w_list:
        arr, sp = _wspec(w, (_wshape(w)[0], tn), lambda i, j: (0, j))
        in_specs.append(sp)
        w_args.append(arr)
    for br in range(N_BRANCH):
        arr, sp = _wspec(w_gate, (d, tn), functools.partial(lambda i, j, br: (0, br * nb + j), br=br))
        in_specs.append(sp)
        w_args.append(arr)
    return pl.pallas_call(
        _merge_kernel, grid=(m // tm, nb), in_specs=in_specs,
        out_specs=pl.BlockSpec((tm, tn), lambda i, j: (i, j)),
        out_shape=jax.ShapeDtypeStruct((m, d), BF16),
        compiler_params=_params("parallel", "arbitrary"), name="merge",
    )(xn, *o_list, *w_args)


def _stacked_weights(a):
    depth, d_model = a["w_in"].shape[:2]
    q_lora, kv_lora = a["mla_q_norm"].shape[1], a["mla_kv_norm"].shape[1]
    conv_dim = a["ssm_conv_w"].shape[2]
    sizes = (q_lora, kv_lora, MLA_ROPE, SSM_D_INNER, conv_dim, SSM_HEADS,
             FOX_HEADS * FOX_HEAD_DIM, FOX_HEADS * FOX_HEAD_DIM, FOX_HEADS * FOX_HEAD_DIM, FOX_HEADS,
             N_BRANCH * d_model)
    w_in = a["w_in"]
    assert w_in.shape[2] == sum(sizes)
    cols, start = [], 0
    for n in sizes:
        cols.append(w_in[:, :, start:start + n])
        start += n
    w_q, w_ckv, w_kpe, w_z, w_xbc, w_dt, w_fq, w_fk, w_fv, w_ff, w_gate = cols
    bf = lambda t: t.astype(BF16)
    small = [w_kpe, w_dt, w_ff]
    n_small = sum(t.shape[2] for t in small)
    assert n_small <= LANES
    parts = [w_q, w_ckv, *small, jnp.zeros((depth, d_model, INPROJ_TN - n_small), w_in.dtype),
             w_z, w_xbc, w_fq, w_fk, w_fv]
    st = {"in_cat": bf(jnp.concatenate(parts, axis=2)), "in_gate": bf(w_gate)}
    wq = a["mla_w_uq"].reshape(depth, q_lora, MLA_HEADS, MLA_NOPE + MLA_ROPE)
    wq = jnp.pad(wq, ((0, 0), (0, 0), (0, 0), (0, MLA_QK_PAD - MLA_NOPE - MLA_ROPE)))
    st["uq"] = bf(wq.reshape(depth, q_lora, MLA_HEADS * MLA_QK_PAD))
    wkv = a["mla_w_ukv"].reshape(depth, kv_lora, MLA_HEADS, MLA_NOPE + MLA_V)
    st["ukv"] = bf(jnp.concatenate([wkv[..., :MLA_NOPE].reshape(depth, kv_lora, MLA_HEADS * MLA_NOPE),
                                    wkv[..., MLA_NOPE:].reshape(depth, kv_lora, MLA_HEADS * MLA_V)], axis=2))
    for nm in ("w_br_mla", "w_br_ssd", "w_br_fox", "w_out"):
        st[nm] = bf(a[nm])
    for pre in ("ffn1", "ffn2"):
        for src, dst in (("_w_gate", "_g"), ("_w_up", "_u"), ("_w_down", "_d")):
            st[pre + dst] = bf(a[pre + src])
    return st


def _layer_weights(l, a, stacked):
    q_lora, kv_lora = a["mla_q_norm"].shape[1], a["mla_kv_norm"].shape[1]
    conv_dim = a["ssm_conv_w"].shape[2]
    w = {name: (arr, l) for name, arr in stacked.items()}
    w["in_widths"] = (q_lora, kv_lora, LANES, SSM_D_INNER, conv_dim) + (FOX_HEADS * FOX_HEAD_DIM,) * 3
    for pre in ("ffn1", "ffn2"):
        w[pre + "_norm"] = a[pre + "_norm"][l]
    w["mix_norm"] = a["mix_norm"][l]
    w["q_norm"], w["kv_norm"] = a["mla_q_norm"][l], a["mla_kv_norm"][l]
    ff_lo = MLA_ROPE + SSM_HEADS
    w["fb_lanes"] = jnp.pad(a["fox_b_f"][l].astype(F32), (ff_lo, LANES - ff_lo - FOX_HEADS)).reshape(1, LANES)
    w["ssd"] = {
        "conv_w": a["ssm_conv_w"][l].astype(F32), "conv_b": a["ssm_conv_b"][l].astype(F32).reshape(1, conv_dim),
        "dt_b": a["ssm_dt_bias"][l].astype(F32).reshape(1, SSM_HEADS),
        "dt_bt": a["ssm_dt_bias"][l].astype(F32).reshape(SSM_HEADS, 1),
        "a_log": a["ssm_a_log"][l].astype(F32).reshape(1, SSM_HEADS),
        "a_logt": a["ssm_a_log"][l].astype(F32).reshape(SSM_HEADS, 1),
        "d_x": jnp.repeat(a["ssm_d"][l].astype(F32), SSM_HEAD_DIM).reshape(1, SSM_D_INNER),
        "norm_w": a["ssm_norm"][l].astype(F32).reshape(1, SSM_D_INNER),
    }
    return w


def _rope_tables(pos):
    half = MLA_ROPE // 2
    inv_freq = ROPE_BASE ** (-jnp.arange(half, dtype=F32) / half)
    ang = pos.astype(F32)[:, None] * inv_freq[None, :]
    cos, sin = jnp.cos(ang), jnp.sin(ang)
    z = jnp.zeros_like(cos)
    pad = jnp.zeros((pos.shape[0], LANES - MLA_ROPE), F32)
    return (jnp.concatenate([cos, cos, pad], axis=1),
            jnp.concatenate([-sin, z, pad], axis=1),
            jnp.concatenate([z, sin, pad], axis=1))


def _pad_keys(t, sk_pad):
    return jnp.pad(t, ((0, 0), (0, sk_pad - t.shape[1])) + ((0, 0),) * (t.ndim - 2))


def _layer(x, bsz, s, w, tabs, past, final_gain):
    m = bsz * s
    kv_lora = w["kv_norm"].shape[0]
    conv_dim = w["ssd"]["conv_w"].shape[1]
    x, xn = _ffn(x, w["ffn1_norm"], w["ffn1_g"], w["ffn1_u"], w["ffn1_d"], post="norm_bf16", post_gain=w["mix_norm"])

    u_q, u_ckv, u_small, u_z, u_xbc, fq, fk, fv = _inproj(
        xn, w["in_cat"], w["in_widths"], (F32, F32, F32, F32, F32, BF16, F32, F32), tn=INPROJ_TN)

    ff_lo = MLA_ROPE + SSM_HEADS
    ckv_new, small2 = _prep(u_ckv, w["kv_norm"], u_small, tabs, w["fb_lanes"], ff_lo=ff_lo, ff_hi=ff_lo + FOX_HEADS)
    kpe_new = small2[:, :MLA_ROPE]
    logf_new = small2[:, ff_lo:ff_lo + FOX_HEADS]
    u_dt = u_small[:, MLA_ROPE:ff_lo]

    if past is not None:
        caches, l = past
        past_len = caches["mla_ckv"].shape[2]
    ukv = functools.partial(_mm, w=w["ukv"], out_dtype=BF16, prologue="cast", tn=1024, name="mla_ukv")

    q_full = _mm(u_q, w["uq"], out_dtype=BF16, prologue="rms", gain=w["q_norm"], rope_tabs=tabs, name="mla_q")
    q_full = q_full.reshape(bsz, s, -1)
    kv_new = ukv(ckv_new).reshape(bsz, s, -1)
    mla = dict(heads=MLA_HEADS, dq=MLA_QK_PAD, dk=MLA_NOPE, dv=MLA_V, scale=MLA_SCALE, mode="chunk")
    sk_pad = -(-s // LANES) * LANES
    if past is None:
        kpe_pad = jnp.pad(kpe_new.astype(BF16).reshape(bsz, s, -1),
                          ((0, 0), (0, sk_pad - s), (0, MLA_QK_PAD - MLA_NOPE - MLA_ROPE)))
        kv_all = _pad_keys(kv_new, sk_pad)
        o_mla = _attention(q_full, kv_all, kv_all, kpe_pad, None, k_col=0, v_col=1, q_off=0, n_valid=s, **mla)
    else:
        kv_past = ukv(caches["mla_ckv"][l].reshape(bsz * past_len, kv_lora)).reshape(bsz, past_len, -1)
        o_mla = _decode_attention(
            q_full, ((kv_past, (), 0), (kv_past, (), 1)), ((kv_new, (), 0), (kv_new, (), 1)),
            ((caches["mla_kpe"], (l,)), (kpe_new.reshape(bsz, s, MLA_ROPE), ())), None, **mla)

    if past is None:
        conv_state = jnp.zeros((bsz, SSM_CONV_W - 1, conv_dim), F32)
        h0 = jnp.zeros((bsz, SSM_STATE, SSM_D_INNER), F32)
    else:
        conv_state = caches["conv"][l].astype(F32)
        h0 = jnp.transpose(caches["ssm"][l].astype(F32), (0, 3, 1, 2)).reshape(bsz, SSM_STATE, SSM_D_INNER)
    c0 = jnp.pad(conv_state, ((0, 0), (SUBLANES - (SSM_CONV_W - 1), 0), (0, 0)))
    xbc3 = u_xbc.reshape(bsz, s, conv_dim)
    dt3 = u_dt.reshape(bsz, s, SSM_HEADS)
    o_ssd, h_new = _ssd(u_z.reshape(bsz, s, SSM_D_INNER), xbc3, dt3, jnp.swapaxes(dt3, 1, 2), w["ssd"], h0, c0)
    ssm_new = jnp.transpose(h_new.reshape(bsz, SSM_STATE, SSM_HEADS, SSM_HEAD_DIM), (0, 2, 3, 1))
    keep = SSM_CONV_W - 1
    conv_new = xbc3[:, s - keep:] if s >= keep else jnp.concatenate([conv_state, xbc3], axis=1)[:, -keep:]

    hw = FOX_HEADS * FOX_HEAD_DIM
    fox = dict(heads=FOX_HEADS, dq=FOX_HEAD_DIM, dk=FOX_HEAD_DIM, dv=FOX_HEAD_DIM, scale=FOX_SCALE, mode="causal")
    logf_all = logf_new.reshape(bsz, s, FOX_HEADS)
    if past is not None:
        logf_all = jnp.concatenate([caches["fox_logf"][l].astype(F32), logf_all], axis=1)
    n_keys = logf_all.shape[1]
    neg_cum = _cumsum_last(jnp.swapaxes(_pad_keys(logf_all, -(-n_keys // LANES) * LANES), 1, 2), -1.0 / FOX_SCALE)
    fq3 = fq.reshape(bsz, s, hw)
    if past is None:
        o_fox = _attention(fq3, _pad_keys(fk.reshape(bsz, s, hw).astype(BF16), sk_pad),
                           _pad_keys(fv.reshape(bsz, s, hw).astype(BF16), sk_pad), None, neg_cum,
                           q_off=0, n_valid=s, **fox)
    else:
        o_fox = _decode_attention(
            fq3, ((caches["fox_k"], (l,), 0), (caches["fox_v"], (l,), 0)),
            ((fk.reshape(bsz, s, hw), (), 0), (fv.reshape(bsz, s, hw), (), 0)), None,
            (neg_cum[:, :, :past_len], neg_cum[:, :, past_len:n_keys]), **fox)

    merged = _merge(xn, [o_mla.reshape(m, -1), o_ssd.reshape(m, -1), o_fox.reshape(m, -1)],
                    [w["w_br_mla"], w["w_br_ssd"], w["w_br_fox"]], w["in_gate"])
    x = _mm(merged, w["w_out"], out_dtype=F32, residual=x, name="out_proj")
    (x,) = _ffn(x, w["ffn2_norm"], w["ffn2_g"], w["ffn2_u"], w["ffn2_d"],
                post=None if final_gain is None else "norm_only", post_gain=final_gain)
    state = (ckv_new.reshape(bsz, s, kv_lora), kpe_new.reshape(bsz, s, MLA_ROPE),
             fk.reshape(bsz, s, FOX_HEADS, FOX_HEAD_DIM), fv.reshape(bsz, s, FOX_HEADS, FOX_HEAD_DIM),
             logf_new.reshape(bsz, s, FOX_HEADS), ssm_new, conv_new)
    return x, state


def kernel(x_prompt, x_sample, cache_mla_ckv, cache_mla_kpe, cache_fox_k, cache_fox_v, cache_fox_logf, state_ssm,
           state_conv, ffn1_norm, ffn1_w_gate, ffn1_w_up, ffn1_w_down, mix_norm, w_in, mla_q_norm, mla_w_uq,
           mla_kv_norm, mla_w_ukv, ssm_conv_w, ssm_conv_b, ssm_dt_bias, ssm_a_log, ssm_d, ssm_norm, fox_b_f,
           w_br_mla, w_br_ssd, w_br_fox, w_out, ffn2_norm, ffn2_w_gate, ffn2_w_up, ffn2_w_down, final_norm):
    a = dict(ffn1_norm=ffn1_norm, ffn1_w_gate=ffn1_w_gate, ffn1_w_up=ffn1_w_up, ffn1_w_down=ffn1_w_down,
             mix_norm=mix_norm, w_in=w_in, mla_q_norm=mla_q_norm, mla_w_uq=mla_w_uq, mla_kv_norm=mla_kv_norm,
             mla_w_ukv=mla_w_ukv, ssm_conv_w=ssm_conv_w, ssm_conv_b=ssm_conv_b, ssm_dt_bias=ssm_dt_bias,
             ssm_a_log=ssm_a_log, ssm_d=ssm_d, ssm_norm=ssm_norm, fox_b_f=fox_b_f, w_br_mla=w_br_mla,
             w_br_ssd=w_br_ssd, w_br_fox=w_br_fox, w_out=w_out, ffn2_norm=ffn2_norm, ffn2_w_gate=ffn2_w_gate,
             ffn2_w_up=ffn2_w_up, ffn2_w_down=ffn2_w_down)
    depth = w_in.shape[0]
    stacked = _stacked_weights(a)
    bp, sp, d_model = x_prompt.shape
    bs, ss, _ = x_sample.shape
    past_len = cache_mla_ckv.shape[2]
    tabs_p = tuple(jnp.tile(t, (bp, 1)) for t in _rope_tables(jnp.arange(sp, dtype=jnp.int32)))
    tabs_s = tuple(jnp.tile(t, (bs, 1)) for t in _rope_tables(past_len + jnp.arange(ss, dtype=jnp.int32)))
    hp = x_prompt.reshape(bp * sp, d_model).astype(F32)
    hs = x_sample.reshape(bs * ss, d_model).astype(F32)
    caches = {"mla_ckv": cache_mla_ckv, "mla_kpe": cache_mla_kpe, "fox_k": cache_fox_k, "fox_v": cache_fox_v,
              "fox_logf": cache_fox_logf, "ssm": state_ssm, "conv": state_conv}
    new_p, new_s = [], []
    for l in range(depth):
        w = _layer_weights(l, a, stacked)
        fg = final_norm if l == depth - 1 else None
        hp, st_p = _layer(hp, bp, sp, w, tabs_p, None, fg)
        hs, st_s = _layer(hs, bs, ss, w, tabs_s, (caches, l), fg)
        new_p.append(st_p)
        new_s.append(st_s)
    y_prompt = hp.reshape(bp, sp, d_model)
    y_sample = hs.reshape(bs, ss, d_model)
    stk = lambda states, i: jnp.stack([st[i] for st in states], axis=0)
    return (y_prompt, y_sample) + tuple(stk(new_p, i) for i in range(7)) + tuple(stk(new_s, i) for i in range(7))
```

```python
import functools
import math

import jax
import jax.numpy as jnp
from jax import lax
from jax.experimental import pallas as pl
from jax.experimental.pallas import tpu as pltpu

F32 = jnp.float32
BF16 = jnp.bfloat16

EPS = 1e-6
CHUNK = 64
FFN_RES = 0.5
MLA_HEADS, MLA_NOPE, MLA_ROPE, MLA_V = 8, 128, 64, 128
MLA_SCALE = (MLA_NOPE + MLA_ROPE) ** -0.5
ROPE_BASE = 10000.0
SSM_HEADS, SSM_HEAD_DIM, SSM_GROUPS, SSM_STATE, SSM_CONV_W = 16, 64, 2, 128, 4
SSM_D_INNER = SSM_HEADS * SSM_HEAD_DIM
FOX_HEADS, FOX_HEAD_DIM = 8, 128
FOX_SCALE = FOX_HEAD_DIM ** -0.5
N_BRANCH = 3

LANES = 128
SUBLANES = 8
MXU_DIM = 256
VMEM_LIMIT = 56 * 1024 * 1024

MLA_QK_PAD = MXU_DIM
INPROJ_TN = 512
NEG_BIG = -1e30
LOG2E = math.log2(math.e)
HI = lax.Precision.HIGHEST


def _tile(n, pref, align):
    t = (min(pref, n) // align) * align
    while t >= align:
        if n % t == 0:
            return t
        t -= align
    return n


def _params(*sem):
    return pltpu.CompilerParams(dimension_semantics=sem, vmem_limit_bytes=VMEM_LIMIT)


def _wshape(w):
    return w[0].shape[1:] if isinstance(w, tuple) else w.shape


def _wspec(w, block, index):
    if isinstance(w, tuple):
        arr, layer = w
        return arr, pl.BlockSpec((None,) + tuple(block), lambda *g: (layer,) + tuple(index(*g)))
    return w, pl.BlockSpec(tuple(block), index)


def _rms(x, g):
    return x * lax.rsqrt(jnp.mean(x * x, axis=-1, keepdims=True) + EPS) * g


def _softplus(x):
    return jnp.maximum(x, 0.0) + jnp.log1p(jnp.exp(-jnp.abs(x)))


def _rope_lanes(pe, cos, s1, s2):
    half = MLA_ROPE // 2
    return pe * cos + pltpu.roll(pe, LANES - half, 1) * s1 + pltpu.roll(pe, half, 1) * s2


def _mm_kernel(*refs, prologue, rope, residual, tn):
    it = iter(refs)
    x_ref = next(it)
    g_ref = next(it) if prologue == "rms" else None
    w_ref = next(it)
    res_ref = next(it) if residual else None
    tabs = (next(it), next(it), next(it)) if rope else None
    o_ref = next(it)
    xn_ref = next(it) if prologue != "none" else None

    if prologue == "none":
        lhs = x_ref[...]
    else:
        @pl.when(pl.program_id(1) == 0)
        def _():
            x = x_ref[...].astype(F32)
            if prologue == "rms":
                x = _rms(x, g_ref[...])
            xn_ref[...] = x.astype(BF16)
        lhs = xn_ref[...]
    acc = jnp.dot(lhs, w_ref[...], preferred_element_type=F32)
    if residual:
        acc = res_ref[...] + acc
    if rope:
        cos, s1, s2 = (t[...] for t in tabs)
        for c in range(tn // MLA_QK_PAD):
            a = c * MLA_QK_PAD
            o_ref[:, a:a + LANES] = acc[:, a:a + LANES].astype(o_ref.dtype)
            o_ref[:, a + LANES:a + 2 * LANES] = _rope_lanes(acc[:, a + LANES:a + 2 * LANES], cos, s1, s2).astype(o_ref.dtype)
    else:
        o_ref[...] = acc.astype(o_ref.dtype)


def _mm(x, w, *, out_dtype, tm=1024, tn=512, prologue="none", gain=None, residual=None, rope_tabs=None, name="mm"):
    m = x.shape[0]
    k, n = _wshape(w)
    assert x.shape[1] == k and (prologue != "none" or x.dtype == BF16)
    tm = _tile(m, tm, SUBLANES)
    tn = _tile(n, tn, MLA_QK_PAD if rope_tabs is not None else LANES)
    grid = (m // tm, n // tn)
    in_specs = [pl.BlockSpec((tm, k), lambda i, j: (i, 0))]
    args = [x]
    if prologue == "rms":
        in_specs.append(pl.BlockSpec((1, k), lambda i, j: (0, 0)))
        args.append(gain.reshape(1, k).astype(F32))
    w_arr, w_spec = _wspec(w, (k, tn), lambda i, j: (0, j))
    in_specs.append(w_spec)
    args.append(w_arr)
    if residual is not None:
        in_specs.append(pl.BlockSpec((tm, tn), lambda i, j: (i, j)))
        args.append(residual)
    if rope_tabs is not None:
        for t in rope_tabs:
            in_specs.append(pl.BlockSpec((tm, LANES), lambda i, j: (i, 0)))
            args.append(t)
    scratch = [pltpu.VMEM((tm, k), BF16)] if prologue != "none" else []
    kern = functools.partial(_mm_kernel, prologue=prologue, rope=rope_tabs is not None,
                             residual=residual is not None, tn=tn)
    return pl.pallas_call(
        kern, grid=grid, in_specs=in_specs,
        out_specs=pl.BlockSpec((tm, tn), lambda i, j: (i, j)),
        out_shape=jax.ShapeDtypeStruct((m, n), out_dtype),
        scratch_shapes=scratch, compiler_params=_params("parallel", "arbitrary"), name=name,
    )(*args)


def _inproj_kernel(xn_ref, w_ref, *outs, groups):
    j = pl.program_id(1)
    acc = jnp.dot(xn_ref[...], w_ref[...], preferred_element_type=F32)
    outs = iter(outs)
    for lo, hi, width, copies in groups:
        o_refs = [next(outs) for _ in range(copies)]

        @pl.when((j >= lo) & (j < hi))
        def _(o_refs=o_refs, width=width):
            for o_ref in o_refs:
                o_ref[...] = acc[:, :width].astype(o_ref.dtype)


def _inproj_layout(widths, dtypes, tn):
    groups, start = [], 0
    for n, dts in zip(widths, dtypes):
        nt = -(-n // tn)
        assert n % tn == 0 or n < tn
        groups.append((start, start + nt, min(n, tn), len(dts)))
        start += nt
    return groups, start


def _inproj(xn, w_cat, widths, dtypes, *, tm=1024, tn=512):
    m, d = xn.shape
    tm = _tile(m, tm, SUBLANES)
    groups, n_tiles = _inproj_layout(widths, dtypes, tn)
    assert _wshape(w_cat) == (d, n_tiles * tn)
    w_arr, w_spec = _wspec(w_cat, (d, tn), lambda i, j: (0, j))
    out_specs, out_shape = [], []
    for (lo, hi, bw, _), n, dts in zip(groups, widths, dtypes):
        for dt in dts:
            out_specs.append(pl.BlockSpec(
                (tm, bw), functools.partial(lambda i, j, lo, hi: (i, jnp.clip(j - lo, 0, hi - lo - 1)), lo=lo, hi=hi)))
            out_shape.append(jax.ShapeDtypeStruct((m, n), dt))
    return pl.pallas_call(
        functools.partial(_inproj_kernel, groups=groups), grid=(m // tm, n_tiles),
        in_specs=[pl.BlockSpec((tm, d), lambda i, j: (i, 0)), w_spec],
        out_specs=out_specs, out_shape=out_shape,
        compiler_params=_params("parallel", "arbitrary"), name="inproj",
    )(xn, w_arr)


def _ffn_kernel(*refs, nf, tf, f, post):
    x_ref, g_ref, wg_ref, wu_ref, wd_ref = refs[:5]
    refs = refs[5:]
    pg_ref = None
    if post is not None:
        pg_ref, refs = refs[0], refs[1:]
    outs, (xn_ref, acc_ref) = refs[:-2], refs[-2:]
    j = pl.program_id(1)

    @pl.when(j == 0)
    def _():
        xn_ref[...] = _rms(x_ref[...], g_ref[...]).astype(BF16)
        acc_ref[...] = jnp.zeros_like(acc_ref)

    xn = xn_ref[...]
    a = jnp.dot(xn, wg_ref[...], preferred_element_type=F32)
    b = jnp.dot(xn, wu_ref[...], preferred_element_type=F32)
    h = a * jax.nn.sigmoid(a) * b
    wd = wd_ref[...]
    if f % tf:
        valid = f - j * tf
        h = jnp.where(lax.broadcasted_iota(jnp.int32, h.shape, 1) < valid, h, 0.0)
        wd = jnp.where(lax.broadcasted_iota(jnp.int32, wd.shape, 0) < valid, wd, jnp.zeros_like(wd))
    acc_ref[...] += jnp.dot(h.astype(BF16), wd, preferred_element_type=F32)

    @pl.when(j == nf - 1)
    def _():
        y = x_ref[...] + FFN_RES * acc_ref[...]
        if post is None:
            outs[0][...] = y
        elif post == "norm_bf16":
            outs[0][...] = y
            outs[1][...] = _rms(y, pg_ref[...]).astype(BF16)
        else:
            outs[0][...] = _rms(y, pg_ref[...])


def _ffn(x, gain, wg, wu, wd, *, post=None, post_gain=None, tm=512, tf=512):
    m, d = x.shape
    f = _wshape(wg)[1]
    tm = _tile(m, tm, SUBLANES)
    nf = pl.cdiv(f, tf)
    row = pl.BlockSpec((tm, d), lambda i, j: (i, 0))
    vec = pl.BlockSpec((1, d), lambda i, j: (0, 0))
    wg_arr, wg_spec = _wspec(wg, (d, tf), lambda i, j: (0, j))
    wu_arr, wu_spec = _wspec(wu, (d, tf), lambda i, j: (0, j))
    wd_arr, wd_spec = _wspec(wd, (tf, d), lambda i, j: (j, 0))
    in_specs = [row, vec, wg_spec, wu_spec, wd_spec]
    args = [x, gain.reshape(1, d).astype(F32), wg_arr, wu_arr, wd_arr]
    out_specs, out_shape = [row], [jax.ShapeDtypeStruct((m, d), F32)]
    if post is not None:
        in_specs.append(vec)
        args.append(post_gain.reshape(1, d).astype(F32))
    if post == "norm_bf16":
        out_specs.append(row)
        out_shape.append(jax.ShapeDtypeStruct((m, d), BF16))
    return pl.pallas_call(
        functools.partial(_ffn_kernel, nf=nf, tf=tf, f=f, post=post), grid=(m // tm, nf),
        in_specs=in_specs, out_specs=out_specs, out_shape=out_shape,
        scratch_shapes=[pltpu.VMEM((tm, d), BF16), pltpu.VMEM((tm, d), F32)],
        compiler_params=_params("parallel", "arbitrary"), name="ffn",
    )(*args)


def _prep_kernel(uc_ref, g_ref, us_ref, cos_ref, s1_ref, s2_ref, fb_ref, ckv_ref, sm_ref, *, ff_lo, ff_hi):
    ckv_ref[...] = _rms(uc_ref[...], g_ref[...])
    us = us_ref[...]
    lane = lax.broadcasted_iota(jnp.int32, us.shape, 1)
    pe = jnp.where(lane < MLA_ROPE, us, 0.0)
    rot = _rope_lanes(pe, cos_ref[...], s1_ref[...], s2_ref[...])
    logf = -_softplus(-(us + fb_ref[...]))
    sm_ref[...] = jnp.where((lane >= ff_lo) & (lane < ff_hi), logf, rot)


def _prep(u_ckv, kv_gain, u_small, tabs, fb_lanes, *, ff_lo, ff_hi, tm=1024):
    m = u_small.shape[0]
    kv = kv_gain.shape[0]
    tm = _tile(m, tm, SUBLANES)
    row = lambda i: (i, 0)
    return pl.pallas_call(
        functools.partial(_prep_kernel, ff_lo=ff_lo, ff_hi=ff_hi), grid=(m // tm,),
        in_specs=[pl.BlockSpec((tm, kv), row),
                  pl.BlockSpec((1, kv), lambda i: (0, 0)),
                  pl.BlockSpec((tm, LANES), row), pl.BlockSpec((tm, LANES), row),
                  pl.BlockSpec((tm, LANES), row), pl.BlockSpec((tm, LANES), row),
                  pl.BlockSpec((1, LANES), lambda i: (0, 0))],
        out_specs=[pl.BlockSpec((tm, kv), row), pl.BlockSpec((tm, LANES), row)],
        out_shape=[jax.ShapeDtypeStruct((m, kv), F32), jax.ShapeDtypeStruct((m, LANES), F32)],
        compiler_params=_params("parallel"), name="prep",
    )(u_ckv, kv_gain.reshape(1, kv).astype(F32), u_small, *tabs, fb_lanes)


def _cumsum_kernel(x_ref, o_ref, carry_ref, *, tc, mult):
    @pl.when(pl.program_id(1) == 0)
    def _():
        carry_ref[...] = jnp.zeros_like(carry_ref)

    r = lax.broadcasted_iota(jnp.int32, (tc, tc), 0)
    c = lax.broadcasted_iota(jnp.int32, (tc, tc), 1)
    upper = (r <= c).astype(F32)
    y = jnp.dot(x_ref[0], upper, preferred_element_type=F32, precision=HI) + carry_ref[:, :1]
    o_ref[0] = y * mult
    carry_ref[...] = jnp.broadcast_to(y[:, tc - 1:tc], carry_ref.shape)


def _cumsum_last(x, mult, *, tc=512):
    b, h, s = x.shape
    tc = _tile(s, tc, LANES)
    return pl.pallas_call(
        functools.partial(_cumsum_kernel, tc=tc, mult=mult), grid=(b, s // tc),
        in_specs=[pl.BlockSpec((1, h, tc), lambda i, j: (i, 0, j))],
        out_specs=pl.BlockSpec((1, h, tc), lambda i, j: (i, 0, j)),
        out_shape=jax.ShapeDtypeStruct((b, h, s), F32),
        scratch_shapes=[pltpu.VMEM((h, LANES), F32)],
        compiler_params=_params("parallel", "arbitrary"), name="cumsum",
    )(x)


def _last_visible(q_end, mode):
    if mode == "chunk":
        return (q_end // CHUNK) * CHUNK + (CHUNK - 1)
    return q_end


def _attn_kernel(*refs, heads, dq, dk, dv, tq, tk, nk, scale, mode, q_off, n_valid, has_bias, has_shared):
    it = iter(refs)
    qi_ref, ki_ref = next(it), next(it)
    q_ref, k_ref, v_ref = next(it), next(it), next(it)
    ks_ref = next(it) if has_shared else None
    b_ref = next(it) if has_bias else None
    o_ref, m_ref, acc_ref = next(it), next(it), next(it)
    t = pl.program_id(1)
    qi, ki = qi_ref[t], ki_ref[t]
    nch = tk // LANES
    c = scale * LOG2E
    aw = dv + LANES

    @pl.when(ki == 0)
    def _():
        m_ref[...] = jnp.full_like(m_ref, NEG_BIG)
        acc_ref[...] = jnp.zeros_like(acc_ref)

    q_lo = q_off + qi * tq
    k_lo = ki * tk
    last_tile = jnp.minimum(_last_visible(q_lo + (tq - 1), mode) // tk, nk - 1)
    first_maskable = _last_visible(q_lo, mode) + 1
    ones_col = (lax.broadcasted_iota(jnp.int32, (tk, LANES), 1) == 0).astype(BF16)

    def body(masked):
        if masked:
            qpos = q_lo + lax.broadcasted_iota(jnp.int32, (tq, tk), 0)
            kpos = k_lo + lax.broadcasted_iota(jnp.int32, (tq, tk), 1)
            if mode == "chunk":
                sh = CHUNK.bit_length() - 1
                vis = lax.shift_right_logical(kpos, sh) <= lax.shift_right_logical(qpos, sh)
            else:
                vis = kpos <= qpos
            vis = vis & (kpos < n_valid)
        for h in range(heads):
            q = q_ref[0, :, h * dq:(h + 1) * dq]
            k = k_ref[0, :, h * dk:(h + 1) * dk]
            if has_shared:
                k = jnp.concatenate([k, ks_ref[0]], axis=1)
            s = lax.dot_general(q, k, (((1,), (1,)), ((), ())), preferred_element_type=F32)
            if has_bias:
                s = s + b_ref[0, h:h + 1, :]
            if masked:
                s = jnp.where(vis, s, NEG_BIG)
            m_prev = m_ref[h]
            mc = s[:, :LANES]
            for j in range(1, nch):
                mc = jnp.maximum(mc, s[:, j * LANES:(j + 1) * LANES])
            m_new = jnp.maximum(m_prev, jnp.max(mc, axis=1, keepdims=True))
            m_ref[h] = m_new
            alpha = jnp.exp2((m_prev - m_new) * c)
            p = jnp.concatenate([jnp.exp2((s[:, j * LANES:(j + 1) * LANES] - m_new) * c).astype(BF16)
                                 for j in range(nch)], axis=1)
            vx = jnp.concatenate([v_ref[0, :, h * dv:(h + 1) * dv], ones_col], axis=1)
            pv = jnp.dot(p, vx, preferred_element_type=F32)
            for a0 in range(h * aw, (h + 1) * aw, LANES):
                acc_ref[:, a0:a0 + LANES] = alpha * acc_ref[:, a0:a0 + LANES] + pv[:, a0 - h * aw:a0 - h * aw + LANES]

    need_mask = (k_lo + (tk - 1) >= first_maskable) | (k_lo + tk > n_valid)

    @pl.when(need_mask)
    def _():
        body(True)

    @pl.when(jnp.logical_not(need_mask))
    def _():
        body(False)

    @pl.when(ki == last_tile)
    def _():
        for h in range(heads):
            l = acc_ref[:, h * aw + dv:h * aw + dv + 1]
            o_ref[0, :, h * dv:(h + 1) * dv] = (acc_ref[:, h * aw:h * aw + dv] / l).astype(o_ref.dtype)


def _attention(q, k, v, k_shared, bias, *, heads, dq, dk, dv, scale, mode, q_off, n_valid, k_col=0, v_col=0,
               tq=1024, tk=1024):
    b, sq, _ = q.shape
    sk = k.shape[1]
    assert dk + (0 if k_shared is None else k_shared.shape[2]) == dq
    tq = _tile(sq, tq, SUBLANES)
    tk = sk if sk <= 2 * tk else _tile(sk, tk, LANES)
    nq, nk = sq // tq, sk // tk
    pairs = [(i, j) for i in range(nq)
             for j in range(min(_last_visible(q_off + i * tq + (tq - 1), mode) // tk, nk - 1) + 1)]
    qi_arr = jnp.asarray([p[0] for p in pairs], jnp.int32)
    ki_arr = jnp.asarray([p[1] for p in pairs], jnp.int32)

    in_specs = [pl.BlockSpec((1, tq, heads * dq), lambda bi, t, qi, ki: (bi, qi[t], 0)),
                pl.BlockSpec((1, tk, heads * dk), lambda bi, t, qi, ki: (bi, ki[t], k_col)),
                pl.BlockSpec((1, tk, heads * dv), lambda bi, t, qi, ki: (bi, ki[t], v_col))]
    args = [q, k, v]
    if k_shared is not None:
        in_specs.append(pl.BlockSpec((1, tk, dq - dk), lambda bi, t, qi, ki: (bi, ki[t], 0)))
        args.append(k_shared)
    if bias is not None:
        in_specs.append(pl.BlockSpec((1, heads, tk), lambda bi, t, qi, ki: (bi, 0, ki[t])))
        args.append(bias)
    kern = functools.partial(_attn_kernel, heads=heads, dq=dq, dk=dk, dv=dv, tq=tq, tk=tk, nk=nk, scale=scale,
                             mode=mode, q_off=q_off, n_valid=n_valid, has_bias=bias is not None,
                             has_shared=k_shared is not None)
    return pl.pallas_call(
        kern,
        grid_spec=pltpu.PrefetchScalarGridSpec(
            num_scalar_prefetch=2, grid=(b, len(pairs)), in_specs=in_specs,
            out_specs=pl.BlockSpec((1, tq, heads * dv), lambda bi, t, qi, ki: (bi, qi[t], 0)),
            scratch_shapes=[pltpu.VMEM((heads, tq, LANES), F32), pltpu.VMEM((tq, heads * (dv + LANES)), F32)]),
        out_shape=jax.ShapeDtypeStruct((b, sq, heads * dv), BF16),
        compiler_params=_params("parallel", "arbitrary"), name="attn_" + mode,
    )(qi_arr, ki_arr, *args)


def _decode_kernel(*refs, heads, dq, dk, dv, sq, p_len, scale, mode, has_shared, has_bias):
    it = iter(refs)
    q_ref, kp_ref, vp_ref, kn_ref, vn_ref = (next(it) for _ in range(5))
    ksp_ref, ksn_ref = (next(it), next(it)) if has_shared else (None, None)
    bp_ref, bn_ref = (next(it), next(it)) if has_bias else (None, None)
    o_ref = next(it)
    c = scale * LOG2E

    def head(ref, h, d):
        x = ref[:, h, :] if len(ref.shape) == 3 else ref[:, h * d:(h + 1) * d]
        return x.astype(BF16)

    row = lax.broadcasted_iota(jnp.int32, (sq, sq), 0)
    col = lax.broadcasted_iota(jnp.int32, (sq, sq), 1)
    if mode == "chunk":
        sh = CHUNK.bit_length() - 1
        vis = lax.shift_right_logical(p_len + col, sh) <= lax.shift_right_logical(p_len + row, sh)
    else:
        vis = col <= row
    if has_shared:
        pad = dq - dk - ksp_ref.shape[1]
        ksp = jnp.concatenate([ksp_ref[...].astype(BF16), jnp.zeros((p_len, pad), BF16)], axis=1)
        ksn = jnp.concatenate([ksn_ref[...].astype(BF16), jnp.zeros((sq, pad), BF16)], axis=1)
    contract_last = (((1,), (1,)), ((), ()))
    for h in range(heads):
        q = q_ref[:, h * dq:(h + 1) * dq]
        kp, kn = head(kp_ref, h, dk), head(kn_ref, h, dk)
        if has_shared:
            kp = jnp.concatenate([kp, ksp], axis=1)
            kn = jnp.concatenate([kn, ksn], axis=1)
        s_p = lax.dot_general(q, kp, contract_last, preferred_element_type=F32)
        s_n = lax.dot_general(q, kn, contract_last, preferred_element_type=F32)
        if has_bias:
            s_p = s_p + bp_ref[h:h + 1, :]
            s_n = s_n + bn_ref[h:h + 1, :]
        s_n = jnp.where(vis, s_n, NEG_BIG)
        m = jnp.maximum(jnp.max(s_p, axis=1, keepdims=True), jnp.max(s_n, axis=1, keepdims=True))
        p_p = jnp.exp2((s_p - m) * c)
        p_n = jnp.exp2((s_n - m) * c)
        l = jnp.sum(p_p, axis=1, keepdims=True) + jnp.sum(p_n, axis=1, keepdims=True)
        pv = (jnp.dot(p_p.astype(BF16), head(vp_ref, h, dv), preferred_element_type=F32)
              + jnp.dot(p_n.astype(BF16), head(vn_ref, h, dv), preferred_element_type=F32))
        o_ref[:, h * dv:(h + 1) * dv] = (pv / l).astype(o_ref.dtype)


def _decode_attention(q, past_kv, new_kv, shared, bias, *, heads, dq, dk, dv, scale, mode):
    b, sq, _ = q.shape

    def spec(arr, prefix, colblk, width):
        inner = arr.shape[len(prefix) + 1:]
        if len(inner) == 3:
            block, idx = inner, (0, 0, 0)
        else:
            block, idx = (inner[0], width), (0, colblk)
        return pl.BlockSpec((None,) * (len(prefix) + 1) + tuple(block), lambda i: tuple(prefix) + (i,) + idx)

    (kp, kp_pre, kp_col), (vp, vp_pre, vp_col) = past_kv
    (kn, kn_pre, kn_col), (vn, vn_pre, vn_col) = new_kv
    p_len = kp.shape[len(kp_pre) + 1]
    in_specs = [pl.BlockSpec((None, sq, heads * dq), lambda i: (i, 0, 0)),
                spec(kp, kp_pre, kp_col, heads * dk), spec(vp, vp_pre, vp_col, heads * dv),
                spec(kn, kn_pre, kn_col, heads * dk), spec(vn, vn_pre, vn_col, heads * dv)]
    args = [q, kp, vp, kn, vn]
    if shared is not None:
        for arr, pre in shared:
            in_specs.append(spec(arr, pre, 0, arr.shape[-1]))
            args.append(arr)
    if bias is not None:
        for arr in bias:
            in_specs.append(pl.BlockSpec((None,) + arr.shape[1:], lambda i: (i, 0, 0)))
            args.append(arr)
    kern = functools.partial(_decode_kernel, heads=heads, dq=dq, dk=dk, dv=dv, sq=sq, p_len=p_len, scale=scale,
                             mode=mode, has_shared=shared is not None, has_bias=bias is not None)
    return pl.pallas_call(
        kern, grid=(b,), in_specs=in_specs,
        out_specs=pl.BlockSpec((None, sq, heads * dv), lambda i: (i, 0, 0)),
        out_shape=jax.ShapeDtypeStruct((b, sq, heads * dv), BF16),
        compiler_params=_params("parallel"), name="decode_" + mode,
    )(*args)


def _ssd_kernel(z_ref, xbc_ref, dt_ref, dtt_ref, cw_ref, cb_ref, dtb_ref, dtbt_ref, al_ref, alt_ref, dx_ref, nw_ref,
                h0_ref, c0_ref, y_ref, hout_ref, state_ref, carry_ref, *, lc, nc):
    c = pl.program_id(1)
    gw = SSM_D_INNER // SSM_GROUPS
    hpg = SSM_HEADS // SSM_GROUPS
    halo = SUBLANES

    @pl.when(c == 0)
    def _():
        state_ref[...] = h0_ref[0]
        carry_ref[...] = c0_ref[0]

    x = xbc_ref[0]
    cat = jnp.concatenate([carry_ref[...], x], axis=0)
    conv = cb_ref[...]
    for kk in range(SSM_CONV_W):
        shift = SSM_CONV_W - 1 - kk
        src = pltpu.roll(cat, shift, 0) if shift else cat
        conv = conv + src[halo:, :] * cw_ref[kk:kk + 1, :]
    carry_ref[...] = x[lc - halo:, :]
    act = conv * jax.nn.sigmoid(conv)
    xs = act[:, :SSM_D_INNER]
    bm = act[:, SSM_D_INNER:SSM_D_INNER + SSM_GROUPS * SSM_STATE]
    cm = act[:, SSM_D_INNER + SSM_GROUPS * SSM_STATE:]

    dt = _softplus(dt_ref[0] + dtb_ref[...])
    dtt = _softplus(dtt_ref[0] + dtbt_ref[...])
    adt = dt * (-jnp.exp(al_ref[...]))
    adtt = dtt * (-jnp.exp(alt_ref[...]))
    r = lax.broadcasted_iota(jnp.int32, (lc, lc), 0)
    cc = lax.broadcasted_iota(jnp.int32, (lc, lc), 1)
    tril = cc <= r
    acs = jnp.dot(tril.astype(F32), adt, preferred_element_type=F32, precision=HI)
    acst = jnp.dot(adtt, (r <= cc).astype(F32), preferred_element_type=F32, precision=HI)
    hh = lax.broadcasted_iota(jnp.int32, (SSM_HEADS, SSM_D_INNER), 0)
    ll = lax.broadcasted_iota(jnp.int32, (SSM_HEADS, SSM_D_INNER), 1)
    expand = ((ll >= hh * SSM_HEAD_DIM) & (ll < (hh + 1) * SSM_HEAD_DIM)).astype(F32)
    dt_x = jnp.dot(dt, expand, preferred_element_type=F32, precision=HI)
    acs_x = jnp.dot(acs, expand, preferred_element_type=F32, precision=HI)
    tot_x = acs_x[lc - 1:lc, :]
    xdt = xs * dt_x
    xdt_b = xdt.astype(BF16)
    w_end = (xdt * jnp.exp(tot_x - acs_x)).astype(BF16)
    state = state_ref[...]
    state_b = state.astype(BF16)

    y_parts, new_parts = [], []
    for g in range(SSM_GROUPS):
        bg = bm[:, g * SSM_STATE:(g + 1) * SSM_STATE].astype(BF16)
        cg = cm[:, g * SSM_STATE:(g + 1) * SSM_STATE].astype(BF16)
        cb = lax.dot_general(cg, bg, (((1,), (1,)), ((), ())), preferred_element_type=F32)
        for hl in range(hpg):
            h = g * hpg + hl
            seg = acs[:, h:h + 1] - acst[h:h + 1, :]
            mh = (cb * jnp.exp(jnp.where(tril, seg, NEG_BIG))).astype(BF16)
            y_parts.append(jnp.dot(mh, xdt_b[:, h * SSM_HEAD_DIM:(h + 1) * SSM_HEAD_DIM], preferred_element_type=F32))
        new_parts.append(lax.dot_general(bg, w_end[:, g * gw:(g + 1) * gw], (((0,), (0,)), ((), ())),
                                         preferred_element_type=F32))
    y_off = jnp.concatenate(
        [jnp.dot(cm[:, g * SSM_STATE:(g + 1) * SSM_STATE].astype(BF16), state_b[:, g * gw:(g + 1) * gw],
                 preferred_element_type=F32) for g in range(SSM_GROUPS)], axis=1) * jnp.exp(acs_x)
    y = jnp.concatenate(y_parts, axis=1) + y_off + dx_ref[...] * xs
    state_ref[...] = jnp.exp(tot_x) * state + jnp.concatenate(new_parts, axis=1)

    zz = z_ref[0]
    y = y * (zz * jax.nn.sigmoid(zz))
    for g in range(SSM_GROUPS):
        y_ref[0, :, g * gw:(g + 1) * gw] = _rms(y[:, g * gw:(g + 1) * gw], nw_ref[:, g * gw:(g + 1) * gw]).astype(y_ref.dtype)

    @pl.when(c == nc - 1)
    def _():
        hout_ref[0] = state_ref[...]


def _ssd(z, xbc, dt, dtt, p, h0, c0, *, lc=256):
    b, s, cd = xbc.shape
    lc = _tile(s, lc, LANES) if s % LANES == 0 else s
    nc = s // lc
    hh = SSM_HEADS
    full2 = lambda shape: pl.BlockSpec(shape, lambda i, j: (0, 0))
    return pl.pallas_call(
        functools.partial(_ssd_kernel, lc=lc, nc=nc), grid=(b, nc),
        in_specs=[pl.BlockSpec((1, lc, SSM_D_INNER), lambda i, j: (i, j, 0)),
                  pl.BlockSpec((1, lc, cd), lambda i, j: (i, j, 0)),
                  pl.BlockSpec((1, lc, hh), lambda i, j: (i, j, 0)),
                  pl.BlockSpec((1, hh, lc), lambda i, j: (i, 0, j)),
                  full2((SSM_CONV_W, cd)), full2((1, cd)),
                  full2((1, hh)), full2((hh, 1)), full2((1, hh)), full2((hh, 1)),
                  full2((1, SSM_D_INNER)), full2((1, SSM_D_INNER)),
                  pl.BlockSpec((1, SSM_STATE, SSM_D_INNER), lambda i, j: (i, 0, 0)),
                  pl.BlockSpec((1, SUBLANES, cd), lambda i, j: (i, 0, 0))],
        out_specs=[pl.BlockSpec((1, lc, SSM_D_INNER), lambda i, j: (i, j, 0)),
                   pl.BlockSpec((1, SSM_STATE, SSM_D_INNER), lambda i, j: (i, 0, 0))],
        out_shape=[jax.ShapeDtypeStruct((b, s, SSM_D_INNER), BF16),
                   jax.ShapeDtypeStruct((b, SSM_STATE, SSM_D_INNER), F32)],
        scratch_shapes=[pltpu.VMEM((SSM_STATE, SSM_D_INNER), F32), pltpu.VMEM((SUBLANES, cd), F32)],
        compiler_params=_params("parallel", "arbitrary"), name="ssd",
    )(z, xbc, dt, dtt, p["conv_w"], p["conv_b"], p["dt_b"], p["dt_bt"], p["a_log"], p["a_logt"], p["d_x"], p["norm_w"],
      h0, c0)


def _merge_kernel(xn_ref, o0, o1, o2, w0, w1, w2, wg0, wg1, wg2, out_ref):
    xn = xn_ref[...]
    acc = None
    for o_ref, w_ref, wg_ref in ((o0, w0, wg0), (o1, w1, wg1), (o2, w2, wg2)):
        gate = jax.nn.sigmoid(jnp.dot(xn, wg_ref[...], preferred_element_type=F32))
        t = gate * jnp.dot(o_ref[...], w_ref[...], preferred_element_type=F32)
        acc = t if acc is None else acc + t
    out_ref[...] = acc.astype(out_ref.dtype)


def _merge(xn, o_list, w_list, w_gate, *, tm=1024, tn=512):
    m, d = xn.shape
    tm = _tile(m, tm, SUBLANES)
    tn = _tile(d, tn, LANES)
    nb = d // tn
    in_specs = [pl.BlockSpec((tm, d), lambda i, j: (i, 0))]
    in_specs += [pl.BlockSpec((tm, o.shape[1]), lambda i, j: (i, 0)) for o in o_list]
    w_args = []
    for w in w_list:
        arr, sp = _wspec(w, (_wshape(w)[0], tn), lambda i, j: (0, j))
        in_specs.append(sp)
        w_args.append(arr)
    for br in range(N_BRANCH):
        arr, sp = _wspec(w_gate, (d, tn), functools.partial(lambda i, j, br: (0, br * nb + j), br=br))
        in_specs.append(sp)
        w_args.append(arr)
    return pl.pallas_call(
        _merge_kernel, grid=(m // tm, nb), in_specs=in_specs,
        out_specs=pl.BlockSpec((tm, tn), lambda i, j: (i, j)),
        out_shape=jax.ShapeDtypeStruct((m, d), BF16),
        compiler_params=_params("parallel", "arbitrary"), name="merge",
    )(xn, *o_list, *w_args)


def _stacked_weights(a):
    depth, d_model = a["w_in"].shape[:2]
    q_lora, kv_lora = a["mla_q_norm"].shape[1], a["mla_kv_norm"].shape[1]
    conv_dim = a["ssm_conv_w"].shape[2]
    sizes = (q_lora, kv_lora, MLA_ROPE, SSM_D_INNER, conv_dim, SSM_HEADS,
             FOX_HEADS * FOX_HEAD_DIM, FOX_HEADS * FOX_HEAD_DIM, FOX_HEADS * FOX_HEAD_DIM, FOX_HEADS,
             N_BRANCH * d_model)
    w_in = a["w_in"]
    assert w_in.shape[2] == sum(sizes)
    cols, start = [], 0
    for n in sizes:
        cols.append(w_in[:, :, start:start + n])
        start += n
    w_q, w_ckv, w_kpe, w_z, w_xbc, w_dt, w_fq, w_fk, w_fv, w_ff, w_gate = cols
    bf = lambda t: t.astype(BF16)
    small = [w_kpe, w_dt, w_ff]
    n_small = sum(t.shape[2] for t in small)
    assert n_small <= LANES
    parts = [w_q, w_ckv, *small, jnp.zeros((depth, d_model, INPROJ_TN - n_small), w_in.dtype),
             w_z, w_xbc, w_fq, w_fk, w_fv]
    st = {"in_cat": bf(jnp.concatenate(parts, axis=2)), "in_gate": bf(w_gate)}
    wq = a["mla_w_uq"].reshape(depth, q_lora, MLA_HEADS, MLA_NOPE + MLA_ROPE)
    wq = jnp.pad(wq, ((0, 0), (0, 0), (0, 0), (0, MLA_QK_PAD - MLA_NOPE - MLA_ROPE)))
    st["uq"] = bf(wq.reshape(depth, q_lora, MLA_HEADS * MLA_QK_PAD))
    wkv = a["mla_w_ukv"].reshape(depth, kv_lora, MLA_HEADS, MLA_NOPE + MLA_V)
    st["ukv"] = bf(jnp.concatenate([wkv[..., :MLA_NOPE].reshape(depth, kv_lora, MLA_HEADS * MLA_NOPE),
                                    wkv[..., MLA_NOPE:].reshape(depth, kv_lora, MLA_HEADS * MLA_V)], axis=2))
    for nm in ("w_br_mla", "w_br_ssd", "w_br_fox", "w_out"):
        st[nm] = bf(a[nm])
    for pre in ("ffn1", "ffn2"):
        for src, dst in (("_w_gate", "_g"), ("_w_up", "_u"), ("_w_down", "_d")):
            st[pre + dst] = bf(a[pre + src])
    return st


def _layer_weights(l, a, stacked):
    q_lora, kv_lora = a["mla_q_norm"].shape[1], a["mla_kv_norm"].shape[1]
    conv_dim = a["ssm_conv_w"].shape[2]
    w = {name: (arr, l) for name, arr in stacked.items()}
    w["in_widths"] = (q_lora, kv_lora, LANES, SSM_D_INNER, conv_dim) + (FOX_HEADS * FOX_HEAD_DIM,) * 3
    for pre in ("ffn1", "ffn2"):
        w[pre + "_norm"] = a[pre + "_norm"][l]
    w["mix_norm"] = a["mix_norm"][l]
    w["q_norm"], w["kv_norm"] = a["mla_q_norm"][l], a["mla_kv_norm"][l]
    ff_lo = MLA_ROPE + SSM_HEADS
    w["fb_lanes"] = jnp.pad(a["fox_b_f"][l].astype(F32), (ff_lo, LANES - ff_lo - FOX_HEADS)).reshape(1, LANES)
    w["ssd"] = {
        "conv_w": a["ssm_conv_w"][l].astype(F32), "conv_b": a["ssm_conv_b"][l].astype(F32).reshape(1, conv_dim),
        "dt_b": a["ssm_dt_bias"][l].astype(F32).reshape(1, SSM_HEADS),
        "dt_bt": a["ssm_dt_bias"][l].astype(F32).reshape(SSM_HEADS, 1),
        "a_log": a["ssm_a_log"][l].astype(F32).reshape(1, SSM_HEADS),
        "a_logt": a["ssm_a_log"][l].astype(F32).reshape(SSM_HEADS, 1),
        "d_x": jnp.repeat(a["ssm_d"][l].astype(F32), SSM_HEAD_DIM).reshape(1, SSM_D_INNER),
        "norm_w": a["ssm_norm"][l].astype(F32).reshape(1, SSM_D_INNER),
    }
    return w


def _rope_tables(pos):
    half = MLA_ROPE // 2
    inv_freq = ROPE_BASE ** (-jnp.arange(half, dtype=F32) / half)
    ang = pos.astype(F32)[:, None] * inv_freq[None, :]
    cos, sin = jnp.cos(ang), jnp.sin(ang)
    z = jnp.zeros_like(cos)
    pad = jnp.zeros((pos.shape[0], LANES - MLA_ROPE), F32)
    return (jnp.concatenate([cos, cos, pad], axis=1),
            jnp.concatenate([-sin, z, pad], axis=1),
            jnp.concatenate([z, sin, pad], axis=1))


def _pad_keys(t, sk_pad):
    return jnp.pad(t, ((0, 0), (0, sk_pad - t.shape[1])) + ((0, 0),) * (t.ndim - 2))


def _layer(x, bsz, s, w, tabs, past, final_gain):
    m = bsz * s
    kv_lora = w["kv_norm"].shape[0]
    conv_dim = w["ssd"]["conv_w"].shape[1]
    x, xn = _ffn(x, w["ffn1_norm"], w["ffn1_g"], w["ffn1_u"], w["ffn1_d"], post="norm_bf16", post_gain=w["mix_norm"])

    u_q, u_ckv, u_small, u_z, u_xbc, fq, fk, fk_b, fv, fv_b = _inproj(
        xn, w["in_cat"], w["in_widths"], ((F32,),) * 5 + ((BF16,), (F32, BF16), (F32, BF16)), tn=INPROJ_TN)

    ff_lo = MLA_ROPE + SSM_HEADS
    ckv_new, small2 = _prep(u_ckv, w["kv_norm"], u_small, tabs, w["fb_lanes"], ff_lo=ff_lo, ff_hi=ff_lo + FOX_HEADS)
    kpe_new = small2[:, :MLA_ROPE]
    logf_new = small2[:, ff_lo:ff_lo + FOX_HEADS]
    u_dt = u_small[:, MLA_ROPE:ff_lo]

    if past is not None:
        caches, l = past
        past_len = caches["mla_ckv"].shape[2]
    ukv = functools.partial(_mm, w=w["ukv"], out_dtype=BF16, prologue="cast", tn=1024, name="mla_ukv")

    q_full = _mm(u_q, w["uq"], out_dtype=BF16, prologue="rms", gain=w["q_norm"], rope_tabs=tabs, name="mla_q")
    q_full = q_full.reshape(bsz, s, -1)
    kv_new = ukv(ckv_new).reshape(bsz, s, -1)
    mla = dict(heads=MLA_HEADS, dq=MLA_QK_PAD, dk=MLA_NOPE, dv=MLA_V, scale=MLA_SCALE, mode="chunk")
    sk_pad = -(-s // LANES) * LANES
    if past is None:
        kpe_pad = jnp.pad(kpe_new.astype(BF16).reshape(bsz, s, -1),
                          ((0, 0), (0, sk_pad - s), (0, MLA_QK_PAD - MLA_NOPE - MLA_ROPE)))
        kv_all = _pad_keys(kv_new, sk_pad)
        o_mla = _attention(q_full, kv_all, kv_all, kpe_pad, None, k_col=0, v_col=1, q_off=0, n_valid=s, **mla)
    else:
        kv_past = ukv(caches["mla_ckv"][l].reshape(bsz * past_len, kv_lora)).reshape(bsz, past_len, -1)
        o_mla = _decode_attention(
            q_full, ((kv_past, (), 0), (kv_past, (), 1)), ((kv_new, (), 0), (kv_new, (), 1)),
            ((caches["mla_kpe"], (l,)), (kpe_new.reshape(bsz, s, MLA_ROPE), ())), None, **mla)

    if past is None:
        conv_state = jnp.zeros((bsz, SSM_CONV_W - 1, conv_dim), F32)
        h0 = jnp.zeros((bsz, SSM_STATE, SSM_D_INNER), F32)
    else:
        conv_state = caches["conv"][l].astype(F32)
        h0 = jnp.transpose(caches["ssm"][l].astype(F32), (0, 3, 1, 2)).reshape(bsz, SSM_STATE, SSM_D_INNER)
    c0 = jnp.pad(conv_state, ((0, 0), (SUBLANES - (SSM_CONV_W - 1), 0), (0, 0)))
    xbc3 = u_xbc.reshape(bsz, s, conv_dim)
    dt3 = u_dt.reshape(bsz, s, SSM_HEADS)
    o_ssd, h_new = _ssd(u_z.reshape(bsz, s, SSM_D_INNER), xbc3, dt3, jnp.swapaxes(dt3, 1, 2), w["ssd"], h0, c0)
    ssm_new = jnp.transpose(h_new.reshape(bsz, SSM_STATE, SSM_HEADS, SSM_HEAD_DIM), (0, 2, 3, 1))
    keep = SSM_CONV_W - 1
    conv_new = xbc3[:, s - keep:] if s >= keep else jnp.concatenate([conv_state, xbc3], axis=1)[:, -keep:]

    hw = FOX_HEADS * FOX_HEAD_DIM
    fox = dict(heads=FOX_HEADS, dq=FOX_HEAD_DIM, dk=FOX_HEAD_DIM, dv=FOX_HEAD_DIM, scale=FOX_SCALE, mode="causal")
    logf_all = logf_new.reshape(bsz, s, FOX_HEADS)
    if past is not None:
        logf_all = jnp.concatenate([caches["fox_logf"][l].astype(F32), logf_all], axis=1)
    n_keys = logf_all.shape[1]
    neg_cum = _cumsum_last(jnp.swapaxes(_pad_keys(logf_all, -(-n_keys // LANES) * LANES), 1, 2), -1.0 / FOX_SCALE)
    fq3 = fq.reshape(bsz, s, hw)
    if past is None:
        o_fox = _attention(fq3, _pad_keys(fk_b.reshape(bsz, s, hw), sk_pad), _pad_keys(fv_b.reshape(bsz, s, hw), sk_pad),
                           None, neg_cum, q_off=0, n_valid=s, **fox)
    else:
        o_fox = _decode_attention(
            fq3, ((caches["fox_k"], (l,), 0), (caches["fox_v"], (l,), 0)),
            ((fk.reshape(bsz, s, hw), (), 0), (fv.reshape(bsz, s, hw), (), 0)), None,
            (neg_cum[:, :, :past_len], neg_cum[:, :, past_len:n_keys]), **fox)

    merged = _merge(xn, [o_mla.reshape(m, -1), o_ssd.reshape(m, -1), o_fox.reshape(m, -1)],
                    [w["w_br_mla"], w["w_br_ssd"], w["w_br_fox"]], w["in_gate"])
    x = _mm(merged, w["w_out"], out_dtype=F32, residual=x, name="out_proj")
    (x,) = _ffn(x, w["ffn2_norm"], w["ffn2_g"], w["ffn2_u"], w["ffn2_d"],
                post=None if final_gain is None else "norm_only", post_gain=final_gain)
    state = (ckv_new.reshape(bsz, s, kv_lora), kpe_new.reshape(bsz, s, MLA_ROPE),
             fk.reshape(bsz, s, FOX_HEADS, FOX_HEAD_DIM), fv.reshape(bsz, s, FOX_HEADS, FOX_HEAD_DIM),
             logf_new.reshape(bsz, s, FOX_HEADS), ssm_new, conv_new)
    return x, state


def kernel(x_prompt, x_sample, cache_mla_ckv, cache_mla_kpe, cache_fox_k, cache_fox_v, cache_fox_logf, state_ssm,
           state_conv, ffn1_norm, ffn1_w_gate, ffn1_w_up, ffn1_w_down, mix_norm, w_in, mla_q_norm, mla_w_uq,
           mla_kv_norm, mla_w_ukv, ssm_conv_w, ssm_conv_b, ssm_dt_bias, ssm_a_log, ssm_d, ssm_norm, fox_b_f,
           w_br_mla, w_br_ssd, w_br_fox, w_out, ffn2_norm, ffn2_w_gate, ffn2_w_up, ffn2_w_down, final_norm):
    a = dict(ffn1_norm=ffn1_norm, ffn1_w_gate=ffn1_w_gate, ffn1_w_up=ffn1_w_up, ffn1_w_down=ffn1_w_down,
             mix_norm=mix_norm, w_in=w_in, mla_q_norm=mla_q_norm, mla_w_uq=mla_w_uq, mla_kv_norm=mla_kv_norm,
             mla_w_ukv=mla_w_ukv, ssm_conv_w=ssm_conv_w, ssm_conv_b=ssm_conv_b, ssm_dt_bias=ssm_dt_bias,
             ssm_a_log=ssm_a_log, ssm_d=ssm_d, ssm_norm=ssm_norm, fox_b_f=fox_b_f, w_br_mla=w_br_mla,
             w_br_ssd=w_br_ssd, w_br_fox=w_br_fox, w_out=w_out, ffn2_norm=ffn2_norm, ffn2_w_gate=ffn2_w_gate,
             ffn2_w_up=ffn2_w_up, ffn2_w_down=ffn2_w_down)
    depth = w_in.shape[0]
    stacked = _stacked_weights(a)
    bp, sp, d_model = x_prompt.shape
    bs, ss, _ = x_sample.shape
    past_len = cache_mla_ckv.shape[2]
    tabs_p = tuple(jnp.tile(t, (bp, 1)) for t in _rope_tables(jnp.arange(sp, dtype=jnp.int32)))
    tabs_s = tuple(jnp.tile(t, (bs, 1)) for t in _rope_tables(past_len + jnp.arange(ss, dtype=jnp.int32)))
    hp = x_prompt.reshape(bp * sp, d_model).astype(F32)
    hs = x_sample.reshape(bs * ss, d_model).astype(F32)
    caches = {"mla_ckv": cache_mla_ckv, "mla_kpe": cache_mla_kpe, "fox_k": cache_fox_k, "fox_v": cache_fox_v,
              "fox_logf": cache_fox_logf, "ssm": state_ssm, "conv": state_conv}
    new_p, new_s = [], []
    for l in range(depth):
        w = _layer_weights(l, a, stacked)
        fg = final_norm if l == depth - 1 else None
        hp, st_p = _layer(hp, bp, sp, w, tabs_p, None, fg)
        hs, st_s = _layer(hs, bs, ss, w, tabs_s, (caches, l), fg)
        new_p.append(st_p)
        new_s.append(st_s)
    y_prompt = hp.reshape(bp, sp, d_model)
    y_sample = hs.reshape(bs, ss, d_model)
    stk = lambda states, i: jnp.stack([st[i] for st in states], axis=0)
    return (y_prompt, y_sample) + tuple(stk(new_p, i) for i in range(7)) + tuple(stk(new_s, i) for i in range(7))
```

```python
import functools
import math

import jax
import jax.numpy as jnp
from jax import lax
from jax.experimental import pallas as pl
from jax.experimental.pallas import tpu as pltpu

F32 = jnp.float32
BF16 = jnp.bfloat16

EPS = 1e-6
CHUNK = 64
FFN_RES = 0.5
MLA_HEADS, MLA_NOPE, MLA_ROPE, MLA_V = 8, 128, 64, 128
MLA_SCALE = (MLA_NOPE + MLA_ROPE) ** -0.5
ROPE_BASE = 10000.0
SSM_HEADS, SSM_HEAD_DIM, SSM_GROUPS, SSM_STATE, SSM_CONV_W = 16, 64, 2, 128, 4
SSM_D_INNER = SSM_HEADS * SSM_HEAD_DIM
FOX_HEADS, FOX_HEAD_DIM = 8, 128
FOX_SCALE = FOX_HEAD_DIM ** -0.5
N_BRANCH = 3

LANES = 128
SUBLANES = 8
MXU_DIM = 256
VMEM_LIMIT = 56 * 1024 * 1024

MLA_QK_PAD = MXU_DIM
INPROJ_TN = 512
NEG_BIG = -1e30
LOG2E = math.log2(math.e)
HI = lax.Precision.HIGHEST


def _tile(n, pref, align):
    t = (min(pref, n) // align) * align
    while t >= align:
        if n % t == 0:
            return t
        t -= align
    return n


def _params(*sem):
    return pltpu.CompilerParams(dimension_semantics=sem, vmem_limit_bytes=VMEM_LIMIT)


def _wshape(w):
    return w[0].shape[1:] if isinstance(w, tuple) else w.shape


def _wspec(w, block, index):
    if isinstance(w, tuple):
        arr, layer = w
        return arr, pl.BlockSpec((None,) + tuple(block), lambda *g: (layer,) + tuple(index(*g)))
    return w, pl.BlockSpec(tuple(block), index)


def _rms(x, g):
    return x * lax.rsqrt(jnp.mean(x * x, axis=-1, keepdims=True) + EPS) * g


def _softplus(x):
    return jnp.maximum(x, 0.0) + jnp.log1p(jnp.exp(-jnp.abs(x)))


def _rope_lanes(pe, cos, s1, s2):
    half = MLA_ROPE // 2
    return pe * cos + pltpu.roll(pe, LANES - half, 1) * s1 + pltpu.roll(pe, half, 1) * s2


def _mm_kernel(*refs, prologue, rope, residual, tn):
    it = iter(refs)
    x_ref = next(it)
    g_ref = next(it) if prologue == "rms" else None
    w_ref = next(it)
    res_ref = next(it) if residual else None
    tabs = (next(it), next(it), next(it)) if rope else None
    o_ref = next(it)
    xn_ref = next(it) if prologue != "none" else None

    if prologue == "none":
        lhs = x_ref[...]
    else:
        @pl.when(pl.program_id(1) == 0)
        def _():
            x = x_ref[...].astype(F32)
            if prologue == "rms":
                x = _rms(x, g_ref[...])
            xn_ref[...] = x.astype(BF16)
        lhs = xn_ref[...]
    acc = jnp.dot(lhs, w_ref[...], preferred_element_type=F32)
    if residual:
        acc = res_ref[...] + acc
    if rope:
        cos, s1, s2 = (t[...] for t in tabs)
        for c in range(tn // MLA_QK_PAD):
            a = c * MLA_QK_PAD
            o_ref[:, a:a + LANES] = acc[:, a:a + LANES].astype(o_ref.dtype)
            o_ref[:, a + LANES:a + 2 * LANES] = _rope_lanes(acc[:, a + LANES:a + 2 * LANES], cos, s1, s2).astype(o_ref.dtype)
    else:
        o_ref[...] = acc.astype(o_ref.dtype)


def _mm(x, w, *, out_dtype, tm=1024, tn=512, prologue="none", gain=None, residual=None, rope_tabs=None, name="mm"):
    m = x.shape[0]
    k, n = _wshape(w)
    assert x.shape[1] == k and (prologue != "none" or x.dtype == BF16)
    tm = _tile(m, tm, SUBLANES)
    tn = _tile(n, tn, MLA_QK_PAD if rope_tabs is not None else LANES)
    grid = (m // tm, n // tn)
    in_specs = [pl.BlockSpec((tm, k), lambda i, j: (i, 0))]
    args = [x]
    if prologue == "rms":
        in_specs.append(pl.BlockSpec((1, k), lambda i, j: (0, 0)))
        args.append(gain.reshape(1, k).astype(F32))
    w_arr, w_spec = _wspec(w, (k, tn), lambda i, j: (0, j))
    in_specs.append(w_spec)
    args.append(w_arr)
    if residual is not None:
        in_specs.append(pl.BlockSpec((tm, tn), lambda i, j: (i, j)))
        args.append(residual)
    if rope_tabs is not None:
        for t in rope_tabs:
            in_specs.append(pl.BlockSpec((tm, LANES), lambda i, j: (i, 0)))
            args.append(t)
    scratch = [pltpu.VMEM((tm, k), BF16)] if prologue != "none" else []
    kern = functools.partial(_mm_kernel, prologue=prologue, rope=rope_tabs is not None,
                             residual=residual is not None, tn=tn)
    return pl.pallas_call(
        kern, grid=grid, in_specs=in_specs,
        out_specs=pl.BlockSpec((tm, tn), lambda i, j: (i, j)),
        out_shape=jax.ShapeDtypeStruct((m, n), out_dtype),
        scratch_shapes=scratch, compiler_params=_params("parallel", "arbitrary"), name=name,
    )(*args)


def _inproj_kernel(xn_ref, w_ref, *outs, groups):
    j = pl.program_id(1)
    outs = iter(outs)
    for lo, hi, width, copies in groups:
        o_refs = [next(outs) for _ in range(copies)]

        @pl.when((j >= lo) & (j < hi))
        def _(o_refs=o_refs, width=width):
            acc = jnp.dot(xn_ref[...], w_ref[:, :width], preferred_element_type=F32)
            for o_ref in o_refs:
                o_ref[...] = acc.astype(o_ref.dtype)


def _inproj_layout(widths, dtypes, tn):
    groups, start = [], 0
    for n, dts in zip(widths, dtypes):
        nt = -(-n // tn)
        assert n % tn == 0 or n < tn
        groups.append((start, start + nt, min(n, tn), len(dts)))
        start += nt
    return groups, start


def _inproj(xn, w_cat, widths, dtypes, *, tm=1024, tn=512):
    m, d = xn.shape
    tm = _tile(m, tm, SUBLANES)
    groups, n_tiles = _inproj_layout(widths, dtypes, tn)
    assert _wshape(w_cat) == (d, n_tiles * tn)
    w_arr, w_spec = _wspec(w_cat, (d, tn), lambda i, j: (0, j))
    out_specs, out_shape = [], []
    for (lo, hi, bw, _), n, dts in zip(groups, widths, dtypes):
        for dt in dts:
            out_specs.append(pl.BlockSpec(
                (tm, bw), functools.partial(lambda i, j, lo, hi: (i, jnp.clip(j - lo, 0, hi - lo - 1)), lo=lo, hi=hi)))
            out_shape.append(jax.ShapeDtypeStruct((m, n), dt))
    return pl.pallas_call(
        functools.partial(_inproj_kernel, groups=groups), grid=(m // tm, n_tiles),
        in_specs=[pl.BlockSpec((tm, d), lambda i, j: (i, 0)), w_spec],
        out_specs=out_specs, out_shape=out_shape,
        compiler_params=_params("parallel", "arbitrary"), name="inproj",
    )(xn, w_arr)


def _ffn_kernel(*refs, nf, tf, f, post):
    x_ref, g_ref, wg_ref, wu_ref, wd_ref = refs[:5]
    refs = refs[5:]
    pg_ref = None
    if post is not None:
        pg_ref, refs = refs[0], refs[1:]
    outs, (xn_ref, acc_ref) = refs[:-2], refs[-2:]
    j = pl.program_id(1)

    @pl.when(j == 0)
    def _():
        xn_ref[...] = _rms(x_ref[...], g_ref[...]).astype(BF16)
        acc_ref[...] = jnp.zeros_like(acc_ref)

    xn = xn_ref[...]
    a = jnp.dot(xn, wg_ref[...], preferred_element_type=F32)
    b = jnp.dot(xn, wu_ref[...], preferred_element_type=F32)
    h = a * jax.nn.sigmoid(a) * b
    wd = wd_ref[...]
    if f % tf:
        valid = f - j * tf
        h = jnp.where(lax.broadcasted_iota(jnp.int32, h.shape, 1) < valid, h, 0.0)
        wd = jnp.where(lax.broadcasted_iota(jnp.int32, wd.shape, 0) < valid, wd, jnp.zeros_like(wd))
    acc_ref[...] += jnp.dot(h.astype(BF16), wd, preferred_element_type=F32)

    @pl.when(j == nf - 1)
    def _():
        y = x_ref[...] + FFN_RES * acc_ref[...]
        if post is None:
            outs[0][...] = y
        elif post == "norm_bf16":
            outs[0][...] = y
            outs[1][...] = _rms(y, pg_ref[...]).astype(BF16)
        else:
            outs[0][...] = _rms(y, pg_ref[...])


def _ffn(x, gain, wg, wu, wd, *, post=None, post_gain=None, tm=512, tf=512):
    m, d = x.shape
    f = _wshape(wg)[1]
    tm = _tile(m, tm, SUBLANES)
    nf = pl.cdiv(f, tf)
    row = pl.BlockSpec((tm, d), lambda i, j: (i, 0))
    vec = pl.BlockSpec((1, d), lambda i, j: (0, 0))
    wg_arr, wg_spec = _wspec(wg, (d, tf), lambda i, j: (0, j))
    wu_arr, wu_spec = _wspec(wu, (d, tf), lambda i, j: (0, j))
    wd_arr, wd_spec = _wspec(wd, (tf, d), lambda i, j: (j, 0))
    in_specs = [row, vec, wg_spec, wu_spec, wd_spec]
    args = [x, gain.reshape(1, d).astype(F32), wg_arr, wu_arr, wd_arr]
    out_specs, out_shape = [row], [jax.ShapeDtypeStruct((m, d), F32)]
    if post is not None:
        in_specs.append(vec)
        args.append(post_gain.reshape(1, d).astype(F32))
    if post == "norm_bf16":
        out_specs.append(row)
        out_shape.append(jax.ShapeDtypeStruct((m, d), BF16))
    return pl.pallas_call(
        functools.partial(_ffn_kernel, nf=nf, tf=tf, f=f, post=post), grid=(m // tm, nf),
        in_specs=in_specs, out_specs=out_specs, out_shape=out_shape,
        scratch_shapes=[pltpu.VMEM((tm, d), BF16), pltpu.VMEM((tm, d), F32)],
        compiler_params=_params("parallel", "arbitrary"), name="ffn",
    )(*args)


def _prep_kernel(uc_ref, g_ref, us_ref, cos_ref, s1_ref, s2_ref, fb_ref, ckv_ref, sm_ref, *, ff_lo, ff_hi):
    ckv_ref[...] = _rms(uc_ref[...], g_ref[...])
    us = us_ref[...]
    lane = lax.broadcasted_iota(jnp.int32, us.shape, 1)
    pe = jnp.where(lane < MLA_ROPE, us, 0.0)
    rot = _rope_lanes(pe, cos_ref[...], s1_ref[...], s2_ref[...])
    logf = -_softplus(-(us + fb_ref[...]))
    sm_ref[...] = jnp.where((lane >= ff_lo) & (lane < ff_hi), logf, rot)


def _prep(u_ckv, kv_gain, u_small, tabs, fb_lanes, *, ff_lo, ff_hi, tm=1024):
    m = u_small.shape[0]
    kv = kv_gain.shape[0]
    tm = _tile(m, tm, SUBLANES)
    row = lambda i: (i, 0)
    return pl.pallas_call(
        functools.partial(_prep_kernel, ff_lo=ff_lo, ff_hi=ff_hi), grid=(m // tm,),
        in_specs=[pl.BlockSpec((tm, kv), row),
                  pl.BlockSpec((1, kv), lambda i: (0, 0)),
                  pl.BlockSpec((tm, LANES), row), pl.BlockSpec((tm, LANES), row),
                  pl.BlockSpec((tm, LANES), row), pl.BlockSpec((tm, LANES), row),
                  pl.BlockSpec((1, LANES), lambda i: (0, 0))],
        out_specs=[pl.BlockSpec((tm, kv), row), pl.BlockSpec((tm, LANES), row)],
        out_shape=[jax.ShapeDtypeStruct((m, kv), F32), jax.ShapeDtypeStruct((m, LANES), F32)],
        compiler_params=_params("parallel"), name="prep",
    )(u_ckv, kv_gain.reshape(1, kv).astype(F32), u_small, *tabs, fb_lanes)


def _cumsum_kernel(x_ref, o_ref, carry_ref, *, tc, mult):
    @pl.when(pl.program_id(1) == 0)
    def _():
        carry_ref[...] = jnp.zeros_like(carry_ref)

    r = lax.broadcasted_iota(jnp.int32, (tc, tc), 0)
    c = lax.broadcasted_iota(jnp.int32, (tc, tc), 1)
    upper = (r <= c).astype(F32)
    y = jnp.dot(x_ref[0], upper, preferred_element_type=F32, precision=HI) + carry_ref[:, :1]
    o_ref[0] = y * mult
    carry_ref[...] = jnp.broadcast_to(y[:, tc - 1:tc], carry_ref.shape)


def _cumsum_last(x, mult, *, tc=512):
    b, h, s = x.shape
    tc = _tile(s, tc, LANES)
    return pl.pallas_call(
        functools.partial(_cumsum_kernel, tc=tc, mult=mult), grid=(b, s // tc),
        in_specs=[pl.BlockSpec((1, h, tc), lambda i, j: (i, 0, j))],
        out_specs=pl.BlockSpec((1, h, tc), lambda i, j: (i, 0, j)),
        out_shape=jax.ShapeDtypeStruct((b, h, s), F32),
        scratch_shapes=[pltpu.VMEM((h, LANES), F32)],
        compiler_params=_params("parallel", "arbitrary"), name="cumsum",
    )(x)


def _last_visible(q_end, mode):
    if mode == "chunk":
        return (q_end // CHUNK) * CHUNK + (CHUNK - 1)
    return q_end


def _attn_kernel(*refs, heads, dq, dk, dv, tq, tk, nk, scale, mode, q_off, n_valid, has_bias, has_shared):
    it = iter(refs)
    qi_ref, ki_ref = next(it), next(it)
    q_ref, k_ref, v_ref = next(it), next(it), next(it)
    ks_ref = next(it) if has_shared else None
    b_ref = next(it) if has_bias else None
    o_ref, m_ref, acc_ref = next(it), next(it), next(it)
    t = pl.program_id(1)
    qi, ki = qi_ref[t], ki_ref[t]
    nch = tk // LANES
    c = scale * LOG2E
    aw = dv + LANES

    @pl.when(ki == 0)
    def _():
        m_ref[...] = jnp.full_like(m_ref, NEG_BIG)
        acc_ref[...] = jnp.zeros_like(acc_ref)

    q_lo = q_off + qi * tq
    k_lo = ki * tk
    last_tile = jnp.minimum(_last_visible(q_lo + (tq - 1), mode) // tk, nk - 1)
    first_maskable = _last_visible(q_lo, mode) + 1
    ones_col = (lax.broadcasted_iota(jnp.int32, (tk, LANES), 1) == 0).astype(BF16)

    def body(masked):
        if masked:
            qpos = q_lo + lax.broadcasted_iota(jnp.int32, (tq, tk), 0)
            kpos = k_lo + lax.broadcasted_iota(jnp.int32, (tq, tk), 1)
            if mode == "chunk":
                sh = CHUNK.bit_length() - 1
                vis = lax.shift_right_logical(kpos, sh) <= lax.shift_right_logical(qpos, sh)
            else:
                vis = kpos <= qpos
            vis = vis & (kpos < n_valid)
        for h in range(heads):
            q = q_ref[0, :, h * dq:(h + 1) * dq]
            k = k_ref[0, :, h * dk:(h + 1) * dk]
            if has_shared:
                k = jnp.concatenate([k, ks_ref[0]], axis=1)
            s = lax.dot_general(q, k, (((1,), (1,)), ((), ())), preferred_element_type=F32)
            if has_bias:
                s = s + b_ref[0, h:h + 1, :]
            if masked:
                s = jnp.where(vis, s, NEG_BIG)
            m_prev = m_ref[h]
            mc = s[:, :LANES]
            for j in range(1, nch):
                mc = jnp.maximum(mc, s[:, j * LANES:(j + 1) * LANES])
            m_new = jnp.maximum(m_prev, jnp.max(mc, axis=1, keepdims=True))
            m_ref[h] = m_new
            alpha = jnp.exp2((m_prev - m_new) * c)
            p = jnp.concatenate([jnp.exp2((s[:, j * LANES:(j + 1) * LANES] - m_new) * c).astype(BF16)
                                 for j in range(nch)], axis=1)
            vx = jnp.concatenate([v_ref[0, :, h * dv:(h + 1) * dv], ones_col], axis=1)
            pv = jnp.dot(p, vx, preferred_element_type=F32)
            for a0 in range(h * aw, (h + 1) * aw, LANES):
                acc_ref[:, a0:a0 + LANES] = alpha * acc_ref[:, a0:a0 + LANES] + pv[:, a0 - h * aw:a0 - h * aw + LANES]

    need_mask = (k_lo + (tk - 1) >= first_maskable) | (k_lo + tk > n_valid)

    @pl.when(need_mask)
    def _():
        body(True)

    @pl.when(jnp.logical_not(need_mask))
    def _():
        body(False)

    @pl.when(ki == last_tile)
    def _():
        for h in range(heads):
            l = acc_ref[:, h * aw + dv:h * aw + dv + 1]
            o_ref[0, :, h * dv:(h + 1) * dv] = (acc_ref[:, h * aw:h * aw + dv] / l).astype(o_ref.dtype)


def _attention(q, k, v, k_shared, bias, *, heads, dq, dk, dv, scale, mode, q_off, n_valid, k_col=0, v_col=0,
               tq=1024, tk=1024):
    b, sq, _ = q.shape
    sk = k.shape[1]
    assert dk + (0 if k_shared is None else k_shared.shape[2]) == dq
    tq = _tile(sq, tq, SUBLANES)
    tk = sk if sk <= 2 * tk else _tile(sk, tk, LANES)
    nq, nk = sq // tq, sk // tk
    pairs = [(i, j) for i in range(nq)
             for j in range(min(_last_visible(q_off + i * tq + (tq - 1), mode) // tk, nk - 1) + 1)]
    qi_arr = jnp.asarray([p[0] for p in pairs], jnp.int32)
    ki_arr = jnp.asarray([p[1] for p in pairs], jnp.int32)

    in_specs = [pl.BlockSpec((1, tq, heads * dq), lambda bi, t, qi, ki: (bi, qi[t], 0)),
                pl.BlockSpec((1, tk, heads * dk), lambda bi, t, qi, ki: (bi, ki[t], k_col)),
                pl.BlockSpec((1, tk, heads * dv), lambda bi, t, qi, ki: (bi, ki[t], v_col))]
    args = [q, k, v]
    if k_shared is not None:
        in_specs.append(pl.BlockSpec((1, tk, dq - dk), lambda bi, t, qi, ki: (bi, ki[t], 0)))
        args.append(k_shared)
    if bias is not None:
        in_specs.append(pl.BlockSpec((1, heads, tk), lambda bi, t, qi, ki: (bi, 0, ki[t])))
        args.append(bias)
    kern = functools.partial(_attn_kernel, heads=heads, dq=dq, dk=dk, dv=dv, tq=tq, tk=tk, nk=nk, scale=scale,
                             mode=mode, q_off=q_off, n_valid=n_valid, has_bias=bias is not None,
                             has_shared=k_shared is not None)
    return pl.pallas_call(
        kern,
        grid_spec=pltpu.PrefetchScalarGridSpec(
            num_scalar_prefetch=2, grid=(b, len(pairs)), in_specs=in_specs,
            out_specs=pl.BlockSpec((1, tq, heads * dv), lambda bi, t, qi, ki: (bi, qi[t], 0)),
            scratch_shapes=[pltpu.VMEM((heads, tq, LANES), F32), pltpu.VMEM((tq, heads * (dv + LANES)), F32)]),
        out_shape=jax.ShapeDtypeStruct((b, sq, heads * dv), BF16),
        compiler_params=_params("parallel", "arbitrary"), name="attn_" + mode,
    )(qi_arr, ki_arr, *args)


def _decode_kernel(*refs, heads, dq, dk, dv, sq, p_len, scale, mode, has_shared, has_bias):
    it = iter(refs)
    q_ref, kp_ref, vp_ref, kn_ref, vn_ref = (next(it) for _ in range(5))
    ksp_ref, ksn_ref = (next(it), next(it)) if has_shared else (None, None)
    bp_ref, bn_ref = (next(it), next(it)) if has_bias else (None, None)
    o_ref = next(it)
    c = scale * LOG2E

    def head(ref, h, d):
        x = ref[:, h, :] if len(ref.shape) == 3 else ref[:, h * d:(h + 1) * d]
        return x.astype(BF16)

    row = lax.broadcasted_iota(jnp.int32, (sq, sq), 0)
    col = lax.broadcasted_iota(jnp.int32, (sq, sq), 1)
    if mode == "chunk":
        sh = CHUNK.bit_length() - 1
        vis = lax.shift_right_logical(p_len + col, sh) <= lax.shift_right_logical(p_len + row, sh)
    else:
        vis = col <= row
    if has_shared:
        pad = dq - dk - ksp_ref.shape[1]
        ksp = jnp.concatenate([ksp_ref[...].astype(BF16), jnp.zeros((p_len, pad), BF16)], axis=1)
        ksn = jnp.concatenate([ksn_ref[...].astype(BF16), jnp.zeros((sq, pad), BF16)], axis=1)
    contract_last = (((1,), (1,)), ((), ()))
    for h in range(heads):
        q = q_ref[:, h * dq:(h + 1) * dq]
        kp, kn = head(kp_ref, h, dk), head(kn_ref, h, dk)
        if has_shared:
            kp = jnp.concatenate([kp, ksp], axis=1)
            kn = jnp.concatenate([kn, ksn], axis=1)
        s_p = lax.dot_general(q, kp, contract_last, preferred_element_type=F32)
        s_n = lax.dot_general(q, kn, contract_last, preferred_element_type=F32)
        if has_bias:
            s_p = s_p + bp_ref[h:h + 1, :]
            s_n = s_n + bn_ref[h:h + 1, :]
        s_n = jnp.where(vis, s_n, NEG_BIG)
        m = jnp.maximum(jnp.max(s_p, axis=1, keepdims=True), jnp.max(s_n, axis=1, keepdims=True))
        p_p = jnp.exp2((s_p - m) * c)
        p_n = jnp.exp2((s_n - m) * c)
        l = jnp.sum(p_p, axis=1, keepdims=True) + jnp.sum(p_n, axis=1, keepdims=True)
        pv = (jnp.dot(p_p.astype(BF16), head(vp_ref, h, dv), preferred_element_type=F32)
              + jnp.dot(p_n.astype(BF16), head(vn_ref, h, dv), preferred_element_type=F32))
        o_ref[:, h * dv:(h + 1) * dv] = (pv / l).astype(o_ref.dtype)


def _decode_attention(q, past_kv, new_kv, shared, bias, *, heads, dq, dk, dv, scale, mode):
    b, sq, _ = q.shape

    def spec(arr, prefix, colblk, width):
        inner = arr.shape[len(prefix) + 1:]
        if len(inner) == 3:
            block, idx = inner, (0, 0, 0)
        else:
            block, idx = (inner[0], width), (0, colblk)
        return pl.BlockSpec((None,) * (len(prefix) + 1) + tuple(block), lambda i: tuple(prefix) + (i,) + idx)

    (kp, kp_pre, kp_col), (vp, vp_pre, vp_col) = past_kv
    (kn, kn_pre, kn_col), (vn, vn_pre, vn_col) = new_kv
    p_len = kp.shape[len(kp_pre) + 1]
    in_specs = [pl.BlockSpec((None, sq, heads * dq), lambda i: (i, 0, 0)),
                spec(kp, kp_pre, kp_col, heads * dk), spec(vp, vp_pre, vp_col, heads * dv),
                spec(kn, kn_pre, kn_col, heads * dk), spec(vn, vn_pre, vn_col, heads * dv)]
    args = [q, kp, vp, kn, vn]
    if shared is not None:
        for arr, pre in shared:
            in_specs.append(spec(arr, pre, 0, arr.shape[-1]))
            args.append(arr)
    if bias is not None:
        for arr in bias:
            in_specs.append(pl.BlockSpec((None,) + arr.shape[1:], lambda i: (i, 0, 0)))
            args.append(arr)
    kern = functools.partial(_decode_kernel, heads=heads, dq=dq, dk=dk, dv=dv, sq=sq, p_len=p_len, scale=scale,
                             mode=mode, has_shared=shared is not None, has_bias=bias is not None)
    return pl.pallas_call(
        kern, grid=(b,), in_specs=in_specs,
        out_specs=pl.BlockSpec((None, sq, heads * dv), lambda i: (i, 0, 0)),
        out_shape=jax.ShapeDtypeStruct((b, sq, heads * dv), BF16),
        compiler_params=_params("parallel"), name="decode_" + mode,
    )(*args)


def _ssd_kernel(z_ref, xbc_ref, dt_ref, dtt_ref, cw_ref, cb_ref, dtb_ref, dtbt_ref, al_ref, alt_ref, dx_ref, nw_ref,
                h0_ref, c0_ref, y_ref, hout_ref, state_ref, carry_ref, *, lc, nc):
    c = pl.program_id(1)
    gw = SSM_D_INNER // SSM_GROUPS
    hpg = SSM_HEADS // SSM_GROUPS
    halo = SUBLANES

    @pl.when(c == 0)
    def _():
        state_ref[...] = h0_ref[0]
        carry_ref[...] = c0_ref[0]

    x = xbc_ref[0]
    cat = jnp.concatenate([carry_ref[...], x], axis=0)
    conv = cb_ref[...]
    for kk in range(SSM_CONV_W):
        shift = SSM_CONV_W - 1 - kk
        src = pltpu.roll(cat, shift, 0) if shift else cat
        conv = conv + src[halo:, :] * cw_ref[kk:kk + 1, :]
    carry_ref[...] = x[lc - halo:, :]
    act = conv * jax.nn.sigmoid(conv)
    xs = act[:, :SSM_D_INNER]
    bm = act[:, SSM_D_INNER:SSM_D_INNER + SSM_GROUPS * SSM_STATE]
    cm = act[:, SSM_D_INNER + SSM_GROUPS * SSM_STATE:]

    dt = _softplus(dt_ref[0] + dtb_ref[...])
    dtt = _softplus(dtt_ref[0] + dtbt_ref[...])
    adt = dt * (-jnp.exp(al_ref[...]))
    adtt = dtt * (-jnp.exp(alt_ref[...]))
    r = lax.broadcasted_iota(jnp.int32, (lc, lc), 0)
    cc = lax.broadcasted_iota(jnp.int32, (lc, lc), 1)
    tril = cc <= r
    acs = jnp.dot(tril.astype(F32), adt, preferred_element_type=F32, precision=HI)
    acst = jnp.dot(adtt, (r <= cc).astype(F32), preferred_element_type=F32, precision=HI)
    hh = lax.broadcasted_iota(jnp.int32, (SSM_HEADS, SSM_D_INNER), 0)
    ll = lax.broadcasted_iota(jnp.int32, (SSM_HEADS, SSM_D_INNER), 1)
    expand = ((ll >= hh * SSM_HEAD_DIM) & (ll < (hh + 1) * SSM_HEAD_DIM)).astype(F32)
    dt_x = jnp.dot(dt, expand, preferred_element_type=F32, precision=HI)
    acs_x = jnp.dot(acs, expand, preferred_element_type=F32, precision=HI)
    tot_x = acs_x[lc - 1:lc, :]
    xdt = xs * dt_x
    xdt_b = xdt.astype(BF16)
    w_end = (xdt * jnp.exp(tot_x - acs_x)).astype(BF16)
    state = state_ref[...]
    state_b = state.astype(BF16)

    y_parts, new_parts = [], []
    for g in range(SSM_GROUPS):
        bg = bm[:, g * SSM_STATE:(g + 1) * SSM_STATE].astype(BF16)
        cg = cm[:, g * SSM_STATE:(g + 1) * SSM_STATE].astype(BF16)
        cb = lax.dot_general(cg, bg, (((1,), (1,)), ((), ())), preferred_element_type=F32)
        for hl in range(hpg):
            h = g * hpg + hl
            seg = acs[:, h:h + 1] - acst[h:h + 1, :]
            mh = (cb * jnp.exp(jnp.where(tril, seg, NEG_BIG))).astype(BF16)
            y_parts.append(jnp.dot(mh, xdt_b[:, h * SSM_HEAD_DIM:(h + 1) * SSM_HEAD_DIM], preferred_element_type=F32))
        new_parts.append(lax.dot_general(bg, w_end[:, g * gw:(g + 1) * gw], (((0,), (0,)), ((), ())),
                                         preferred_element_type=F32))
    y_off = jnp.concatenate(
        [jnp.dot(cm[:, g * SSM_STATE:(g + 1) * SSM_STATE].astype(BF16), state_b[:, g * gw:(g + 1) * gw],
                 preferred_element_type=F32) for g in range(SSM_GROUPS)], axis=1) * jnp.exp(acs_x)
    y = jnp.concatenate(y_parts, axis=1) + y_off + dx_ref[...] * xs
    state_ref[...] = jnp.exp(tot_x) * state + jnp.concatenate(new_parts, axis=1)

    zz = z_ref[0]
    y = y * (zz * jax.nn.sigmoid(zz))
    for g in range(SSM_GROUPS):
        y_ref[0, :, g * gw:(g + 1) * gw] = _rms(y[:, g * gw:(g + 1) * gw], nw_ref[:, g * gw:(g + 1) * gw]).astype(y_ref.dtype)

    @pl.when(c == nc - 1)
    def _():
        hout_ref[0] = state_ref[...]


def _ssd(z, xbc, dt, dtt, p, h0, c0, *, lc=256):
    b, s, cd = xbc.shape
    lc = _tile(s, lc, LANES) if s % LANES == 0 else s
    nc = s // lc
    hh = SSM_HEADS
    full2 = lambda shape: pl.BlockSpec(shape, lambda i, j: (0, 0))
    return pl.pallas_call(
        functools.partial(_ssd_kernel, lc=lc, nc=nc), grid=(b, nc),
        in_specs=[pl.BlockSpec((1, lc, SSM_D_INNER), lambda i, j: (i, j, 0)),
                  pl.BlockSpec((1, lc, cd), lambda i, j: (i, j, 0)),
                  pl.BlockSpec((1, lc, hh), lambda i, j: (i, j, 0)),
                  pl.BlockSpec((1, hh, lc), lambda i, j: (i, 0, j)),
                  full2((SSM_CONV_W, cd)), full2((1, cd)),
                  full2((1, hh)), full2((hh, 1)), full2((1, hh)), full2((hh, 1)),
                  full2((1, SSM_D_INNER)), full2((1, SSM_D_INNER)),
                  pl.BlockSpec((1, SSM_STATE, SSM_D_INNER), lambda i, j: (i, 0, 0)),
                  pl.BlockSpec((1, SUBLANES, cd), lambda i, j: (i, 0, 0))],
        out_specs=[pl.BlockSpec((1, lc, SSM_D_INNER), lambda i, j: (i, j, 0)),
                   pl.BlockSpec((1, SSM_STATE, SSM_D_INNER), lambda i, j: (i, 0, 0))],
        out_shape=[jax.ShapeDtypeStruct((b, s, SSM_D_INNER), BF16),
                   jax.ShapeDtypeStruct((b, SSM_STATE, SSM_D_INNER), F32)],
        scratch_shapes=[pltpu.VMEM((SSM_STATE, SSM_D_INNER), F32), pltpu.VMEM((SUBLANES, cd), F32)],
        compiler_params=_params("parallel", "arbitrary"), name="ssd",
    )(z, xbc, dt, dtt, p["conv_w"], p["conv_b"], p["dt_b"], p["dt_bt"], p["a_log"], p["a_logt"], p["d_x"], p["norm_w"],
      h0, c0)


def _merge_kernel(xn_ref, o0, o1, o2, w0, w1, w2, wg0, wg1, wg2, out_ref):
    xn = xn_ref[...]
    acc = None
    for o_ref, w_ref, wg_ref in ((o0, w0, wg0), (o1, w1, wg1), (o2, w2, wg2)):
        gate = jax.nn.sigmoid(jnp.dot(xn, wg_ref[...], preferred_element_type=F32))
        t = gate * jnp.dot(o_ref[...], w_ref[...], preferred_element_type=F32)
        acc = t if acc is None else acc + t
    out_ref[...] = acc.astype(out_ref.dtype)


def _merge(xn, o_list, w_list, w_gate, *, tm=1024, tn=512):
    m, d = xn.shape
    tm = _tile(m, tm, SUBLANES)
    tn = _tile(d, tn, LANES)
    nb = d // tn
    in_specs = [pl.BlockSpec((tm, d), lambda i, j: (i, 0))]
    in_specs += [pl.BlockSpec((tm, o.shape[1]), lambda i, j: (i, 0)) for o in o_list]
    w_args = []
    for w in w_list:
        arr, sp = _wspec(w, (_wshape(w)[0], tn), lambda i, j: (0, j))
        in_specs.append(sp)
        w_args.append(arr)
    for br in range(N_BRANCH):
        arr, sp = _wspec(w_gate, (d, tn), functools.partial(lambda i, j, br: (0, br * nb + j), br=br))
        in_specs.append(sp)
        w_args.append(arr)
    return pl.pallas_call(
        _merge_kernel, grid=(m // tm, nb), in_specs=in_specs,
        out_specs=pl.BlockSpec((tm, tn), lambda i, j: (i, j)),
        out_shape=jax.ShapeDtypeStruct((m, d), BF16),
        compiler_params=_params("parallel", "arbitrary"), name="merge",
    )(xn, *o_list, *w_args)


def _stacked_weights(a):
    depth, d_model = a["w_in"].shape[:2]
    q_lora, kv_lora = a["mla_q_norm"].shape[1], a["mla_kv_norm"].shape[1]
    conv_dim = a["ssm_conv_w"].shape[2]
    sizes = (q_lora, kv_lora, MLA_ROPE, SSM_D_INNER, conv_dim, SSM_HEADS,
             FOX_HEADS * FOX_HEAD_DIM, FOX_HEADS * FOX_HEAD_DIM, FOX_HEADS * FOX_HEAD_DIM, FOX_HEADS,
             N_BRANCH * d_model)
    w_in = a["w_in"]
    assert w_in.shape[2] == sum(sizes)
    cols, start = [], 0
    for n in sizes:
        cols.append(w_in[:, :, start:start + n])
        start += n
    w_q, w_ckv, w_kpe, w_z, w_xbc, w_dt, w_fq, w_fk, w_fv, w_ff, w_gate = cols
    bf = lambda t: t.astype(BF16)
    small = [w_kpe, w_dt, w_ff]
    n_small = sum(t.shape[2] for t in small)
    assert n_small <= LANES
    parts = [w_q, w_ckv, *small, jnp.zeros((depth, d_model, INPROJ_TN - n_small), w_in.dtype),
             w_z, w_xbc, w_fq, w_fk, w_fv]
    st = {"in_cat": bf(jnp.concatenate(parts, axis=2)), "in_gate": bf(w_gate)}
    wq = a["mla_w_uq"].reshape(depth, q_lora, MLA_HEADS, MLA_NOPE + MLA_ROPE)
    wq = jnp.pad(wq, ((0, 0), (0, 0), (0, 0), (0, MLA_QK_PAD - MLA_NOPE - MLA_ROPE)))
    st["uq"] = bf(wq.reshape(depth, q_lora, MLA_HEADS * MLA_QK_PAD))
    wkv = a["mla_w_ukv"].reshape(depth, kv_lora, MLA_HEADS, MLA_NOPE + MLA_V)
    st["ukv"] = bf(jnp.concatenate([wkv[..., :MLA_NOPE].reshape(depth, kv_lora, MLA_HEADS * MLA_NOPE),
                                    wkv[..., MLA_NOPE:].reshape(depth, kv_lora, MLA_HEADS * MLA_V)], axis=2))
    for nm in ("w_br_mla", "w_br_ssd", "w_br_fox", "w_out"):
        st[nm] = bf(a[nm])
    for pre in ("ffn1", "ffn2"):
        for src, dst in (("_w_gate", "_g"), ("_w_up", "_u"), ("_w_down", "_d")):
            st[pre + dst] = bf(a[pre + src])
    return st


def _layer_weights(l, a, stacked):
    q_lora, kv_lora = a["mla_q_norm"].shape[1], a["mla_kv_norm"].shape[1]
    conv_dim = a["ssm_conv_w"].shape[2]
    w = {name: (arr, l) for name, arr in stacked.items()}
    w["in_widths"] = (q_lora, kv_lora, LANES, SSM_D_INNER, conv_dim) + (FOX_HEADS * FOX_HEAD_DIM,) * 3
    for pre in ("ffn1", "ffn2"):
        w[pre + "_norm"] = a[pre + "_norm"][l]
    w["mix_norm"] = a["mix_norm"][l]
    w["q_norm"], w["kv_norm"] = a["mla_q_norm"][l], a["mla_kv_norm"][l]
    ff_lo = MLA_ROPE + SSM_HEADS
    w["fb_lanes"] = jnp.pad(a["fox_b_f"][l].astype(F32), (ff_lo, LANES - ff_lo - FOX_HEADS)).reshape(1, LANES)
    w["ssd"] = {
        "conv_w": a["ssm_conv_w"][l].astype(F32), "conv_b": a["ssm_conv_b"][l].astype(F32).reshape(1, conv_dim),
        "dt_b": a["ssm_dt_bias"][l].astype(F32).reshape(1, SSM_HEADS),
        "dt_bt": a["ssm_dt_bias"][l].astype(F32).reshape(SSM_HEADS, 1),
        "a_log": a["ssm_a_log"][l].astype(F32).reshape(1, SSM_HEADS),
        "a_logt": a["ssm_a_log"][l].astype(F32).reshape(SSM_HEADS, 1),
        "d_x": jnp.repeat(a["ssm_d"][l].astype(F32), SSM_HEAD_DIM).reshape(1, SSM_D_INNER),
        "norm_w": a["ssm_norm"][l].astype(F32).reshape(1, SSM_D_INNER),
    }
    return w


def _rope_tables(pos):
    half = MLA_ROPE // 2
    inv_freq = ROPE_BASE ** (-jnp.arange(half, dtype=F32) / half)
    ang = pos.astype(F32)[:, None] * inv_freq[None, :]
    cos, sin = jnp.cos(ang), jnp.sin(ang)
    z = jnp.zeros_like(cos)
    pad = jnp.zeros((pos.shape[0], LANES - MLA_ROPE), F32)
    return (jnp.concatenate([cos, cos, pad], axis=1),
            jnp.concatenate([-sin, z, pad], axis=1),
            jnp.concatenate([z, sin, pad], axis=1))


def _pad_keys(t, sk_pad):
    return jnp.pad(t, ((0, 0), (0, sk_pad - t.shape[1])) + ((0, 0),) * (t.ndim - 2))


def _layer(x, bsz, s, w, tabs, past, final_gain):
    m = bsz * s
    kv_lora = w["kv_norm"].shape[0]
    conv_dim = w["ssd"]["conv_w"].shape[1]
    x, xn = _ffn(x, w["ffn1_norm"], w["ffn1_g"], w["ffn1_u"], w["ffn1_d"], post="norm_bf16", post_gain=w["mix_norm"])

    u_q, u_ckv, u_small, u_z, u_xbc, fq, fk, fk_b, fv, fv_b = _inproj(
        xn, w["in_cat"], w["in_widths"], ((F32,),) * 5 + ((BF16,), (F32, BF16), (F32, BF16)), tn=INPROJ_TN)

    ff_lo = MLA_ROPE + SSM_HEADS
    ckv_new, small2 = _prep(u_ckv, w["kv_norm"], u_small, tabs, w["fb_lanes"], ff_lo=ff_lo, ff_hi=ff_lo + FOX_HEADS)
    kpe_new = small2[:, :MLA_ROPE]
    logf_new = small2[:, ff_lo:ff_lo + FOX_HEADS]
    u_dt = u_small[:, MLA_ROPE:ff_lo]

    if past is not None:
        caches, l = past
        past_len = caches["mla_ckv"].shape[2]
    ukv = functools.partial(_mm, w=w["ukv"], out_dtype=BF16, prologue="cast", tn=1024, name="mla_ukv")

    q_full = _mm(u_q, w["uq"], out_dtype=BF16, prologue="rms", gain=w["q_norm"], rope_tabs=tabs, name="mla_q")
    q_full = q_full.reshape(bsz, s, -1)
    kv_new = ukv(ckv_new).reshape(bsz, s, -1)
    mla = dict(heads=MLA_HEADS, dq=MLA_QK_PAD, dk=MLA_NOPE, dv=MLA_V, scale=MLA_SCALE, mode="chunk")
    sk_pad = -(-s // LANES) * LANES
    if past is None:
        kpe_pad = jnp.pad(kpe_new.astype(BF16).reshape(bsz, s, -1),
                          ((0, 0), (0, sk_pad - s), (0, MLA_QK_PAD - MLA_NOPE - MLA_ROPE)))
        kv_all = _pad_keys(kv_new, sk_pad)
        o_mla = _attention(q_full, kv_all, kv_all, kpe_pad, None, k_col=0, v_col=1, q_off=0, n_valid=s, **mla)
    else:
        kv_past = ukv(caches["mla_ckv"][l].reshape(bsz * past_len, kv_lora)).reshape(bsz, past_len, -1)
        o_mla = _decode_attention(
            q_full, ((kv_past, (), 0), (kv_past, (), 1)), ((kv_new, (), 0), (kv_new, (), 1)),
            ((caches["mla_kpe"], (l,)), (kpe_new.reshape(bsz, s, MLA_ROPE), ())), None, **mla)

    if past is None:
        conv_state = jnp.zeros((bsz, SSM_CONV_W - 1, conv_dim), F32)
        h0 = jnp.zeros((bsz, SSM_STATE, SSM_D_INNER), F32)
    else:
        conv_state = caches["conv"][l].astype(F32)
        h0 = jnp.transpose(caches["ssm"][l].astype(F32), (0, 3, 1, 2)).reshape(bsz, SSM_STATE, SSM_D_INNER)
    c0 = jnp.pad(conv_state, ((0, 0), (SUBLANES - (SSM_CONV_W - 1), 0), (0, 0)))
    xbc3 = u_xbc.reshape(bsz, s, conv_dim)
    dt3 = u_dt.reshape(bsz, s, SSM_HEADS)
    o_ssd, h_new = _ssd(u_z.reshape(bsz, s, SSM_D_INNER), xbc3, dt3, jnp.swapaxes(dt3, 1, 2), w["ssd"], h0, c0)
    ssm_new = jnp.transpose(h_new.reshape(bsz, SSM_STATE, SSM_HEADS, SSM_HEAD_DIM), (0, 2, 3, 1))
    keep = SSM_CONV_W - 1
    conv_new = xbc3[:, s - keep:] if s >= keep else jnp.concatenate([conv_state, xbc3], axis=1)[:, -keep:]

    hw = FOX_HEADS * FOX_HEAD_DIM
    fox = dict(heads=FOX_HEADS, dq=FOX_HEAD_DIM, dk=FOX_HEAD_DIM, dv=FOX_HEAD_DIM, scale=FOX_SCALE, mode="causal")
    logf_all = logf_new.reshape(bsz, s, FOX_HEADS)
    if past is not None:
        logf_all = jnp.concatenate([caches["fox_logf"][l].astype(F32), logf_all], axis=1)
    n_keys = logf_all.shape[1]
    neg_cum = _cumsum_last(jnp.swapaxes(_pad_keys(logf_all, -(-n_keys // LANES) * LANES), 1, 2), -1.0 / FOX_SCALE)
    fq3 = fq.reshape(bsz, s, hw)
    if past is None:
        o_fox = _attention(fq3, _pad_keys(fk_b.reshape(bsz, s, hw), sk_pad), _pad_keys(fv_b.reshape(bsz, s, hw), sk_pad),
                           None, neg_cum, q_off=0, n_valid=s, **fox)
    else:
        o_fox = _decode_attention(
            fq3, ((caches["fox_k"], (l,), 0), (caches["fox_v"], (l,), 0)),
            ((fk.reshape(bsz, s, hw), (), 0), (fv.reshape(bsz, s, hw), (), 0)), None,
            (neg_cum[:, :, :past_len], neg_cum[:, :, past_len:n_keys]), **fox)

    merged = _merge(xn, [o_mla.reshape(m, -1), o_ssd.reshape(m, -1), o_fox.reshape(m, -1)],
                    [w["w_br_mla"], w["w_br_ssd"], w["w_br_fox"]], w["in_gate"])
    x = _mm(merged, w["w_out"], out_dtype=F32, residual=x, name="out_proj")
    (x,) = _ffn(x, w["ffn2_norm"], w["ffn2_g"], w["ffn2_u"], w["ffn2_d"],
                post=None if final_gain is None else "norm_only", post_gain=final_gain)
    state = (ckv_new.reshape(bsz, s, kv_lora), kpe_new.reshape(bsz, s, MLA_ROPE),
             fk.reshape(bsz, s, FOX_HEADS, FOX_HEAD_DIM), fv.reshape(bsz, s, FOX_HEADS, FOX_HEAD_DIM),
             logf_new.reshape(bsz, s, FOX_HEADS), ssm_new, conv_new)
    return x, state


def kernel(x_prompt, x_sample, cache_mla_ckv, cache_mla_kpe, cache_fox_k, cache_fox_v, cache_fox_logf, state_ssm,
           state_conv, ffn1_norm, ffn1_w_gate, ffn1_w_up, ffn1_w_down, mix_norm, w_in, mla_q_norm, mla_w_uq,
           mla_kv_norm, mla_w_ukv, ssm_conv_w, ssm_conv_b, ssm_dt_bias, ssm_a_log, ssm_d, ssm_norm, fox_b_f,
           w_br_mla, w_br_ssd, w_br_fox, w_out, ffn2_norm, ffn2_w_gate, ffn2_w_up, ffn2_w_down, final_norm):
    a = dict(ffn1_norm=ffn1_norm, ffn1_w_gate=ffn1_w_gate, ffn1_w_up=ffn1_w_up, ffn1_w_down=ffn1_w_down,
             mix_norm=mix_norm, w_in=w_in, mla_q_norm=mla_q_norm, mla_w_uq=mla_w_uq, mla_kv_norm=mla_kv_norm,
             mla_w_ukv=mla_w_ukv, ssm_conv_w=ssm_conv_w, ssm_conv_b=ssm_conv_b, ssm_dt_bias=ssm_dt_bias,
             ssm_a_log=ssm_a_log, ssm_d=ssm_d, ssm_norm=ssm_norm, fox_b_f=fox_b_f, w_br_mla=w_br_mla,
             w_br_ssd=w_br_ssd, w_br_fox=w_br_fox, w_out=w_out, ffn2_norm=ffn2_norm, ffn2_w_gate=ffn2_w_gate,
             ffn2_w_up=ffn2_w_up, ffn2_w_down=ffn2_w_down)
    depth = w_in.shape[0]
    stacked = _stacked_weights(a)
    bp, sp, d_model = x_prompt.shape
    bs, ss, _ = x_sample.shape
    past_len = cache_mla_ckv.shape[2]
    tabs_p = tuple(jnp.tile(t, (bp, 1)) for t in _rope_tables(jnp.arange(sp, dtype=jnp.int32)))
    tabs_s = tuple(jnp.tile(t, (bs, 1)) for t in _rope_tables(past_len + jnp.arange(ss, dtype=jnp.int32)))
    hp = x_prompt.reshape(bp * sp, d_model).astype(F32)
    hs = x_sample.reshape(bs * ss, d_model).astype(F32)
    caches = {"mla_ckv": cache_mla_ckv, "mla_kpe": cache_mla_kpe, "fox_k": cache_fox_k, "fox_v": cache_fox_v,
              "fox_logf": cache_fox_logf, "ssm": state_ssm, "conv": state_conv}
    new_p, new_s = [], []
    for l in range(depth):
        w = _layer_weights(l, a, stacked)
        fg = final_norm if l == depth - 1 else None
        hp, st_p = _layer(hp, bp, sp, w, tabs_p, None, fg)
        hs, st_s = _layer(hs, bs, ss, w, tabs_s, (caches, l), fg)
        new_p.append(st_p)
        new_s.append(st_s)
    y_prompt = hp.reshape(bp, sp, d_model)
    y_sample = hs.reshape(bs, ss, d_model)
    stk = lambda states, i: jnp.stack([st[i] for st in states], axis=0)
    return (y_prompt, y_sample) + tuple(stk(new_p, i) for i in range(7)) + tuple(stk(new_s, i) for i in range(7))
```

```python
import functools
import math

import jax
import jax.numpy as jnp
from jax import lax
from jax.experimental import pallas as pl
from jax.experimental.pallas import tpu as pltpu

F32 = jnp.float32
BF16 = jnp.bfloat16

EPS = 1e-6
CHUNK = 64
FFN_RES = 0.5
MLA_HEADS, MLA_NOPE, MLA_ROPE, MLA_V = 8, 128, 64, 128
MLA_SCALE = (MLA_NOPE + MLA_ROPE) ** -0.5
ROPE_BASE = 10000.0
SSM_HEADS, SSM_HEAD_DIM, SSM_GROUPS, SSM_STATE, SSM_CONV_W = 16, 64, 2, 128, 4
SSM_D_INNER = SSM_HEADS * SSM_HEAD_DIM
FOX_HEADS, FOX_HEAD_DIM = 8, 128
FOX_SCALE = FOX_HEAD_DIM ** -0.5
N_BRANCH = 3

LANES = 128
SUBLANES = 8
MXU_DIM = 256
VMEM_LIMIT = 56 * 1024 * 1024

MLA_QK_PAD = MXU_DIM
INPROJ_TN = 512
NEG_BIG = -1e30
LOG2E = math.log2(math.e)
HI = lax.Precision.HIGHEST


def _tile(n, pref, align):
    t = (min(pref, n) // align) * align
    while t >= align:
        if n % t == 0:
            return t
        t -= align
    return n


def _params(*sem):
    return pltpu.CompilerParams(dimension_semantics=sem, vmem_limit_bytes=VMEM_LIMIT)


def _wshape(w):
    return w[0].shape[1:] if isinstance(w, tuple) else w.shape


def _wspec(w, block, index):
    if isinstance(w, tuple):
        arr, layer = w
        return arr, pl.BlockSpec((None,) + tuple(block), lambda *g: (layer,) + tuple(index(*g)))
    return w, pl.BlockSpec(tuple(block), index)


def _rms(x, g):
    return x * lax.rsqrt(jnp.mean(x * x, axis=-1, keepdims=True) + EPS) * g


def _softplus(x):
    return jnp.maximum(x, 0.0) + jnp.log1p(jnp.exp(-jnp.abs(x)))


def _rope_lanes(pe, cos, s1, s2):
    half = MLA_ROPE // 2
    return pe * cos + pltpu.roll(pe, LANES - half, 1) * s1 + pltpu.roll(pe, half, 1) * s2


def _mm_kernel(*refs, prologue, rope, residual, tn):
    it = iter(refs)
    x_ref = next(it)
    g_ref = next(it) if prologue == "rms" else None
    w_ref = next(it)
    res_ref = next(it) if residual else None
    tabs = (next(it), next(it), next(it)) if rope else None
    o_ref = next(it)
    xn_ref = next(it) if prologue != "none" else None

    if prologue == "none":
        lhs = x_ref[...]
    else:
        @pl.when(pl.program_id(1) == 0)
        def _():
            x = x_ref[...].astype(F32)
            if prologue == "rms":
                x = _rms(x, g_ref[...])
            xn_ref[...] = x.astype(BF16)
        lhs = xn_ref[...]
    acc = jnp.dot(lhs, w_ref[...], preferred_element_type=F32)
    if residual:
        acc = res_ref[...] + acc
    if rope:
        cos, s1, s2 = (t[...] for t in tabs)
        for c in range(tn // MLA_QK_PAD):
            a = c * MLA_QK_PAD
            o_ref[:, a:a + LANES] = acc[:, a:a + LANES].astype(o_ref.dtype)
            o_ref[:, a + LANES:a + 2 * LANES] = _rope_lanes(acc[:, a + LANES:a + 2 * LANES], cos, s1, s2).astype(o_ref.dtype)
    else:
        o_ref[...] = acc.astype(o_ref.dtype)


def _mm(x, w, *, out_dtype, tm=1024, tn=512, prologue="none", gain=None, residual=None, rope_tabs=None, name="mm"):
    m = x.shape[0]
    k, n = _wshape(w)
    assert x.shape[1] == k and (prologue != "none" or x.dtype == BF16)
    tm = _tile(m, tm, SUBLANES)
    tn = _tile(n, tn, MLA_QK_PAD if rope_tabs is not None else LANES)
    grid = (m // tm, n // tn)
    in_specs = [pl.BlockSpec((tm, k), lambda i, j: (i, 0))]
    args = [x]
    if prologue == "rms":
        in_specs.append(pl.BlockSpec((1, k), lambda i, j: (0, 0)))
        args.append(gain.reshape(1, k).astype(F32))
    w_arr, w_spec = _wspec(w, (k, tn), lambda i, j: (0, j))
    in_specs.append(w_spec)
    args.append(w_arr)
    if residual is not None:
        in_specs.append(pl.BlockSpec((tm, tn), lambda i, j: (i, j)))
        args.append(residual)
    if rope_tabs is not None:
        for t in rope_tabs:
            in_specs.append(pl.BlockSpec((tm, LANES), lambda i, j: (i, 0)))
            args.append(t)
    scratch = [pltpu.VMEM((tm, k), BF16)] if prologue != "none" else []
    kern = functools.partial(_mm_kernel, prologue=prologue, rope=rope_tabs is not None,
                             residual=residual is not None, tn=tn)
    return pl.pallas_call(
        kern, grid=grid, in_specs=in_specs,
        out_specs=pl.BlockSpec((tm, tn), lambda i, j: (i, j)),
        out_shape=jax.ShapeDtypeStruct((m, n), out_dtype),
        scratch_shapes=scratch, compiler_params=_params("parallel", "arbitrary"), name=name,
    )(*args)


def _inproj_kernel(xn_ref, w_ref, *outs, groups):
    j = pl.program_id(1)
    outs = iter(outs)
    for lo, hi, width, copies in groups:
        o_refs = [next(outs) for _ in range(copies)]

        @pl.when((j >= lo) & (j < hi))
        def _(o_refs=o_refs, width=width):
            acc = jnp.dot(xn_ref[...], w_ref[:, :width], preferred_element_type=F32)
            for o_ref in o_refs:
                o_ref[...] = acc.astype(o_ref.dtype)


def _inproj_layout(widths, dtypes, tn):
    groups, start = [], 0
    for n, dts in zip(widths, dtypes):
        nt = -(-n // tn)
        assert n % tn == 0 or n < tn
        groups.append((start, start + nt, min(n, tn), len(dts)))
        start += nt
    return groups, start


def _inproj(xn, w_cat, widths, dtypes, *, tm=1024, tn=512):
    m, d = xn.shape
    tm = _tile(m, tm, SUBLANES)
    groups, n_tiles = _inproj_layout(widths, dtypes, tn)
    assert _wshape(w_cat) == (d, n_tiles * tn)
    w_arr, w_spec = _wspec(w_cat, (d, tn), lambda i, j: (0, j))
    out_specs, out_shape = [], []
    for (lo, hi, bw, _), n, dts in zip(groups, widths, dtypes):
        for dt in dts:
            out_specs.append(pl.BlockSpec(
                (tm, bw), functools.partial(lambda i, j, lo, hi: (i, jnp.clip(j - lo, 0, hi - lo - 1)), lo=lo, hi=hi)))
            out_shape.append(jax.ShapeDtypeStruct((m, n), dt))
    return pl.pallas_call(
        functools.partial(_inproj_kernel, groups=groups), grid=(m // tm, n_tiles),
        in_specs=[pl.BlockSpec((tm, d), lambda i, j: (i, 0)), w_spec],
        out_specs=out_specs, out_shape=out_shape,
        compiler_params=_params("parallel", "arbitrary"), name="inproj",
    )(xn, w_arr)


def _ffn_kernel(*refs, nf, tf, f, post):
    x_ref, g_ref, wg_ref, wu_ref, wd_ref = refs[:5]
    refs = refs[5:]
    pg_ref = None
    if post is not None:
        pg_ref, refs = refs[0], refs[1:]
    outs, (xn_ref, acc_ref) = refs[:-2], refs[-2:]
    j = pl.program_id(1)

    @pl.when(j == 0)
    def _():
        xn_ref[...] = _rms(x_ref[...], g_ref[...]).astype(BF16)
        acc_ref[...] = jnp.zeros_like(acc_ref)

    xn = xn_ref[...]
    a = jnp.dot(xn, wg_ref[...], preferred_element_type=F32)
    b = jnp.dot(xn, wu_ref[...], preferred_element_type=F32)
    h = a * jax.nn.sigmoid(a) * b
    wd = wd_ref[...]
    if f % tf:
        valid = f - j * tf
        h = jnp.where(lax.broadcasted_iota(jnp.int32, h.shape, 1) < valid, h, 0.0)
        wd = jnp.where(lax.broadcasted_iota(jnp.int32, wd.shape, 0) < valid, wd, jnp.zeros_like(wd))
    acc_ref[...] += jnp.dot(h.astype(BF16), wd, preferred_element_type=F32)

    @pl.when(j == nf - 1)
    def _():
        y = x_ref[...] + FFN_RES * acc_ref[...]
        if post is None:
            outs[0][...] = y
        elif post == "norm_bf16":
            outs[0][...] = y
            outs[1][...] = _rms(y, pg_ref[...]).astype(BF16)
        else:
            outs[0][...] = _rms(y, pg_ref[...])


def _ffn(x, gain, wg, wu, wd, *, post=None, post_gain=None, tm=512, tf=512):
    m, d = x.shape
    f = _wshape(wg)[1]
    tm = _tile(m, tm, SUBLANES)
    nf = pl.cdiv(f, tf)
    row = pl.BlockSpec((tm, d), lambda i, j: (i, 0))
    vec = pl.BlockSpec((1, d), lambda i, j: (0, 0))
    wg_arr, wg_spec = _wspec(wg, (d, tf), lambda i, j: (0, j))
    wu_arr, wu_spec = _wspec(wu, (d, tf), lambda i, j: (0, j))
    wd_arr, wd_spec = _wspec(wd, (tf, d), lambda i, j: (j, 0))
    in_specs = [row, vec, wg_spec, wu_spec, wd_spec]
    args = [x, gain.reshape(1, d).astype(F32), wg_arr, wu_arr, wd_arr]
    out_specs, out_shape = [row], [jax.ShapeDtypeStruct((m, d), F32)]
    if post is not None:
        in_specs.append(vec)
        args.append(post_gain.reshape(1, d).astype(F32))
    if post == "norm_bf16":
        out_specs.append(row)
        out_shape.append(jax.ShapeDtypeStruct((m, d), BF16))
    return pl.pallas_call(
        functools.partial(_ffn_kernel, nf=nf, tf=tf, f=f, post=post), grid=(m // tm, nf),
        in_specs=in_specs, out_specs=out_specs, out_shape=out_shape,
        scratch_shapes=[pltpu.VMEM((tm, d), BF16), pltpu.VMEM((tm, d), F32)],
        compiler_params=_params("parallel", "arbitrary"), name="ffn",
    )(*args)


def _prep_kernel(uc_ref, g_ref, us_ref, cos_ref, s1_ref, s2_ref, fb_ref, ckv_ref, sm_ref, *, ff_lo, ff_hi):
    ckv_ref[...] = _rms(uc_ref[...], g_ref[...])
    us = us_ref[...]
    lane = lax.broadcasted_iota(jnp.int32, us.shape, 1)
    pe = jnp.where(lane < MLA_ROPE, us, 0.0)
    rot = _rope_lanes(pe, cos_ref[...], s1_ref[...], s2_ref[...])
    logf = -_softplus(-(us + fb_ref[...]))
    sm_ref[...] = jnp.where((lane >= ff_lo) & (lane < ff_hi), logf, rot)


def _prep(u_ckv, kv_gain, u_small, tabs, fb_lanes, *, ff_lo, ff_hi, tm=1024):
    m = u_small.shape[0]
    kv = kv_gain.shape[0]
    tm = _tile(m, tm, SUBLANES)
    row = lambda i: (i, 0)
    return pl.pallas_call(
        functools.partial(_prep_kernel, ff_lo=ff_lo, ff_hi=ff_hi), grid=(m // tm,),
        in_specs=[pl.BlockSpec((tm, kv), row),
                  pl.BlockSpec((1, kv), lambda i: (0, 0)),
                  pl.BlockSpec((tm, LANES), row), pl.BlockSpec((tm, LANES), row),
                  pl.BlockSpec((tm, LANES), row), pl.BlockSpec((tm, LANES), row),
                  pl.BlockSpec((1, LANES), lambda i: (0, 0))],
        out_specs=[pl.BlockSpec((tm, kv), row), pl.BlockSpec((tm, LANES), row)],
        out_shape=[jax.ShapeDtypeStruct((m, kv), F32), jax.ShapeDtypeStruct((m, LANES), F32)],
        compiler_params=_params("parallel"), name="prep",
    )(u_ckv, kv_gain.reshape(1, kv).astype(F32), u_small, *tabs, fb_lanes)


def _cumsum_kernel(x_ref, o_ref, carry_ref, *, tc, mult):
    @pl.when(pl.program_id(1) == 0)
    def _():
        carry_ref[...] = jnp.zeros_like(carry_ref)

    r = lax.broadcasted_iota(jnp.int32, (tc, tc), 0)
    c = lax.broadcasted_iota(jnp.int32, (tc, tc), 1)
    upper = (r <= c).astype(F32)
    y = jnp.dot(x_ref[0], upper, preferred_element_type=F32, precision=HI) + carry_ref[:, :1]
    o_ref[0] = y * mult
    carry_ref[...] = jnp.broadcast_to(y[:, tc - 1:tc], carry_ref.shape)


def _cumsum_last(x, mult, *, tc=512):
    b, h, s = x.shape
    tc = _tile(s, tc, LANES)
    return pl.pallas_call(
        functools.partial(_cumsum_kernel, tc=tc, mult=mult), grid=(b, s // tc),
        in_specs=[pl.BlockSpec((1, h, tc), lambda i, j: (i, 0, j))],
        out_specs=pl.BlockSpec((1, h, tc), lambda i, j: (i, 0, j)),
        out_shape=jax.ShapeDtypeStruct((b, h, s), F32),
        scratch_shapes=[pltpu.VMEM((h, LANES), F32)],
        compiler_params=_params("parallel", "arbitrary"), name="cumsum",
    )(x)


def _last_visible(q_end, mode):
    if mode == "chunk":
        return (q_end // CHUNK) * CHUNK + (CHUNK - 1)
    return q_end


def _attn_kernel(*refs, heads, dq, dk, dv, tq, tk, nk, scale, mode, q_off, n_valid, has_bias, has_shared):
    it = iter(refs)
    qi_ref, ki_ref = next(it), next(it)
    q_ref, k_ref, v_ref = next(it), next(it), next(it)
    ks_ref = next(it) if has_shared else None
    b_ref = next(it) if has_bias else None
    o_ref, m_ref, acc_ref = next(it), next(it), next(it)
    t = pl.program_id(1)
    qi, ki = qi_ref[t], ki_ref[t]
    nch = tk // LANES
    c = scale * LOG2E
    aw = dv + LANES

    @pl.when(ki == 0)
    def _():
        m_ref[...] = jnp.full_like(m_ref, NEG_BIG)
        acc_ref[...] = jnp.zeros_like(acc_ref)

    q_lo = q_off + qi * tq
    k_lo = ki * tk
    last_tile = jnp.minimum(_last_visible(q_lo + (tq - 1), mode) // tk, nk - 1)
    first_maskable = _last_visible(q_lo, mode) + 1
    ones_col = (lax.broadcasted_iota(jnp.int32, (tk, LANES), 1) == 0).astype(BF16)

    def body(masked):
        if masked:
            qpos = q_lo + lax.broadcasted_iota(jnp.int32, (tq, tk), 0)
            kpos = k_lo + lax.broadcasted_iota(jnp.int32, (tq, tk), 1)
            if mode == "chunk":
                sh = CHUNK.bit_length() - 1
                vis = lax.shift_right_logical(kpos, sh) <= lax.shift_right_logical(qpos, sh)
            else:
                vis = kpos <= qpos
            vis = vis & (kpos < n_valid)
        for h in range(heads):
            q = q_ref[0, :, h * dq:(h + 1) * dq]
            k = k_ref[0, :, h * dk:(h + 1) * dk]
            if has_shared:
                k = jnp.concatenate([k, ks_ref[0]], axis=1)
            s = lax.dot_general(q, k, (((1,), (1,)), ((), ())), preferred_element_type=F32)
            if has_bias:
                s = s + b_ref[0, h:h + 1, :]
            if masked:
                s = jnp.where(vis, s, NEG_BIG)
            m_prev = m_ref[h]
            mc = s[:, :LANES]
            for j in range(1, nch):
                mc = jnp.maximum(mc, s[:, j * LANES:(j + 1) * LANES])
            m_new = jnp.maximum(m_prev, jnp.max(mc, axis=1, keepdims=True))
            m_ref[h] = m_new
            alpha = jnp.exp2((m_prev - m_new) * c)
            p = jnp.concatenate([jnp.exp2((s[:, j * LANES:(j + 1) * LANES] - m_new) * c).astype(BF16)
                                 for j in range(nch)], axis=1)
            vx = jnp.concatenate([v_ref[0, :, h * dv:(h + 1) * dv], ones_col], axis=1)
            pv = jnp.dot(p, vx, preferred_element_type=F32)
            for a0 in range(h * aw, (h + 1) * aw, LANES):
                acc_ref[:, a0:a0 + LANES] = alpha * acc_ref[:, a0:a0 + LANES] + pv[:, a0 - h * aw:a0 - h * aw + LANES]

    need_mask = (k_lo + (tk - 1) >= first_maskable) | (k_lo + tk > n_valid)

    @pl.when(need_mask)
    def _():
        body(True)

    @pl.when(jnp.logical_not(need_mask))
    def _():
        body(False)

    @pl.when(ki == last_tile)
    def _():
        for h in range(heads):
            l = acc_ref[:, h * aw + dv:h * aw + dv + 1]
            o_ref[0, :, h * dv:(h + 1) * dv] = (acc_ref[:, h * aw:h * aw + dv] / l).astype(o_ref.dtype)


def _attention(q, k, v, k_shared, bias, *, heads, dq, dk, dv, scale, mode, q_off, n_valid, k_col=0, v_col=0,
               tq=1024, tk=1024):
    b, sq, _ = q.shape
    sk = k.shape[1]
    assert dk + (0 if k_shared is None else k_shared.shape[2]) == dq
    tq = _tile(sq, tq, SUBLANES)
    tk = sk if sk <= 2 * tk else _tile(sk, tk, LANES)
    nq, nk = sq // tq, sk // tk
    pairs = [(i, j) for i in range(nq)
             for j in range(min(_last_visible(q_off + i * tq + (tq - 1), mode) // tk, nk - 1) + 1)]
    qi_arr = jnp.asarray([p[0] for p in pairs], jnp.int32)
    ki_arr = jnp.asarray([p[1] for p in pairs], jnp.int32)

    in_specs = [pl.BlockSpec((1, tq, heads * dq), lambda bi, t, qi, ki: (bi, qi[t], 0)),
                pl.BlockSpec((1, tk, heads * dk), lambda bi, t, qi, ki: (bi, ki[t], k_col)),
                pl.BlockSpec((1, tk, heads * dv), lambda bi, t, qi, ki: (bi, ki[t], v_col))]
    args = [q, k, v]
    if k_shared is not None:
        in_specs.append(pl.BlockSpec((1, tk, dq - dk), lambda bi, t, qi, ki: (bi, ki[t], 0)))
        args.append(k_shared)
    if bias is not None:
        in_specs.append(pl.BlockSpec((1, heads, tk), lambda bi, t, qi, ki: (bi, 0, ki[t])))
        args.append(bias)
    kern = functools.partial(_attn_kernel, heads=heads, dq=dq, dk=dk, dv=dv, tq=tq, tk=tk, nk=nk, scale=scale,
                             mode=mode, q_off=q_off, n_valid=n_valid, has_bias=bias is not None,
                             has_shared=k_shared is not None)
    return pl.pallas_call(
        kern,
        grid_spec=pltpu.PrefetchScalarGridSpec(
            num_scalar_prefetch=2, grid=(b, len(pairs)), in_specs=in_specs,
            out_specs=pl.BlockSpec((1, tq, heads * dv), lambda bi, t, qi, ki: (bi, qi[t], 0)),
            scratch_shapes=[pltpu.VMEM((heads, tq, LANES), F32), pltpu.VMEM((tq, heads * (dv + LANES)), F32)]),
        out_shape=jax.ShapeDtypeStruct((b, sq, heads * dv), BF16),
        compiler_params=_params("parallel", "arbitrary"), name="attn_" + mode,
    )(qi_arr, ki_arr, *args)


def _decode_kernel(*refs, heads, dq, dk, dv, sq, p_len, scale, mode, has_shared, has_bias):
    it = iter(refs)
    q_ref, kp_ref, vp_ref, kn_ref, vn_ref = (next(it) for _ in range(5))
    ksp_ref, ksn_ref = (next(it), next(it)) if has_shared else (None, None)
    bp_ref, bn_ref = (next(it), next(it)) if has_bias else (None, None)
    o_ref = next(it)
    c = scale * LOG2E

    def head(ref, h, d):
        x = ref[:, h, :] if len(ref.shape) == 3 else ref[:, h * d:(h + 1) * d]
        return x.astype(BF16)

    row = lax.broadcasted_iota(jnp.int32, (sq, sq), 0)
    col = lax.broadcasted_iota(jnp.int32, (sq, sq), 1)
    if mode == "chunk":
        sh = CHUNK.bit_length() - 1
        vis = lax.shift_right_logical(p_len + col, sh) <= lax.shift_right_logical(p_len + row, sh)
    else:
        vis = col <= row
    if has_shared:
        pad = dq - dk - ksp_ref.shape[1]
        ksp = jnp.concatenate([ksp_ref[...].astype(BF16), jnp.zeros((p_len, pad), BF16)], axis=1)
        ksn = jnp.concatenate([ksn_ref[...].astype(BF16), jnp.zeros((sq, pad), BF16)], axis=1)
    contract_last = (((1,), (1,)), ((), ()))
    for h in range(heads):
        q = q_ref[:, h * dq:(h + 1) * dq]
        kp, kn = head(kp_ref, h, dk), head(kn_ref, h, dk)
        if has_shared:
            kp = jnp.concatenate([kp, ksp], axis=1)
            kn = jnp.concatenate([kn, ksn], axis=1)
        s_p = lax.dot_general(q, kp, contract_last, preferred_element_type=F32)
        s_n = lax.dot_general(q, kn, contract_last, preferred_element_type=F32)
        if has_bias:
            s_p = s_p + bp_ref[h:h + 1, :]
            s_n = s_n + bn_ref[h:h + 1, :]
        s_n = jnp.where(vis, s_n, NEG_BIG)
        m = jnp.maximum(jnp.max(s_p, axis=1, keepdims=True), jnp.max(s_n, axis=1, keepdims=True))
        p_p = jnp.exp2((s_p - m) * c)
        p_n = jnp.exp2((s_n - m) * c)
        l = jnp.sum(p_p, axis=1, keepdims=True) + jnp.sum(p_n, axis=1, keepdims=True)
        pv = (jnp.dot(p_p.astype(BF16), head(vp_ref, h, dv), preferred_element_type=F32)
              + jnp.dot(p_n.astype(BF16), head(vn_ref, h, dv), preferred_element_type=F32))
        o_ref[:, h * dv:(h + 1) * dv] = (pv / l).astype(o_ref.dtype)


def _decode_attention(q, past_kv, new_kv, shared, bias, *, heads, dq, dk, dv, scale, mode):
    b, sq, _ = q.shape

    def spec(arr, prefix, colblk, width):
        inner = arr.shape[len(prefix) + 1:]
        if len(inner) == 3:
            block, idx = inner, (0, 0, 0)
        else:
            block, idx = (inner[0], width), (0, colblk)
        return pl.BlockSpec((None,) * (len(prefix) + 1) + tuple(block), lambda i: tuple(prefix) + (i,) + idx)

    (kp, kp_pre, kp_col), (vp, vp_pre, vp_col) = past_kv
    (kn, kn_pre, kn_col), (vn, vn_pre, vn_col) = new_kv
    p_len = kp.shape[len(kp_pre) + 1]
    in_specs = [pl.BlockSpec((None, sq, heads * dq), lambda i: (i, 0, 0)),
                spec(kp, kp_pre, kp_col, heads * dk), spec(vp, vp_pre, vp_col, heads * dv),
                spec(kn, kn_pre, kn_col, heads * dk), spec(vn, vn_pre, vn_col, heads * dv)]
    args = [q, kp, vp, kn, vn]
    if shared is not None:
        for arr, pre in shared:
            in_specs.append(spec(arr, pre, 0, arr.shape[-1]))
            args.append(arr)
    if bias is not None:
        for arr in bias:
            in_specs.append(pl.BlockSpec((None,) + arr.shape[1:], lambda i: (i, 0, 0)))
            args.append(arr)
    kern = functools.partial(_decode_kernel, heads=heads, dq=dq, dk=dk, dv=dv, sq=sq, p_len=p_len, scale=scale,
                             mode=mode, has_shared=shared is not None, has_bias=bias is not None)
    return pl.pallas_call(
        kern, grid=(b,), in_specs=in_specs,
        out_specs=pl.BlockSpec((None, sq, heads * dv), lambda i: (i, 0, 0)),
        out_shape=jax.ShapeDtypeStruct((b, sq, heads * dv), BF16),
        compiler_params=_params("parallel"), name="decode_" + mode,
    )(*args)


def _ssd_kernel(z_ref, xbc_ref, dt_ref, dtt_ref, cw_ref, cb_ref, dtb_ref, dtbt_ref, al_ref, alt_ref, dx_ref, nw_ref,
                h0_ref, c0_ref, y_ref, hout_ref, state_ref, carry_ref, *, lc, nc):
    c = pl.program_id(1)
    gw = SSM_D_INNER // SSM_GROUPS
    hpg = SSM_HEADS // SSM_GROUPS
    halo = SUBLANES

    @pl.when(c == 0)
    def _():
        state_ref[...] = h0_ref[0]
        carry_ref[...] = c0_ref[0]

    x = xbc_ref[0]
    cat = jnp.concatenate([carry_ref[...], x], axis=0)
    conv = cb_ref[...]
    for kk in range(SSM_CONV_W):
        shift = SSM_CONV_W - 1 - kk
        src = pltpu.roll(cat, shift, 0) if shift else cat
        conv = conv + src[halo:, :] * cw_ref[kk:kk + 1, :]
    carry_ref[...] = x[lc - halo:, :]
    act = conv * jax.nn.sigmoid(conv)
    xs = act[:, :SSM_D_INNER]
    bm = act[:, SSM_D_INNER:SSM_D_INNER + SSM_GROUPS * SSM_STATE]
    cm = act[:, SSM_D_INNER + SSM_GROUPS * SSM_STATE:]

    dt = _softplus(dt_ref[0] + dtb_ref[...])
    dtt = _softplus(dtt_ref[0] + dtbt_ref[...])
    adt = dt * (-jnp.exp(al_ref[...]))
    adtt = dtt * (-jnp.exp(alt_ref[...]))
    r = lax.broadcasted_iota(jnp.int32, (lc, lc), 0)
    cc = lax.broadcasted_iota(jnp.int32, (lc, lc), 1)
    tril = cc <= r
    acs = jnp.dot(tril.astype(F32), adt, preferred_element_type=F32, precision=HI)
    acst = jnp.dot(adtt, (r <= cc).astype(F32), preferred_element_type=F32, precision=HI)
    hh = lax.broadcasted_iota(jnp.int32, (SSM_HEADS, SSM_D_INNER), 0)
    ll = lax.broadcasted_iota(jnp.int32, (SSM_HEADS, SSM_D_INNER), 1)
    expand = ((ll >= hh * SSM_HEAD_DIM) & (ll < (hh + 1) * SSM_HEAD_DIM)).astype(F32)
    dt_x = jnp.dot(dt, expand, preferred_element_type=F32, precision=HI)
    acs_x = jnp.dot(acs, expand, preferred_element_type=F32, precision=HI)
    tot_x = acs_x[lc - 1:lc, :]
    xdt = xs * dt_x
    xdt_b = xdt.astype(BF16)
    w_end = (xdt * jnp.exp(tot_x - acs_x)).astype(BF16)
    state = state_ref[...]
    state_b = state.astype(BF16)

    y_parts, new_parts = [], []
    for g in range(SSM_GROUPS):
        bg = bm[:, g * SSM_STATE:(g + 1) * SSM_STATE].astype(BF16)
        cg = cm[:, g * SSM_STATE:(g + 1) * SSM_STATE].astype(BF16)
        cb = lax.dot_general(cg, bg, (((1,), (1,)), ((), ())), preferred_element_type=F32)
        for hl in range(hpg):
            h = g * hpg + hl
            seg = acs[:, h:h + 1] - acst[h:h + 1, :]
            mh = (cb * jnp.exp(jnp.where(tril, seg, NEG_BIG))).astype(BF16)
            y_parts.append(jnp.dot(mh, xdt_b[:, h * SSM_HEAD_DIM:(h + 1) * SSM_HEAD_DIM], preferred_element_type=F32))
        new_parts.append(lax.dot_general(bg, w_end[:, g * gw:(g + 1) * gw], (((0,), (0,)), ((), ())),
                                         preferred_element_type=F32))
    y_off = jnp.concatenate(
        [jnp.dot(cm[:, g * SSM_STATE:(g + 1) * SSM_STATE].astype(BF16), state_b[:, g * gw:(g + 1) * gw],
                 preferred_element_type=F32) for g in range(SSM_GROUPS)], axis=1) * jnp.exp(acs_x)
    y = jnp.concatenate(y_parts, axis=1) + y_off + dx_ref[...] * xs
    state_ref[...] = jnp.exp(tot_x) * state + jnp.concatenate(new_parts, axis=1)

    zz = z_ref[0]
    y = y * (zz * jax.nn.sigmoid(zz))
    for g in range(SSM_GROUPS):
        y_ref[0, :, g * gw:(g + 1) * gw] = _rms(y[:, g * gw:(g + 1) * gw], nw_ref[:, g * gw:(g + 1) * gw]).astype(y_ref.dtype)

    @pl.when(c == nc - 1)
    def _():
        hout_ref[0] = state_ref[...]


def _ssd(z, xbc, dt, dtt, p, h0, c0, *, lc=256):
    b, s, cd = xbc.shape
    lc = _tile(s, lc, LANES) if s % LANES == 0 else s
    nc = s // lc
    hh = SSM_HEADS
    full2 = lambda shape: pl.BlockSpec(shape, lambda i, j: (0, 0))
    return pl.pallas_call(
        functools.partial(_ssd_kernel, lc=lc, nc=nc), grid=(b, nc),
        in_specs=[pl.BlockSpec((1, lc, SSM_D_INNER), lambda i, j: (i, j, 0)),
                  pl.BlockSpec((1, lc, cd), lambda i, j: (i, j, 0)),
                  pl.BlockSpec((1, lc, hh), lambda i, j: (i, j, 0)),
                  pl.BlockSpec((1, hh, lc), lambda i, j: (i, 0, j)),
                  full2((SSM_CONV_W, cd)), full2((1, cd)),
                  full2((1, hh)), full2((hh, 1)), full2((1, hh)), full2((hh, 1)),
                  full2((1, SSM_D_INNER)), full2((1, SSM_D_INNER)),
                  pl.BlockSpec((1, SSM_STATE, SSM_D_INNER), lambda i, j: (i, 0, 0)),
                  pl.BlockSpec((1, SUBLANES, cd), lambda i, j: (i, 0, 0))],
        out_specs=[pl.BlockSpec((1, lc, SSM_D_INNER), lambda i, j: (i, j, 0)),
                   pl.BlockSpec((1, SSM_STATE, SSM_D_INNER), lambda i, j: (i, 0, 0))],
        out_shape=[jax.ShapeDtypeStruct((b, s, SSM_D_INNER), BF16),
                   jax.ShapeDtypeStruct((b, SSM_STATE, SSM_D_INNER), F32)],
        scratch_shapes=[pltpu.VMEM((SSM_STATE, SSM_D_INNER), F32), pltpu.VMEM((SUBLANES, cd), F32)],
        compiler_params=_params("parallel", "arbitrary"), name="ssd",
    )(z, xbc, dt, dtt, p["conv_w"], p["conv_b"], p["dt_b"], p["dt_bt"], p["a_log"], p["a_logt"], p["d_x"], p["norm_w"],
      h0, c0)


def _merge_kernel(xn_ref, o0, o1, o2, w0, w1, w2, wg0, wg1, wg2, out_ref):
    xn = xn_ref[...]
    acc = None
    for o_ref, w_ref, wg_ref in ((o0, w0, wg0), (o1, w1, wg1), (o2, w2, wg2)):
        gate = jax.nn.sigmoid(jnp.dot(xn, wg_ref[...], preferred_element_type=F32))
        t = gate * jnp.dot(o_ref[...], w_ref[...], preferred_element_type=F32)
        acc = t if acc is None else acc + t
    out_ref[...] = acc.astype(out_ref.dtype)


def _merge(xn, o_list, w_list, w_gate, *, tm=1024, tn=512):
    m, d = xn.shape
    tm = _tile(m, tm, SUBLANES)
    tn = _tile(d, tn, LANES)
    nb = d // tn
    in_specs = [pl.BlockSpec((tm, d), lambda i, j: (i, 0))]
    in_specs += [pl.BlockSpec((tm, o.shape[1]), lambda i, j: (i, 0)) for o in o_list]
    w_args = []
    for w in w_list:
        arr, sp = _wspec(w, (_wshape(w)[0], tn), lambda i, j: (0, j))
        in_specs.append(sp)
        w_args.append(arr)
    for br in range(N_BRANCH):
        arr, sp = _wspec(w_gate, (d, tn), functools.partial(lambda i, j, br: (0, br * nb + j), br=br))
        in_specs.append(sp)
        w_args.append(arr)
    return pl.pallas_call(
        _merge_kernel, grid=(m // tm, nb), in_specs=in_specs,
        out_specs=pl.BlockSpec((tm, tn), lambda i, j: (i, j)),
        out_shape=jax.ShapeDtypeStruct((m, d), BF16),
        compiler_params=_params("parallel", "arbitrary"), name="merge",
    )(xn, *o_list, *w_args)


def _stacked_weights(a):
    depth, d_model = a["w_in"].shape[:2]
    q_lora, kv_lora = a["mla_q_norm"].shape[1], a["mla_kv_norm"].shape[1]
    conv_dim = a["ssm_conv_w"].shape[2]
    sizes = (q_lora, kv_lora, MLA_ROPE, SSM_D_INNER, conv_dim, SSM_HEADS,
             FOX_HEADS * FOX_HEAD_DIM, FOX_HEADS * FOX_HEAD_DIM, FOX_HEADS * FOX_HEAD_DIM, FOX_HEADS,
             N_BRANCH * d_model)
    w_in = a["w_in"]
    assert w_in.shape[2] == sum(sizes)
    cols, start = [], 0
    for n in sizes:
        cols.append(w_in[:, :, start:start + n])
        start += n
    w_q, w_ckv, w_kpe, w_z, w_xbc, w_dt, w_fq, w_fk, w_fv, w_ff, w_gate = cols
    bf = lambda t: t.astype(BF16)
    small = [w_kpe, w_dt, w_ff]
    n_small = sum(t.shape[2] for t in small)
    assert n_small <= LANES
    parts = [w_q, w_ckv, *small, jnp.zeros((depth, d_model, INPROJ_TN - n_small), w_in.dtype),
             w_z, w_xbc, w_fq, w_fk, w_fv]
    st = {"in_cat": bf(jnp.concatenate(parts, axis=2)), "in_gate": bf(w_gate)}
    wq = a["mla_w_uq"].reshape(depth, q_lora, MLA_HEADS, MLA_NOPE + MLA_ROPE)
    wq = jnp.pad(wq, ((0, 0), (0, 0), (0, 0), (0, MLA_QK_PAD - MLA_NOPE - MLA_ROPE)))
    st["uq"] = bf(wq.reshape(depth, q_lora, MLA_HEADS * MLA_QK_PAD))
    wkv = a["mla_w_ukv"].reshape(depth, kv_lora, MLA_HEADS, MLA_NOPE + MLA_V)
    st["ukv"] = bf(jnp.concatenate([wkv[..., :MLA_NOPE].reshape(depth, kv_lora, MLA_HEADS * MLA_NOPE),
                                    wkv[..., MLA_NOPE:].reshape(depth, kv_lora, MLA_HEADS * MLA_V)], axis=2))
    for nm in ("w_br_mla", "w_br_ssd", "w_br_fox", "w_out"):
        st[nm] = bf(a[nm])
    for pre in ("ffn1", "ffn2"):
        for src, dst in (("_w_gate", "_g"), ("_w_up", "_u"), ("_w_down", "_d")):
            st[pre + dst] = bf(a[pre + src])
    return st


def _layer_weights(l, a, stacked):
    q_lora, kv_lora = a["mla_q_norm"].shape[1], a["mla_kv_norm"].shape[1]
    conv_dim = a["ssm_conv_w"].shape[2]
    w = {name: (arr, l) for name, arr in stacked.items()}
    w["in_widths"] = (q_lora, kv_lora, LANES, SSM_D_INNER, conv_dim) + (FOX_HEADS * FOX_HEAD_DIM,) * 3
    for pre in ("ffn1", "ffn2"):
        w[pre + "_norm"] = a[pre + "_norm"][l]
    w["mix_norm"] = a["mix_norm"][l]
    w["q_norm"], w["kv_norm"] = a["mla_q_norm"][l], a["mla_kv_norm"][l]
    ff_lo = MLA_ROPE + SSM_HEADS
    w["fb_lanes"] = jnp.pad(a["fox_b_f"][l].astype(F32), (ff_lo, LANES - ff_lo - FOX_HEADS)).reshape(1, LANES)
    w["ssd"] = {
        "conv_w": a["ssm_conv_w"][l].astype(F32), "conv_b": a["ssm_conv_b"][l].astype(F32).reshape(1, conv_dim),
        "dt_b": a["ssm_dt_bias"][l].astype(F32).reshape(1, SSM_HEADS),
        "dt_bt": a["ssm_dt_bias"][l].astype(F32).reshape(SSM_HEADS, 1),
        "a_log": a["ssm_a_log"][l].astype(F32).reshape(1, SSM_HEADS),
        "a_logt": a["ssm_a_log"][l].astype(F32).reshape(SSM_HEADS, 1),
        "d_x": jnp.repeat(a["ssm_d"][l].astype(F32), SSM_HEAD_DIM).reshape(1, SSM_D_INNER),
        "norm_w": a["ssm_norm"][l].astype(F32).reshape(1, SSM_D_INNER),
    }
    return w


def _rope_tables(pos):
    half = MLA_ROPE // 2
    inv_freq = ROPE_BASE ** (-jnp.arange(half, dtype=F32) / half)
    ang = pos.astype(F32)[:, None] * inv_freq[None, :]
    cos, sin = jnp.cos(ang), jnp.sin(ang)
    z = jnp.zeros_like(cos)
    pad = jnp.zeros((pos.shape[0], LANES - MLA_ROPE), F32)
    return (jnp.concatenate([cos, cos, pad], axis=1),
            jnp.concatenate([-sin, z, pad], axis=1),
            jnp.concatenate([z, sin, pad], axis=1))


def _pad_keys(t, sk_pad):
    return jnp.pad(t, ((0, 0), (0, sk_pad - t.shape[1])) + ((0, 0),) * (t.ndim - 2))


def _layer(x, bsz, s, w, tabs, past, final_gain):
    m = bsz * s
    kv_lora = w["kv_norm"].shape[0]
    conv_dim = w["ssd"]["conv_w"].shape[1]
    x, xn = _ffn(x, w["ffn1_norm"], w["ffn1_g"], w["ffn1_u"], w["ffn1_d"], post="norm_bf16", post_gain=w["mix_norm"])

    u_q, u_ckv, u_small, u_z, u_xbc, fq, fk, fk_b, fv, fv_b = _inproj(
        xn, w["in_cat"], w["in_widths"], ((F32,),) * 5 + ((BF16,), (F32, BF16), (F32, BF16)), tn=INPROJ_TN)

    ff_lo = MLA_ROPE + SSM_HEADS
    ckv_new, small2 = _prep(u_ckv, w["kv_norm"], u_small, tabs, w["fb_lanes"], ff_lo=ff_lo, ff_hi=ff_lo + FOX_HEADS)
    kpe_new = small2[:, :MLA_ROPE]
    logf_new = small2[:, ff_lo:ff_lo + FOX_HEADS]
    u_dt = u_small[:, MLA_ROPE:ff_lo]

    if past is not None:
        caches, l = past
        past_len = caches["mla_ckv"].shape[2]
    ukv = functools.partial(_mm, w=w["ukv"], out_dtype=BF16, prologue="cast", tn=2048, name="mla_ukv")

    q_full = _mm(u_q, w["uq"], out_dtype=BF16, prologue="rms", gain=w["q_norm"], rope_tabs=tabs, tn=2048, name="mla_q")
    q_full = q_full.reshape(bsz, s, -1)
    kv_new = ukv(ckv_new).reshape(bsz, s, -1)
    mla = dict(heads=MLA_HEADS, dq=MLA_QK_PAD, dk=MLA_NOPE, dv=MLA_V, scale=MLA_SCALE, mode="chunk")
    sk_pad = -(-s // LANES) * LANES
    if past is None:
        kpe_pad = jnp.pad(kpe_new.astype(BF16).reshape(bsz, s, -1),
                          ((0, 0), (0, sk_pad - s), (0, MLA_QK_PAD - MLA_NOPE - MLA_ROPE)))
        kv_all = _pad_keys(kv_new, sk_pad)
        o_mla = _attention(q_full, kv_all, kv_all, kpe_pad, None, k_col=0, v_col=1, q_off=0, n_valid=s, **mla)
    else:
        kv_past = ukv(caches["mla_ckv"][l].reshape(bsz * past_len, kv_lora)).reshape(bsz, past_len, -1)
        o_mla = _decode_attention(
            q_full, ((kv_past, (), 0), (kv_past, (), 1)), ((kv_new, (), 0), (kv_new, (), 1)),
            ((caches["mla_kpe"], (l,)), (kpe_new.reshape(bsz, s, MLA_ROPE), ())), None, **mla)

    if past is None:
        conv_state = jnp.zeros((bsz, SSM_CONV_W - 1, conv_dim), F32)
        h0 = jnp.zeros((bsz, SSM_STATE, SSM_D_INNER), F32)
    else:
        conv_state = caches["conv"][l].astype(F32)
        h0 = jnp.transpose(caches["ssm"][l].astype(F32), (0, 3, 1, 2)).reshape(bsz, SSM_STATE, SSM_D_INNER)
    c0 = jnp.pad(conv_state, ((0, 0), (SUBLANES - (SSM_CONV_W - 1), 0), (0, 0)))
    xbc3 = u_xbc.reshape(bsz, s, conv_dim)
    dt3 = u_dt.reshape(bsz, s, SSM_HEADS)
    o_ssd, h_new = _ssd(u_z.reshape(bsz, s, SSM_D_INNER), xbc3, dt3, jnp.swapaxes(dt3, 1, 2), w["ssd"], h0, c0)
    ssm_new = jnp.transpose(h_new.reshape(bsz, SSM_STATE, SSM_HEADS, SSM_HEAD_DIM), (0, 2, 3, 1))
    keep = SSM_CONV_W - 1
    conv_new = xbc3[:, s - keep:] if s >= keep else jnp.concatenate([conv_state, xbc3], axis=1)[:, -keep:]

    hw = FOX_HEADS * FOX_HEAD_DIM
    fox = dict(heads=FOX_HEADS, dq=FOX_HEAD_DIM, dk=FOX_HEAD_DIM, dv=FOX_HEAD_DIM, scale=FOX_SCALE, mode="causal")
    logf_all = logf_new.reshape(bsz, s, FOX_HEADS)
    if past is not None:
        logf_all = jnp.concatenate([caches["fox_logf"][l].astype(F32), logf_all], axis=1)
    n_keys = logf_all.shape[1]
    neg_cum = _cumsum_last(jnp.swapaxes(_pad_keys(logf_all, -(-n_keys // LANES) * LANES), 1, 2), -1.0 / FOX_SCALE)
    fq3 = fq.reshape(bsz, s, hw)
    if past is None:
        o_fox = _attention(fq3, _pad_keys(fk_b.reshape(bsz, s, hw), sk_pad), _pad_keys(fv_b.reshape(bsz, s, hw), sk_pad),
                           None, neg_cum, q_off=0, n_valid=s, **fox)
    else:
        o_fox = _decode_attention(
            fq3, ((caches["fox_k"], (l,), 0), (caches["fox_v"], (l,), 0)),
            ((fk.reshape(bsz, s, hw), (), 0), (fv.reshape(bsz, s, hw), (), 0)), None,
            (neg_cum[:, :, :past_len], neg_cum[:, :, past_len:n_keys]), **fox)

    merged = _merge(xn, [o_mla.reshape(m, -1), o_ssd.reshape(m, -1), o_fox.reshape(m, -1)],
                    [w["w_br_mla"], w["w_br_ssd"], w["w_br_fox"]], w["in_gate"])
    x = _mm(merged, w["w_out"], out_dtype=F32, residual=x, tm=512, tn=2048, name="out_proj")
    (x,) = _ffn(x, w["ffn2_norm"], w["ffn2_g"], w["ffn2_u"], w["ffn2_d"],
                post=None if final_gain is None else "norm_only", post_gain=final_gain)
    state = (ckv_new.reshape(bsz, s, kv_lora), kpe_new.reshape(bsz, s, MLA_ROPE),
             fk.reshape(bsz, s, FOX_HEADS, FOX_HEAD_DIM), fv.reshape(bsz, s, FOX_HEADS, FOX_HEAD_DIM),
             logf_new.reshape(bsz, s, FOX_HEADS), ssm_new, conv_new)
    return x, state


def kernel(x_prompt, x_sample, cache_mla_ckv, cache_mla_kpe, cache_fox_k, cache_fox_v, cache_fox_logf, state_ssm,
           state_conv, ffn1_norm, ffn1_w_gate, ffn1_w_up, ffn1_w_down, mix_norm, w_in, mla_q_norm, mla_w_uq,
           mla_kv_norm, mla_w_ukv, ssm_conv_w, ssm_conv_b, ssm_dt_bias, ssm_a_log, ssm_d, ssm_norm, fox_b_f,
           w_br_mla, w_br_ssd, w_br_fox, w_out, ffn2_norm, ffn2_w_gate, ffn2_w_up, ffn2_w_down, final_norm):
    a = dict(ffn1_norm=ffn1_norm, ffn1_w_gate=ffn1_w_gate, ffn1_w_up=ffn1_w_up, ffn1_w_down=ffn1_w_down,
             mix_norm=mix_norm, w_in=w_in, mla_q_norm=mla_q_norm, mla_w_uq=mla_w_uq, mla_kv_norm=mla_kv_norm,
             mla_w_ukv=mla_w_ukv, ssm_conv_w=ssm_conv_w, ssm_conv_b=ssm_conv_b, ssm_dt_bias=ssm_dt_bias,
             ssm_a_log=ssm_a_log, ssm_d=ssm_d, ssm_norm=ssm_norm, fox_b_f=fox_b_f, w_br_mla=w_br_mla,
             w_br_ssd=w_br_ssd, w_br_fox=w_br_fox, w_out=w_out, ffn2_norm=ffn2_norm, ffn2_w_gate=ffn2_w_gate,
             ffn2_w_up=ffn2_w_up, ffn2_w_down=ffn2_w_down)
    depth = w_in.shape[0]
    stacked = _stacked_weights(a)
    bp, sp, d_model = x_prompt.shape
    bs, ss, _ = x_sample.shape
    past_len = cache_mla_ckv.shape[2]
    tabs_p = tuple(jnp.tile(t, (bp, 1)) for t in _rope_tables(jnp.arange(sp, dtype=jnp.int32)))
    tabs_s = tuple(jnp.tile(t, (bs, 1)) for t in _rope_tables(past_len + jnp.arange(ss, dtype=jnp.int32)))
    hp = x_prompt.reshape(bp * sp, d_model).astype(F32)
    hs = x_sample.reshape(bs * ss, d_model).astype(F32)
    caches = {"mla_ckv": cache_mla_ckv, "mla_kpe": cache_mla_kpe, "fox_k": cache_fox_k, "fox_v": cache_fox_v,
              "fox_logf": cache_fox_logf, "ssm": state_ssm, "conv": state_conv}
    new_p, new_s = [], []
    for l in range(depth):
        w = _layer_weights(l, a, stacked)
        fg = final_norm if l == depth - 1 else None
        hp, st_p = _layer(hp, bp, sp, w, tabs_p, None, fg)
        hs, st_s = _layer(hs, bs, ss, w, tabs_s, (caches, l), fg)
        new_p.append(st_p)
        new_s.append(st_s)
    y_prompt = hp.reshape(bp, sp, d_model)
    y_sample = hs.reshape(bs, ss, d_model)
    stk = lambda states, i: jnp.stack([st[i] for st in states], axis=0)
    return (y_prompt, y_sample) + tuple(stk(new_p, i) for i in range(7)) + tuple(stk(new_s, i) for i in range(7))
```

```python
import functools
import math

import jax
import jax.numpy as jnp
from jax import lax
from jax.experimental import pallas as pl
from jax.experimental.pallas import tpu as pltpu

F32 = jnp.float32
BF16 = jnp.bfloat16

EPS = 1e-6
CHUNK = 64
FFN_RES = 0.5
MLA_HEADS, MLA_NOPE, MLA_ROPE, MLA_V = 8, 128, 64, 128
MLA_SCALE = (MLA_NOPE + MLA_ROPE) ** -0.5
ROPE_BASE = 10000.0
SSM_HEADS, SSM_HEAD_DIM, SSM_GROUPS, SSM_STATE, SSM_CONV_W = 16, 64, 2, 128, 4
SSM_D_INNER = SSM_HEADS * SSM_HEAD_DIM
FOX_HEADS, FOX_HEAD_DIM = 8, 128
FOX_SCALE = FOX_HEAD_DIM ** -0.5
N_BRANCH = 3

LANES = 128
SUBLANES = 8
MXU_DIM = 256
VMEM_LIMIT = 56 * 1024 * 1024

MLA_QK_PAD = MXU_DIM
INPROJ_TN = 512
NEG_BIG = -1e30
LOG2E = math.log2(math.e)
HI = lax.Precision.HIGHEST


def _tile(n, pref, align):
    t = (min(pref, n) // align) * align
    while t >= align:
        if n % t == 0:
            return t
        t -= align
    return n


def _params(*sem):
    return pltpu.CompilerParams(dimension_semantics=sem, vmem_limit_bytes=VMEM_LIMIT)


def _wshape(w):
    return w[0].shape[1:] if isinstance(w, tuple) else w.shape


def _wspec(w, block, index):
    if isinstance(w, tuple):
        arr, layer = w
        return arr, pl.BlockSpec((None,) + tuple(block), lambda *g: (layer,) + tuple(index(*g)))
    return w, pl.BlockSpec(tuple(block), index)


def _rms(x, g):
    return x * lax.rsqrt(jnp.mean(x * x, axis=-1, keepdims=True) + EPS) * g


def _softplus(x):
    return jnp.maximum(x, 0.0) + jnp.log1p(jnp.exp(-jnp.abs(x)))


def _rope_lanes(pe, cos, s1, s2):
    half = MLA_ROPE // 2
    return pe * cos + pltpu.roll(pe, LANES - half, 1) * s1 + pltpu.roll(pe, half, 1) * s2


def _mm_kernel(*refs, prologue, rope, residual, tn):
    it = iter(refs)
    x_ref = next(it)
    g_ref = next(it) if prologue == "rms" else None
    w_ref = next(it)
    res_ref = next(it) if residual else None
    tabs = (next(it), next(it), next(it)) if rope else None
    o_ref = next(it)
    xn_ref = next(it) if prologue != "none" else None

    if prologue == "none":
        lhs = x_ref[...]
    else:
        @pl.when(pl.program_id(1) == 0)
        def _():
            x = x_ref[...].astype(F32)
            if prologue == "rms":
                x = _rms(x, g_ref[...])
            xn_ref[...] = x.astype(BF16)
        lhs = xn_ref[...]
    acc = jnp.dot(lhs, w_ref[...], preferred_element_type=F32)
    if residual:
        acc = res_ref[...] + acc
    if rope:
        cos, s1, s2 = (t[...] for t in tabs)
        for c in range(tn // MLA_QK_PAD):
            a = c * MLA_QK_PAD
            o_ref[:, a:a + LANES] = acc[:, a:a + LANES].astype(o_ref.dtype)
            o_ref[:, a + LANES:a + 2 * LANES] = _rope_lanes(acc[:, a + LANES:a + 2 * LANES], cos, s1, s2).astype(o_ref.dtype)
    else:
        o_ref[...] = acc.astype(o_ref.dtype)


def _mm(x, w, *, out_dtype, tm=1024, tn=512, prologue="none", gain=None, residual=None, rope_tabs=None, name="mm"):
    m = x.shape[0]
    k, n = _wshape(w)
    assert x.shape[1] == k and (prologue != "none" or x.dtype == BF16)
    tm = _tile(m, tm, SUBLANES)
    tn = _tile(n, tn, MLA_QK_PAD if rope_tabs is not None else LANES)
    grid = (m // tm, n // tn)
    in_specs = [pl.BlockSpec((tm, k), lambda i, j: (i, 0))]
    args = [x]
    if prologue == "rms":
        in_specs.append(pl.BlockSpec((1, k), lambda i, j: (0, 0)))
        args.append(gain.reshape(1, k).astype(F32))
    w_arr, w_spec = _wspec(w, (k, tn), lambda i, j: (0, j))
    in_specs.append(w_spec)
    args.append(w_arr)
    if residual is not None:
        in_specs.append(pl.BlockSpec((tm, tn), lambda i, j: (i, j)))
        args.append(residual)
    if rope_tabs is not None:
        for t in rope_tabs:
            in_specs.append(pl.BlockSpec((tm, LANES), lambda i, j: (i, 0)))
            args.append(t)
    scratch = [pltpu.VMEM((tm, k), BF16)] if prologue != "none" else []
    kern = functools.partial(_mm_kernel, prologue=prologue, rope=rope_tabs is not None,
                             residual=residual is not None, tn=tn)
    return pl.pallas_call(
        kern, grid=grid, in_specs=in_specs,
        out_specs=pl.BlockSpec((tm, tn), lambda i, j: (i, j)),
        out_shape=jax.ShapeDtypeStruct((m, n), out_dtype),
        scratch_shapes=scratch, compiler_params=_params("parallel", "arbitrary"), name=name,
    )(*args)


def _inproj_kernel(xn_ref, w_ref, *outs, groups):
    j = pl.program_id(1)
    outs = iter(outs)
    for lo, hi, width, copies in groups:
        o_refs = [next(outs) for _ in range(copies)]

        @pl.when((j >= lo) & (j < hi))
        def _(o_refs=o_refs, width=width):
            acc = jnp.dot(xn_ref[...], w_ref[:, :width], preferred_element_type=F32)
            for o_ref in o_refs:
                o_ref[...] = acc.astype(o_ref.dtype)


def _inproj_layout(widths, dtypes, tn):
    groups, start = [], 0
    for n, dts in zip(widths, dtypes):
        nt = -(-n // tn)
        assert n % tn == 0 or n < tn
        groups.append((start, start + nt, min(n, tn), len(dts)))
        start += nt
    return groups, start


def _inproj(xn, w_cat, widths, dtypes, *, tm=1024, tn=512):
    m, d = xn.shape
    tm = _tile(m, tm, SUBLANES)
    groups, n_tiles = _inproj_layout(widths, dtypes, tn)
    assert _wshape(w_cat) == (d, n_tiles * tn)
    w_arr, w_spec = _wspec(w_cat, (d, tn), lambda i, j: (0, j))
    out_specs, out_shape = [], []
    for (lo, hi, bw, _), n, dts in zip(groups, widths, dtypes):
        for dt in dts:
            out_specs.append(pl.BlockSpec(
                (tm, bw), functools.partial(lambda i, j, lo, hi: (i, jnp.clip(j - lo, 0, hi - lo - 1)), lo=lo, hi=hi)))
            out_shape.append(jax.ShapeDtypeStruct((m, n), dt))
    return pl.pallas_call(
        functools.partial(_inproj_kernel, groups=groups), grid=(m // tm, n_tiles),
        in_specs=[pl.BlockSpec((tm, d), lambda i, j: (i, 0)), w_spec],
        out_specs=out_specs, out_shape=out_shape,
        compiler_params=_params("parallel", "arbitrary"), name="inproj",
    )(xn, w_arr)


def _ffn_kernel(*refs, nf, tf, f, post):
    x_ref, g_ref, wg_ref, wu_ref, wd_ref = refs[:5]
    refs = refs[5:]
    pg_ref = None
    if post is not None:
        pg_ref, refs = refs[0], refs[1:]
    outs, (xn_ref, acc_ref) = refs[:-2], refs[-2:]
    j = pl.program_id(1)

    @pl.when(j == 0)
    def _():
        xn_ref[...] = _rms(x_ref[...], g_ref[...]).astype(BF16)
        acc_ref[...] = jnp.zeros_like(acc_ref)

    xn = xn_ref[...]
    a = jnp.dot(xn, wg_ref[...], preferred_element_type=F32)
    b = jnp.dot(xn, wu_ref[...], preferred_element_type=F32)
    h = a * jax.nn.sigmoid(a) * b
    wd = wd_ref[...]
    if f % tf:
        valid = f - j * tf
        h = jnp.where(lax.broadcasted_iota(jnp.int32, h.shape, 1) < valid, h, 0.0)
        wd = jnp.where(lax.broadcasted_iota(jnp.int32, wd.shape, 0) < valid, wd, jnp.zeros_like(wd))
    acc_ref[...] += jnp.dot(h.astype(BF16), wd, preferred_element_type=F32)

    @pl.when(j == nf - 1)
    def _():
        y = x_ref[...] + FFN_RES * acc_ref[...]
        if post is None:
            outs[0][...] = y
        elif post == "norm_bf16":
            outs[0][...] = y
            outs[1][...] = _rms(y, pg_ref[...]).astype(BF16)
        else:
            outs[0][...] = _rms(y, pg_ref[...])


def _ffn(x, gain, wg, wu, wd, *, post=None, post_gain=None, tm=512, tf=512):
    m, d = x.shape
    f = _wshape(wg)[1]
    tm = _tile(m, tm, SUBLANES)
    nf = pl.cdiv(f, tf)
    row = pl.BlockSpec((tm, d), lambda i, j: (i, 0))
    vec = pl.BlockSpec((1, d), lambda i, j: (0, 0))
    wg_arr, wg_spec = _wspec(wg, (d, tf), lambda i, j: (0, j))
    wu_arr, wu_spec = _wspec(wu, (d, tf), lambda i, j: (0, j))
    wd_arr, wd_spec = _wspec(wd, (tf, d), lambda i, j: (j, 0))
    in_specs = [row, vec, wg_spec, wu_spec, wd_spec]
    args = [x, gain.reshape(1, d).astype(F32), wg_arr, wu_arr, wd_arr]
    out_specs, out_shape = [row], [jax.ShapeDtypeStruct((m, d), F32)]
    if post is not None:
        in_specs.append(vec)
        args.append(post_gain.reshape(1, d).astype(F32))
    if post == "norm_bf16":
        out_specs.append(row)
        out_shape.append(jax.ShapeDtypeStruct((m, d), BF16))
    return pl.pallas_call(
        functools.partial(_ffn_kernel, nf=nf, tf=tf, f=f, post=post), grid=(m // tm, nf),
        in_specs=in_specs, out_specs=out_specs, out_shape=out_shape,
        scratch_shapes=[pltpu.VMEM((tm, d), BF16), pltpu.VMEM((tm, d), F32)],
        compiler_params=_params("parallel", "arbitrary"), name="ffn",
    )(*args)


def _prep_kernel(uc_ref, g_ref, us_ref, cos_ref, s1_ref, s2_ref, fb_ref, ckv_ref, sm_ref, *, ff_lo, ff_hi):
    ckv_ref[...] = _rms(uc_ref[...], g_ref[...])
    us = us_ref[...]
    lane = lax.broadcasted_iota(jnp.int32, us.shape, 1)
    pe = jnp.where(lane < MLA_ROPE, us, 0.0)
    rot = _rope_lanes(pe, cos_ref[...], s1_ref[...], s2_ref[...])
    logf = -_softplus(-(us + fb_ref[...]))
    sm_ref[...] = jnp.where((lane >= ff_lo) & (lane < ff_hi), logf, rot)


def _prep(u_ckv, kv_gain, u_small, tabs, fb_lanes, *, ff_lo, ff_hi, tm=1024):
    m = u_small.shape[0]
    kv = kv_gain.shape[0]
    tm = _tile(m, tm, SUBLANES)
    row = lambda i: (i, 0)
    return pl.pallas_call(
        functools.partial(_prep_kernel, ff_lo=ff_lo, ff_hi=ff_hi), grid=(m // tm,),
        in_specs=[pl.BlockSpec((tm, kv), row),
                  pl.BlockSpec((1, kv), lambda i: (0, 0)),
                  pl.BlockSpec((tm, LANES), row), pl.BlockSpec((tm, LANES), row),
                  pl.BlockSpec((tm, LANES), row), pl.BlockSpec((tm, LANES), row),
                  pl.BlockSpec((1, LANES), lambda i: (0, 0))],
        out_specs=[pl.BlockSpec((tm, kv), row), pl.BlockSpec((tm, LANES), row)],
        out_shape=[jax.ShapeDtypeStruct((m, kv), F32), jax.ShapeDtypeStruct((m, LANES), F32)],
        compiler_params=_params("parallel"), name="prep",
    )(u_ckv, kv_gain.reshape(1, kv).astype(F32), u_small, *tabs, fb_lanes)


def _cumsum_kernel(x_ref, o_ref, carry_ref, *, tc, mult):
    @pl.when(pl.program_id(1) == 0)
    def _():
        carry_ref[...] = jnp.zeros_like(carry_ref)

    r = lax.broadcasted_iota(jnp.int32, (tc, tc), 0)
    c = lax.broadcasted_iota(jnp.int32, (tc, tc), 1)
    upper = (r <= c).astype(F32)
    y = jnp.dot(x_ref[0], upper, preferred_element_type=F32, precision=HI) + carry_ref[:, :1]
    o_ref[0] = y * mult
    carry_ref[...] = jnp.broadcast_to(y[:, tc - 1:tc], carry_ref.shape)


def _cumsum_last(x, mult, *, tc=512):
    b, h, s = x.shape
    tc = _tile(s, tc, LANES)
    return pl.pallas_call(
        functools.partial(_cumsum_kernel, tc=tc, mult=mult), grid=(b, s // tc),
        in_specs=[pl.BlockSpec((1, h, tc), lambda i, j: (i, 0, j))],
        out_specs=pl.BlockSpec((1, h, tc), lambda i, j: (i, 0, j)),
        out_shape=jax.ShapeDtypeStruct((b, h, s), F32),
        scratch_shapes=[pltpu.VMEM((h, LANES), F32)],
        compiler_params=_params("parallel", "arbitrary"), name="cumsum",
    )(x)


def _last_visible(q_end, mode):
    if mode == "chunk":
        return (q_end // CHUNK) * CHUNK + (CHUNK - 1)
    return q_end


def _attn_kernel(*refs, heads, dq, dk, dv, tq, tk, nk, scale, mode, q_off, n_valid, has_bias, has_shared):
    it = iter(refs)
    qi_ref, ki_ref = next(it), next(it)
    q_ref, k_ref, v_ref = next(it), next(it), next(it)
    ks_ref = next(it) if has_shared else None
    b_ref = next(it) if has_bias else None
    o_ref, m_ref, acc_ref = next(it), next(it), next(it)
    t = pl.program_id(1)
    qi, ki = qi_ref[t], ki_ref[t]
    nch = tk // LANES
    c = scale * LOG2E
    aw = dv + LANES

    @pl.when(ki == 0)
    def _():
        m_ref[...] = jnp.full_like(m_ref, NEG_BIG)
        acc_ref[...] = jnp.zeros_like(acc_ref)

    q_lo = q_off + qi * tq
    k_lo = ki * tk
    last_tile = jnp.minimum(_last_visible(q_lo + (tq - 1), mode) // tk, nk - 1)
    first_maskable = _last_visible(q_lo, mode) + 1
    ones_col = (lax.broadcasted_iota(jnp.int32, (tk, LANES), 1) == 0).astype(BF16)

    def body(masked):
        if masked:
            qpos = q_lo + lax.broadcasted_iota(jnp.int32, (tq, tk), 0)
            kpos = k_lo + lax.broadcasted_iota(jnp.int32, (tq, tk), 1)
            if mode == "chunk":
                sh = CHUNK.bit_length() - 1
                vis = lax.shift_right_logical(kpos, sh) <= lax.shift_right_logical(qpos, sh)
            else:
                vis = kpos <= qpos
            vis = vis & (kpos < n_valid)
        for h in range(heads):
            q = q_ref[0, :, h * dq:(h + 1) * dq]
            k = k_ref[0, :, h * dk:(h + 1) * dk]
            if has_shared:
                k = jnp.concatenate([k, ks_ref[0]], axis=1)
            s = lax.dot_general(q, k, (((1,), (1,)), ((), ())), preferred_element_type=F32)
            if has_bias:
                s = s + b_ref[0, h:h + 1, :]
            if masked:
                s = jnp.where(vis, s, NEG_BIG)
            m_prev = m_ref[h]
            mc = s[:, :LANES]
            for j in range(1, nch):
                mc = jnp.maximum(mc, s[:, j * LANES:(j + 1) * LANES])
            m_new = jnp.maximum(m_prev, jnp.max(mc, axis=1, keepdims=True))
            m_ref[h] = m_new
            alpha = jnp.exp2((m_prev - m_new) * c)
            p = jnp.concatenate([jnp.exp2((s[:, j * LANES:(j + 1) * LANES] - m_new) * c).astype(BF16)
                                 for j in range(nch)], axis=1)
            vx = jnp.concatenate([v_ref[0, :, h * dv:(h + 1) * dv], ones_col], axis=1)
            pv = jnp.dot(p, vx, preferred_element_type=F32)
            for a0 in range(h * aw, (h + 1) * aw, LANES):
                acc_ref[:, a0:a0 + LANES] = alpha * acc_ref[:, a0:a0 + LANES] + pv[:, a0 - h * aw:a0 - h * aw + LANES]

    need_mask = (k_lo + (tk - 1) >= first_maskable) | (k_lo + tk > n_valid)

    @pl.when(need_mask)
    def _():
        body(True)

    @pl.when(jnp.logical_not(need_mask))
    def _():
        body(False)

    @pl.when(ki == last_tile)
    def _():
        for h in range(heads):
            l = acc_ref[:, h * aw + dv:h * aw + dv + 1]
            o_ref[0, :, h * dv:(h + 1) * dv] = (acc_ref[:, h * aw:h * aw + dv] / l).astype(o_ref.dtype)


def _attention(q, k, v, k_shared, bias, *, heads, dq, dk, dv, scale, mode, q_off, n_valid, k_col=0, v_col=0,
               tq=1024, tk=1024):
    b, sq, _ = q.shape
    sk = k.shape[1]
    assert dk + (0 if k_shared is None else k_shared.shape[2]) == dq
    tq = _tile(sq, tq, SUBLANES)
    tk = sk if sk <= 2 * tk else _tile(sk, tk, LANES)
    nq, nk = sq // tq, sk // tk
    pairs = [(i, j) for i in range(nq)
             for j in range(min(_last_visible(q_off + i * tq + (tq - 1), mode) // tk, nk - 1) + 1)]
    qi_arr = jnp.asarray([p[0] for p in pairs], jnp.int32)
    ki_arr = jnp.asarray([p[1] for p in pairs], jnp.int32)

    in_specs = [pl.BlockSpec((1, tq, heads * dq), lambda bi, t, qi, ki: (bi, qi[t], 0)),
                pl.BlockSpec((1, tk, heads * dk), lambda bi, t, qi, ki: (bi, ki[t], k_col)),
                pl.BlockSpec((1, tk, heads * dv), lambda bi, t, qi, ki: (bi, ki[t], v_col))]
    args = [q, k, v]
    if k_shared is not None:
        in_specs.append(pl.BlockSpec((1, tk, dq - dk), lambda bi, t, qi, ki: (bi, ki[t], 0)))
        args.append(k_shared)
    if bias is not None:
        in_specs.append(pl.BlockSpec((1, heads, tk), lambda bi, t, qi, ki: (bi, 0, ki[t])))
        args.append(bias)
    kern = functools.partial(_attn_kernel, heads=heads, dq=dq, dk=dk, dv=dv, tq=tq, tk=tk, nk=nk, scale=scale,
                             mode=mode, q_off=q_off, n_valid=n_valid, has_bias=bias is not None,
                             has_shared=k_shared is not None)
    return pl.pallas_call(
        kern,
        grid_spec=pltpu.PrefetchScalarGridSpec(
            num_scalar_prefetch=2, grid=(b, len(pairs)), in_specs=in_specs,
            out_specs=pl.BlockSpec((1, tq, heads * dv), lambda bi, t, qi, ki: (bi, qi[t], 0)),
            scratch_shapes=[pltpu.VMEM((heads, tq, LANES), F32), pltpu.VMEM((tq, heads * (dv + LANES)), F32)]),
        out_shape=jax.ShapeDtypeStruct((b, sq, heads * dv), BF16),
        compiler_params=_params("parallel", "arbitrary"), name="attn_" + mode,
    )(qi_arr, ki_arr, *args)


def _decode_kernel(*refs, heads, dq, dk, dv, sq, p_len, scale, mode, has_shared, has_bias):
    it = iter(refs)
    q_ref, kp_ref, vp_ref, kn_ref, vn_ref = (next(it) for _ in range(5))
    ksp_ref, ksn_ref = (next(it), next(it)) if has_shared else (None, None)
    bp_ref, bn_ref = (next(it), next(it)) if has_bias else (None, None)
    o_ref = next(it)
    c = scale * LOG2E

    def head(ref, h, d):
        if ref.shape[1] == d and heads > 1:
            x = ref[pl.ds(h, ref.shape[0] // heads, stride=heads), :]
        else:
            x = ref[:, h * d:(h + 1) * d]
        return x.astype(BF16)

    row = lax.broadcasted_iota(jnp.int32, (sq, sq), 0)
    col = lax.broadcasted_iota(jnp.int32, (sq, sq), 1)
    if mode == "chunk":
        sh = CHUNK.bit_length() - 1
        vis = lax.shift_right_logical(p_len + col, sh) <= lax.shift_right_logical(p_len + row, sh)
    else:
        vis = col <= row
    if has_shared:
        pad = dq - dk - ksp_ref.shape[1]
        ksp = jnp.concatenate([ksp_ref[...].astype(BF16), jnp.zeros((p_len, pad), BF16)], axis=1)
        ksn = jnp.concatenate([ksn_ref[...].astype(BF16), jnp.zeros((sq, pad), BF16)], axis=1)
    contract_last = (((1,), (1,)), ((), ()))
    for h in range(heads):
        q = q_ref[:, h * dq:(h + 1) * dq]
        kp, kn = head(kp_ref, h, dk), head(kn_ref, h, dk)
        if has_shared:
            kp = jnp.concatenate([kp, ksp], axis=1)
            kn = jnp.concatenate([kn, ksn], axis=1)
        s_p = lax.dot_general(q, kp, contract_last, preferred_element_type=F32)
        s_n = lax.dot_general(q, kn, contract_last, preferred_element_type=F32)
        if has_bias:
            s_p = s_p + bp_ref[h:h + 1, :]
            s_n = s_n + bn_ref[h:h + 1, :]
        s_n = jnp.where(vis, s_n, NEG_BIG)
        m = jnp.maximum(jnp.max(s_p, axis=1, keepdims=True), jnp.max(s_n, axis=1, keepdims=True))
        p_p = jnp.exp2((s_p - m) * c)
        p_n = jnp.exp2((s_n - m) * c)
        l = jnp.sum(p_p, axis=1, keepdims=True) + jnp.sum(p_n, axis=1, keepdims=True)
        pv = (jnp.dot(p_p.astype(BF16), head(vp_ref, h, dv), preferred_element_type=F32)
              + jnp.dot(p_n.astype(BF16), head(vn_ref, h, dv), preferred_element_type=F32))
        o_ref[:, h * dv:(h + 1) * dv] = (pv / l).astype(o_ref.dtype)


def _decode_attention(q, past_kv, new_kv, shared, bias, *, heads, dq, dk, dv, scale, mode):
    b, sq, _ = q.shape

    def spec(arr, prefix, colblk, width):
        lead, inner = arr.shape[:len(prefix) + 1], arr.shape[len(prefix) + 1:]
        if len(inner) == 3:
            arr = arr.reshape(lead + (inner[0] * inner[1], inner[2]))
            block, idx = arr.shape[-2:], (0, 0)
        else:
            block, idx = (inner[0], width), (0, colblk)
        return arr, pl.BlockSpec((None,) * len(lead) + tuple(block), lambda i: tuple(prefix) + (i,) + idx)

    (kp, kp_pre, kp_col), (vp, vp_pre, vp_col) = past_kv
    (kn, kn_pre, kn_col), (vn, vn_pre, vn_col) = new_kv
    p_len = kp.shape[len(kp_pre) + 1]
    in_specs, args = [pl.BlockSpec((None, sq, heads * dq), lambda i: (i, 0, 0))], [q]
    for arr, pre, col, width in ((kp, kp_pre, kp_col, heads * dk), (vp, vp_pre, vp_col, heads * dv),
                                 (kn, kn_pre, kn_col, heads * dk), (vn, vn_pre, vn_col, heads * dv)):
        arr, sp = spec(arr, pre, col, width)
        in_specs.append(sp)
        args.append(arr)
    if shared is not None:
        for arr, pre in shared:
            arr, sp = spec(arr, pre, 0, arr.shape[-1])
            in_specs.append(sp)
            args.append(arr)
    if bias is not None:
        for arr in bias:
            in_specs.append(pl.BlockSpec((None,) + arr.shape[1:], lambda i: (i, 0, 0)))
            args.append(arr)
    kern = functools.partial(_decode_kernel, heads=heads, dq=dq, dk=dk, dv=dv, sq=sq, p_len=p_len, scale=scale,
                             mode=mode, has_shared=shared is not None, has_bias=bias is not None)
    return pl.pallas_call(
        kern, grid=(b,), in_specs=in_specs,
        out_specs=pl.BlockSpec((None, sq, heads * dv), lambda i: (i, 0, 0)),
        out_shape=jax.ShapeDtypeStruct((b, sq, heads * dv), BF16),
        compiler_params=_params("parallel"), name="decode_" + mode,
    )(*args)


def _ssd_kernel(z_ref, xbc_ref, dt_ref, dtt_ref, cw_ref, cb_ref, dtb_ref, dtbt_ref, al_ref, alt_ref, dx_ref, nw_ref,
                h0_ref, c0_ref, y_ref, hout_ref, state_ref, carry_ref, *, lc, nc):
    c = pl.program_id(1)
    gw = SSM_D_INNER // SSM_GROUPS
    hpg = SSM_HEADS // SSM_GROUPS
    halo = SUBLANES

    @pl.when(c == 0)
    def _():
        state_ref[...] = h0_ref[0]
        carry_ref[...] = c0_ref[0]

    x = xbc_ref[0]
    cat = jnp.concatenate([carry_ref[...], x], axis=0)
    conv = cb_ref[...]
    for kk in range(SSM_CONV_W):
        shift = SSM_CONV_W - 1 - kk
        src = pltpu.roll(cat, shift, 0) if shift else cat
        conv = conv + src[halo:, :] * cw_ref[kk:kk + 1, :]
    carry_ref[...] = x[lc - halo:, :]
    act = conv * jax.nn.sigmoid(conv)
    xs = act[:, :SSM_D_INNER]
    bm = act[:, SSM_D_INNER:SSM_D_INNER + SSM_GROUPS * SSM_STATE]
    cm = act[:, SSM_D_INNER + SSM_GROUPS * SSM_STATE:]

    dt = _softplus(dt_ref[0] + dtb_ref[...])
    dtt = _softplus(dtt_ref[0] + dtbt_ref[...])
    adt = dt * (-jnp.exp(al_ref[...]))
    adtt = dtt * (-jnp.exp(alt_ref[...]))
    r = lax.broadcasted_iota(jnp.int32, (lc, lc), 0)
    cc = lax.broadcasted_iota(jnp.int32, (lc, lc), 1)
    tril = cc <= r
    acs = jnp.dot(tril.astype(F32), adt, preferred_element_type=F32, precision=HI)
    acst = jnp.dot(adtt, (r <= cc).astype(F32), preferred_element_type=F32, precision=HI)
    hh = lax.broadcasted_iota(jnp.int32, (SSM_HEADS, SSM_D_INNER), 0)
    ll = lax.broadcasted_iota(jnp.int32, (SSM_HEADS, SSM_D_INNER), 1)
    expand = ((ll >= hh * SSM_HEAD_DIM) & (ll < (hh + 1) * SSM_HEAD_DIM)).astype(F32)
    dt_x = jnp.dot(dt, expand, preferred_element_type=F32, precision=HI)
    acs_x = jnp.dot(acs, expand, preferred_element_type=F32, precision=HI)
    tot_x = acs_x[lc - 1:lc, :]
    xdt = xs * dt_x
    xdt_b = xdt.astype(BF16)
    w_end = (xdt * jnp.exp(tot_x - acs_x)).astype(BF16)
    state = state_ref[...]
    state_b = state.astype(BF16)

    y_parts, new_parts = [], []
    for g in range(SSM_GROUPS):
        bg = bm[:, g * SSM_STATE:(g + 1) * SSM_STATE].astype(BF16)
        cg = cm[:, g * SSM_STATE:(g + 1) * SSM_STATE].astype(BF16)
        cb = lax.dot_general(cg, bg, (((1,), (1,)), ((), ())), preferred_element_type=F32)
        for hl in range(hpg):
            h = g * hpg + hl
            seg = acs[:, h:h + 1] - acst[h:h + 1, :]
            mh = (cb * jnp.exp(jnp.where(tril, seg, NEG_BIG))).astype(BF16)
            y_parts.append(jnp.dot(mh, xdt_b[:, h * SSM_HEAD_DIM:(h + 1) * SSM_HEAD_DIM], preferred_element_type=F32))
        new_parts.append(lax.dot_general(bg, w_end[:, g * gw:(g + 1) * gw], (((0,), (0,)), ((), ())),
                                         preferred_element_type=F32))
    y_off = jnp.concatenate(
        [jnp.dot(cm[:, g * SSM_STATE:(g + 1) * SSM_STATE].astype(BF16), state_b[:, g * gw:(g + 1) * gw],
                 preferred_element_type=F32) for g in range(SSM_GROUPS)], axis=1) * jnp.exp(acs_x)
    y = jnp.concatenate(y_parts, axis=1) + y_off + dx_ref[...] * xs
    state_ref[...] = jnp.exp(tot_x) * state + jnp.concatenate(new_parts, axis=1)

    zz = z_ref[0]
    y = y * (zz * jax.nn.sigmoid(zz))
    for g in range(SSM_GROUPS):
        y_ref[0, :, g * gw:(g + 1) * gw] = _rms(y[:, g * gw:(g + 1) * gw], nw_ref[:, g * gw:(g + 1) * gw]).astype(y_ref.dtype)

    @pl.when(c == nc - 1)
    def _():
        hout_ref[0] = state_ref[...]


def _ssd(z, xbc, dt, dtt, p, h0, c0, *, lc=256):
    b, s, cd = xbc.shape
    lc = _tile(s, lc, LANES) if s % LANES == 0 else s
    nc = s // lc
    hh = SSM_HEADS
    full2 = lambda shape: pl.BlockSpec(shape, lambda i, j: (0, 0))
    return pl.pallas_call(
        functools.partial(_ssd_kernel, lc=lc, nc=nc), grid=(b, nc),
        in_specs=[pl.BlockSpec((1, lc, SSM_D_INNER), lambda i, j: (i, j, 0)),
                  pl.BlockSpec((1, lc, cd), lambda i, j: (i, j, 0)),
                  pl.BlockSpec((1, lc, hh), lambda i, j: (i, j, 0)),
                  pl.BlockSpec((1, hh, lc), lambda i, j: (i, 0, j)),
                  full2((SSM_CONV_W, cd)), full2((1, cd)),
                  full2((1, hh)), full2((hh, 1)), full2((1, hh)), full2((hh, 1)),
                  full2((1, SSM_D_INNER)), full2((1, SSM_D_INNER)),
                  pl.BlockSpec((1, SSM_STATE, SSM_D_INNER), lambda i, j: (i, 0, 0)),
                  pl.BlockSpec((1, SUBLANES, cd), lambda i, j: (i, 0, 0))],
        out_specs=[pl.BlockSpec((1, lc, SSM_D_INNER), lambda i, j: (i, j, 0)),
                   pl.BlockSpec((1, SSM_STATE, SSM_D_INNER), lambda i, j: (i, 0, 0))],
        out_shape=[jax.ShapeDtypeStruct((b, s, SSM_D_INNER), BF16),
                   jax.ShapeDtypeStruct((b, SSM_STATE, SSM_D_INNER), F32)],
        scratch_shapes=[pltpu.VMEM((SSM_STATE, SSM_D_INNER), F32), pltpu.VMEM((SUBLANES, cd), F32)],
        compiler_params=_params("parallel", "arbitrary"), name="ssd",
    )(z, xbc, dt, dtt, p["conv_w"], p["conv_b"], p["dt_b"], p["dt_bt"], p["a_log"], p["a_logt"], p["d_x"], p["norm_w"],
      h0, c0)


def _merge_kernel(xn_ref, o0, o1, o2, w0, w1, w2, wg0, wg1, wg2, out_ref):
    xn = xn_ref[...]
    acc = None
    for o_ref, w_ref, wg_ref in ((o0, w0, wg0), (o1, w1, wg1), (o2, w2, wg2)):
        gate = jax.nn.sigmoid(jnp.dot(xn, wg_ref[...], preferred_element_type=F32))
        t = gate * jnp.dot(o_ref[...], w_ref[...], preferred_element_type=F32)
        acc = t if acc is None else acc + t
    out_ref[...] = acc.astype(out_ref.dtype)


def _merge(xn, o_list, w_list, w_gate, *, tm=1024, tn=512):
    m, d = xn.shape
    tm = _tile(m, tm, SUBLANES)
    tn = _tile(d, tn, LANES)
    nb = d // tn
    in_specs = [pl.BlockSpec((tm, d), lambda i, j: (i, 0))]
    in_specs += [pl.BlockSpec((tm, o.shape[1]), lambda i, j: (i, 0)) for o in o_list]
    w_args = []
    for w in w_list:
        arr, sp = _wspec(w, (_wshape(w)[0], tn), lambda i, j: (0, j))
        in_specs.append(sp)
        w_args.append(arr)
    for br in range(N_BRANCH):
        arr, sp = _wspec(w_gate, (d, tn), functools.partial(lambda i, j, br: (0, br * nb + j), br=br))
        in_specs.append(sp)
        w_args.append(arr)
    return pl.pallas_call(
        _merge_kernel, grid=(m // tm, nb), in_specs=in_specs,
        out_specs=pl.BlockSpec((tm, tn), lambda i, j: (i, j)),
        out_shape=jax.ShapeDtypeStruct((m, d), BF16),
        compiler_params=_params("parallel", "arbitrary"), name="merge",
    )(xn, *o_list, *w_args)


def _stacked_weights(a):
    depth, d_model = a["w_in"].shape[:2]
    q_lora, kv_lora = a["mla_q_norm"].shape[1], a["mla_kv_norm"].shape[1]
    conv_dim = a["ssm_conv_w"].shape[2]
    sizes = (q_lora, kv_lora, MLA_ROPE, SSM_D_INNER, conv_dim, SSM_HEADS,
             FOX_HEADS * FOX_HEAD_DIM, FOX_HEADS * FOX_HEAD_DIM, FOX_HEADS * FOX_HEAD_DIM, FOX_HEADS,
             N_BRANCH * d_model)
    w_in = a["w_in"]
    assert w_in.shape[2] == sum(sizes)
    cols, start = [], 0
    for n in sizes:
        cols.append(w_in[:, :, start:start + n])
        start += n
    w_q, w_ckv, w_kpe, w_z, w_xbc, w_dt, w_fq, w_fk, w_fv, w_ff, w_gate = cols
    bf = lambda t: t.astype(BF16)
    small = [w_kpe, w_dt, w_ff]
    n_small = sum(t.shape[2] for t in small)
    assert n_small <= LANES
    parts = [w_q, w_ckv, *small, jnp.zeros((depth, d_model, INPROJ_TN - n_small), w_in.dtype),
             w_z, w_xbc, w_fq, w_fk, w_fv]
    st = {"in_cat": bf(jnp.concatenate(parts, axis=2)), "in_gate": bf(w_gate)}
    wq = a["mla_w_uq"].reshape(depth, q_lora, MLA_HEADS, MLA_NOPE + MLA_ROPE)
    wq = jnp.pad(wq, ((0, 0), (0, 0), (0, 0), (0, MLA_QK_PAD - MLA_NOPE - MLA_ROPE)))
    st["uq"] = bf(wq.reshape(depth, q_lora, MLA_HEADS * MLA_QK_PAD))
    wkv = a["mla_w_ukv"].reshape(depth, kv_lora, MLA_HEADS, MLA_NOPE + MLA_V)
    st["ukv"] = bf(jnp.concatenate([wkv[..., :MLA_NOPE].reshape(depth, kv_lora, MLA_HEADS * MLA_NOPE),
                                    wkv[..., MLA_NOPE:].reshape(depth, kv_lora, MLA_HEADS * MLA_V)], axis=2))
    for nm in ("w_br_mla", "w_br_ssd", "w_br_fox", "w_out"):
        st[nm] = bf(a[nm])
    for pre in ("ffn1", "ffn2"):
        for src, dst in (("_w_gate", "_g"), ("_w_up", "_u"), ("_w_down", "_d")):
            st[pre + dst] = bf(a[pre + src])
    return st


def _layer_weights(l, a, stacked):
    q_lora, kv_lora = a["mla_q_norm"].shape[1], a["mla_kv_norm"].shape[1]
    conv_dim = a["ssm_conv_w"].shape[2]
    w = {name: (arr, l) for name, arr in stacked.items()}
    w["in_widths"] = (q_lora, kv_lora, LANES, SSM_D_INNER, conv_dim) + (FOX_HEADS * FOX_HEAD_DIM,) * 3
    for pre in ("ffn1", "ffn2"):
        w[pre + "_norm"] = a[pre + "_norm"][l]
    w["mix_norm"] = a["mix_norm"][l]
    w["q_norm"], w["kv_norm"] = a["mla_q_norm"][l], a["mla_kv_norm"][l]
    ff_lo = MLA_ROPE + SSM_HEADS
    w["fb_lanes"] = jnp.pad(a["fox_b_f"][l].astype(F32), (ff_lo, LANES - ff_lo - FOX_HEADS)).reshape(1, LANES)
    w["ssd"] = {
        "conv_w": a["ssm_conv_w"][l].astype(F32), "conv_b": a["ssm_conv_b"][l].astype(F32).reshape(1, conv_dim),
        "dt_b": a["ssm_dt_bias"][l].astype(F32).reshape(1, SSM_HEADS),
        "dt_bt": a["ssm_dt_bias"][l].astype(F32).reshape(SSM_HEADS, 1),
        "a_log": a["ssm_a_log"][l].astype(F32).reshape(1, SSM_HEADS),
        "a_logt": a["ssm_a_log"][l].astype(F32).reshape(SSM_HEADS, 1),
        "d_x": jnp.repeat(a["ssm_d"][l].astype(F32), SSM_HEAD_DIM).reshape(1, SSM_D_INNER),
        "norm_w": a["ssm_norm"][l].astype(F32).reshape(1, SSM_D_INNER),
    }
    return w


def _rope_tables(pos):
    half = MLA_ROPE // 2
    inv_freq = ROPE_BASE ** (-jnp.arange(half, dtype=F32) / half)
    ang = pos.astype(F32)[:, None] * inv_freq[None, :]
    cos, sin = jnp.cos(ang), jnp.sin(ang)
    z = jnp.zeros_like(cos)
    pad = jnp.zeros((pos.shape[0], LANES - MLA_ROPE), F32)
    return (jnp.concatenate([cos, cos, pad], axis=1),
            jnp.concatenate([-sin, z, pad], axis=1),
            jnp.concatenate([z, sin, pad], axis=1))


def _pad_keys(t, sk_pad):
    return jnp.pad(t, ((0, 0), (0, sk_pad - t.shape[1])) + ((0, 0),) * (t.ndim - 2))


def _layer(x, bsz, s, w, tabs, past, final_gain):
    m = bsz * s
    kv_lora = w["kv_norm"].shape[0]
    conv_dim = w["ssd"]["conv_w"].shape[1]
    x, xn = _ffn(x, w["ffn1_norm"], w["ffn1_g"], w["ffn1_u"], w["ffn1_d"], post="norm_bf16", post_gain=w["mix_norm"])

    u_q, u_ckv, u_small, u_z, u_xbc, fq, fk, fk_b, fv, fv_b = _inproj(
        xn, w["in_cat"], w["in_widths"], ((F32,),) * 5 + ((BF16,), (F32, BF16), (F32, BF16)), tn=INPROJ_TN)

    ff_lo = MLA_ROPE + SSM_HEADS
    ckv_new, small2 = _prep(u_ckv, w["kv_norm"], u_small, tabs, w["fb_lanes"], ff_lo=ff_lo, ff_hi=ff_lo + FOX_HEADS)
    kpe_new = small2[:, :MLA_ROPE]
    logf_new = small2[:, ff_lo:ff_lo + FOX_HEADS]
    u_dt = u_small[:, MLA_ROPE:ff_lo]

    if past is not None:
        caches, l = past
        past_len = caches["mla_ckv"].shape[2]
    ukv = functools.partial(_mm, w=w["ukv"], out_dtype=BF16, prologue="cast", tn=2048, name="mla_ukv")

    q_full = _mm(u_q, w["uq"], out_dtype=BF16, prologue="rms", gain=w["q_norm"], rope_tabs=tabs, tn=2048, name="mla_q")
    q_full = q_full.reshape(bsz, s, -1)
    kv_new = ukv(ckv_new).reshape(bsz, s, -1)
    mla = dict(heads=MLA_HEADS, dq=MLA_QK_PAD, dk=MLA_NOPE, dv=MLA_V, scale=MLA_SCALE, mode="chunk")
    sk_pad = -(-s // LANES) * LANES
    if past is None:
        kpe_pad = jnp.pad(kpe_new.astype(BF16).reshape(bsz, s, -1),
                          ((0, 0), (0, sk_pad - s), (0, MLA_QK_PAD - MLA_NOPE - MLA_ROPE)))
        kv_all = _pad_keys(kv_new, sk_pad)
        o_mla = _attention(q_full, kv_all, kv_all, kpe_pad, None, k_col=0, v_col=1, q_off=0, n_valid=s, **mla)
    else:
        kv_past = ukv(caches["mla_ckv"][l].reshape(bsz * past_len, kv_lora)).reshape(bsz, past_len, -1)
        o_mla = _decode_attention(
            q_full, ((kv_past, (), 0), (kv_past, (), 1)), ((kv_new, (), 0), (kv_new, (), 1)),
            ((caches["mla_kpe"], (l,)), (kpe_new.reshape(bsz, s, MLA_ROPE), ())), None, **mla)

    if past is None:
        conv_state = jnp.zeros((bsz, SSM_CONV_W - 1, conv_dim), F32)
        h0 = jnp.zeros((bsz, SSM_STATE, SSM_D_INNER), F32)
    else:
        conv_state = caches["conv"][l].astype(F32)
        h0 = jnp.transpose(caches["ssm"][l].astype(F32), (0, 3, 1, 2)).reshape(bsz, SSM_STATE, SSM_D_INNER)
    c0 = jnp.pad(conv_state, ((0, 0), (SUBLANES - (SSM_CONV_W - 1), 0), (0, 0)))
    xbc3 = u_xbc.reshape(bsz, s, conv_dim)
    dt3 = u_dt.reshape(bsz, s, SSM_HEADS)
    o_ssd, h_new = _ssd(u_z.reshape(bsz, s, SSM_D_INNER), xbc3, dt3, jnp.swapaxes(dt3, 1, 2), w["ssd"], h0, c0)
    ssm_new = jnp.transpose(h_new.reshape(bsz, SSM_STATE, SSM_HEADS, SSM_HEAD_DIM), (0, 2, 3, 1))
    keep = SSM_CONV_W - 1
    conv_new = xbc3[:, s - keep:] if s >= keep else jnp.concatenate([conv_state, xbc3], axis=1)[:, -keep:]

    hw = FOX_HEADS * FOX_HEAD_DIM
    fox = dict(heads=FOX_HEADS, dq=FOX_HEAD_DIM, dk=FOX_HEAD_DIM, dv=FOX_HEAD_DIM, scale=FOX_SCALE, mode="causal")
    logf_all = logf_new.reshape(bsz, s, FOX_HEADS)
    if past is not None:
        logf_all = jnp.concatenate([caches["fox_logf"][l].astype(F32), logf_all], axis=1)
    n_keys = logf_all.shape[1]
    neg_cum = _cumsum_last(jnp.swapaxes(_pad_keys(logf_all, -(-n_keys // LANES) * LANES), 1, 2), -1.0 / FOX_SCALE)
    fq3 = fq.reshape(bsz, s, hw)
    if past is None:
        o_fox = _attention(fq3, _pad_keys(fk_b.reshape(bsz, s, hw), sk_pad), _pad_keys(fv_b.reshape(bsz, s, hw), sk_pad),
                           None, neg_cum, q_off=0, n_valid=s, **fox)
    else:
        o_fox = _decode_attention(
            fq3, ((caches["fox_k"], (l,), 0), (caches["fox_v"], (l,), 0)),
            ((fk.reshape(bsz, s, hw), (), 0), (fv.reshape(bsz, s, hw), (), 0)), None,
            (neg_cum[:, :, :past_len], neg_cum[:, :, past_len:n_keys]), **fox)

    merged = _merge(xn, [o_mla.reshape(m, -1), o_ssd.reshape(m, -1), o_fox.reshape(m, -1)],
                    [w["w_br_mla"], w["w_br_ssd"], w["w_br_fox"]], w["in_gate"])
    x = _mm(merged, w["w_out"], out_dtype=F32, residual=x, tm=512, tn=2048, name="out_proj")
    (x,) = _ffn(x, w["ffn2_norm"], w["ffn2_g"], w["ffn2_u"], w["ffn2_d"],
                post=None if final_gain is None else "norm_only", post_gain=final_gain)
    state = (ckv_new.reshape(bsz, s, kv_lora), kpe_new.reshape(bsz, s, MLA_ROPE),
             fk.reshape(bsz, s, FOX_HEADS, FOX_HEAD_DIM), fv.reshape(bsz, s, FOX_HEADS, FOX_HEAD_DIM),
             logf_new.reshape(bsz, s, FOX_HEADS), ssm_new, conv_new)
    return x, state


def kernel(x_prompt, x_sample, cache_mla_ckv, cache_mla_kpe, cache_fox_k, cache_fox_v, cache_fox_logf, state_ssm,
           state_conv, ffn1_norm, ffn1_w_gate, ffn1_w_up, ffn1_w_down, mix_norm, w_in, mla_q_norm, mla_w_uq,
           mla_kv_norm, mla_w_ukv, ssm_conv_w, ssm_conv_b, ssm_dt_bias, ssm_a_log, ssm_d, ssm_norm, fox_b_f,
           w_br_mla, w_br_ssd, w_br_fox, w_out, ffn2_norm, ffn2_w_gate, ffn2_w_up, ffn2_w_down, final_norm):
    a = dict(ffn1_norm=ffn1_norm, ffn1_w_gate=ffn1_w_gate, ffn1_w_up=ffn1_w_up, ffn1_w_down=ffn1_w_down,
             mix_norm=mix_norm, w_in=w_in, mla_q_norm=mla_q_norm, mla_w_uq=mla_w_uq, mla_kv_norm=mla_kv_norm,
             mla_w_ukv=mla_w_ukv, ssm_conv_w=ssm_conv_w, ssm_conv_b=ssm_conv_b, ssm_dt_bias=ssm_dt_bias,
             ssm_a_log=ssm_a_log, ssm_d=ssm_d, ssm_norm=ssm_norm, fox_b_f=fox_b_f, w_br_mla=w_br_mla,
             w_br_ssd=w_br_ssd, w_br_fox=w_br_fox, w_out=w_out, ffn2_norm=ffn2_norm, ffn2_w_gate=ffn2_w_gate,
             ffn2_w_up=ffn2_w_up, ffn2_w_down=ffn2_w_down)
    depth = w_in.shape[0]
    stacked = _stacked_weights(a)
    bp, sp, d_model = x_prompt.shape
    bs, ss, _ = x_sample.shape
    past_len = cache_mla_ckv.shape[2]
    tabs_p = tuple(jnp.tile(t, (bp, 1)) for t in _rope_tables(jnp.arange(sp, dtype=jnp.int32)))
    tabs_s = tuple(jnp.tile(t, (bs, 1)) for t in _rope_tables(past_len + jnp.arange(ss, dtype=jnp.int32)))
    hp = x_prompt.reshape(bp * sp, d_model).astype(F32)
    hs = x_sample.reshape(bs * ss, d_model).astype(F32)
    caches = {"mla_ckv": cache_mla_ckv, "mla_kpe": cache_mla_kpe, "fox_k": cache_fox_k, "fox_v": cache_fox_v,
              "fox_logf": cache_fox_logf, "ssm": state_ssm, "conv": state_conv}
    new_p, new_s = [], []
    for l in range(depth):
        w = _layer_weights(l, a, stacked)
        fg = final_norm if l == depth - 1 else None
        hp, st_p = _layer(hp, bp, sp, w, tabs_p, None, fg)
        hs, st_s = _layer(hs, bs, ss, w, tabs_s, (caches, l), fg)
        new_p.append(st_p)
        new_s.append(st_s)
    y_prompt = hp.reshape(bp, sp, d_model)
    y_sample = hs.reshape(bs, ss, d_model)
    stk = lambda states, i: jnp.stack([st[i] for st in states], axis=0)
    return (y_prompt, y_sample) + tuple(stk(new_p, i) for i in range(7)) + tuple(stk(new_s, i) for i in range(7))
```

```python
import functools
import math

import jax
import jax.numpy as jnp
from jax import lax
from jax.experimental import pallas as pl
from jax.experimental.pallas import tpu as pltpu

F32 = jnp.float32
BF16 = jnp.bfloat16

EPS = 1e-6
CHUNK = 64
FFN_RES = 0.5
MLA_HEADS, MLA_NOPE, MLA_ROPE, MLA_V = 8, 128, 64, 128
MLA_SCALE = (MLA_NOPE + MLA_ROPE) ** -0.5
ROPE_BASE = 10000.0
SSM_HEADS, SSM_HEAD_DIM, SSM_GROUPS, SSM_STATE, SSM_CONV_W = 16, 64, 2, 128, 4
SSM_D_INNER = SSM_HEADS * SSM_HEAD_DIM
FOX_HEADS, FOX_HEAD_DIM = 8, 128
FOX_SCALE = FOX_HEAD_DIM ** -0.5
N_BRANCH = 3

LANES = 128
SUBLANES = 8
MXU_DIM = 256
VMEM_LIMIT = 56 * 1024 * 1024

MLA_QK_PAD = MXU_DIM
INPROJ_TN = 512
NEG_BIG = -1e30
LOG2E = math.log2(math.e)
HI = lax.Precision.HIGHEST


def _tile(n, pref, align):
    t = (min(pref, n) // align) * align
    while t >= align:
        if n % t == 0:
            return t
        t -= align
    return n


def _params(*sem):
    return pltpu.CompilerParams(dimension_semantics=sem, vmem_limit_bytes=VMEM_LIMIT)


def _wshape(w):
    return w[0].shape[1:] if isinstance(w, tuple) else w.shape


def _wspec(w, block, index):
    if isinstance(w, tuple):
        arr, layer = w
        return arr, pl.BlockSpec((None,) + tuple(block), lambda *g: (layer,) + tuple(index(*g)))
    return w, pl.BlockSpec(tuple(block), index)


def _rms(x, g):
    return x * lax.rsqrt(jnp.mean(x * x, axis=-1, keepdims=True) + EPS) * g


def _softplus(x):
    return jnp.maximum(x, 0.0) + jnp.log1p(jnp.exp(-jnp.abs(x)))


def _rope_lanes(pe, cos, s1, s2):
    half = MLA_ROPE // 2
    return pe * cos + pltpu.roll(pe, LANES - half, 1) * s1 + pltpu.roll(pe, half, 1) * s2


def _mm_kernel(*refs, prologue, rope, residual, tn):
    it = iter(refs)
    x_ref = next(it)
    g_ref = next(it) if prologue == "rms" else None
    w_ref = next(it)
    res_ref = next(it) if residual else None
    tabs = (next(it), next(it), next(it)) if rope else None
    o_ref = next(it)
    xn_ref = next(it) if prologue != "none" else None

    if prologue == "none":
        lhs = x_ref[...]
    else:
        @pl.when(pl.program_id(1) == 0)
        def _():
            x = x_ref[...].astype(F32)
            if prologue == "rms":
                x = _rms(x, g_ref[...])
            xn_ref[...] = x.astype(BF16)
        lhs = xn_ref[...]
    acc = jnp.dot(lhs, w_ref[...], preferred_element_type=F32)
    if residual:
        acc = res_ref[...] + acc
    if rope:
        cos, s1, s2 = (t[...] for t in tabs)
        for c in range(tn // MLA_QK_PAD):
            a = c * MLA_QK_PAD
            o_ref[:, a:a + LANES] = acc[:, a:a + LANES].astype(o_ref.dtype)
            o_ref[:, a + LANES:a + 2 * LANES] = _rope_lanes(acc[:, a + LANES:a + 2 * LANES], cos, s1, s2).astype(o_ref.dtype)
    else:
        o_ref[...] = acc.astype(o_ref.dtype)


def _mm(x, w, *, out_dtype, tm=1024, tn=512, prologue="none", gain=None, residual=None, rope_tabs=None, name="mm"):
    m = x.shape[0]
    k, n = _wshape(w)
    assert x.shape[1] == k and (prologue != "none" or x.dtype == BF16)
    tm = _tile(m, tm, SUBLANES)
    tn = _tile(n, tn, MLA_QK_PAD if rope_tabs is not None else LANES)
    grid = (m // tm, n // tn)
    in_specs = [pl.BlockSpec((tm, k), lambda i, j: (i, 0))]
    args = [x]
    if prologue == "rms":
        in_specs.append(pl.BlockSpec((1, k), lambda i, j: (0, 0)))
        args.append(gain.reshape(1, k).astype(F32))
    w_arr, w_spec = _wspec(w, (k, tn), lambda i, j: (0, j))
    in_specs.append(w_spec)
    args.append(w_arr)
    if residual is not None:
        in_specs.append(pl.BlockSpec((tm, tn), lambda i, j: (i, j)))
        args.append(residual)
    if rope_tabs is not None:
        for t in rope_tabs:
            in_specs.append(pl.BlockSpec((tm, LANES), lambda i, j: (i, 0)))
            args.append(t)
    scratch = [pltpu.VMEM((tm, k), BF16)] if prologue != "none" else []
    kern = functools.partial(_mm_kernel, prologue=prologue, rope=rope_tabs is not None,
                             residual=residual is not None, tn=tn)
    return pl.pallas_call(
        kern, grid=grid, in_specs=in_specs,
        out_specs=pl.BlockSpec((tm, tn), lambda i, j: (i, j)),
        out_shape=jax.ShapeDtypeStruct((m, n), out_dtype),
        scratch_shapes=scratch, compiler_params=_params("parallel", "arbitrary"), name=name,
    )(*args)


def _inproj_kernel(xn_ref, w_ref, *outs, groups):
    j = pl.program_id(1)
    outs = iter(outs)
    for lo, hi, width, copies in groups:
        o_refs = [next(outs) for _ in range(copies)]

        @pl.when((j >= lo) & (j < hi))
        def _(o_refs=o_refs, width=width):
            acc = jnp.dot(xn_ref[...], w_ref[:, :width], preferred_element_type=F32)
            for o_ref in o_refs:
                o_ref[...] = acc.astype(o_ref.dtype)


def _inproj_layout(widths, dtypes, tn):
    groups, start = [], 0
    for n, dts in zip(widths, dtypes):
        nt = -(-n // tn)
        assert n % tn == 0 or n < tn
        groups.append((start, start + nt, min(n, tn), len(dts)))
        start += nt
    return groups, start


def _inproj(xn, w_cat, widths, dtypes, *, tm=1024, tn=512):
    m, d = xn.shape
    tm = _tile(m, tm, SUBLANES)
    groups, n_tiles = _inproj_layout(widths, dtypes, tn)
    assert _wshape(w_cat) == (d, n_tiles * tn)
    w_arr, w_spec = _wspec(w_cat, (d, tn), lambda i, j: (0, j))
    out_specs, out_shape = [], []
    for (lo, hi, bw, _), n, dts in zip(groups, widths, dtypes):
        for dt in dts:
            out_specs.append(pl.BlockSpec(
                (tm, bw), functools.partial(lambda i, j, lo, hi: (i, jnp.clip(j - lo, 0, hi - lo - 1)), lo=lo, hi=hi)))
            out_shape.append(jax.ShapeDtypeStruct((m, n), dt))
    return pl.pallas_call(
        functools.partial(_inproj_kernel, groups=groups), grid=(m // tm, n_tiles),
        in_specs=[pl.BlockSpec((tm, d), lambda i, j: (i, 0)), w_spec],
        out_specs=out_specs, out_shape=out_shape,
        compiler_params=_params("parallel", "arbitrary"), name="inproj",
    )(xn, w_arr)


def _ffn_kernel(*refs, nf, tf, f, post):
    x_ref, g_ref, wg_ref, wu_ref, wd_ref = refs[:5]
    refs = refs[5:]
    pg_ref = None
    if post is not None:
        pg_ref, refs = refs[0], refs[1:]
    outs, (xn_ref, acc_ref) = refs[:-2], refs[-2:]
    j = pl.program_id(1)

    def step(first, last):
        if first:
            xn = _rms(x_ref[...], g_ref[...]).astype(BF16)
            xn_ref[...] = xn
        else:
            xn = xn_ref[...]
        a = jnp.dot(xn, wg_ref[...], preferred_element_type=F32)
        b = jnp.dot(xn, wu_ref[...], preferred_element_type=F32)
        h = a * jax.nn.sigmoid(a) * b
        wd = wd_ref[...]
        if last and f % tf:
            valid = f - (nf - 1) * tf
            h = jnp.where(lax.broadcasted_iota(jnp.int32, h.shape, 1) < valid, h, 0.0)
            wd = jnp.where(lax.broadcasted_iota(jnp.int32, wd.shape, 0) < valid, wd, jnp.zeros_like(wd))
        d = jnp.dot(h.astype(BF16), wd, preferred_element_type=F32)
        acc = d if first else acc_ref[...] + d
        if not last:
            acc_ref[...] = acc
            return
        y = x_ref[...] + FFN_RES * acc
        if post is None:
            outs[0][...] = y
        elif post == "norm_bf16":
            outs[0][...] = y
            outs[1][...] = _rms(y, pg_ref[...]).astype(BF16)
        else:
            outs[0][...] = _rms(y, pg_ref[...])

    if nf == 1:
        step(True, True)
    else:
        pl.when(j == 0)(lambda: step(True, False))
        pl.when((j > 0) & (j < nf - 1))(lambda: step(False, False))
        pl.when(j == nf - 1)(lambda: step(False, True))


def _ffn(x, gain, wg, wu, wd, *, post=None, post_gain=None, tm=512, tf=512):
    m, d = x.shape
    f = _wshape(wg)[1]
    tm = _tile(m, tm, SUBLANES)
    nf = pl.cdiv(f, tf)
    row = pl.BlockSpec((tm, d), lambda i, j: (i, 0))
    vec = pl.BlockSpec((1, d), lambda i, j: (0, 0))
    wg_arr, wg_spec = _wspec(wg, (d, tf), lambda i, j: (0, j))
    wu_arr, wu_spec = _wspec(wu, (d, tf), lambda i, j: (0, j))
    wd_arr, wd_spec = _wspec(wd, (tf, d), lambda i, j: (j, 0))
    in_specs = [row, vec, wg_spec, wu_spec, wd_spec]
    args = [x, gain.reshape(1, d).astype(F32), wg_arr, wu_arr, wd_arr]
    out_specs, out_shape = [row], [jax.ShapeDtypeStruct((m, d), F32)]
    if post is not None:
        in_specs.append(vec)
        args.append(post_gain.reshape(1, d).astype(F32))
    if post == "norm_bf16":
        out_specs.append(row)
        out_shape.append(jax.ShapeDtypeStruct((m, d), BF16))
    return pl.pallas_call(
        functools.partial(_ffn_kernel, nf=nf, tf=tf, f=f, post=post), grid=(m // tm, nf),
        in_specs=in_specs, out_specs=out_specs, out_shape=out_shape,
        scratch_shapes=[pltpu.VMEM((tm, d), BF16), pltpu.VMEM((tm, d), F32)],
        compiler_params=_params("parallel", "arbitrary"), name="ffn",
    )(*args)


def _prep_kernel(uc_ref, g_ref, us_ref, cos_ref, s1_ref, s2_ref, fb_ref, ckv_ref, sm_ref, *, ff_lo, ff_hi):
    ckv_ref[...] = _rms(uc_ref[...], g_ref[...])
    us = us_ref[...]
    lane = lax.broadcasted_iota(jnp.int32, us.shape, 1)
    pe = jnp.where(lane < MLA_ROPE, us, 0.0)
    rot = _rope_lanes(pe, cos_ref[...], s1_ref[...], s2_ref[...])
    logf = -_softplus(-(us + fb_ref[...]))
    sm_ref[...] = jnp.where((lane >= ff_lo) & (lane < ff_hi), logf, rot)


def _prep(u_ckv, kv_gain, u_small, tabs, fb_lanes, *, ff_lo, ff_hi, tm=1024):
    m = u_small.shape[0]
    kv = kv_gain.shape[0]
    tm = _tile(m, tm, SUBLANES)
    row = lambda i: (i, 0)
    return pl.pallas_call(
        functools.partial(_prep_kernel, ff_lo=ff_lo, ff_hi=ff_hi), grid=(m // tm,),
        in_specs=[pl.BlockSpec((tm, kv), row),
                  pl.BlockSpec((1, kv), lambda i: (0, 0)),
                  pl.BlockSpec((tm, LANES), row), pl.BlockSpec((tm, LANES), row),
                  pl.BlockSpec((tm, LANES), row), pl.BlockSpec((tm, LANES), row),
                  pl.BlockSpec((1, LANES), lambda i: (0, 0))],
        out_specs=[pl.BlockSpec((tm, kv), row), pl.BlockSpec((tm, LANES), row)],
        out_shape=[jax.ShapeDtypeStruct((m, kv), F32), jax.ShapeDtypeStruct((m, LANES), F32)],
        compiler_params=_params("parallel"), name="prep",
    )(u_ckv, kv_gain.reshape(1, kv).astype(F32), u_small, *tabs, fb_lanes)


def _cumsum_kernel(x_ref, o_ref, carry_ref, *, tc, mult):
    @pl.when(pl.program_id(1) == 0)
    def _():
        carry_ref[...] = jnp.zeros_like(carry_ref)

    r = lax.broadcasted_iota(jnp.int32, (tc, tc), 0)
    c = lax.broadcasted_iota(jnp.int32, (tc, tc), 1)
    upper = (r <= c).astype(F32)
    y = jnp.dot(x_ref[0], upper, preferred_element_type=F32, precision=HI) + carry_ref[:, :1]
    o_ref[0] = y * mult
    carry_ref[...] = jnp.broadcast_to(y[:, tc - 1:tc], carry_ref.shape)


def _cumsum_last(x, mult, *, tc=512):
    b, h, s = x.shape
    tc = _tile(s, tc, LANES)
    return pl.pallas_call(
        functools.partial(_cumsum_kernel, tc=tc, mult=mult), grid=(b, s // tc),
        in_specs=[pl.BlockSpec((1, h, tc), lambda i, j: (i, 0, j))],
        out_specs=pl.BlockSpec((1, h, tc), lambda i, j: (i, 0, j)),
        out_shape=jax.ShapeDtypeStruct((b, h, s), F32),
        scratch_shapes=[pltpu.VMEM((h, LANES), F32)],
        compiler_params=_params("parallel", "arbitrary"), name="cumsum",
    )(x)


def _last_visible(q_end, mode):
    if mode == "chunk":
        return (q_end // CHUNK) * CHUNK + (CHUNK - 1)
    return q_end


def _attn_kernel(*refs, heads, dq, dk, dv, tq, tk, nk, scale, mode, q_off, n_valid, has_bias, has_shared):
    it = iter(refs)
    qi_ref, ki_ref = next(it), next(it)
    q_ref, k_ref, v_ref = next(it), next(it), next(it)
    ks_ref = next(it) if has_shared else None
    b_ref = next(it) if has_bias else None
    o_ref, m_ref, acc_ref = next(it), next(it), next(it)
    t = pl.program_id(1)
    qi, ki = qi_ref[t], ki_ref[t]
    nch = tk // LANES
    c = scale * LOG2E
    aw = dv + LANES

    @pl.when(ki == 0)
    def _():
        m_ref[...] = jnp.full_like(m_ref, NEG_BIG)
        acc_ref[...] = jnp.zeros_like(acc_ref)

    q_lo = q_off + qi * tq
    k_lo = ki * tk
    last_tile = jnp.minimum(_last_visible(q_lo + (tq - 1), mode) // tk, nk - 1)
    first_maskable = _last_visible(q_lo, mode) + 1
    ones_col = (lax.broadcasted_iota(jnp.int32, (tk, LANES), 1) == 0).astype(BF16)

    def body(masked):
        if masked:
            qpos = q_lo + lax.broadcasted_iota(jnp.int32, (tq, tk), 0)
            kpos = k_lo + lax.broadcasted_iota(jnp.int32, (tq, tk), 1)
            if mode == "chunk":
                sh = CHUNK.bit_length() - 1
                vis = lax.shift_right_logical(kpos, sh) <= lax.shift_right_logical(qpos, sh)
            else:
                vis = kpos <= qpos
            vis = vis & (kpos < n_valid)
        for h in range(heads):
            q = q_ref[0, :, h * dq:(h + 1) * dq]
            k = k_ref[0, :, h * dk:(h + 1) * dk]
            if has_shared:
                k = jnp.concatenate([k, ks_ref[0]], axis=1)
            s = lax.dot_general(q, k, (((1,), (1,)), ((), ())), preferred_element_type=F32)
            if has_bias:
                s = s + b_ref[0, h:h + 1, :]
            if masked:
                s = jnp.where(vis, s, NEG_BIG)
            m_prev = m_ref[h]
            mc = s[:, :LANES]
            for j in range(1, nch):
                mc = jnp.maximum(mc, s[:, j * LANES:(j + 1) * LANES])
            m_new = jnp.maximum(m_prev, jnp.max(mc, axis=1, keepdims=True))
            m_ref[h] = m_new
            alpha = jnp.exp2((m_prev - m_new) * c)
            p = jnp.concatenate([jnp.exp2((s[:, j * LANES:(j + 1) * LANES] - m_new) * c).astype(BF16)
                                 for j in range(nch)], axis=1)
            vx = jnp.concatenate([v_ref[0, :, h * dv:(h + 1) * dv], ones_col], axis=1)
            pv = jnp.dot(p, vx, preferred_element_type=F32)
            for a0 in range(h * aw, (h + 1) * aw, LANES):
                acc_ref[:, a0:a0 + LANES] = alpha * acc_ref[:, a0:a0 + LANES] + pv[:, a0 - h * aw:a0 - h * aw + LANES]

    need_mask = (k_lo + (tk - 1) >= first_maskable) | (k_lo + tk > n_valid)

    @pl.when(need_mask)
    def _():
        body(True)

    @pl.when(jnp.logical_not(need_mask))
    def _():
        body(False)

    @pl.when(ki == last_tile)
    def _():
        for h in range(heads):
            l = acc_ref[:, h * aw + dv:h * aw + dv + 1]
            o_ref[0, :, h * dv:(h + 1) * dv] = (acc_ref[:, h * aw:h * aw + dv] / l).astype(o_ref.dtype)


def _attention(q, k, v, k_shared, bias, *, heads, dq, dk, dv, scale, mode, q_off, n_valid, k_col=0, v_col=0,
               tq=1024, tk=1024):
    b, sq, _ = q.shape
    sk = k.shape[1]
    assert dk + (0 if k_shared is None else k_shared.shape[2]) == dq
    tq = _tile(sq, tq, SUBLANES)
    tk = sk if sk <= 2 * tk else _tile(sk, tk, LANES)
    nq, nk = sq // tq, sk // tk
    pairs = [(i, j) for i in range(nq)
             for j in range(min(_last_visible(q_off + i * tq + (tq - 1), mode) // tk, nk - 1) + 1)]
    qi_arr = jnp.asarray([p[0] for p in pairs], jnp.int32)
    ki_arr = jnp.asarray([p[1] for p in pairs], jnp.int32)

    in_specs = [pl.BlockSpec((1, tq, heads * dq), lambda bi, t, qi, ki: (bi, qi[t], 0)),
                pl.BlockSpec((1, tk, heads * dk), lambda bi, t, qi, ki: (bi, ki[t], k_col)),
                pl.BlockSpec((1, tk, heads * dv), lambda bi, t, qi, ki: (bi, ki[t], v_col))]
    args = [q, k, v]
    if k_shared is not None:
        in_specs.append(pl.BlockSpec((1, tk, dq - dk), lambda bi, t, qi, ki: (bi, ki[t], 0)))
        args.append(k_shared)
    if bias is not None:
        in_specs.append(pl.BlockSpec((1, heads, tk), lambda bi, t, qi, ki: (bi, 0, ki[t])))
        args.append(bias)
    kern = functools.partial(_attn_kernel, heads=heads, dq=dq, dk=dk, dv=dv, tq=tq, tk=tk, nk=nk, scale=scale,
                             mode=mode, q_off=q_off, n_valid=n_valid, has_bias=bias is not None,
                             has_shared=k_shared is not None)
    return pl.pallas_call(
        kern,
        grid_spec=pltpu.PrefetchScalarGridSpec(
            num_scalar_prefetch=2, grid=(b, len(pairs)), in_specs=in_specs,
            out_specs=pl.BlockSpec((1, tq, heads * dv), lambda bi, t, qi, ki: (bi, qi[t], 0)),
            scratch_shapes=[pltpu.VMEM((heads, tq, LANES), F32), pltpu.VMEM((tq, heads * (dv + LANES)), F32)]),
        out_shape=jax.ShapeDtypeStruct((b, sq, heads * dv), BF16),
        compiler_params=_params("parallel", "arbitrary"), name="attn_" + mode,
    )(qi_arr, ki_arr, *args)


def _decode_kernel(*refs, heads, dq, dk, dv, sq, p_len, scale, mode, has_shared, has_bias):
    it = iter(refs)
    q_ref, kp_ref, vp_ref, kn_ref, vn_ref = (next(it) for _ in range(5))
    ksp_ref, ksn_ref = (next(it), next(it)) if has_shared else (None, None)
    bp_ref, bn_ref = (next(it), next(it)) if has_bias else (None, None)
    o_ref = next(it)
    c = scale * LOG2E

    def head(ref, h, d):
        if ref.shape[1] == d and heads > 1:
            x = ref[pl.ds(h, ref.shape[0] // heads, stride=heads), :]
        else:
            x = ref[:, h * d:(h + 1) * d]
        return x.astype(BF16)

    row = lax.broadcasted_iota(jnp.int32, (sq, sq), 0)
    col = lax.broadcasted_iota(jnp.int32, (sq, sq), 1)
    if mode == "chunk":
        sh = CHUNK.bit_length() - 1
        vis = lax.shift_right_logical(p_len + col, sh) <= lax.shift_right_logical(p_len + row, sh)
    else:
        vis = col <= row
    if has_shared:
        pad = dq - dk - ksp_ref.shape[1]
        ksp = jnp.concatenate([ksp_ref[...].astype(BF16), jnp.zeros((p_len, pad), BF16)], axis=1)
        ksn = jnp.concatenate([ksn_ref[...].astype(BF16), jnp.zeros((sq, pad), BF16)], axis=1)
    contract_last = (((1,), (1,)), ((), ()))
    for h in range(heads):
        q = q_ref[:, h * dq:(h + 1) * dq]
        kp, kn = head(kp_ref, h, dk), head(kn_ref, h, dk)
        if has_shared:
            kp = jnp.concatenate([kp, ksp], axis=1)
            kn = jnp.concatenate([kn, ksn], axis=1)
        s_p = lax.dot_general(q, kp, contract_last, preferred_element_type=F32)
        s_n = lax.dot_general(q, kn, contract_last, preferred_element_type=F32)
        if has_bias:
            s_p = s_p + bp_ref[h:h + 1, :]
            s_n = s_n + bn_ref[h:h + 1, :]
        s_n = jnp.where(vis, s_n, NEG_BIG)
        m = jnp.maximum(jnp.max(s_p, axis=1, keepdims=True), jnp.max(s_n, axis=1, keepdims=True))
        p_p = jnp.exp2((s_p - m) * c)
        p_n = jnp.exp2((s_n - m) * c)
        l = jnp.sum(p_p, axis=1, keepdims=True) + jnp.sum(p_n, axis=1, keepdims=True)
        pv = (jnp.dot(p_p.astype(BF16), head(vp_ref, h, dv), preferred_element_type=F32)
              + jnp.dot(p_n.astype(BF16), head(vn_ref, h, dv), preferred_element_type=F32))
        o_ref[:, h * dv:(h + 1) * dv] = (pv / l).astype(o_ref.dtype)


def _decode_attention(q, past_kv, new_kv, shared, bias, *, heads, dq, dk, dv, scale, mode):
    b, sq, _ = q.shape

    def spec(arr, prefix, colblk, width):
        lead, inner = arr.shape[:len(prefix) + 1], arr.shape[len(prefix) + 1:]
        if len(inner) == 3:
            arr = arr.reshape(lead + (inner[0] * inner[1], inner[2]))
            block, idx = arr.shape[-2:], (0, 0)
        else:
            block, idx = (inner[0], width), (0, colblk)
        return arr, pl.BlockSpec((None,) * len(lead) + tuple(block), lambda i: tuple(prefix) + (i,) + idx)

    (kp, kp_pre, kp_col), (vp, vp_pre, vp_col) = past_kv
    (kn, kn_pre, kn_col), (vn, vn_pre, vn_col) = new_kv
    p_len = kp.shape[len(kp_pre) + 1]
    in_specs, args = [pl.BlockSpec((None, sq, heads * dq), lambda i: (i, 0, 0))], [q]
    for arr, pre, col, width in ((kp, kp_pre, kp_col, heads * dk), (vp, vp_pre, vp_col, heads * dv),
                                 (kn, kn_pre, kn_col, heads * dk), (vn, vn_pre, vn_col, heads * dv)):
        arr, sp = spec(arr, pre, col, width)
        in_specs.append(sp)
        args.append(arr)
    if shared is not None:
        for arr, pre in shared:
            arr, sp = spec(arr, pre, 0, arr.shape[-1])
            in_specs.append(sp)
            args.append(arr)
    if bias is not None:
        for arr in bias:
            in_specs.append(pl.BlockSpec((None,) + arr.shape[1:], lambda i: (i, 0, 0)))
            args.append(arr)
    kern = functools.partial(_decode_kernel, heads=heads, dq=dq, dk=dk, dv=dv, sq=sq, p_len=p_len, scale=scale,
                             mode=mode, has_shared=shared is not None, has_bias=bias is not None)
    return pl.pallas_call(
        kern, grid=(b,), in_specs=in_specs,
        out_specs=pl.BlockSpec((None, sq, heads * dv), lambda i: (i, 0, 0)),
        out_shape=jax.ShapeDtypeStruct((b, sq, heads * dv), BF16),
        compiler_params=_params("parallel"), name="decode_" + mode,
    )(*args)


def _ssd_kernel(z_ref, xbc_ref, dt_ref, dtt_ref, cw_ref, cb_ref, dtb_ref, dtbt_ref, al_ref, alt_ref, dx_ref, nw_ref,
                h0_ref, c0_ref, y_ref, hout_ref, state_ref, carry_ref, *, lc, nc):
    c = pl.program_id(1)
    gw = SSM_D_INNER // SSM_GROUPS
    hpg = SSM_HEADS // SSM_GROUPS
    halo = SUBLANES

    @pl.when(c == 0)
    def _():
        state_ref[...] = h0_ref[0]
        carry_ref[...] = c0_ref[0]

    x = xbc_ref[0]
    cat = jnp.concatenate([carry_ref[...], x], axis=0)
    conv = cb_ref[...]
    for kk in range(SSM_CONV_W):
        shift = SSM_CONV_W - 1 - kk
        src = pltpu.roll(cat, shift, 0) if shift else cat
        conv = conv + src[halo:, :] * cw_ref[kk:kk + 1, :]
    carry_ref[...] = x[lc - halo:, :]
    act = conv * jax.nn.sigmoid(conv)
    xs = act[:, :SSM_D_INNER]
    bm = act[:, SSM_D_INNER:SSM_D_INNER + SSM_GROUPS * SSM_STATE]
    cm = act[:, SSM_D_INNER + SSM_GROUPS * SSM_STATE:]

    dt = _softplus(dt_ref[0] + dtb_ref[...])
    dtt = _softplus(dtt_ref[0] + dtbt_ref[...])
    adt = dt * (-jnp.exp(al_ref[...]))
    adtt = dtt * (-jnp.exp(alt_ref[...]))
    r = lax.broadcasted_iota(jnp.int32, (lc, lc), 0)
    cc = lax.broadcasted_iota(jnp.int32, (lc, lc), 1)
    tril = cc <= r
    acs = jnp.dot(tril.astype(F32), adt, preferred_element_type=F32, precision=HI)
    acst = jnp.dot(adtt, (r <= cc).astype(F32), preferred_element_type=F32, precision=HI)
    hh = lax.broadcasted_iota(jnp.int32, (SSM_HEADS, SSM_D_INNER), 0)
    ll = lax.broadcasted_iota(jnp.int32, (SSM_HEADS, SSM_D_INNER), 1)
    expand = ((ll >= hh * SSM_HEAD_DIM) & (ll < (hh + 1) * SSM_HEAD_DIM)).astype(F32)
    dt_x = jnp.dot(dt, expand, preferred_element_type=F32, precision=HI)
    acs_x = jnp.dot(acs, expand, preferred_element_type=F32, precision=HI)
    tot_x = acs_x[lc - 1:lc, :]
    xdt = xs * dt_x
    xdt_b = xdt.astype(BF16)
    w_end = (xdt * jnp.exp(tot_x - acs_x)).astype(BF16)
    state = state_ref[...]
    state_b = state.astype(BF16)

    y_parts, new_parts = [], []
    for g in range(SSM_GROUPS):
        bg = bm[:, g * SSM_STATE:(g + 1) * SSM_STATE].astype(BF16)
        cg = cm[:, g * SSM_STATE:(g + 1) * SSM_STATE].astype(BF16)
        cb = lax.dot_general(cg, bg, (((1,), (1,)), ((), ())), preferred_element_type=F32)
        for hl in range(hpg):
            h = g * hpg + hl
            seg = acs[:, h:h + 1] - acst[h:h + 1, :]
            mh = (cb * jnp.exp(jnp.where(tril, seg, NEG_BIG))).astype(BF16)
            y_parts.append(jnp.dot(mh, xdt_b[:, h * SSM_HEAD_DIM:(h + 1) * SSM_HEAD_DIM], preferred_element_type=F32))
        new_parts.append(lax.dot_general(bg, w_end[:, g * gw:(g + 1) * gw], (((0,), (0,)), ((), ())),
                                         preferred_element_type=F32))
    y_off = jnp.concatenate(
        [jnp.dot(cm[:, g * SSM_STATE:(g + 1) * SSM_STATE].astype(BF16), state_b[:, g * gw:(g + 1) * gw],
                 preferred_element_type=F32) for g in range(SSM_GROUPS)], axis=1) * jnp.exp(acs_x)
    y = jnp.concatenate(y_parts, axis=1) + y_off + dx_ref[...] * xs
    state_ref[...] = jnp.exp(tot_x) * state + jnp.concatenate(new_parts, axis=1)

    zz = z_ref[0]
    y = y * (zz * jax.nn.sigmoid(zz))
    for g in range(SSM_GROUPS):
        y_ref[0, :, g * gw:(g + 1) * gw] = _rms(y[:, g * gw:(g + 1) * gw], nw_ref[:, g * gw:(g + 1) * gw]).astype(y_ref.dtype)

    @pl.when(c == nc - 1)
    def _():
        hout_ref[0] = state_ref[...]


def _ssd(z, xbc, dt, dtt, p, h0, c0, *, lc=256):
    b, s, cd = xbc.shape
    lc = _tile(s, lc, LANES) if s % LANES == 0 else s
    nc = s // lc
    hh = SSM_HEADS
    full2 = lambda shape: pl.BlockSpec(shape, lambda i, j: (0, 0))
    return pl.pallas_call(
        functools.partial(_ssd_kernel, lc=lc, nc=nc), grid=(b, nc),
        in_specs=[pl.BlockSpec((1, lc, SSM_D_INNER), lambda i, j: (i, j, 0)),
                  pl.BlockSpec((1, lc, cd), lambda i, j: (i, j, 0)),
                  pl.BlockSpec((1, lc, hh), lambda i, j: (i, j, 0)),
                  pl.BlockSpec((1, hh, lc), lambda i, j: (i, 0, j)),
                  full2((SSM_CONV_W, cd)), full2((1, cd)),
                  full2((1, hh)), full2((hh, 1)), full2((1, hh)), full2((hh, 1)),
                  full2((1, SSM_D_INNER)), full2((1, SSM_D_INNER)),
                  pl.BlockSpec((1, SSM_STATE, SSM_D_INNER), lambda i, j: (i, 0, 0)),
                  pl.BlockSpec((1, SUBLANES, cd), lambda i, j: (i, 0, 0))],
        out_specs=[pl.BlockSpec((1, lc, SSM_D_INNER), lambda i, j: (i, j, 0)),
                   pl.BlockSpec((1, SSM_STATE, SSM_D_INNER), lambda i, j: (i, 0, 0))],
        out_shape=[jax.ShapeDtypeStruct((b, s, SSM_D_INNER), BF16),
                   jax.ShapeDtypeStruct((b, SSM_STATE, SSM_D_INNER), F32)],
        scratch_shapes=[pltpu.VMEM((SSM_STATE, SSM_D_INNER), F32), pltpu.VMEM((SUBLANES, cd), F32)],
        compiler_params=_params("parallel", "arbitrary"), name="ssd",
    )(z, xbc, dt, dtt, p["conv_w"], p["conv_b"], p["dt_b"], p["dt_bt"], p["a_log"], p["a_logt"], p["d_x"], p["norm_w"],
      h0, c0)


def _merge_kernel(xn_ref, o0, o1, o2, w0, w1, w2, wg0, wg1, wg2, out_ref):
    xn = xn_ref[...]
    acc = None
    for o_ref, w_ref, wg_ref in ((o0, w0, wg0), (o1, w1, wg1), (o2, w2, wg2)):
        gate = jax.nn.sigmoid(jnp.dot(xn, wg_ref[...], preferred_element_type=F32))
        t = gate * jnp.dot(o_ref[...], w_ref[...], preferred_element_type=F32)
        acc = t if acc is None else acc + t
    out_ref[...] = acc.astype(out_ref.dtype)


def _merge(xn, o_list, w_list, w_gate, *, tm=1024, tn=512):
    m, d = xn.shape
    tm = _tile(m, tm, SUBLANES)
    tn = _tile(d, tn, LANES)
    nb = d // tn
    in_specs = [pl.BlockSpec((tm, d), lambda i, j: (i, 0))]
    in_specs += [pl.BlockSpec((tm, o.shape[1]), lambda i, j: (i, 0)) for o in o_list]
    w_args = []
    for w in w_list:
        arr, sp = _wspec(w, (_wshape(w)[0], tn), lambda i, j: (0, j))
        in_specs.append(sp)
        w_args.append(arr)
    for br in range(N_BRANCH):
        arr, sp = _wspec(w_gate, (d, tn), functools.partial(lambda i, j, br: (0, br * nb + j), br=br))
        in_specs.append(sp)
        w_args.append(arr)
    return pl.pallas_call(
        _merge_kernel, grid=(m // tm, nb), in_specs=in_specs,
        out_specs=pl.BlockSpec((tm, tn), lambda i, j: (i, j)),
        out_shape=jax.ShapeDtypeStruct((m, d), BF16),
        compiler_params=_params("parallel", "arbitrary"), name="merge",
    )(xn, *o_list, *w_args)


def _stacked_weights(a):
    depth, d_model = a["w_in"].shape[:2]
    q_lora, kv_lora = a["mla_q_norm"].shape[1], a["mla_kv_norm"].shape[1]
    conv_dim = a["ssm_conv_w"].shape[2]
    sizes = (q_lora, kv_lora, MLA_ROPE, SSM_D_INNER, conv_dim, SSM_HEADS,
             FOX_HEADS * FOX_HEAD_DIM, FOX_HEADS * FOX_HEAD_DIM, FOX_HEADS * FOX_HEAD_DIM, FOX_HEADS,
             N_BRANCH * d_model)
    w_in = a["w_in"]
    assert w_in.shape[2] == sum(sizes)
    cols, start = [], 0
    for n in sizes:
        cols.append(w_in[:, :, start:start + n])
        start += n
    w_q, w_ckv, w_kpe, w_z, w_xbc, w_dt, w_fq, w_fk, w_fv, w_ff, w_gate = cols
    bf = lambda t: t.astype(BF16)
    small = [w_kpe, w_dt, w_ff]
    n_small = sum(t.shape[2] for t in small)
    assert n_small <= LANES
    parts = [w_q, w_ckv, *small, jnp.zeros((depth, d_model, INPROJ_TN - n_small), w_in.dtype),
             w_z, w_xbc, w_fq, w_fk, w_fv]
    st = {"in_cat": bf(jnp.concatenate(parts, axis=2)), "in_gate": bf(w_gate)}
    wq = a["mla_w_uq"].reshape(depth, q_lora, MLA_HEADS, MLA_NOPE + MLA_ROPE)
    wq = jnp.pad(wq, ((0, 0), (0, 0), (0, 0), (0, MLA_QK_PAD - MLA_NOPE - MLA_ROPE)))
    st["uq"] = bf(wq.reshape(depth, q_lora, MLA_HEADS * MLA_QK_PAD))
    wkv = a["mla_w_ukv"].reshape(depth, kv_lora, MLA_HEADS, MLA_NOPE + MLA_V)
    st["ukv"] = bf(jnp.concatenate([wkv[..., :MLA_NOPE].reshape(depth, kv_lora, MLA_HEADS * MLA_NOPE),
                                    wkv[..., MLA_NOPE:].reshape(depth, kv_lora, MLA_HEADS * MLA_V)], axis=2))
    for nm in ("w_br_mla", "w_br_ssd", "w_br_fox", "w_out"):
        st[nm] = bf(a[nm])
    for pre in ("ffn1", "ffn2"):
        for src, dst in (("_w_gate", "_g"), ("_w_up", "_u"), ("_w_down", "_d")):
            st[pre + dst] = bf(a[pre + src])
    return st


def _layer_weights(l, a, stacked):
    q_lora, kv_lora = a["mla_q_norm"].shape[1], a["mla_kv_norm"].shape[1]
    conv_dim = a["ssm_conv_w"].shape[2]
    w = {name: (arr, l) for name, arr in stacked.items()}
    w["in_widths"] = (q_lora, kv_lora, LANES, SSM_D_INNER, conv_dim) + (FOX_HEADS * FOX_HEAD_DIM,) * 3
    for pre in ("ffn1", "ffn2"):
        w[pre + "_norm"] = a[pre + "_norm"][l]
    w["mix_norm"] = a["mix_norm"][l]
    w["q_norm"], w["kv_norm"] = a["mla_q_norm"][l], a["mla_kv_norm"][l]
    ff_lo = MLA_ROPE + SSM_HEADS
    w["fb_lanes"] = jnp.pad(a["fox_b_f"][l].astype(F32), (ff_lo, LANES - ff_lo - FOX_HEADS)).reshape(1, LANES)
    w["ssd"] = {
        "conv_w": a["ssm_conv_w"][l].astype(F32), "conv_b": a["ssm_conv_b"][l].astype(F32).reshape(1, conv_dim),
        "dt_b": a["ssm_dt_bias"][l].astype(F32).reshape(1, SSM_HEADS),
        "dt_bt": a["ssm_dt_bias"][l].astype(F32).reshape(SSM_HEADS, 1),
        "a_log": a["ssm_a_log"][l].astype(F32).reshape(1, SSM_HEADS),
        "a_logt": a["ssm_a_log"][l].astype(F32).reshape(SSM_HEADS, 1),
        "d_x": jnp.repeat(a["ssm_d"][l].astype(F32), SSM_HEAD_DIM).reshape(1, SSM_D_INNER),
        "norm_w": a["ssm_norm"][l].astype(F32).reshape(1, SSM_D_INNER),
    }
    return w


def _rope_tables(pos):
    half = MLA_ROPE // 2
    inv_freq = ROPE_BASE ** (-jnp.arange(half, dtype=F32) / half)
    ang = pos.astype(F32)[:, None] * inv_freq[None, :]
    cos, sin = jnp.cos(ang), jnp.sin(ang)
    z = jnp.zeros_like(cos)
    pad = jnp.zeros((pos.shape[0], LANES - MLA_ROPE), F32)
    return (jnp.concatenate([cos, cos, pad], axis=1),
            jnp.concatenate([-sin, z, pad], axis=1),
            jnp.concatenate([z, sin, pad], axis=1))


def _pad_keys(t, sk_pad):
    return jnp.pad(t, ((0, 0), (0, sk_pad - t.shape[1])) + ((0, 0),) * (t.ndim - 2))


def _layer(x, bsz, s, w, tabs, past, final_gain):
    m = bsz * s
    kv_lora = w["kv_norm"].shape[0]
    conv_dim = w["ssd"]["conv_w"].shape[1]
    x, xn = _ffn(x, w["ffn1_norm"], w["ffn1_g"], w["ffn1_u"], w["ffn1_d"], post="norm_bf16", post_gain=w["mix_norm"])

    u_q, u_ckv, u_small, u_z, u_xbc, fq, fk, fk_b, fv, fv_b = _inproj(
        xn, w["in_cat"], w["in_widths"], ((F32,),) * 5 + ((BF16,), (F32, BF16), (F32, BF16)), tn=INPROJ_TN)

    ff_lo = MLA_ROPE + SSM_HEADS
    ckv_new, small2 = _prep(u_ckv, w["kv_norm"], u_small, tabs, w["fb_lanes"], ff_lo=ff_lo, ff_hi=ff_lo + FOX_HEADS)
    kpe_new = small2[:, :MLA_ROPE]
    logf_new = small2[:, ff_lo:ff_lo + FOX_HEADS]
    u_dt = u_small[:, MLA_ROPE:ff_lo]

    if past is not None:
        caches, l = past
        past_len = caches["mla_ckv"].shape[2]
    ukv = functools.partial(_mm, w=w["ukv"], out_dtype=BF16, prologue="cast", tn=2048, name="mla_ukv")

    q_full = _mm(u_q, w["uq"], out_dtype=BF16, prologue="rms", gain=w["q_norm"], rope_tabs=tabs, tn=2048, name="mla_q")
    q_full = q_full.reshape(bsz, s, -1)
    kv_new = ukv(ckv_new).reshape(bsz, s, -1)
    mla = dict(heads=MLA_HEADS, dq=MLA_QK_PAD, dk=MLA_NOPE, dv=MLA_V, scale=MLA_SCALE, mode="chunk")
    sk_pad = -(-s // LANES) * LANES
    if past is None:
        kpe_pad = jnp.pad(kpe_new.astype(BF16).reshape(bsz, s, -1),
                          ((0, 0), (0, sk_pad - s), (0, MLA_QK_PAD - MLA_NOPE - MLA_ROPE)))
        kv_all = _pad_keys(kv_new, sk_pad)
        o_mla = _attention(q_full, kv_all, kv_all, kpe_pad, None, k_col=0, v_col=1, q_off=0, n_valid=s, **mla)
    else:
        kv_past = ukv(caches["mla_ckv"][l].reshape(bsz * past_len, kv_lora)).reshape(bsz, past_len, -1)
        o_mla = _decode_attention(
            q_full, ((kv_past, (), 0), (kv_past, (), 1)), ((kv_new, (), 0), (kv_new, (), 1)),
            ((caches["mla_kpe"], (l,)), (kpe_new.reshape(bsz, s, MLA_ROPE), ())), None, **mla)

    if past is None:
        conv_state = jnp.zeros((bsz, SSM_CONV_W - 1, conv_dim), F32)
        h0 = jnp.zeros((bsz, SSM_STATE, SSM_D_INNER), F32)
    else:
        conv_state = caches["conv"][l].astype(F32)
        h0 = jnp.transpose(caches["ssm"][l].astype(F32), (0, 3, 1, 2)).reshape(bsz, SSM_STATE, SSM_D_INNER)
    c0 = jnp.pad(conv_state, ((0, 0), (SUBLANES - (SSM_CONV_W - 1), 0), (0, 0)))
    xbc3 = u_xbc.reshape(bsz, s, conv_dim)
    dt3 = u_dt.reshape(bsz, s, SSM_HEADS)
    o_ssd, h_new = _ssd(u_z.reshape(bsz, s, SSM_D_INNER), xbc3, dt3, jnp.swapaxes(dt3, 1, 2), w["ssd"], h0, c0)
    ssm_new = jnp.transpose(h_new.reshape(bsz, SSM_STATE, SSM_HEADS, SSM_HEAD_DIM), (0, 2, 3, 1))
    keep = SSM_CONV_W - 1
    conv_new = xbc3[:, s - keep:] if s >= keep else jnp.concatenate([conv_state, xbc3], axis=1)[:, -keep:]

    hw = FOX_HEADS * FOX_HEAD_DIM
    fox = dict(heads=FOX_HEADS, dq=FOX_HEAD_DIM, dk=FOX_HEAD_DIM, dv=FOX_HEAD_DIM, scale=FOX_SCALE, mode="causal")
    logf_all = logf_new.reshape(bsz, s, FOX_HEADS)
    if past is not None:
        logf_all = jnp.concatenate([caches["fox_logf"][l].astype(F32), logf_all], axis=1)
    n_keys = logf_all.shape[1]
    neg_cum = _cumsum_last(jnp.swapaxes(_pad_keys(logf_all, -(-n_keys // LANES) * LANES), 1, 2), -1.0 / FOX_SCALE)
    fq3 = fq.reshape(bsz, s, hw)
    if past is None:
        o_fox = _attention(fq3, _pad_keys(fk_b.reshape(bsz, s, hw), sk_pad), _pad_keys(fv_b.reshape(bsz, s, hw), sk_pad),
                           None, neg_cum, q_off=0, n_valid=s, **fox)
    else:
        o_fox = _decode_attention(
            fq3, ((caches["fox_k"], (l,), 0), (caches["fox_v"], (l,), 0)),
            ((fk.reshape(bsz, s, hw), (), 0), (fv.reshape(bsz, s, hw), (), 0)), None,
            (neg_cum[:, :, :past_len], neg_cum[:, :, past_len:n_keys]), **fox)

    merged = _merge(xn, [o_mla.reshape(m, -1), o_ssd.reshape(m, -1), o_fox.reshape(m, -1)],
                    [w["w_br_mla"], w["w_br_ssd"], w["w_br_fox"]], w["in_gate"])
    x = _mm(merged, w["w_out"], out_dtype=F32, residual=x, tm=512, tn=2048, name="out_proj")
    (x,) = _ffn(x, w["ffn2_norm"], w["ffn2_g"], w["ffn2_u"], w["ffn2_d"],
                post=None if final_gain is None else "norm_only", post_gain=final_gain)
    state = (ckv_new.reshape(bsz, s, kv_lora), kpe_new.reshape(bsz, s, MLA_ROPE),
             fk.reshape(bsz, s, FOX_HEADS, FOX_HEAD_DIM), fv.reshape(bsz, s, FOX_HEADS, FOX_HEAD_DIM),
             logf_new.reshape(bsz, s, FOX_HEADS), ssm_new, conv_new)
    return x, state


def kernel(x_prompt, x_sample, cache_mla_ckv, cache_mla_kpe, cache_fox_k, cache_fox_v, cache_fox_logf, state_ssm,
           state_conv, ffn1_norm, ffn1_w_gate, ffn1_w_up, ffn1_w_down, mix_norm, w_in, mla_q_norm, mla_w_uq,
           mla_kv_norm, mla_w_ukv, ssm_conv_w, ssm_conv_b, ssm_dt_bias, ssm_a_log, ssm_d, ssm_norm, fox_b_f,
           w_br_mla, w_br_ssd, w_br_fox, w_out, ffn2_norm, ffn2_w_gate, ffn2_w_up, ffn2_w_down, final_norm):
    a = dict(ffn1_norm=ffn1_norm, ffn1_w_gate=ffn1_w_gate, ffn1_w_up=ffn1_w_up, ffn1_w_down=ffn1_w_down,
             mix_norm=mix_norm, w_in=w_in, mla_q_norm=mla_q_norm, mla_w_uq=mla_w_uq, mla_kv_norm=mla_kv_norm,
             mla_w_ukv=mla_w_ukv, ssm_conv_w=ssm_conv_w, ssm_conv_b=ssm_conv_b, ssm_dt_bias=ssm_dt_bias,
             ssm_a_log=ssm_a_log, ssm_d=ssm_d, ssm_norm=ssm_norm, fox_b_f=fox_b_f, w_br_mla=w_br_mla,
             w_br_ssd=w_br_ssd, w_br_fox=w_br_fox, w_out=w_out, ffn2_norm=ffn2_norm, ffn2_w_gate=ffn2_w_gate,
             ffn2_w_up=ffn2_w_up, ffn2_w_down=ffn2_w_down)
    depth = w_in.shape[0]
    stacked = _stacked_weights(a)
    bp, sp, d_model = x_prompt.shape
    bs, ss, _ = x_sample.shape
    past_len = cache_mla_ckv.shape[2]
    tabs_p = tuple(jnp.tile(t, (bp, 1)) for t in _rope_tables(jnp.arange(sp, dtype=jnp.int32)))
    tabs_s = tuple(jnp.tile(t, (bs, 1)) for t in _rope_tables(past_len + jnp.arange(ss, dtype=jnp.int32)))
    hp = x_prompt.reshape(bp * sp, d_model).astype(F32)
    hs = x_sample.reshape(bs * ss, d_model).astype(F32)
    caches = {"mla_ckv": cache_mla_ckv, "mla_kpe": cache_mla_kpe, "fox_k": cache_fox_k, "fox_v": cache_fox_v,
              "fox_logf": cache_fox_logf, "ssm": state_ssm, "conv": state_conv}
    new_p, new_s = [], []
    for l in range(depth):
        w = _layer_weights(l, a, stacked)
        fg = final_norm if l == depth - 1 else None
        hp, st_p = _layer(hp, bp, sp, w, tabs_p, None, fg)
        hs, st_s = _layer(hs, bs, ss, w, tabs_s, (caches, l), fg)
        new_p.append(st_p)
        new_s.append(st_s)
    y_prompt = hp.reshape(bp, sp, d_model)
    y_sample = hs.reshape(bs, ss, d_model)
    stk = lambda states, i: jnp.stack([st[i] for st in states], axis=0)
    return (y_prompt, y_sample) + tuple(stk(new_p, i) for i in range(7)) + tuple(stk(new_s, i) for i in range(7))
```

```python
import functools
import math

import jax
import jax.numpy as jnp
from jax import lax
from jax.experimental import pallas as pl
from jax.experimental.pallas import tpu as pltpu

F32 = jnp.float32
BF16 = jnp.bfloat16

EPS = 1e-6
CHUNK = 64
FFN_RES = 0.5
MLA_HEADS, MLA_NOPE, MLA_ROPE, MLA_V = 8, 128, 64, 128
MLA_SCALE = (MLA_NOPE + MLA_ROPE) ** -0.5
ROPE_BASE = 10000.0
SSM_HEADS, SSM_HEAD_DIM, SSM_GROUPS, SSM_STATE, SSM_CONV_W = 16, 64, 2, 128, 4
SSM_D_INNER = SSM_HEADS * SSM_HEAD_DIM
FOX_HEADS, FOX_HEAD_DIM = 8, 128
FOX_SCALE = FOX_HEAD_DIM ** -0.5
N_BRANCH = 3

LANES = 128
SUBLANES = 8
MXU_DIM = 256
VMEM_LIMIT = 56 * 1024 * 1024

MLA_QK_PAD = MXU_DIM
INPROJ_TN = 512
NEG_BIG = -1e30
LOG2E = math.log2(math.e)
HI = lax.Precision.HIGHEST


def _tile(n, pref, align):
    t = (min(pref, n) // align) * align
    while t >= align:
        if n % t == 0:
            return t
        t -= align
    return n


def _params(*sem):
    return pltpu.CompilerParams(dimension_semantics=sem, vmem_limit_bytes=VMEM_LIMIT)


def _wshape(w):
    return w[0].shape[1:] if isinstance(w, tuple) else w.shape


def _wspec(w, block, index):
    if isinstance(w, tuple):
        arr, layer = w
        return arr, pl.BlockSpec((None,) + tuple(block), lambda *g: (layer,) + tuple(index(*g)))
    return w, pl.BlockSpec(tuple(block), index)


def _rms(x, g):
    return x * lax.rsqrt(jnp.mean(x * x, axis=-1, keepdims=True) + EPS) * g


def _softplus(x):
    return jnp.maximum(x, 0.0) + jnp.log1p(jnp.exp(-jnp.abs(x)))


def _rope_lanes(pe, cos, s1, s2):
    half = MLA_ROPE // 2
    return pe * cos + pltpu.roll(pe, LANES - half, 1) * s1 + pltpu.roll(pe, half, 1) * s2


def _mm_kernel(*refs, prologue, rope, residual, tn):
    it = iter(refs)
    x_ref = next(it)
    g_ref = next(it) if prologue == "rms" else None
    w_ref = next(it)
    res_ref = next(it) if residual else None
    tabs = (next(it), next(it), next(it)) if rope else None
    o_ref = next(it)
    xn_ref = next(it) if prologue != "none" else None

    if prologue == "none":
        lhs = x_ref[...]
    else:
        @pl.when(pl.program_id(1) == 0)
        def _():
            x = x_ref[...].astype(F32)
            if prologue == "rms":
                x = _rms(x, g_ref[...])
            xn_ref[...] = x.astype(BF16)
        lhs = xn_ref[...]
    acc = jnp.dot(lhs, w_ref[...], preferred_element_type=F32)
    if residual:
        acc = res_ref[...] + acc
    if rope:
        cos, s1, s2 = (t[...] for t in tabs)
        for c in range(tn // MLA_QK_PAD):
            a = c * MLA_QK_PAD
            o_ref[:, a:a + LANES] = acc[:, a:a + LANES].astype(o_ref.dtype)
            o_ref[:, a + LANES:a + 2 * LANES] = _rope_lanes(acc[:, a + LANES:a + 2 * LANES], cos, s1, s2).astype(o_ref.dtype)
    else:
        o_ref[...] = acc.astype(o_ref.dtype)


def _mm(x, w, *, out_dtype, tm=1024, tn=512, prologue="none", gain=None, residual=None, rope_tabs=None, name="mm"):
    m = x.shape[0]
    k, n = _wshape(w)
    assert x.shape[1] == k and (prologue != "none" or x.dtype == BF16)
    tm = _tile(m, tm, SUBLANES)
    tn = _tile(n, tn, MLA_QK_PAD if rope_tabs is not None else LANES)
    grid = (m // tm, n // tn)
    in_specs = [pl.BlockSpec((tm, k), lambda i, j: (i, 0))]
    args = [x]
    if prologue == "rms":
        in_specs.append(pl.BlockSpec((1, k), lambda i, j: (0, 0)))
        args.append(gain.reshape(1, k).astype(F32))
    w_arr, w_spec = _wspec(w, (k, tn), lambda i, j: (0, j))
    in_specs.append(w_spec)
    args.append(w_arr)
    if residual is not None:
        in_specs.append(pl.BlockSpec((tm, tn), lambda i, j: (i, j)))
        args.append(residual)
    if rope_tabs is not None:
        for t in rope_tabs:
            in_specs.append(pl.BlockSpec((tm, LANES), lambda i, j: (i, 0)))
            args.append(t)
    scratch = [pltpu.VMEM((tm, k), BF16)] if prologue != "none" else []
    kern = functools.partial(_mm_kernel, prologue=prologue, rope=rope_tabs is not None,
                             residual=residual is not None, tn=tn)
    return pl.pallas_call(
        kern, grid=grid, in_specs=in_specs,
        out_specs=pl.BlockSpec((tm, tn), lambda i, j: (i, j)),
        out_shape=jax.ShapeDtypeStruct((m, n), out_dtype),
        scratch_shapes=scratch, compiler_params=_params("parallel", "arbitrary"), name=name,
    )(*args)


def _inproj_kernel(xn_ref, w_ref, *outs, groups):
    j = pl.program_id(1)
    outs = iter(outs)
    for lo, hi, width, copies in groups:
        o_refs = [next(outs) for _ in range(copies)]

        @pl.when((j >= lo) & (j < hi))
        def _(o_refs=o_refs, width=width):
            acc = jnp.dot(xn_ref[...], w_ref[:, :width], preferred_element_type=F32)
            for o_ref in o_refs:
                o_ref[...] = acc.astype(o_ref.dtype)


def _inproj_layout(widths, dtypes, tn):
    groups, start = [], 0
    for n, dts in zip(widths, dtypes):
        nt = -(-n // tn)
        assert n % tn == 0 or n < tn
        groups.append((start, start + nt, min(n, tn), len(dts)))
        start += nt
    return groups, start


def _inproj(xn, w_cat, widths, dtypes, *, tm=1024, tn=512):
    m, d = xn.shape
    tm = _tile(m, tm, SUBLANES)
    groups, n_tiles = _inproj_layout(widths, dtypes, tn)
    assert _wshape(w_cat) == (d, n_tiles * tn)
    w_arr, w_spec = _wspec(w_cat, (d, tn), lambda i, j: (0, j))
    out_specs, out_shape = [], []
    for (lo, hi, bw, _), n, dts in zip(groups, widths, dtypes):
        for dt in dts:
            out_specs.append(pl.BlockSpec(
                (tm, bw), functools.partial(lambda i, j, lo, hi: (i, jnp.clip(j - lo, 0, hi - lo - 1)), lo=lo, hi=hi)))
            out_shape.append(jax.ShapeDtypeStruct((m, n), dt))
    return pl.pallas_call(
        functools.partial(_inproj_kernel, groups=groups), grid=(m // tm, n_tiles),
        in_specs=[pl.BlockSpec((tm, d), lambda i, j: (i, 0)), w_spec],
        out_specs=out_specs, out_shape=out_shape,
        compiler_params=_params("parallel", "arbitrary"), name="inproj",
    )(xn, w_arr)


def _ffn_kernel(*refs, nf, tf, f, post):
    x_ref, g_ref, wg_ref, wu_ref, wd_ref = refs[:5]
    refs = refs[5:]
    pg_ref = None
    if post is not None:
        pg_ref, refs = refs[0], refs[1:]
    outs, (xn_ref, acc_ref) = refs[:-2], refs[-2:]
    j = pl.program_id(1)

    def step(first, last):
        if first:
            xn = _rms(x_ref[...], g_ref[...]).astype(BF16)
            xn_ref[...] = xn
        else:
            xn = xn_ref[...]
        a = jnp.dot(xn, wg_ref[...], preferred_element_type=F32)
        b = jnp.dot(xn, wu_ref[...], preferred_element_type=F32)
        h = a * jax.nn.sigmoid(a) * b
        wd = wd_ref[...]
        if last and f % tf:
            valid = f - (nf - 1) * tf
            h = jnp.where(lax.broadcasted_iota(jnp.int32, h.shape, 1) < valid, h, 0.0)
            wd = jnp.where(lax.broadcasted_iota(jnp.int32, wd.shape, 0) < valid, wd, jnp.zeros_like(wd))
        d = jnp.dot(h.astype(BF16), wd, preferred_element_type=F32)
        acc = d if first else acc_ref[...] + d
        if not last:
            acc_ref[...] = acc
            return
        y = x_ref[...] + FFN_RES * acc
        if post is None:
            outs[0][...] = y
        elif post == "norm_bf16":
            outs[0][...] = y
            outs[1][...] = _rms(y, pg_ref[...]).astype(BF16)
        else:
            outs[0][...] = _rms(y, pg_ref[...])

    if nf == 1:
        step(True, True)
    else:
        pl.when(j == 0)(lambda: step(True, False))
        pl.when((j > 0) & (j < nf - 1))(lambda: step(False, False))
        pl.when(j == nf - 1)(lambda: step(False, True))


def _ffn(x, gain, wg, wu, wd, *, post=None, post_gain=None, tm=512, tf=512):
    m, d = x.shape
    f = _wshape(wg)[1]
    tm = _tile(m, tm, SUBLANES)
    nf = pl.cdiv(f, tf)
    row = pl.BlockSpec((tm, d), lambda i, j: (i, 0))
    vec = pl.BlockSpec((1, d), lambda i, j: (0, 0))
    wg_arr, wg_spec = _wspec(wg, (d, tf), lambda i, j: (0, j))
    wu_arr, wu_spec = _wspec(wu, (d, tf), lambda i, j: (0, j))
    wd_arr, wd_spec = _wspec(wd, (tf, d), lambda i, j: (j, 0))
    in_specs = [row, vec, wg_spec, wu_spec, wd_spec]
    args = [x, gain.reshape(1, d).astype(F32), wg_arr, wu_arr, wd_arr]
    out_specs, out_shape = [row], [jax.ShapeDtypeStruct((m, d), F32)]
    if post is not None:
        in_specs.append(vec)
        args.append(post_gain.reshape(1, d).astype(F32))
    if post == "norm_bf16":
        out_specs.append(row)
        out_shape.append(jax.ShapeDtypeStruct((m, d), BF16))
    return pl.pallas_call(
        functools.partial(_ffn_kernel, nf=nf, tf=tf, f=f, post=post), grid=(m // tm, nf),
        in_specs=in_specs, out_specs=out_specs, out_shape=out_shape,
        scratch_shapes=[pltpu.VMEM((tm, d), BF16), pltpu.VMEM((tm, d), F32)],
        compiler_params=_params("parallel", "arbitrary"), name="ffn",
    )(*args)


def _prep_kernel(uc_ref, g_ref, us_ref, cos_ref, s1_ref, s2_ref, fb_ref, ckv_ref, sm_ref, *, ff_lo, ff_hi):
    ckv_ref[...] = _rms(uc_ref[...], g_ref[...])
    us = us_ref[...]
    lane = lax.broadcasted_iota(jnp.int32, us.shape, 1)
    pe = jnp.where(lane < MLA_ROPE, us, 0.0)
    rot = _rope_lanes(pe, cos_ref[...], s1_ref[...], s2_ref[...])
    logf = -_softplus(-(us + fb_ref[...]))
    sm_ref[...] = jnp.where((lane >= ff_lo) & (lane < ff_hi), logf, rot)


def _prep(u_ckv, kv_gain, u_small, tabs, fb_lanes, *, ff_lo, ff_hi, tm=1024):
    m = u_small.shape[0]
    kv = kv_gain.shape[0]
    tm = _tile(m, tm, SUBLANES)
    row = lambda i: (i, 0)
    return pl.pallas_call(
        functools.partial(_prep_kernel, ff_lo=ff_lo, ff_hi=ff_hi), grid=(m // tm,),
        in_specs=[pl.BlockSpec((tm, kv), row),
                  pl.BlockSpec((1, kv), lambda i: (0, 0)),
                  pl.BlockSpec((tm, LANES), row), pl.BlockSpec((tm, LANES), row),
                  pl.BlockSpec((tm, LANES), row), pl.BlockSpec((tm, LANES), row),
                  pl.BlockSpec((1, LANES), lambda i: (0, 0))],
        out_specs=[pl.BlockSpec((tm, kv), row), pl.BlockSpec((tm, LANES), row)],
        out_shape=[jax.ShapeDtypeStruct((m, kv), F32), jax.ShapeDtypeStruct((m, LANES), F32)],
        compiler_params=_params("parallel"), name="prep",
    )(u_ckv, kv_gain.reshape(1, kv).astype(F32), u_small, *tabs, fb_lanes)


def _cumsum_kernel(x_ref, o_ref, carry_ref, *, tc, mult):
    @pl.when(pl.program_id(1) == 0)
    def _():
        carry_ref[...] = jnp.zeros_like(carry_ref)

    r = lax.broadcasted_iota(jnp.int32, (tc, tc), 0)
    c = lax.broadcasted_iota(jnp.int32, (tc, tc), 1)
    upper = (r <= c).astype(F32)
    y = jnp.dot(x_ref[0], upper, preferred_element_type=F32, precision=HI) + carry_ref[:, :1]
    o_ref[0] = y * mult
    carry_ref[...] = jnp.broadcast_to(y[:, tc - 1:tc], carry_ref.shape)


def _cumsum_last(x, mult, *, tc=512):
    b, h, s = x.shape
    tc = _tile(s, tc, LANES)
    return pl.pallas_call(
        functools.partial(_cumsum_kernel, tc=tc, mult=mult), grid=(b, s // tc),
        in_specs=[pl.BlockSpec((1, h, tc), lambda i, j: (i, 0, j))],
        out_specs=pl.BlockSpec((1, h, tc), lambda i, j: (i, 0, j)),
        out_shape=jax.ShapeDtypeStruct((b, h, s), F32),
        scratch_shapes=[pltpu.VMEM((h, LANES), F32)],
        compiler_params=_params("parallel", "arbitrary"), name="cumsum",
    )(x)


def _last_visible(q_end, mode):
    if mode == "chunk":
        return (q_end // CHUNK) * CHUNK + (CHUNK - 1)
    return q_end


def _attn_kernel(*refs, heads, dq, dk, dv, tq, tk, nk, scale, mode, q_off, n_valid, has_bias, has_shared):
    it = iter(refs)
    qi_ref, ki_ref = next(it), next(it)
    q_ref, k_ref, v_ref = next(it), next(it), next(it)
    ks_ref = next(it) if has_shared else None
    b_ref = next(it) if has_bias else None
    o_ref, m_ref, acc_ref = next(it), next(it), next(it)
    t = pl.program_id(1)
    qi, ki = qi_ref[t], ki_ref[t]
    nch = tk // LANES
    c = scale * LOG2E
    aw = dv + LANES

    @pl.when(ki == 0)
    def _():
        m_ref[...] = jnp.full_like(m_ref, NEG_BIG)
        acc_ref[...] = jnp.zeros_like(acc_ref)

    q_lo = q_off + qi * tq
    k_lo = ki * tk
    last_tile = jnp.minimum(_last_visible(q_lo + (tq - 1), mode) // tk, nk - 1)
    first_maskable = _last_visible(q_lo, mode) + 1
    def visible(rows, r0, cols, qbase, kbase):
        qpos = qbase + r0 + lax.broadcasted_iota(jnp.int32, (rows, cols), 0)
        kpos = kbase + lax.broadcasted_iota(jnp.int32, (rows, cols), 1)
        if mode == "chunk":
            sh = CHUNK.bit_length() - 1
            return lax.shift_right_logical(kpos, sh) <= lax.shift_right_logical(qpos, sh)
        return kpos <= qpos

    def update(h, r0, r1, k1, vis):
        q = q_ref[0, r0:r1, h * dq:(h + 1) * dq]
        k = k_ref[0, :k1, h * dk:(h + 1) * dk]
        if has_shared:
            k = jnp.concatenate([k, ks_ref[0, :k1, :]], axis=1)
        s = lax.dot_general(q, k, (((1,), (1,)), ((), ())), preferred_element_type=F32)
        if has_bias:
            s = s + b_ref[0, h:h + 1, :k1]
        if vis is not None:
            s = jnp.where(vis, s, NEG_BIG)
        m_prev = m_ref[h, r0:r1, :]
        mc = s[:, :LANES]
        for j in range(1, k1 // LANES):
            mc = jnp.maximum(mc, s[:, j * LANES:(j + 1) * LANES])
        m_new = jnp.maximum(m_prev, jnp.max(mc, axis=1, keepdims=True))
        m_ref[h, r0:r1, :] = m_new
        alpha = jnp.exp2((m_prev - m_new) * c)
        p = jnp.concatenate([jnp.exp2((s[:, j * LANES:(j + 1) * LANES] - m_new) * c).astype(BF16)
                             for j in range(k1 // LANES)], axis=1)
        ones_col = (lax.broadcasted_iota(jnp.int32, (k1, LANES), 1) == 0).astype(BF16)
        vx = jnp.concatenate([v_ref[0, :k1, h * dv:(h + 1) * dv], ones_col], axis=1)
        pv = jnp.dot(p, vx, preferred_element_type=F32)
        for a0 in range(h * aw, (h + 1) * aw, LANES):
            acc_ref[r0:r1, a0:a0 + LANES] = alpha * acc_ref[r0:r1, a0:a0 + LANES] + pv[:, a0 - h * aw:a0 - h * aw + LANES]

    def body(masked):
        vis = (visible(tq, 0, tk, q_lo, k_lo) & (k_lo + lax.broadcasted_iota(jnp.int32, (tq, tk), 1) < n_valid)
               if masked else None)
        for h in range(heads):
            update(h, 0, tq, tk, vis)

    half = tq // 2
    split_diag = (tq == tk and half % LANES == 0 and half % CHUNK == 0 and q_off % CHUNK == 0)

    def diag_body():
        vis_top = visible(half, 0, half, 0, 0)
        vis_bot = visible(half, half, tk, 0, 0)
        for h in range(heads):
            update(h, 0, half, half, vis_top)
            update(h, half, tq, tk, vis_bot)

    need_mask = (k_lo + (tk - 1) >= first_maskable) | (k_lo + tk > n_valid)
    on_diag = (k_lo == q_lo) & (k_lo + tk <= n_valid) if split_diag else False

    if split_diag:
        pl.when(on_diag)(diag_body)

    @pl.when(need_mask & jnp.logical_not(on_diag))
    def _():
        body(True)

    @pl.when(jnp.logical_not(need_mask))
    def _():
        body(False)

    @pl.when(ki == last_tile)
    def _():
        for h in range(heads):
            l = acc_ref[:, h * aw + dv:h * aw + dv + 1]
            o_ref[0, :, h * dv:(h + 1) * dv] = (acc_ref[:, h * aw:h * aw + dv] / l).astype(o_ref.dtype)


def _attention(q, k, v, k_shared, bias, *, heads, dq, dk, dv, scale, mode, q_off, n_valid, k_col=0, v_col=0,
               tq=1024, tk=1024):
    b, sq, _ = q.shape
    sk = k.shape[1]
    assert dk + (0 if k_shared is None else k_shared.shape[2]) == dq
    tq = _tile(sq, tq, SUBLANES)
    tk = sk if sk <= 2 * tk else _tile(sk, tk, LANES)
    nq, nk = sq // tq, sk // tk
    pairs = [(i, j) for i in range(nq)
             for j in range(min(_last_visible(q_off + i * tq + (tq - 1), mode) // tk, nk - 1) + 1)]
    qi_arr = jnp.asarray([p[0] for p in pairs], jnp.int32)
    ki_arr = jnp.asarray([p[1] for p in pairs], jnp.int32)

    in_specs = [pl.BlockSpec((1, tq, heads * dq), lambda bi, t, qi, ki: (bi, qi[t], 0)),
                pl.BlockSpec((1, tk, heads * dk), lambda bi, t, qi, ki: (bi, ki[t], k_col)),
                pl.BlockSpec((1, tk, heads * dv), lambda bi, t, qi, ki: (bi, ki[t], v_col))]
    args = [q, k, v]
    if k_shared is not None:
        in_specs.append(pl.BlockSpec((1, tk, dq - dk), lambda bi, t, qi, ki: (bi, ki[t], 0)))
        args.append(k_shared)
    if bias is not None:
        in_specs.append(pl.BlockSpec((1, heads, tk), lambda bi, t, qi, ki: (bi, 0, ki[t])))
        args.append(bias)
    kern = functools.partial(_attn_kernel, heads=heads, dq=dq, dk=dk, dv=dv, tq=tq, tk=tk, nk=nk, scale=scale,
                             mode=mode, q_off=q_off, n_valid=n_valid, has_bias=bias is not None,
                             has_shared=k_shared is not None)
    return pl.pallas_call(
        kern,
        grid_spec=pltpu.PrefetchScalarGridSpec(
            num_scalar_prefetch=2, grid=(b, len(pairs)), in_specs=in_specs,
            out_specs=pl.BlockSpec((1, tq, heads * dv), lambda bi, t, qi, ki: (bi, qi[t], 0)),
            scratch_shapes=[pltpu.VMEM((heads, tq, LANES), F32), pltpu.VMEM((tq, heads * (dv + LANES)), F32)]),
        out_shape=jax.ShapeDtypeStruct((b, sq, heads * dv), BF16),
        compiler_params=_params("parallel", "arbitrary"), name="attn_" + mode,
    )(qi_arr, ki_arr, *args)


def _decode_kernel(*refs, heads, dq, dk, dv, sq, p_len, scale, mode, has_shared, has_bias):
    it = iter(refs)
    q_ref, kp_ref, vp_ref, kn_ref, vn_ref = (next(it) for _ in range(5))
    ksp_ref, ksn_ref = (next(it), next(it)) if has_shared else (None, None)
    bp_ref, bn_ref = (next(it), next(it)) if has_bias else (None, None)
    o_ref = next(it)
    c = scale * LOG2E

    def head(ref, h, d):
        if ref.shape[1] == d and heads > 1:
            x = ref[pl.ds(h, ref.shape[0] // heads, stride=heads), :]
        else:
            x = ref[:, h * d:(h + 1) * d]
        return x.astype(BF16)

    row = lax.broadcasted_iota(jnp.int32, (sq, sq), 0)
    col = lax.broadcasted_iota(jnp.int32, (sq, sq), 1)
    if mode == "chunk":
        sh = CHUNK.bit_length() - 1
        vis = lax.shift_right_logical(p_len + col, sh) <= lax.shift_right_logical(p_len + row, sh)
    else:
        vis = col <= row
    if has_shared:
        pad = dq - dk - ksp_ref.shape[1]
        ksp = jnp.concatenate([ksp_ref[...].astype(BF16), jnp.zeros((p_len, pad), BF16)], axis=1)
        ksn = jnp.concatenate([ksn_ref[...].astype(BF16), jnp.zeros((sq, pad), BF16)], axis=1)
    contract_last = (((1,), (1,)), ((), ()))
    for h in range(heads):
        q = q_ref[:, h * dq:(h + 1) * dq]
        kp, kn = head(kp_ref, h, dk), head(kn_ref, h, dk)
        if has_shared:
            kp = jnp.concatenate([kp, ksp], axis=1)
            kn = jnp.concatenate([kn, ksn], axis=1)
        s_p = lax.dot_general(q, kp, contract_last, preferred_element_type=F32)
        s_n = lax.dot_general(q, kn, contract_last, preferred_element_type=F32)
        if has_bias:
            s_p = s_p + bp_ref[h:h + 1, :]
            s_n = s_n + bn_ref[h:h + 1, :]
        s_n = jnp.where(vis, s_n, NEG_BIG)
        m = jnp.maximum(jnp.max(s_p, axis=1, keepdims=True), jnp.max(s_n, axis=1, keepdims=True))
        p_p = jnp.exp2((s_p - m) * c)
        p_n = jnp.exp2((s_n - m) * c)
        l = jnp.sum(p_p, axis=1, keepdims=True) + jnp.sum(p_n, axis=1, keepdims=True)
        pv = (jnp.dot(p_p.astype(BF16), head(vp_ref, h, dv), preferred_element_type=F32)
              + jnp.dot(p_n.astype(BF16), head(vn_ref, h, dv), preferred_element_type=F32))
        o_ref[:, h * dv:(h + 1) * dv] = (pv / l).astype(o_ref.dtype)


def _decode_attention(q, past_kv, new_kv, shared, bias, *, heads, dq, dk, dv, scale, mode):
    b, sq, _ = q.shape

    def spec(arr, prefix, colblk, width):
        lead, inner = arr.shape[:len(prefix) + 1], arr.shape[len(prefix) + 1:]
        if len(inner) == 3:
            arr = arr.reshape(lead + (inner[0] * inner[1], inner[2]))
            block, idx = arr.shape[-2:], (0, 0)
        else:
            block, idx = (inner[0], width), (0, colblk)
        return arr, pl.BlockSpec((None,) * len(lead) + tuple(block), lambda i: tuple(prefix) + (i,) + idx)

    (kp, kp_pre, kp_col), (vp, vp_pre, vp_col) = past_kv
    (kn, kn_pre, kn_col), (vn, vn_pre, vn_col) = new_kv
    p_len = kp.shape[len(kp_pre) + 1]
    in_specs, args = [pl.BlockSpec((None, sq, heads * dq), lambda i: (i, 0, 0))], [q]
    for arr, pre, col, width in ((kp, kp_pre, kp_col, heads * dk), (vp, vp_pre, vp_col, heads * dv),
                                 (kn, kn_pre, kn_col, heads * dk), (vn, vn_pre, vn_col, heads * dv)):
        arr, sp = spec(arr, pre, col, width)
        in_specs.append(sp)
        args.append(arr)
    if shared is not None:
        for arr, pre in shared:
            arr, sp = spec(arr, pre, 0, arr.shape[-1])
            in_specs.append(sp)
            args.append(arr)
    if bias is not None:
        for arr in bias:
            in_specs.append(pl.BlockSpec((None,) + arr.shape[1:], lambda i: (i, 0, 0)))
            args.append(arr)
    kern = functools.partial(_decode_kernel, heads=heads, dq=dq, dk=dk, dv=dv, sq=sq, p_len=p_len, scale=scale,
                             mode=mode, has_shared=shared is not None, has_bias=bias is not None)
    return pl.pallas_call(
        kern, grid=(b,), in_specs=in_specs,
        out_specs=pl.BlockSpec((None, sq, heads * dv), lambda i: (i, 0, 0)),
        out_shape=jax.ShapeDtypeStruct((b, sq, heads * dv), BF16),
        compiler_params=_params("parallel"), name="decode_" + mode,
    )(*args)


def _ssd_kernel(z_ref, xbc_ref, dt_ref, dtt_ref, cw_ref, cb_ref, dtb_ref, dtbt_ref, al_ref, alt_ref, dx_ref, nw_ref,
                h0_ref, c0_ref, y_ref, hout_ref, state_ref, carry_ref, *, lc, nc):
    c = pl.program_id(1)
    gw = SSM_D_INNER // SSM_GROUPS
    hpg = SSM_HEADS // SSM_GROUPS
    halo = SUBLANES

    @pl.when(c == 0)
    def _():
        state_ref[...] = h0_ref[0]
        carry_ref[...] = c0_ref[0]

    x = xbc_ref[0]
    cat = jnp.concatenate([carry_ref[...], x], axis=0)
    conv = cb_ref[...]
    for kk in range(SSM_CONV_W):
        shift = SSM_CONV_W - 1 - kk
        src = pltpu.roll(cat, shift, 0) if shift else cat
        conv = conv + src[halo:, :] * cw_ref[kk:kk + 1, :]
    carry_ref[...] = x[lc - halo:, :]
    act = conv * jax.nn.sigmoid(conv)
    xs = act[:, :SSM_D_INNER]
    bm = act[:, SSM_D_INNER:SSM_D_INNER + SSM_GROUPS * SSM_STATE]
    cm = act[:, SSM_D_INNER + SSM_GROUPS * SSM_STATE:]

    dt = _softplus(dt_ref[0] + dtb_ref[...])
    dtt = _softplus(dtt_ref[0] + dtbt_ref[...])
    adt = dt * (-jnp.exp(al_ref[...]))
    adtt = dtt * (-jnp.exp(alt_ref[...]))
    r = lax.broadcasted_iota(jnp.int32, (lc, lc), 0)
    cc = lax.broadcasted_iota(jnp.int32, (lc, lc), 1)
    tril = cc <= r
    acs = jnp.dot(tril.astype(F32), adt, preferred_element_type=F32, precision=HI)
    acst = jnp.dot(adtt, (r <= cc).astype(F32), preferred_element_type=F32, precision=HI)
    hh = lax.broadcasted_iota(jnp.int32, (SSM_HEADS, SSM_D_INNER), 0)
    ll = lax.broadcasted_iota(jnp.int32, (SSM_HEADS, SSM_D_INNER), 1)
    expand = ((ll >= hh * SSM_HEAD_DIM) & (ll < (hh + 1) * SSM_HEAD_DIM)).astype(F32)
    dt_x = jnp.dot(dt, expand, preferred_element_type=F32, precision=HI)
    acs_x = jnp.dot(acs, expand, preferred_element_type=F32, precision=HI)
    tot_x = acs_x[lc - 1:lc, :]
    xdt = xs * dt_x
    xdt_b = xdt.astype(BF16)
    w_end = (xdt * jnp.exp(tot_x - acs_x)).astype(BF16)
    state = state_ref[...]
    state_b = state.astype(BF16)

    y_parts, new_parts = [], []
    for g in range(SSM_GROUPS):
        bg = bm[:, g * SSM_STATE:(g + 1) * SSM_STATE].astype(BF16)
        cg = cm[:, g * SSM_STATE:(g + 1) * SSM_STATE].astype(BF16)
        cb = lax.dot_general(cg, bg, (((1,), (1,)), ((), ())), preferred_element_type=F32)
        for hl in range(hpg):
            h = g * hpg + hl
            seg = acs[:, h:h + 1] - acst[h:h + 1, :]
            mh = (cb * jnp.exp(jnp.where(tril, seg, NEG_BIG))).astype(BF16)
            y_parts.append(jnp.dot(mh, xdt_b[:, h * SSM_HEAD_DIM:(h + 1) * SSM_HEAD_DIM], preferred_element_type=F32))
        new_parts.append(lax.dot_general(bg, w_end[:, g * gw:(g + 1) * gw], (((0,), (0,)), ((), ())),
                                         preferred_element_type=F32))
    y_off = jnp.concatenate(
        [jnp.dot(cm[:, g * SSM_STATE:(g + 1) * SSM_STATE].astype(BF16), state_b[:, g * gw:(g + 1) * gw],
                 preferred_element_type=F32) for g in range(SSM_GROUPS)], axis=1) * jnp.exp(acs_x)
    y = jnp.concatenate(y_parts, axis=1) + y_off + dx_ref[...] * xs
    state_ref[...] = jnp.exp(tot_x) * state + jnp.concatenate(new_parts, axis=1)

    zz = z_ref[0]
    y = y * (zz * jax.nn.sigmoid(zz))
    for g in range(SSM_GROUPS):
        y_ref[0, :, g * gw:(g + 1) * gw] = _rms(y[:, g * gw:(g + 1) * gw], nw_ref[:, g * gw:(g + 1) * gw]).astype(y_ref.dtype)

    @pl.when(c == nc - 1)
    def _():
        hout_ref[0] = state_ref[...]


def _ssd(z, xbc, dt, dtt, p, h0, c0, *, lc=256):
    b, s, cd = xbc.shape
    lc = _tile(s, lc, LANES) if s % LANES == 0 else s
    nc = s // lc
    hh = SSM_HEADS
    full2 = lambda shape: pl.BlockSpec(shape, lambda i, j: (0, 0))
    return pl.pallas_call(
        functools.partial(_ssd_kernel, lc=lc, nc=nc), grid=(b, nc),
        in_specs=[pl.BlockSpec((1, lc, SSM_D_INNER), lambda i, j: (i, j, 0)),
                  pl.BlockSpec((1, lc, cd), lambda i, j: (i, j, 0)),
                  pl.BlockSpec((1, lc, hh), lambda i, j: (i, j, 0)),
                  pl.BlockSpec((1, hh, lc), lambda i, j: (i, 0, j)),
                  full2((SSM_CONV_W, cd)), full2((1, cd)),
                  full2((1, hh)), full2((hh, 1)), full2((1, hh)), full2((hh, 1)),
                  full2((1, SSM_D_INNER)), full2((1, SSM_D_INNER)),
                  pl.BlockSpec((1, SSM_STATE, SSM_D_INNER), lambda i, j: (i, 0, 0)),
                  pl.BlockSpec((1, SUBLANES, cd), lambda i, j: (i, 0, 0))],
        out_specs=[pl.BlockSpec((1, lc, SSM_D_INNER), lambda i, j: (i, j, 0)),
                   pl.BlockSpec((1, SSM_STATE, SSM_D_INNER), lambda i, j: (i, 0, 0))],
        out_shape=[jax.ShapeDtypeStruct((b, s, SSM_D_INNER), BF16),
                   jax.ShapeDtypeStruct((b, SSM_STATE, SSM_D_INNER), F32)],
        scratch_shapes=[pltpu.VMEM((SSM_STATE, SSM_D_INNER), F32), pltpu.VMEM((SUBLANES, cd), F32)],
        compiler_params=_params("parallel", "arbitrary"), name="ssd",
    )(z, xbc, dt, dtt, p["conv_w"], p["conv_b"], p["dt_b"], p["dt_bt"], p["a_log"], p["a_logt"], p["d_x"], p["norm_w"],
      h0, c0)


def _merge_kernel(xn_ref, o0, o1, o2, w0, w1, w2, wg0, wg1, wg2, out_ref):
    xn = xn_ref[...]
    acc = None
    for o_ref, w_ref, wg_ref in ((o0, w0, wg0), (o1, w1, wg1), (o2, w2, wg2)):
        gate = jax.nn.sigmoid(jnp.dot(xn, wg_ref[...], preferred_element_type=F32))
        t = gate * jnp.dot(o_ref[...], w_ref[...], preferred_element_type=F32)
        acc = t if acc is None else acc + t
    out_ref[...] = acc.astype(out_ref.dtype)


def _merge(xn, o_list, w_list, w_gate, *, tm=1024, tn=512):
    m, d = xn.shape
    tm = _tile(m, tm, SUBLANES)
    tn = _tile(d, tn, LANES)
    nb = d // tn
    in_specs = [pl.BlockSpec((tm, d), lambda i, j: (i, 0))]
    in_specs += [pl.BlockSpec((tm, o.shape[1]), lambda i, j: (i, 0)) for o in o_list]
    w_args = []
    for w in w_list:
        arr, sp = _wspec(w, (_wshape(w)[0], tn), lambda i, j: (0, j))
        in_specs.append(sp)
        w_args.append(arr)
    for br in range(N_BRANCH):
        arr, sp = _wspec(w_gate, (d, tn), functools.partial(lambda i, j, br: (0, br * nb + j), br=br))
        in_specs.append(sp)
        w_args.append(arr)
    return pl.pallas_call(
        _merge_kernel, grid=(m // tm, nb), in_specs=in_specs,
        out_specs=pl.BlockSpec((tm, tn), lambda i, j: (i, j)),
        out_shape=jax.ShapeDtypeStruct((m, d), BF16),
        compiler_params=_params("parallel", "arbitrary"), name="merge",
    )(xn, *o_list, *w_args)


def _stacked_weights(a):
    depth, d_model = a["w_in"].shape[:2]
    q_lora, kv_lora = a["mla_q_norm"].shape[1], a["mla_kv_norm"].shape[1]
    conv_dim = a["ssm_conv_w"].shape[2]
    sizes = (q_lora, kv_lora, MLA_ROPE, SSM_D_INNER, conv_dim, SSM_HEADS,
             FOX_HEADS * FOX_HEAD_DIM, FOX_HEADS * FOX_HEAD_DIM, FOX_HEADS * FOX_HEAD_DIM, FOX_HEADS,
             N_BRANCH * d_model)
    w_in = a["w_in"]
    assert w_in.shape[2] == sum(sizes)
    cols, start = [], 0
    for n in sizes:
        cols.append(w_in[:, :, start:start + n])
        start += n
    w_q, w_ckv, w_kpe, w_z, w_xbc, w_dt, w_fq, w_fk, w_fv, w_ff, w_gate = cols
    bf = lambda t: t.astype(BF16)
    small = [w_kpe, w_dt, w_ff]
    n_small = sum(t.shape[2] for t in small)
    assert n_small <= LANES
    parts = [w_q, w_ckv, *small, jnp.zeros((depth, d_model, INPROJ_TN - n_small), w_in.dtype),
             w_z, w_xbc, w_fq, w_fk, w_fv]
    st = {"in_cat": bf(jnp.concatenate(parts, axis=2)), "in_gate": bf(w_gate)}
    wq = a["mla_w_uq"].reshape(depth, q_lora, MLA_HEADS, MLA_NOPE + MLA_ROPE)
    wq = jnp.pad(wq, ((0, 0), (0, 0), (0, 0), (0, MLA_QK_PAD - MLA_NOPE - MLA_ROPE)))
    st["uq"] = bf(wq.reshape(depth, q_lora, MLA_HEADS * MLA_QK_PAD))
    wkv = a["mla_w_ukv"].reshape(depth, kv_lora, MLA_HEADS, MLA_NOPE + MLA_V)
    st["ukv"] = bf(jnp.concatenate([wkv[..., :MLA_NOPE].reshape(depth, kv_lora, MLA_HEADS * MLA_NOPE),
                                    wkv[..., MLA_NOPE:].reshape(depth, kv_lora, MLA_HEADS * MLA_V)], axis=2))
    for nm in ("w_br_mla", "w_br_ssd", "w_br_fox", "w_out"):
        st[nm] = bf(a[nm])
    for pre in ("ffn1", "ffn2"):
        for src, dst in (("_w_gate", "_g"), ("_w_up", "_u"), ("_w_down", "_d")):
            st[pre + dst] = bf(a[pre + src])
    return st


def _layer_weights(l, a, stacked):
    q_lora, kv_lora = a["mla_q_norm"].shape[1], a["mla_kv_norm"].shape[1]
    conv_dim = a["ssm_conv_w"].shape[2]
    w = {name: (arr, l) for name, arr in stacked.items()}
    w["in_widths"] = (q_lora, kv_lora, LANES, SSM_D_INNER, conv_dim) + (FOX_HEADS * FOX_HEAD_DIM,) * 3
    for pre in ("ffn1", "ffn2"):
        w[pre + "_norm"] = a[pre + "_norm"][l]
    w["mix_norm"] = a["mix_norm"][l]
    w["q_norm"], w["kv_norm"] = a["mla_q_norm"][l], a["mla_kv_norm"][l]
    ff_lo = MLA_ROPE + SSM_HEADS
    w["fb_lanes"] = jnp.pad(a["fox_b_f"][l].astype(F32), (ff_lo, LANES - ff_lo - FOX_HEADS)).reshape(1, LANES)
    w["ssd"] = {
        "conv_w": a["ssm_conv_w"][l].astype(F32), "conv_b": a["ssm_conv_b"][l].astype(F32).reshape(1, conv_dim),
        "dt_b": a["ssm_dt_bias"][l].astype(F32).reshape(1, SSM_HEADS),
        "dt_bt": a["ssm_dt_bias"][l].astype(F32).reshape(SSM_HEADS, 1),
        "a_log": a["ssm_a_log"][l].astype(F32).reshape(1, SSM_HEADS),
        "a_logt": a["ssm_a_log"][l].astype(F32).reshape(SSM_HEADS, 1),
        "d_x": jnp.repeat(a["ssm_d"][l].astype(F32), SSM_HEAD_DIM).reshape(1, SSM_D_INNER),
        "norm_w": a["ssm_norm"][l].astype(F32).reshape(1, SSM_D_INNER),
    }
    return w


def _rope_tables(pos):
    half = MLA_ROPE // 2
    inv_freq = ROPE_BASE ** (-jnp.arange(half, dtype=F32) / half)
    ang = pos.astype(F32)[:, None] * inv_freq[None, :]
    cos, sin = jnp.cos(ang), jnp.sin(ang)
    z = jnp.zeros_like(cos)
    pad = jnp.zeros((pos.shape[0], LANES - MLA_ROPE), F32)
    return (jnp.concatenate([cos, cos, pad], axis=1),
            jnp.concatenate([-sin, z, pad], axis=1),
            jnp.concatenate([z, sin, pad], axis=1))


def _pad_keys(t, sk_pad):
    return jnp.pad(t, ((0, 0), (0, sk_pad - t.shape[1])) + ((0, 0),) * (t.ndim - 2))


def _layer(x, bsz, s, w, tabs, past, final_gain):
    m = bsz * s
    kv_lora = w["kv_norm"].shape[0]
    conv_dim = w["ssd"]["conv_w"].shape[1]
    x, xn = _ffn(x, w["ffn1_norm"], w["ffn1_g"], w["ffn1_u"], w["ffn1_d"], post="norm_bf16", post_gain=w["mix_norm"])

    u_q, u_ckv, u_small, u_z, u_xbc, fq, fk, fk_b, fv, fv_b = _inproj(
        xn, w["in_cat"], w["in_widths"], ((F32,),) * 5 + ((BF16,), (F32, BF16), (F32, BF16)), tn=INPROJ_TN)

    ff_lo = MLA_ROPE + SSM_HEADS
    ckv_new, small2 = _prep(u_ckv, w["kv_norm"], u_small, tabs, w["fb_lanes"], ff_lo=ff_lo, ff_hi=ff_lo + FOX_HEADS)
    kpe_new = small2[:, :MLA_ROPE]
    logf_new = small2[:, ff_lo:ff_lo + FOX_HEADS]
    u_dt = u_small[:, MLA_ROPE:ff_lo]

    if past is not None:
        caches, l = past
        past_len = caches["mla_ckv"].shape[2]
    ukv = functools.partial(_mm, w=w["ukv"], out_dtype=BF16, prologue="cast", tn=2048, name="mla_ukv")

    q_full = _mm(u_q, w["uq"], out_dtype=BF16, prologue="rms", gain=w["q_norm"], rope_tabs=tabs, tn=2048, name="mla_q")
    q_full = q_full.reshape(bsz, s, -1)
    kv_new = ukv(ckv_new).reshape(bsz, s, -1)
    mla = dict(heads=MLA_HEADS, dq=MLA_QK_PAD, dk=MLA_NOPE, dv=MLA_V, scale=MLA_SCALE, mode="chunk")
    sk_pad = -(-s // LANES) * LANES
    if past is None:
        kpe_pad = jnp.pad(kpe_new.astype(BF16).reshape(bsz, s, -1),
                          ((0, 0), (0, sk_pad - s), (0, MLA_QK_PAD - MLA_NOPE - MLA_ROPE)))
        kv_all = _pad_keys(kv_new, sk_pad)
        o_mla = _attention(q_full, kv_all, kv_all, kpe_pad, None, k_col=0, v_col=1, q_off=0, n_valid=s, **mla)
    else:
        kv_past = ukv(caches["mla_ckv"][l].reshape(bsz * past_len, kv_lora)).reshape(bsz, past_len, -1)
        o_mla = _decode_attention(
            q_full, ((kv_past, (), 0), (kv_past, (), 1)), ((kv_new, (), 0), (kv_new, (), 1)),
            ((caches["mla_kpe"], (l,)), (kpe_new.reshape(bsz, s, MLA_ROPE), ())), None, **mla)

    if past is None:
        conv_state = jnp.zeros((bsz, SSM_CONV_W - 1, conv_dim), F32)
        h0 = jnp.zeros((bsz, SSM_STATE, SSM_D_INNER), F32)
    else:
        conv_state = caches["conv"][l].astype(F32)
        h0 = jnp.transpose(caches["ssm"][l].astype(F32), (0, 3, 1, 2)).reshape(bsz, SSM_STATE, SSM_D_INNER)
    c0 = jnp.pad(conv_state, ((0, 0), (SUBLANES - (SSM_CONV_W - 1), 0), (0, 0)))
    xbc3 = u_xbc.reshape(bsz, s, conv_dim)
    dt3 = u_dt.reshape(bsz, s, SSM_HEADS)
    o_ssd, h_new = _ssd(u_z.reshape(bsz, s, SSM_D_INNER), xbc3, dt3, jnp.swapaxes(dt3, 1, 2), w["ssd"], h0, c0)
    ssm_new = jnp.transpose(h_new.reshape(bsz, SSM_STATE, SSM_HEADS, SSM_HEAD_DIM), (0, 2, 3, 1))
    keep = SSM_CONV_W - 1
    conv_new = xbc3[:, s - keep:] if s >= keep else jnp.concatenate([conv_state, xbc3], axis=1)[:, -keep:]

    hw = FOX_HEADS * FOX_HEAD_DIM
    fox = dict(heads=FOX_HEADS, dq=FOX_HEAD_DIM, dk=FOX_HEAD_DIM, dv=FOX_HEAD_DIM, scale=FOX_SCALE, mode="causal")
    logf_all = logf_new.reshape(bsz, s, FOX_HEADS)
    if past is not None:
        logf_all = jnp.concatenate([caches["fox_logf"][l].astype(F32), logf_all], axis=1)
    n_keys = logf_all.shape[1]
    neg_cum = _cumsum_last(jnp.swapaxes(_pad_keys(logf_all, -(-n_keys // LANES) * LANES), 1, 2), -1.0 / FOX_SCALE)
    fq3 = fq.reshape(bsz, s, hw)
    if past is None:
        o_fox = _attention(fq3, _pad_keys(fk_b.reshape(bsz, s, hw), sk_pad), _pad_keys(fv_b.reshape(bsz, s, hw), sk_pad),
                           None, neg_cum, q_off=0, n_valid=s, **fox)
    else:
        o_fox = _decode_attention(
            fq3, ((caches["fox_k"], (l,), 0), (caches["fox_v"], (l,), 0)),
            ((fk.reshape(bsz, s, hw), (), 0), (fv.reshape(bsz, s, hw), (), 0)), None,
            (neg_cum[:, :, :past_len], neg_cum[:, :, past_len:n_keys]), **fox)

    merged = _merge(xn, [o_mla.reshape(m, -1), o_ssd.reshape(m, -1), o_fox.reshape(m, -1)],
                    [w["w_br_mla"], w["w_br_ssd"], w["w_br_fox"]], w["in_gate"])
    x = _mm(merged, w["w_out"], out_dtype=F32, residual=x, tm=512, tn=2048, name="out_proj")
    (x,) = _ffn(x, w["ffn2_norm"], w["ffn2_g"], w["ffn2_u"], w["ffn2_d"],
                post=None if final_gain is None else "norm_only", post_gain=final_gain)
    state = (ckv_new.reshape(bsz, s, kv_lora), kpe_new.reshape(bsz, s, MLA_ROPE),
             fk.reshape(bsz, s, FOX_HEADS, FOX_HEAD_DIM), fv.reshape(bsz, s, FOX_HEADS, FOX_HEAD_DIM),
             logf_new.reshape(bsz, s, FOX_HEADS), ssm_new, conv_new)
    return x, state


def kernel(x_prompt, x_sample, cache_mla_ckv, cache_mla_kpe, cache_fox_k, cache_fox_v, cache_fox_logf, state_ssm,
           state_conv, ffn1_norm, ffn1_w_gate, ffn1_w_up, ffn1_w_down, mix_norm, w_in, mla_q_norm, mla_w_uq,
           mla_kv_norm, mla_w_ukv, ssm_conv_w, ssm_conv_b, ssm_dt_bias, ssm_a_log, ssm_d, ssm_norm, fox_b_f,
           w_br_mla, w_br_ssd, w_br_fox, w_out, ffn2_norm, ffn2_w_gate, ffn2_w_up, ffn2_w_down, final_norm):
    a = dict(ffn1_norm=ffn1_norm, ffn1_w_gate=ffn1_w_gate, ffn1_w_up=ffn1_w_up, ffn1_w_down=ffn1_w_down,
             mix_norm=mix_norm, w_in=w_in, mla_q_norm=mla_q_norm, mla_w_uq=mla_w_uq, mla_kv_norm=mla_kv_norm,
             mla_w_ukv=mla_w_ukv, ssm_conv_w=ssm_conv_w, ssm_conv_b=ssm_conv_b, ssm_dt_bias=ssm_dt_bias,
             ssm_a_log=ssm_a_log, ssm_d=ssm_d, ssm_norm=ssm_norm, fox_b_f=fox_b_f, w_br_mla=w_br_mla,
             w_br_ssd=w_br_ssd, w_br_fox=w_br_fox, w_out=w_out, ffn2_norm=ffn2_norm, ffn2_w_gate=ffn2_w_gate,
             ffn2_w_up=ffn2_w_up, ffn2_w_down=ffn2_w_down)
    depth = w_in.shape[0]
    stacked = _stacked_weights(a)
    bp, sp, d_model = x_prompt.shape
    bs, ss, _ = x_sample.shape
    past_len = cache_mla_ckv.shape[2]
    tabs_p = tuple(jnp.tile(t, (bp, 1)) for t in _rope_tables(jnp.arange(sp, dtype=jnp.int32)))
    tabs_s = tuple(jnp.tile(t, (bs, 1)) for t in _rope_tables(past_len + jnp.arange(ss, dtype=jnp.int32)))
    hp = x_prompt.reshape(bp * sp, d_model).astype(F32)
    hs = x_sample.reshape(bs * ss, d_model).astype(F32)
    caches = {"mla_ckv": cache_mla_ckv, "mla_kpe": cache_mla_kpe, "fox_k": cache_fox_k, "fox_v": cache_fox_v,
              "fox_logf": cache_fox_logf, "ssm": state_ssm, "conv": state_conv}
    new_p, new_s = [], []
    for l in range(depth):
        w = _layer_weights(l, a, stacked)
        fg = final_norm if l == depth - 1 else None
        hp, st_p = _layer(hp, bp, sp, w, tabs_p, None, fg)
        hs, st_s = _layer(hs, bs, ss, w, tabs_s, (caches, l), fg)
        new_p.append(st_p)
        new_s.append(st_s)
    y_prompt = hp.reshape(bp, sp, d_model)
    y_sample = hs.reshape(bs, ss, d_model)
    stk = lambda states, i: jnp.stack([st[i] for st in states], axis=0)
    return (y_prompt, y_sample) + tuple(stk(new_p, i) for i in range(7)) + tuple(stk(new_s, i) for i in range(7))
```

```python
import functools
import math

import jax
import jax.numpy as jnp
from jax import lax
from jax.experimental import pallas as pl
from jax.experimental.pallas import tpu as pltpu

F32 = jnp.float32
BF16 = jnp.bfloat16

EPS = 1e-6
CHUNK = 64
FFN_RES = 0.5
MLA_HEADS, MLA_NOPE, MLA_ROPE, MLA_V = 8, 128, 64, 128
MLA_SCALE = (MLA_NOPE + MLA_ROPE) ** -0.5
ROPE_BASE = 10000.0
SSM_HEADS, SSM_HEAD_DIM, SSM_GROUPS, SSM_STATE, SSM_CONV_W = 16, 64, 2, 128, 4
SSM_D_INNER = SSM_HEADS * SSM_HEAD_DIM
FOX_HEADS, FOX_HEAD_DIM = 8, 128
FOX_SCALE = FOX_HEAD_DIM ** -0.5
N_BRANCH = 3

LANES = 128
SUBLANES = 8
MXU_DIM = 256
VMEM_LIMIT = 56 * 1024 * 1024

MLA_QK_PAD = MXU_DIM
INPROJ_TN = 512
NEG_BIG = -1e30
LOG2E = math.log2(math.e)
HI = lax.Precision.HIGHEST


def _tile(n, pref, align):
    t = (min(pref, n) // align) * align
    while t >= align:
        if n % t == 0:
            return t
        t -= align
    return n


def _params(*sem):
    return pltpu.CompilerParams(dimension_semantics=sem, vmem_limit_bytes=VMEM_LIMIT)


def _wshape(w):
    return w[0].shape[1:] if isinstance(w, tuple) else w.shape


def _wspec(w, block, index):
    if isinstance(w, tuple):
        arr, layer = w
        return arr, pl.BlockSpec((None,) + tuple(block), lambda *g: (layer,) + tuple(index(*g)))
    return w, pl.BlockSpec(tuple(block), index)


def _rms(x, g):
    return x * lax.rsqrt(jnp.mean(x * x, axis=-1, keepdims=True) + EPS) * g


def _softplus(x):
    return jnp.maximum(x, 0.0) + jnp.log1p(jnp.exp(-jnp.abs(x)))


def _rope_lanes(pe, cos, s1, s2):
    half = MLA_ROPE // 2
    return pe * cos + pltpu.roll(pe, LANES - half, 1) * s1 + pltpu.roll(pe, half, 1) * s2


def _mm_kernel(*refs, prologue, rope, residual, tn):
    it = iter(refs)
    x_ref = next(it)
    g_ref = next(it) if prologue == "rms" else None
    w_ref = next(it)
    res_ref = next(it) if residual else None
    tabs = (next(it), next(it), next(it)) if rope else None
    o_ref = next(it)
    xn_ref = next(it) if prologue != "none" else None

    if prologue == "none":
        lhs = x_ref[...]
    else:
        @pl.when(pl.program_id(1) == 0)
        def _():
            x = x_ref[...].astype(F32)
            if prologue == "rms":
                x = _rms(x, g_ref[...])
            xn_ref[...] = x.astype(BF16)
        lhs = xn_ref[...]
    acc = jnp.dot(lhs, w_ref[...], preferred_element_type=F32)
    if residual:
        acc = res_ref[...] + acc
    if rope:
        cos, s1, s2 = (t[...] for t in tabs)
        for c in range(tn // MLA_QK_PAD):
            a = c * MLA_QK_PAD
            o_ref[:, a:a + LANES] = acc[:, a:a + LANES].astype(o_ref.dtype)
            o_ref[:, a + LANES:a + 2 * LANES] = _rope_lanes(acc[:, a + LANES:a + 2 * LANES], cos, s1, s2).astype(o_ref.dtype)
    else:
        o_ref[...] = acc.astype(o_ref.dtype)


def _mm(x, w, *, out_dtype, tm=1024, tn=512, prologue="none", gain=None, residual=None, rope_tabs=None, name="mm"):
    m = x.shape[0]
    k, n = _wshape(w)
    assert x.shape[1] == k and (prologue != "none" or x.dtype == BF16)
    tm = _tile(m, tm, SUBLANES)
    tn = _tile(n, tn, MLA_QK_PAD if rope_tabs is not None else LANES)
    grid = (m // tm, n // tn)
    in_specs = [pl.BlockSpec((tm, k), lambda i, j: (i, 0))]
    args = [x]
    if prologue == "rms":
        in_specs.append(pl.BlockSpec((1, k), lambda i, j: (0, 0)))
        args.append(gain.reshape(1, k).astype(F32))
    w_arr, w_spec = _wspec(w, (k, tn), lambda i, j: (0, j))
    in_specs.append(w_spec)
    args.append(w_arr)
    if residual is not None:
        in_specs.append(pl.BlockSpec((tm, tn), lambda i, j: (i, j)))
        args.append(residual)
    if rope_tabs is not None:
        for t in rope_tabs:
            in_specs.append(pl.BlockSpec((tm, LANES), lambda i, j: (i, 0)))
            args.append(t)
    scratch = [pltpu.VMEM((tm, k), BF16)] if prologue != "none" else []
    kern = functools.partial(_mm_kernel, prologue=prologue, rope=rope_tabs is not None,
                             residual=residual is not None, tn=tn)
    return pl.pallas_call(
        kern, grid=grid, in_specs=in_specs,
        out_specs=pl.BlockSpec((tm, tn), lambda i, j: (i, j)),
        out_shape=jax.ShapeDtypeStruct((m, n), out_dtype),
        scratch_shapes=scratch, compiler_params=_params("parallel", "arbitrary"), name=name,
    )(*args)


def _inproj_kernel(xn_ref, w_ref, *outs, groups):
    j = pl.program_id(1)
    outs = iter(outs)
    for lo, hi, width, copies in groups:
        o_refs = [next(outs) for _ in range(copies)]

        @pl.when((j >= lo) & (j < hi))
        def _(o_refs=o_refs, width=width):
            acc = jnp.dot(xn_ref[...], w_ref[:, :width], preferred_element_type=F32)
            for o_ref in o_refs:
                o_ref[...] = acc.astype(o_ref.dtype)


def _inproj_layout(widths, dtypes, tn):
    groups, start = [], 0
    for n, dts in zip(widths, dtypes):
        nt = -(-n // tn)
        assert n % tn == 0 or n < tn
        groups.append((start, start + nt, min(n, tn), len(dts)))
        start += nt
    return groups, start


def _inproj(xn, w_cat, widths, dtypes, *, tm=1024, tn=512):
    m, d = xn.shape
    tm = _tile(m, tm, SUBLANES)
    groups, n_tiles = _inproj_layout(widths, dtypes, tn)
    assert _wshape(w_cat) == (d, n_tiles * tn)
    w_arr, w_spec = _wspec(w_cat, (d, tn), lambda i, j: (0, j))
    out_specs, out_shape = [], []
    for (lo, hi, bw, _), n, dts in zip(groups, widths, dtypes):
        for dt in dts:
            out_specs.append(pl.BlockSpec(
                (tm, bw), functools.partial(lambda i, j, lo, hi: (i, jnp.clip(j - lo, 0, hi - lo - 1)), lo=lo, hi=hi)))
            out_shape.append(jax.ShapeDtypeStruct((m, n), dt))
    return pl.pallas_call(
        functools.partial(_inproj_kernel, groups=groups), grid=(m // tm, n_tiles),
        in_specs=[pl.BlockSpec((tm, d), lambda i, j: (i, 0)), w_spec],
        out_specs=out_specs, out_shape=out_shape,
        compiler_params=_params("parallel", "arbitrary"), name="inproj",
    )(xn, w_arr)


def _ffn_kernel(*refs, nf, tf, f, post):
    x_ref, g_ref, wg_ref, wu_ref, wd_ref = refs[:5]
    refs = refs[5:]
    pg_ref = None
    if post is not None:
        pg_ref, refs = refs[0], refs[1:]
    outs, (xn_ref, acc_ref) = refs[:-2], refs[-2:]
    j = pl.program_id(1)

    def step(first, last):
        if first:
            xn = _rms(x_ref[...], g_ref[...]).astype(BF16)
            xn_ref[...] = xn
        else:
            xn = xn_ref[...]
        a = jnp.dot(xn, wg_ref[...], preferred_element_type=F32)
        b = jnp.dot(xn, wu_ref[...], preferred_element_type=F32)
        h = a * jax.nn.sigmoid(a) * b
        wd = wd_ref[...]
        if last and f % tf:
            valid = f - (nf - 1) * tf
            h = jnp.where(lax.broadcasted_iota(jnp.int32, h.shape, 1) < valid, h, 0.0)
            wd = jnp.where(lax.broadcasted_iota(jnp.int32, wd.shape, 0) < valid, wd, jnp.zeros_like(wd))
        d = jnp.dot(h.astype(BF16), wd, preferred_element_type=F32)
        acc = d if first else acc_ref[...] + d
        if not last:
            acc_ref[...] = acc
            return
        y = x_ref[...] + FFN_RES * acc
        if post is None:
            outs[0][...] = y
        elif post == "norm_bf16":
            outs[0][...] = y
            outs[1][...] = _rms(y, pg_ref[...]).astype(BF16)
        else:
            outs[0][...] = _rms(y, pg_ref[...])

    if nf == 1:
        step(True, True)
    else:
        pl.when(j == 0)(lambda: step(True, False))
        pl.when((j > 0) & (j < nf - 1))(lambda: step(False, False))
        pl.when(j == nf - 1)(lambda: step(False, True))


def _ffn(x, gain, wg, wu, wd, *, post=None, post_gain=None, tm=512, tf=512):
    m, d = x.shape
    f = _wshape(wg)[1]
    tm = _tile(m, tm, SUBLANES)
    nf = pl.cdiv(f, tf)
    row = pl.BlockSpec((tm, d), lambda i, j: (i, 0))
    vec = pl.BlockSpec((1, d), lambda i, j: (0, 0))
    wg_arr, wg_spec = _wspec(wg, (d, tf), lambda i, j: (0, j))
    wu_arr, wu_spec = _wspec(wu, (d, tf), lambda i, j: (0, j))
    wd_arr, wd_spec = _wspec(wd, (tf, d), lambda i, j: (j, 0))
    in_specs = [row, vec, wg_spec, wu_spec, wd_spec]
    args = [x, gain.reshape(1, d).astype(F32), wg_arr, wu_arr, wd_arr]
    out_specs, out_shape = [row], [jax.ShapeDtypeStruct((m, d), F32)]
    if post is not None:
        in_specs.append(vec)
        args.append(post_gain.reshape(1, d).astype(F32))
    if post == "norm_bf16":
        out_specs.append(row)
        out_shape.append(jax.ShapeDtypeStruct((m, d), BF16))
    return pl.pallas_call(
        functools.partial(_ffn_kernel, nf=nf, tf=tf, f=f, post=post), grid=(m // tm, nf),
        in_specs=in_specs, out_specs=out_specs, out_shape=out_shape,
        scratch_shapes=[pltpu.VMEM((tm, d), BF16), pltpu.VMEM((tm, d), F32)],
        compiler_params=_params("parallel", "arbitrary"), name="ffn",
    )(*args)


def _prep_kernel(uc_ref, g_ref, us_ref, cos_ref, s1_ref, s2_ref, fb_ref, ckv_ref, sm_ref, *, ff_lo, ff_hi):
    ckv_ref[...] = _rms(uc_ref[...], g_ref[...])
    us = us_ref[...]
    lane = lax.broadcasted_iota(jnp.int32, us.shape, 1)
    pe = jnp.where(lane < MLA_ROPE, us, 0.0)
    rot = _rope_lanes(pe, cos_ref[...], s1_ref[...], s2_ref[...])
    logf = -_softplus(-(us + fb_ref[...]))
    sm_ref[...] = jnp.where((lane >= ff_lo) & (lane < ff_hi), logf, rot)


def _prep(u_ckv, kv_gain, u_small, tabs, fb_lanes, *, ff_lo, ff_hi, tm=1024):
    m = u_small.shape[0]
    kv = kv_gain.shape[0]
    tm = _tile(m, tm, SUBLANES)
    row = lambda i: (i, 0)
    return pl.pallas_call(
        functools.partial(_prep_kernel, ff_lo=ff_lo, ff_hi=ff_hi), grid=(m // tm,),
        in_specs=[pl.BlockSpec((tm, kv), row),
                  pl.BlockSpec((1, kv), lambda i: (0, 0)),
                  pl.BlockSpec((tm, LANES), row), pl.BlockSpec((tm, LANES), row),
                  pl.BlockSpec((tm, LANES), row), pl.BlockSpec((tm, LANES), row),
                  pl.BlockSpec((1, LANES), lambda i: (0, 0))],
        out_specs=[pl.BlockSpec((tm, kv), row), pl.BlockSpec((tm, LANES), row)],
        out_shape=[jax.ShapeDtypeStruct((m, kv), F32), jax.ShapeDtypeStruct((m, LANES), F32)],
        compiler_params=_params("parallel"), name="prep",
    )(u_ckv, kv_gain.reshape(1, kv).astype(F32), u_small, *tabs, fb_lanes)


def _cumsum_kernel(x_ref, o_ref, carry_ref, *, tc, mult):
    @pl.when(pl.program_id(1) == 0)
    def _():
        carry_ref[...] = jnp.zeros_like(carry_ref)

    r = lax.broadcasted_iota(jnp.int32, (tc, tc), 0)
    c = lax.broadcasted_iota(jnp.int32, (tc, tc), 1)
    upper = (r <= c).astype(F32)
    y = jnp.dot(x_ref[0], upper, preferred_element_type=F32, precision=HI) + carry_ref[:, :1]
    o_ref[0] = y * mult
    carry_ref[...] = jnp.broadcast_to(y[:, tc - 1:tc], carry_ref.shape)


def _cumsum_last(x, mult, *, tc=512):
    b, h, s = x.shape
    tc = _tile(s, tc, LANES)
    return pl.pallas_call(
        functools.partial(_cumsum_kernel, tc=tc, mult=mult), grid=(b, s // tc),
        in_specs=[pl.BlockSpec((1, h, tc), lambda i, j: (i, 0, j))],
        out_specs=pl.BlockSpec((1, h, tc), lambda i, j: (i, 0, j)),
        out_shape=jax.ShapeDtypeStruct((b, h, s), F32),
        scratch_shapes=[pltpu.VMEM((h, LANES), F32)],
        compiler_params=_params("parallel", "arbitrary"), name="cumsum",
    )(x)


def _last_visible(q_end, mode):
    if mode == "chunk":
        return (q_end // CHUNK) * CHUNK + (CHUNK - 1)
    return q_end


def _attn_kernel(*refs, heads, dq, dk, dv, tq, tk, nk, scale, mode, q_off, n_valid, has_bias, has_shared):
    it = iter(refs)
    qi_ref, ki_ref = next(it), next(it)
    q_ref, k_ref, v_ref = next(it), next(it), next(it)
    ks_ref = next(it) if has_shared else None
    b_ref = next(it) if has_bias else None
    o_ref, m_ref, acc_ref = next(it), next(it), next(it)
    t = pl.program_id(1)
    qi, ki = qi_ref[t], ki_ref[t]
    nch = tk // LANES
    c = scale * LOG2E
    aw = dv + LANES

    @pl.when(ki == 0)
    def _():
        m_ref[...] = jnp.full_like(m_ref, NEG_BIG)
        acc_ref[...] = jnp.zeros_like(acc_ref)

    q_lo = q_off + qi * tq
    k_lo = ki * tk
    last_tile = jnp.minimum(_last_visible(q_lo + (tq - 1), mode) // tk, nk - 1)
    first_maskable = _last_visible(q_lo, mode) + 1
    def visible(rows, r0, cols, qbase, kbase):
        qpos = qbase + r0 + lax.broadcasted_iota(jnp.int32, (rows, cols), 0)
        kpos = kbase + lax.broadcasted_iota(jnp.int32, (rows, cols), 1)
        if mode == "chunk":
            sh = CHUNK.bit_length() - 1
            return lax.shift_right_logical(kpos, sh) <= lax.shift_right_logical(qpos, sh)
        return kpos <= qpos

    def update(h, r0, r1, k1, vis):
        q = q_ref[0, r0:r1, h * dq:(h + 1) * dq]
        k = k_ref[0, :k1, h * dk:(h + 1) * dk]
        if has_shared:
            k = jnp.concatenate([k, ks_ref[0, :k1, :]], axis=1)
        s = lax.dot_general(q, k, (((1,), (1,)), ((), ())), preferred_element_type=F32)
        if has_bias:
            s = s + b_ref[0, h:h + 1, :k1]
        if vis is not None:
            s = jnp.where(vis, s, NEG_BIG)
        m_prev = m_ref[h, r0:r1, :]
        mc = s[:, :LANES]
        for j in range(1, k1 // LANES):
            mc = jnp.maximum(mc, s[:, j * LANES:(j + 1) * LANES])
        m_new = jnp.maximum(m_prev, jnp.max(mc, axis=1, keepdims=True))
        m_ref[h, r0:r1, :] = m_new
        alpha = jnp.exp2((m_prev - m_new) * c)
        p = jnp.concatenate([jnp.exp2((s[:, j * LANES:(j + 1) * LANES] - m_new) * c).astype(BF16)
                             for j in range(k1 // LANES)], axis=1)
        ones_col = (lax.broadcasted_iota(jnp.int32, (k1, LANES), 1) == 0).astype(BF16)
        vx = jnp.concatenate([v_ref[0, :k1, h * dv:(h + 1) * dv], ones_col], axis=1)
        pv = jnp.dot(p, vx, preferred_element_type=F32)
        for a0 in range(h * aw, (h + 1) * aw, LANES):
            acc_ref[r0:r1, a0:a0 + LANES] = alpha * acc_ref[r0:r1, a0:a0 + LANES] + pv[:, a0 - h * aw:a0 - h * aw + LANES]

    def body(masked):
        vis = (visible(tq, 0, tk, q_lo, k_lo) & (k_lo + lax.broadcasted_iota(jnp.int32, (tq, tk), 1) < n_valid)
               if masked else None)
        for h in range(heads):
            update(h, 0, tq, tk, vis)

    n_split = 4 if tq % (4 * 2 * LANES) == 0 else 2
    half = tq // n_split
    split_diag = (tq == tk and half % LANES == 0 and half % CHUNK == 0 and q_off % CHUNK == 0)

    def diag_body():
        vis = [visible(half, i * half, (i + 1) * half, 0, 0) for i in range(n_split)]
        for h in range(heads):
            for i in range(n_split):
                update(h, i * half, (i + 1) * half, (i + 1) * half, vis[i])

    need_mask = (k_lo + (tk - 1) >= first_maskable) | (k_lo + tk > n_valid)
    on_diag = (k_lo == q_lo) & (k_lo + tk <= n_valid) if split_diag else False

    if split_diag:
        pl.when(on_diag)(diag_body)

    @pl.when(need_mask & jnp.logical_not(on_diag))
    def _():
        body(True)

    @pl.when(jnp.logical_not(need_mask))
    def _():
        body(False)

    @pl.when(ki == last_tile)
    def _():
        for h in range(heads):
            l = acc_ref[:, h * aw + dv:h * aw + dv + 1]
            o_ref[0, :, h * dv:(h + 1) * dv] = (acc_ref[:, h * aw:h * aw + dv] / l).astype(o_ref.dtype)


def _attention(q, k, v, k_shared, bias, *, heads, dq, dk, dv, scale, mode, q_off, n_valid, k_col=0, v_col=0,
               tq=1024, tk=1024):
    b, sq, _ = q.shape
    sk = k.shape[1]
    assert dk + (0 if k_shared is None else k_shared.shape[2]) == dq
    tq = _tile(sq, tq, SUBLANES)
    tk = sk if sk <= 2 * tk else _tile(sk, tk, LANES)
    nq, nk = sq // tq, sk // tk
    pairs = [(i, j) for i in range(nq)
             for j in range(min(_last_visible(q_off + i * tq + (tq - 1), mode) // tk, nk - 1) + 1)]
    qi_arr = jnp.asarray([p[0] for p in pairs], jnp.int32)
    ki_arr = jnp.asarray([p[1] for p in pairs], jnp.int32)

    in_specs = [pl.BlockSpec((1, tq, heads * dq), lambda bi, t, qi, ki: (bi, qi[t], 0)),
                pl.BlockSpec((1, tk, heads * dk), lambda bi, t, qi, ki: (bi, ki[t], k_col)),
                pl.BlockSpec((1, tk, heads * dv), lambda bi, t, qi, ki: (bi, ki[t], v_col))]
    args = [q, k, v]
    if k_shared is not None:
        in_specs.append(pl.BlockSpec((1, tk, dq - dk), lambda bi, t, qi, ki: (bi, ki[t], 0)))
        args.append(k_shared)
    if bias is not None:
        in_specs.append(pl.BlockSpec((1, heads, tk), lambda bi, t, qi, ki: (bi, 0, ki[t])))
        args.append(bias)
    kern = functools.partial(_attn_kernel, heads=heads, dq=dq, dk=dk, dv=dv, tq=tq, tk=tk, nk=nk, scale=scale,
                             mode=mode, q_off=q_off, n_valid=n_valid, has_bias=bias is not None,
                             has_shared=k_shared is not None)
    return pl.pallas_call(
        kern,
        grid_spec=pltpu.PrefetchScalarGridSpec(
            num_scalar_prefetch=2, grid=(b, len(pairs)), in_specs=in_specs,
            out_specs=pl.BlockSpec((1, tq, heads * dv), lambda bi, t, qi, ki: (bi, qi[t], 0)),
            scratch_shapes=[pltpu.VMEM((heads, tq, LANES), F32), pltpu.VMEM((tq, heads * (dv + LANES)), F32)]),
        out_shape=jax.ShapeDtypeStruct((b, sq, heads * dv), BF16),
        compiler_params=_params("parallel", "arbitrary"), name="attn_" + mode,
    )(qi_arr, ki_arr, *args)


def _decode_kernel(*refs, heads, dq, dk, dv, sq, p_len, scale, mode, has_shared, has_bias):
    it = iter(refs)
    q_ref, kp_ref, vp_ref, kn_ref, vn_ref = (next(it) for _ in range(5))
    ksp_ref, ksn_ref = (next(it), next(it)) if has_shared else (None, None)
    bp_ref, bn_ref = (next(it), next(it)) if has_bias else (None, None)
    o_ref = next(it)
    c = scale * LOG2E

    def head(ref, h, d):
        if ref.shape[1] == d and heads > 1:
            x = ref[pl.ds(h, ref.shape[0] // heads, stride=heads), :]
        else:
            x = ref[:, h * d:(h + 1) * d]
        return x.astype(BF16)

    row = lax.broadcasted_iota(jnp.int32, (sq, sq), 0)
    col = lax.broadcasted_iota(jnp.int32, (sq, sq), 1)
    if mode == "chunk":
        sh = CHUNK.bit_length() - 1
        vis = lax.shift_right_logical(p_len + col, sh) <= lax.shift_right_logical(p_len + row, sh)
    else:
        vis = col <= row
    if has_shared:
        pad = dq - dk - ksp_ref.shape[1]
        ksp = jnp.concatenate([ksp_ref[...].astype(BF16), jnp.zeros((p_len, pad), BF16)], axis=1)
        ksn = jnp.concatenate([ksn_ref[...].astype(BF16), jnp.zeros((sq, pad), BF16)], axis=1)
    contract_last = (((1,), (1,)), ((), ()))
    for h in range(heads):
        q = q_ref[:, h * dq:(h + 1) * dq]
        kp, kn = head(kp_ref, h, dk), head(kn_ref, h, dk)
        if has_shared:
            kp = jnp.concatenate([kp, ksp], axis=1)
            kn = jnp.concatenate([kn, ksn], axis=1)
        s_p = lax.dot_general(q, kp, contract_last, preferred_element_type=F32)
        s_n = lax.dot_general(q, kn, contract_last, preferred_element_type=F32)
        if has_bias:
            s_p = s_p + bp_ref[h:h + 1, :]
            s_n = s_n + bn_ref[h:h + 1, :]
        s_n = jnp.where(vis, s_n, NEG_BIG)
        m = jnp.maximum(jnp.max(s_p, axis=1, keepdims=True), jnp.max(s_n, axis=1, keepdims=True))
        p_p = jnp.exp2((s_p - m) * c)
        p_n = jnp.exp2((s_n - m) * c)
        l = jnp.sum(p_p, axis=1, keepdims=True) + jnp.sum(p_n, axis=1, keepdims=True)
        pv = (jnp.dot(p_p.astype(BF16), head(vp_ref, h, dv), preferred_element_type=F32)
              + jnp.dot(p_n.astype(BF16), head(vn_ref, h, dv), preferred_element_type=F32))
        o_ref[:, h * dv:(h + 1) * dv] = (pv / l).astype(o_ref.dtype)


def _decode_attention(q, past_kv, new_kv, shared, bias, *, heads, dq, dk, dv, scale, mode):
    b, sq, _ = q.shape

    def spec(arr, prefix, colblk, width):
        lead, inner = arr.shape[:len(prefix) + 1], arr.shape[len(prefix) + 1:]
        if len(inner) == 3:
            arr = arr.reshape(lead + (inner[0] * inner[1], inner[2]))
            block, idx = arr.shape[-2:], (0, 0)
        else:
            block, idx = (inner[0], width), (0, colblk)
        return arr, pl.BlockSpec((None,) * len(lead) + tuple(block), lambda i: tuple(prefix) + (i,) + idx)

    (kp, kp_pre, kp_col), (vp, vp_pre, vp_col) = past_kv
    (kn, kn_pre, kn_col), (vn, vn_pre, vn_col) = new_kv
    p_len = kp.shape[len(kp_pre) + 1]
    in_specs, args = [pl.BlockSpec((None, sq, heads * dq), lambda i: (i, 0, 0))], [q]
    for arr, pre, col, width in ((kp, kp_pre, kp_col, heads * dk), (vp, vp_pre, vp_col, heads * dv),
                                 (kn, kn_pre, kn_col, heads * dk), (vn, vn_pre, vn_col, heads * dv)):
        arr, sp = spec(arr, pre, col, width)
        in_specs.append(sp)
        args.append(arr)
    if shared is not None:
        for arr, pre in shared:
            arr, sp = spec(arr, pre, 0, arr.shape[-1])
            in_specs.append(sp)
            args.append(arr)
    if bias is not None:
        for arr in bias:
            in_specs.append(pl.BlockSpec((None,) + arr.shape[1:], lambda i: (i, 0, 0)))
            args.append(arr)
    kern = functools.partial(_decode_kernel, heads=heads, dq=dq, dk=dk, dv=dv, sq=sq, p_len=p_len, scale=scale,
                             mode=mode, has_shared=shared is not None, has_bias=bias is not None)
    return pl.pallas_call(
        kern, grid=(b,), in_specs=in_specs,
        out_specs=pl.BlockSpec((None, sq, heads * dv), lambda i: (i, 0, 0)),
        out_shape=jax.ShapeDtypeStruct((b, sq, heads * dv), BF16),
        compiler_params=_params("parallel"), name="decode_" + mode,
    )(*args)


def _ssd_kernel(z_ref, xbc_ref, dt_ref, dtt_ref, cw_ref, cb_ref, dtb_ref, dtbt_ref, al_ref, alt_ref, dx_ref, nw_ref,
                h0_ref, c0_ref, y_ref, hout_ref, state_ref, carry_ref, *, lc, nc):
    c = pl.program_id(1)
    gw = SSM_D_INNER // SSM_GROUPS
    hpg = SSM_HEADS // SSM_GROUPS
    halo = SUBLANES

    @pl.when(c == 0)
    def _():
        state_ref[...] = h0_ref[0]
        carry_ref[...] = c0_ref[0]

    x = xbc_ref[0]
    cat = jnp.concatenate([carry_ref[...], x], axis=0)
    conv = cb_ref[...]
    for kk in range(SSM_CONV_W):
        shift = SSM_CONV_W - 1 - kk
        src = pltpu.roll(cat, shift, 0) if shift else cat
        conv = conv + src[halo:, :] * cw_ref[kk:kk + 1, :]
    carry_ref[...] = x[lc - halo:, :]
    act = conv * jax.nn.sigmoid(conv)
    xs = act[:, :SSM_D_INNER]
    bm = act[:, SSM_D_INNER:SSM_D_INNER + SSM_GROUPS * SSM_STATE]
    cm = act[:, SSM_D_INNER + SSM_GROUPS * SSM_STATE:]

    dt = _softplus(dt_ref[0] + dtb_ref[...])
    dtt = _softplus(dtt_ref[0] + dtbt_ref[...])
    adt = dt * (-jnp.exp(al_ref[...]))
    adtt = dtt * (-jnp.exp(alt_ref[...]))
    r = lax.broadcasted_iota(jnp.int32, (lc, lc), 0)
    cc = lax.broadcasted_iota(jnp.int32, (lc, lc), 1)
    tril = cc <= r
    acs = jnp.dot(tril.astype(F32), adt, preferred_element_type=F32, precision=HI)
    acst = jnp.dot(adtt, (r <= cc).astype(F32), preferred_element_type=F32, precision=HI)
    hh = lax.broadcasted_iota(jnp.int32, (SSM_HEADS, SSM_D_INNER), 0)
    ll = lax.broadcasted_iota(jnp.int32, (SSM_HEADS, SSM_D_INNER), 1)
    expand = ((ll >= hh * SSM_HEAD_DIM) & (ll < (hh + 1) * SSM_HEAD_DIM)).astype(F32)
    dt_x = jnp.dot(dt, expand, preferred_element_type=F32, precision=HI)
    acs_x = jnp.dot(acs, expand, preferred_element_type=F32, precision=HI)
    tot_x = acs_x[lc - 1:lc, :]
    xdt = xs * dt_x
    xdt_b = xdt.astype(BF16)
    w_end = (xdt * jnp.exp(tot_x - acs_x)).astype(BF16)
    state = state_ref[...]
    state_b = state.astype(BF16)

    y_parts, new_parts = [], []
    for g in range(SSM_GROUPS):
        bg = bm[:, g * SSM_STATE:(g + 1) * SSM_STATE].astype(BF16)
        cg = cm[:, g * SSM_STATE:(g + 1) * SSM_STATE].astype(BF16)
        cb = lax.dot_general(cg, bg, (((1,), (1,)), ((), ())), preferred_element_type=F32)
        for hl in range(hpg):
            h = g * hpg + hl
            seg = acs[:, h:h + 1] - acst[h:h + 1, :]
            mh = (cb * jnp.exp(jnp.where(tril, seg, NEG_BIG))).astype(BF16)
            y_parts.append(jnp.dot(mh, xdt_b[:, h * SSM_HEAD_DIM:(h + 1) * SSM_HEAD_DIM], preferred_element_type=F32))
        new_parts.append(lax.dot_general(bg, w_end[:, g * gw:(g + 1) * gw], (((0,), (0,)), ((), ())),
                                         preferred_element_type=F32))
    y_off = jnp.concatenate(
        [jnp.dot(cm[:, g * SSM_STATE:(g + 1) * SSM_STATE].astype(BF16), state_b[:, g * gw:(g + 1) * gw],
                 preferred_element_type=F32) for g in range(SSM_GROUPS)], axis=1) * jnp.exp(acs_x)
    y = jnp.concatenate(y_parts, axis=1) + y_off + dx_ref[...] * xs
    state_ref[...] = jnp.exp(tot_x) * state + jnp.concatenate(new_parts, axis=1)

    zz = z_ref[0]
    y = y * (zz * jax.nn.sigmoid(zz))
    for g in range(SSM_GROUPS):
        y_ref[0, :, g * gw:(g + 1) * gw] = _rms(y[:, g * gw:(g + 1) * gw], nw_ref[:, g * gw:(g + 1) * gw]).astype(y_ref.dtype)

    @pl.when(c == nc - 1)
    def _():
        hout_ref[0] = state_ref[...]


def _ssd(z, xbc, dt, dtt, p, h0, c0, *, lc=256):
    b, s, cd = xbc.shape
    lc = _tile(s, lc, LANES) if s % LANES == 0 else s
    nc = s // lc
    hh = SSM_HEADS
    full2 = lambda shape: pl.BlockSpec(shape, lambda i, j: (0, 0))
    return pl.pallas_call(
        functools.partial(_ssd_kernel, lc=lc, nc=nc), grid=(b, nc),
        in_specs=[pl.BlockSpec((1, lc, SSM_D_INNER), lambda i, j: (i, j, 0)),
                  pl.BlockSpec((1, lc, cd), lambda i, j: (i, j, 0)),
                  pl.BlockSpec((1, lc, hh), lambda i, j: (i, j, 0)),
                  pl.BlockSpec((1, hh, lc), lambda i, j: (i, 0, j)),
                  full2((SSM_CONV_W, cd)), full2((1, cd)),
                  full2((1, hh)), full2((hh, 1)), full2((1, hh)), full2((hh, 1)),
                  full2((1, SSM_D_INNER)), full2((1, SSM_D_INNER)),
                  pl.BlockSpec((1, SSM_STATE, SSM_D_INNER), lambda i, j: (i, 0, 0)),
                  pl.BlockSpec((1, SUBLANES, cd), lambda i, j: (i, 0, 0))],
        out_specs=[pl.BlockSpec((1, lc, SSM_D_INNER), lambda i, j: (i, j, 0)),
                   pl.BlockSpec((1, SSM_STATE, SSM_D_INNER), lambda i, j: (i, 0, 0))],
        out_shape=[jax.ShapeDtypeStruct((b, s, SSM_D_INNER), BF16),
                   jax.ShapeDtypeStruct((b, SSM_STATE, SSM_D_INNER), F32)],
        scratch_shapes=[pltpu.VMEM((SSM_STATE, SSM_D_INNER), F32), pltpu.VMEM((SUBLANES, cd), F32)],
        compiler_params=_params("parallel", "arbitrary"), name="ssd",
    )(z, xbc, dt, dtt, p["conv_w"], p["conv_b"], p["dt_b"], p["dt_bt"], p["a_log"], p["a_logt"], p["d_x"], p["norm_w"],
      h0, c0)


def _merge_kernel(xn_ref, o0, o1, o2, w0, w1, w2, wg0, wg1, wg2, out_ref):
    xn = xn_ref[...]
    acc = None
    for o_ref, w_ref, wg_ref in ((o0, w0, wg0), (o1, w1, wg1), (o2, w2, wg2)):
        gate = jax.nn.sigmoid(jnp.dot(xn, wg_ref[...], preferred_element_type=F32))
        t = gate * jnp.dot(o_ref[...], w_ref[...], preferred_element_type=F32)
        acc = t if acc is None else acc + t
    out_ref[...] = acc.astype(out_ref.dtype)


def _merge(xn, o_list, w_list, w_gate, *, tm=1024, tn=512):
    m, d = xn.shape
    tm = _tile(m, tm, SUBLANES)
    tn = _tile(d, tn, LANES)
    nb = d // tn
    in_specs = [pl.BlockSpec((tm, d), lambda i, j: (i, 0))]
    in_specs += [pl.BlockSpec((tm, o.shape[1]), lambda i, j: (i, 0)) for o in o_list]
    w_args = []
    for w in w_list:
        arr, sp = _wspec(w, (_wshape(w)[0], tn), lambda i, j: (0, j))
        in_specs.append(sp)
        w_args.append(arr)
    for br in range(N_BRANCH):
        arr, sp = _wspec(w_gate, (d, tn), functools.partial(lambda i, j, br: (0, br * nb + j), br=br))
        in_specs.append(sp)
        w_args.append(arr)
    return pl.pallas_call(
        _merge_kernel, grid=(m // tm, nb), in_specs=in_specs,
        out_specs=pl.BlockSpec((tm, tn), lambda i, j: (i, j)),
        out_shape=jax.ShapeDtypeStruct((m, d), BF16),
        compiler_params=_params("parallel", "arbitrary"), name="merge",
    )(xn, *o_list, *w_args)


def _stacked_weights(a):
    depth, d_model = a["w_in"].shape[:2]
    q_lora, kv_lora = a["mla_q_norm"].shape[1], a["mla_kv_norm"].shape[1]
    conv_dim = a["ssm_conv_w"].shape[2]
    sizes = (q_lora, kv_lora, MLA_ROPE, SSM_D_INNER, conv_dim, SSM_HEADS,
             FOX_HEADS * FOX_HEAD_DIM, FOX_HEADS * FOX_HEAD_DIM, FOX_HEADS * FOX_HEAD_DIM, FOX_HEADS,
             N_BRANCH * d_model)
    w_in = a["w_in"]
    assert w_in.shape[2] == sum(sizes)
    cols, start = [], 0
    for n in sizes:
        cols.append(w_in[:, :, start:start + n])
        start += n
    w_q, w_ckv, w_kpe, w_z, w_xbc, w_dt, w_fq, w_fk, w_fv, w_ff, w_gate = cols
    bf = lambda t: t.astype(BF16)
    small = [w_kpe, w_dt, w_ff]
    n_small = sum(t.shape[2] for t in small)
    assert n_small <= LANES
    parts = [w_q, w_ckv, *small, jnp.zeros((depth, d_model, INPROJ_TN - n_small), w_in.dtype),
             w_z, w_xbc, w_fq, w_fk, w_fv]
    st = {"in_cat": bf(jnp.concatenate(parts, axis=2)), "in_gate": bf(w_gate)}
    wq = a["mla_w_uq"].reshape(depth, q_lora, MLA_HEADS, MLA_NOPE + MLA_ROPE)
    wq = jnp.pad(wq, ((0, 0), (0, 0), (0, 0), (0, MLA_QK_PAD - MLA_NOPE - MLA_ROPE)))
    st["uq"] = bf(wq.reshape(depth, q_lora, MLA_HEADS * MLA_QK_PAD))
    wkv = a["mla_w_ukv"].reshape(depth, kv_lora, MLA_HEADS, MLA_NOPE + MLA_V)
    st["ukv"] = bf(jnp.concatenate([wkv[..., :MLA_NOPE].reshape(depth, kv_lora, MLA_HEADS * MLA_NOPE),
                                    wkv[..., MLA_NOPE:].reshape(depth, kv_lora, MLA_HEADS * MLA_V)], axis=2))
    for nm in ("w_br_mla", "w_br_ssd", "w_br_fox", "w_out"):
        st[nm] = bf(a[nm])
    for pre in ("ffn1", "ffn2"):
        for src, dst in (("_w_gate", "_g"), ("_w_up", "_u"), ("_w_down", "_d")):
            st[pre + dst] = bf(a[pre + src])
    return st


def _layer_weights(l, a, stacked):
    q_lora, kv_lora = a["mla_q_norm"].shape[1], a["mla_kv_norm"].shape[1]
    conv_dim = a["ssm_conv_w"].shape[2]
    w = {name: (arr, l) for name, arr in stacked.items()}
    w["in_widths"] = (q_lora, kv_lora, LANES, SSM_D_INNER, conv_dim) + (FOX_HEADS * FOX_HEAD_DIM,) * 3
    for pre in ("ffn1", "ffn2"):
        w[pre + "_norm"] = a[pre + "_norm"][l]
    w["mix_norm"] = a["mix_norm"][l]
    w["q_norm"], w["kv_norm"] = a["mla_q_norm"][l], a["mla_kv_norm"][l]
    ff_lo = MLA_ROPE + SSM_HEADS
    w["fb_lanes"] = jnp.pad(a["fox_b_f"][l].astype(F32), (ff_lo, LANES - ff_lo - FOX_HEADS)).reshape(1, LANES)
    w["ssd"] = {
        "conv_w": a["ssm_conv_w"][l].astype(F32), "conv_b": a["ssm_conv_b"][l].astype(F32).reshape(1, conv_dim),
        "dt_b": a["ssm_dt_bias"][l].astype(F32).reshape(1, SSM_HEADS),
        "dt_bt": a["ssm_dt_bias"][l].astype(F32).reshape(SSM_HEADS, 1),
        "a_log": a["ssm_a_log"][l].astype(F32).reshape(1, SSM_HEADS),
        "a_logt": a["ssm_a_log"][l].astype(F32).reshape(SSM_HEADS, 1),
        "d_x": jnp.repeat(a["ssm_d"][l].astype(F32), SSM_HEAD_DIM).reshape(1, SSM_D_INNER),
        "norm_w": a["ssm_norm"][l].astype(F32).reshape(1, SSM_D_INNER),
    }
    return w


def _rope_tables(pos):
    half = MLA_ROPE // 2
    inv_freq = ROPE_BASE ** (-jnp.arange(half, dtype=F32) / half)
    ang = pos.astype(F32)[:, None] * inv_freq[None, :]
    cos, sin = jnp.cos(ang), jnp.sin(ang)
    z = jnp.zeros_like(cos)
    pad = jnp.zeros((pos.shape[0], LANES - MLA_ROPE), F32)
    return (jnp.concatenate([cos, cos, pad], axis=1),
            jnp.concatenate([-sin, z, pad], axis=1),
            jnp.concatenate([z, sin, pad], axis=1))


def _pad_keys(t, sk_pad):
    return jnp.pad(t, ((0, 0), (0, sk_pad - t.shape[1])) + ((0, 0),) * (t.ndim - 2))


def _layer(x, bsz, s, w, tabs, past, final_gain):
    m = bsz * s
    kv_lora = w["kv_norm"].shape[0]
    conv_dim = w["ssd"]["conv_w"].shape[1]
    x, xn = _ffn(x, w["ffn1_norm"], w["ffn1_g"], w["ffn1_u"], w["ffn1_d"], post="norm_bf16", post_gain=w["mix_norm"])

    u_q, u_ckv, u_small, u_z, u_xbc, fq, fk, fk_b, fv, fv_b = _inproj(
        xn, w["in_cat"], w["in_widths"], ((F32,),) * 5 + ((BF16,), (F32, BF16), (F32, BF16)), tn=INPROJ_TN)

    ff_lo = MLA_ROPE + SSM_HEADS
    ckv_new, small2 = _prep(u_ckv, w["kv_norm"], u_small, tabs, w["fb_lanes"], ff_lo=ff_lo, ff_hi=ff_lo + FOX_HEADS)
    kpe_new = small2[:, :MLA_ROPE]
    logf_new = small2[:, ff_lo:ff_lo + FOX_HEADS]
    u_dt = u_small[:, MLA_ROPE:ff_lo]

    if past is not None:
        caches, l = past
        past_len = caches["mla_ckv"].shape[2]
    ukv = functools.partial(_mm, w=w["ukv"], out_dtype=BF16, prologue="cast", tn=2048, name="mla_ukv")

    q_full = _mm(u_q, w["uq"], out_dtype=BF16, prologue="rms", gain=w["q_norm"], rope_tabs=tabs, tn=2048, name="mla_q")
    q_full = q_full.reshape(bsz, s, -1)
    kv_new = ukv(ckv_new).reshape(bsz, s, -1)
    mla = dict(heads=MLA_HEADS, dq=MLA_QK_PAD, dk=MLA_NOPE, dv=MLA_V, scale=MLA_SCALE, mode="chunk")
    sk_pad = -(-s // LANES) * LANES
    if past is None:
        kpe_pad = jnp.pad(kpe_new.astype(BF16).reshape(bsz, s, -1),
                          ((0, 0), (0, sk_pad - s), (0, MLA_QK_PAD - MLA_NOPE - MLA_ROPE)))
        kv_all = _pad_keys(kv_new, sk_pad)
        o_mla = _attention(q_full, kv_all, kv_all, kpe_pad, None, k_col=0, v_col=1, q_off=0, n_valid=s, **mla)
    else:
        kv_past = ukv(caches["mla_ckv"][l].reshape(bsz * past_len, kv_lora)).reshape(bsz, past_len, -1)
        o_mla = _decode_attention(
            q_full, ((kv_past, (), 0), (kv_past, (), 1)), ((kv_new, (), 0), (kv_new, (), 1)),
            ((caches["mla_kpe"], (l,)), (kpe_new.reshape(bsz, s, MLA_ROPE), ())), None, **mla)

    if past is None:
        conv_state = jnp.zeros((bsz, SSM_CONV_W - 1, conv_dim), F32)
        h0 = jnp.zeros((bsz, SSM_STATE, SSM_D_INNER), F32)
    else:
        conv_state = caches["conv"][l].astype(F32)
        h0 = jnp.transpose(caches["ssm"][l].astype(F32), (0, 3, 1, 2)).reshape(bsz, SSM_STATE, SSM_D_INNER)
    c0 = jnp.pad(conv_state, ((0, 0), (SUBLANES - (SSM_CONV_W - 1), 0), (0, 0)))
    xbc3 = u_xbc.reshape(bsz, s, conv_dim)
    dt3 = u_dt.reshape(bsz, s, SSM_HEADS)
    o_ssd, h_new = _ssd(u_z.reshape(bsz, s, SSM_D_INNER), xbc3, dt3, jnp.swapaxes(dt3, 1, 2), w["ssd"], h0, c0)
    ssm_new = jnp.transpose(h_new.reshape(bsz, SSM_STATE, SSM_HEADS, SSM_HEAD_DIM), (0, 2, 3, 1))
    keep = SSM_CONV_W - 1
    conv_new = xbc3[:, s - keep:] if s >= keep else jnp.concatenate([conv_state, xbc3], axis=1)[:, -keep:]

    hw = FOX_HEADS * FOX_HEAD_DIM
    fox = dict(heads=FOX_HEADS, dq=FOX_HEAD_DIM, dk=FOX_HEAD_DIM, dv=FOX_HEAD_DIM, scale=FOX_SCALE, mode="causal")
    logf_all = logf_new.reshape(bsz, s, FOX_HEADS)
    if past is not None:
        logf_all = jnp.concatenate([caches["fox_logf"][l].astype(F32), logf_all], axis=1)
    n_keys = logf_all.shape[1]
    neg_cum = _cumsum_last(jnp.swapaxes(_pad_keys(logf_all, -(-n_keys // LANES) * LANES), 1, 2), -1.0 / FOX_SCALE)
    fq3 = fq.reshape(bsz, s, hw)
    if past is None:
        o_fox = _attention(fq3, _pad_keys(fk_b.reshape(bsz, s, hw), sk_pad), _pad_keys(fv_b.reshape(bsz, s, hw), sk_pad),
                           None, neg_cum, q_off=0, n_valid=s, **fox)
    else:
        o_fox = _decode_attention(
            fq3, ((caches["fox_k"], (l,), 0), (caches["fox_v"], (l,), 0)),
            ((fk.reshape(bsz, s, hw), (), 0), (fv.reshape(bsz, s, hw), (), 0)), None,
            (neg_cum[:, :, :past_len], neg_cum[:, :, past_len:n_keys]), **fox)

    merged = _merge(xn, [o_mla.reshape(m, -1), o_ssd.reshape(m, -1), o_fox.reshape(m, -1)],
                    [w["w_br_mla"], w["w_br_ssd"], w["w_br_fox"]], w["in_gate"])
    x = _mm(merged, w["w_out"], out_dtype=F32, residual=x, tm=512, tn=2048, name="out_proj")
    (x,) = _ffn(x, w["ffn2_norm"], w["ffn2_g"], w["ffn2_u"], w["ffn2_d"],
                post=None if final_gain is None else "norm_only", post_gain=final_gain)
    state = (ckv_new.reshape(bsz, s, kv_lora), kpe_new.reshape(bsz, s, MLA_ROPE),
             fk.reshape(bsz, s, FOX_HEADS, FOX_HEAD_DIM), fv.reshape(bsz, s, FOX_HEADS, FOX_HEAD_DIM),
             logf_new.reshape(bsz, s, FOX_HEADS), ssm_new, conv_new)
    return x, state


def kernel(x_prompt, x_sample, cache_mla_ckv, cache_mla_kpe, cache_fox_k, cache_fox_v, cache_fox_logf, state_ssm,
           state_conv, ffn1_norm, ffn1_w_gate, ffn1_w_up, ffn1_w_down, mix_norm, w_in, mla_q_norm, mla_w_uq,
           mla_kv_norm, mla_w_ukv, ssm_conv_w, ssm_conv_b, ssm_dt_bias, ssm_a_log, ssm_d, ssm_norm, fox_b_f,
           w_br_mla, w_br_ssd, w_br_fox, w_out, ffn2_norm, ffn2_w_gate, ffn2_w_up, ffn2_w_down, final_norm):
    a = dict(ffn1_norm=ffn1_norm, ffn1_w_gate=ffn1_w_gate, ffn1_w_up=ffn1_w_up, ffn1_w_down=ffn1_w_down,
             mix_norm=mix_norm, w_in=w_in, mla_q_norm=mla_q_norm, mla_w_uq=mla_w_uq, mla_kv_norm=mla_kv_norm,
             mla_w_ukv=mla_w_ukv, ssm_conv_w=ssm_conv_w, ssm_conv_b=ssm_conv_b, ssm_dt_bias=ssm_dt_bias,
             ssm_a_log=ssm_a_log, ssm_d=ssm_d, ssm_norm=ssm_norm, fox_b_f=fox_b_f, w_br_mla=w_br_mla,
             w_br_ssd=w_br_ssd, w_br_fox=w_br_fox, w_out=w_out, ffn2_norm=ffn2_norm, ffn2_w_gate=ffn2_w_gate,
             ffn2_w_up=ffn2_w_up, ffn2_w_down=ffn2_w_down)
    depth = w_in.shape[0]
    stacked = _stacked_weights(a)
    bp, sp, d_model = x_prompt.shape
    bs, ss, _ = x_sample.shape
    past_len = cache_mla_ckv.shape[2]
    tabs_p = tuple(jnp.tile(t, (bp, 1)) for t in _rope_tables(jnp.arange(sp, dtype=jnp.int32)))
    tabs_s = tuple(jnp.tile(t, (bs, 1)) for t in _rope_tables(past_len + jnp.arange(ss, dtype=jnp.int32)))
    hp = x_prompt.reshape(bp * sp, d_model).astype(F32)
    hs = x_sample.reshape(bs * ss, d_model).astype(F32)
    caches = {"mla_ckv": cache_mla_ckv, "mla_kpe": cache_mla_kpe, "fox_k": cache_fox_k, "fox_v": cache_fox_v,
              "fox_logf": cache_fox_logf, "ssm": state_ssm, "conv": state_conv}
    new_p, new_s = [], []
    for l in range(depth):
        w = _layer_weights(l, a, stacked)
        fg = final_norm if l == depth - 1 else None
        hp, st_p = _layer(hp, bp, sp, w, tabs_p, None, fg)
        hs, st_s = _layer(hs, bs, ss, w, tabs_s, (caches, l), fg)
        new_p.append(st_p)
        new_s.append(st_s)
    y_prompt = hp.reshape(bp, sp, d_model)
    y_sample = hs.reshape(bs, ss, d_model)
    stk = lambda states, i: jnp.stack([st[i] for st in states], axis=0)
    return (y_prompt, y_sample) + tuple(stk(new_p, i) for i in range(7)) + tuple(stk(new_s, i) for i in range(7))
```
